```python
import jax, jax.numpy as jnp
from jax import lax
import numpy as np

D_MODEL = 1024
BATCH = 8
SEQ = 8192
DEPTH = 1

CONV_DIM = 512
CONV_KERNEL = 31
HGRN_DIM = 1024
HGRN_HEADS = 8
HGRN_HEAD_DIM = HGRN_DIM // HGRN_HEADS
HGRN_CHUNK = 64
N_BRANCHES = 2
D_FF = 2816
FFN_KERNEL = 3
LN_EPS = 1e-5
RMS_EPS = 1e-6
ALPHA = (2.0 * DEPTH) ** 0.25
BETA = (8.0 * DEPTH) ** -0.25

IN_SPLITS = [CONV_DIM, CONV_DIM, HGRN_DIM, HGRN_DIM, HGRN_DIM, HGRN_DIM, N_BRANCHES * D_MODEL]
IN_COLS = sum(IN_SPLITS)
IN_OFFSETS = list(np.cumsum(IN_SPLITS)[:-1])

kernel_name = "hybrid_conformer_conv_hgrn2_gated_merge_convffn"


def layer_norm(x, g, b):
    xf = x.astype(jnp.float32)
    mu = jnp.mean(xf, axis=-1, keepdims=True)
    var = jnp.mean(jnp.square(xf - mu), axis=-1, keepdims=True)
    y = (xf - mu) * lax.rsqrt(var + LN_EPS) * g.astype(jnp.float32) + b.astype(jnp.float32)
    return y.astype(x.dtype)


def causal_dwconv(x, w, b):
    k_w = w.shape[0]
    c = x.shape[-1]
    y = lax.conv_general_dilated(
        x, w[:, None, :].astype(x.dtype), window_strides=(1,), padding=[(k_w - 1, 0)],
        dimension_numbers=("NWC", "WIO", "NWC"), feature_group_count=c)
    return y + b.astype(x.dtype)


def hgrn2_chunked(q, k, v, logf):
    bsz, seq, nh, dk = q.shape
    dv = v.shape[-1]
    nc = seq // HGRN_CHUNK

    def to_chunks(t):
        return t.reshape(bsz, nc, HGRN_CHUNK, nh, t.shape[-1]).transpose(1, 0, 3, 2, 4)

    qc, kc, vc, lfc = to_chunks(q), to_chunks(k), to_chunks(v), to_chunks(logf)
    bc = jnp.cumsum(lfc, axis=3)
    mask = jnp.tril(jnp.ones((HGRN_CHUNK, HGRN_CHUNK), dtype=bool))[:, :, None]

    def step(state, inp):
        q_c, k_c, v_c, b_c = inp
        diff = b_c[:, :, :, None, :] - b_c[:, :, None, :, :]
        decay = jnp.exp(jnp.where(mask, diff, -jnp.inf))
        scores = jnp.einsum("bhtk,bhsk,bhtsk->bhts", q_c, k_c, decay)
        o_intra = jnp.einsum("bhts,bhsv->bhtv", scores, v_c)
        o_inter = jnp.einsum("bhtk,bhkv->bhtv", q_c * jnp.exp(b_c), state)
        b_last = b_c[:, :, -1, :]
        k_tail = k_c * jnp.exp(b_last[:, :, None, :] - b_c)
        new_state = jnp.exp(b_last)[..., None] * state + jnp.einsum("bhsk,bhsv->bhkv", k_tail, v_c)
        return new_state, o_intra + o_inter

    s0 = jnp.zeros((bsz, nh, dk, dv), jnp.float32)
    _, oc = lax.scan(step, s0, (qc, kc, vc, bc))
    return oc.transpose(1, 0, 3, 2, 4).reshape(bsz, seq, nh, dv)


def _fwd_setup_inputs(seed: int = 0) -> dict:
    key = jax.random.key(seed)
    ks = jax.random.split(key, 20)

    def nrm(k, shape, scale):
        return jax.random.normal(k, shape, jnp.float32) * scale

    col_scale = jnp.concatenate([
        jnp.ones((2 * CONV_DIM + 2 * HGRN_DIM,), jnp.float32),
        jnp.full((HGRN_DIM,), BETA, jnp.float32),
        jnp.ones((HGRN_DIM + N_BRANCHES * D_MODEL,), jnp.float32)])
    ffn_scale = jnp.concatenate([jnp.full((D_FF,), BETA, jnp.float32), jnp.ones((D_FF,), jnp.float32)])
    return {
        "x": nrm(ks[0], (BATCH, SEQ, D_MODEL), 1.0),
        "w_in": nrm(ks[1], (DEPTH, D_MODEL, IN_COLS), D_MODEL ** -0.5) * col_scale,
        "w_conv_dw": nrm(ks[2], (DEPTH, CONV_KERNEL, CONV_DIM), CONV_KERNEL ** -0.5),
        "b_conv_dw": nrm(ks[3], (DEPTH, CONV_DIM), 0.02),
        "conv_ln_g": 1.0 + nrm(ks[4], (DEPTH, CONV_DIM), 0.02),
        "conv_ln_b": nrm(ks[5], (DEPTH, CONV_DIM), 0.02),
        "w_conv_out": nrm(ks[6], (DEPTH, CONV_DIM, D_MODEL), BETA * CONV_DIM ** -0.5),
        "hgrn_lb_logits": nrm(ks[7], (DEPTH + 1, HGRN_DIM), 0.5),
        "hgrn_norm_g": 1.0 + nrm(ks[8], (DEPTH, HGRN_DIM), 0.02),
        "w_hgrn_out": nrm(ks[9], (DEPTH, HGRN_DIM, D_MODEL), BETA * HGRN_DIM ** -0.5),
        "w_out": nrm(ks[10], (DEPTH, D_MODEL, D_MODEL), BETA * D_MODEL ** -0.5),
        "ln1_g": 1.0 + nrm(ks[11], (DEPTH, D_MODEL), 0.02),
        "ln1_b": nrm(ks[12], (DEPTH, D_MODEL), 0.02),
        "w_ffn_in": nrm(ks[13], (DEPTH, D_MODEL, 2 * D_FF), D_MODEL ** -0.5) * ffn_scale,
        "w_ffn_dw": nrm(ks[14], (DEPTH, FFN_KERNEL, D_FF), FFN_KERNEL ** -0.5),
        "b_ffn_dw": nrm(ks[15], (DEPTH, D_FF), 0.02),
        "w_ffn_out": nrm(ks[16], (DEPTH, D_FF, D_MODEL), BETA * D_FF ** -0.5),
        "ln2_g": 1.0 + nrm(ks[17], (DEPTH, D_MODEL), 0.02),
        "ln2_b": nrm(ks[18], (DEPTH, D_MODEL), 0.02),
    }


def _fwd_reference(x, w_in, w_conv_dw, b_conv_dw, conv_ln_g, conv_ln_b, w_conv_out,
              hgrn_lb_logits, hgrn_norm_g, w_hgrn_out, w_out, ln1_g, ln1_b,
              w_ffn_in, w_ffn_dw, b_ffn_dw, w_ffn_out, ln2_g, ln2_b):
    bsz, seq, _ = x.shape
    lb_all = jnp.cumsum(jax.nn.softmax(hgrn_lb_logits.astype(jnp.float32), axis=0), axis=0)
    for l in range(DEPTH):
        h = x
        proj = h @ w_in[l]
        c_val, c_gate, q_z, f_z, i_v, g_z, m_z = jnp.split(proj, IN_OFFSETS, axis=-1)

        c = c_val * jax.nn.sigmoid(c_gate)
        c = causal_dwconv(c, w_conv_dw[l], b_conv_dw[l])
        c = jax.nn.silu(layer_norm(c, conv_ln_g[l], conv_ln_b[l]))
        y_conv = c @ w_conv_out[l]

        lb = lb_all[l]
        zf = f_z.astype(jnp.float32)
        logf = jnp.log(lb + (1.0 - lb) * jax.nn.sigmoid(zf))
        k_in = (1.0 - lb) * jax.nn.sigmoid(-zf)
        qf = jax.nn.silu(q_z.astype(jnp.float32))
        heads = lambda t: t.reshape(bsz, seq, HGRN_HEADS, HGRN_HEAD_DIM)
        o = hgrn2_chunked(heads(qf), heads(k_in), heads(i_v.astype(jnp.float32)), heads(logf))
        o = o * lax.rsqrt(jnp.mean(jnp.square(o), axis=-1, keepdims=True) + RMS_EPS)
        o = o.reshape(bsz, seq, HGRN_DIM) * hgrn_norm_g[l].astype(jnp.float32)
        o = o.astype(x.dtype) * jax.nn.silu(g_z)
        y_hgrn = o @ w_hgrn_out[l]

        gates = jax.nn.sigmoid(m_z).reshape(bsz, seq, N_BRANCHES, D_MODEL)
        mixed = gates[:, :, 0, :] * y_conv + gates[:, :, 1, :] * y_hgrn
        mix = mixed @ w_out[l]
        x = layer_norm(ALPHA * x + mix, ln1_g[l], ln1_b[l])

        z = x @ w_ffn_in[l]
        u, gv = jnp.split(z, [D_FF], axis=-1)
        u = causal_dwconv(u, w_ffn_dw[l], b_ffn_dw[l])
        y_ffn = (jax.nn.gelu(u) * gv) @ w_ffn_out[l]
        x = layer_norm(ALPHA * x + y_ffn, ln2_g[l], ln2_b[l])
    return x


import jax as _jax
import jax.numpy as _jnp

TWIN_FORMAT = 'train_step'
FWD_PARAMS = ['x', 'w_in', 'w_conv_dw', 'b_conv_dw', 'conv_ln_g', 'conv_ln_b', 'w_conv_out', 'hgrn_lb_logits', 'hgrn_norm_g', 'w_hgrn_out', 'w_out', 'ln1_g', 'ln1_b', 'w_ffn_in', 'w_ffn_dw', 'b_ffn_dw', 'w_ffn_out', 'ln2_g', 'ln2_b']
TWIN_WEIGHTS = ['w_in', 'w_conv_dw', 'b_conv_dw', 'conv_ln_g', 'conv_ln_b', 'w_conv_out', 'hgrn_lb_logits', 'hgrn_norm_g', 'w_hgrn_out', 'w_out', 'ln1_g', 'ln1_b', 'w_ffn_in', 'w_ffn_dw', 'b_ffn_dw', 'w_ffn_out', 'ln2_g', 'ln2_b']
TWIN_DIFF_INPUT = 'x'
TWIN_INPUTS = ['x', 'w_in', 'w_conv_dw', 'b_conv_dw', 'conv_ln_g', 'conv_ln_b', 'w_conv_out', 'hgrn_lb_logits', 'hgrn_norm_g', 'w_hgrn_out', 'w_out', 'ln1_g', 'ln1_b', 'w_ffn_in', 'w_ffn_dw', 'b_ffn_dw', 'w_ffn_out', 'ln2_g', 'ln2_b', 'loss_target', 'm_w_in', 'm_w_conv_dw', 'm_b_conv_dw', 'm_conv_ln_g', 'm_conv_ln_b', 'm_w_conv_out', 'm_hgrn_lb_logits', 'm_hgrn_norm_g', 'm_w_hgrn_out', 'm_w_out', 'm_ln1_g', 'm_ln1_b', 'm_w_ffn_in', 'm_w_ffn_dw', 'm_b_ffn_dw', 'm_w_ffn_out', 'm_ln2_g', 'm_ln2_b', 'v_w_in', 'v_w_conv_dw', 'v_b_conv_dw', 'v_conv_ln_g', 'v_conv_ln_b', 'v_w_conv_out', 'v_hgrn_lb_logits', 'v_hgrn_norm_g', 'v_w_hgrn_out', 'v_w_out', 'v_ln1_g', 'v_ln1_b', 'v_w_ffn_in', 'v_w_ffn_dw', 'v_b_ffn_dw', 'v_w_ffn_out', 'v_ln2_g', 'v_ln2_b']
TWIN_OUTPUTS = ['loss', 'grad_x', 'grad_w_in', 'grad_w_conv_dw', 'grad_b_conv_dw', 'grad_conv_ln_g', 'grad_conv_ln_b', 'grad_w_conv_out', 'grad_hgrn_lb_logits', 'grad_hgrn_norm_g', 'grad_w_hgrn_out', 'grad_w_out', 'grad_ln1_g', 'grad_ln1_b', 'grad_w_ffn_in', 'grad_w_ffn_dw', 'grad_b_ffn_dw', 'grad_w_ffn_out', 'grad_ln2_g', 'grad_ln2_b', 'delta_w_in', 'delta_w_conv_dw', 'delta_b_conv_dw', 'delta_conv_ln_g', 'delta_conv_ln_b', 'delta_w_conv_out', 'delta_hgrn_lb_logits', 'delta_hgrn_norm_g', 'delta_w_hgrn_out', 'delta_w_out', 'delta_ln1_g', 'delta_ln1_b', 'delta_w_ffn_in', 'delta_w_ffn_dw', 'delta_b_ffn_dw', 'delta_w_ffn_out', 'delta_ln2_g', 'delta_ln2_b', 'new_m_w_in', 'new_m_w_conv_dw', 'new_m_b_conv_dw', 'new_m_conv_ln_g', 'new_m_conv_ln_b', 'new_m_w_conv_out', 'new_m_hgrn_lb_logits', 'new_m_hgrn_norm_g', 'new_m_w_hgrn_out', 'new_m_w_out', 'new_m_ln1_g', 'new_m_ln1_b', 'new_m_w_ffn_in', 'new_m_w_ffn_dw', 'new_m_b_ffn_dw', 'new_m_w_ffn_out', 'new_m_ln2_g', 'new_m_ln2_b', 'new_v_w_in', 'new_v_w_conv_dw', 'new_v_b_conv_dw', 'new_v_conv_ln_g', 'new_v_conv_ln_b', 'new_v_w_conv_out', 'new_v_hgrn_lb_logits', 'new_v_hgrn_norm_g', 'new_v_w_hgrn_out', 'new_v_w_out', 'new_v_ln1_g', 'new_v_ln1_b', 'new_v_w_ffn_in', 'new_v_w_ffn_dw', 'new_v_b_ffn_dw', 'new_v_w_ffn_out', 'new_v_ln2_g', 'new_v_ln2_b']
TWIN_LEAF_KINDS = {'loss': 'loss', 'grad_x': 'grad_x', 'grad_w_in': 'grad_w', 'grad_w_conv_dw': 'grad_w', 'grad_b_conv_dw': 'grad_w', 'grad_conv_ln_g': 'grad_w', 'grad_conv_ln_b': 'grad_w', 'grad_w_conv_out': 'grad_w', 'grad_hgrn_lb_logits': 'grad_w', 'grad_hgrn_norm_g': 'grad_w', 'grad_w_hgrn_out': 'grad_w', 'grad_w_out': 'grad_w', 'grad_ln1_g': 'grad_w', 'grad_ln1_b': 'grad_w', 'grad_w_ffn_in': 'grad_w', 'grad_w_ffn_dw': 'grad_w', 'grad_b_ffn_dw': 'grad_w', 'grad_w_ffn_out': 'grad_w', 'grad_ln2_g': 'grad_w', 'grad_ln2_b': 'grad_w', 'delta_w_in': 'delta_w', 'delta_w_conv_dw': 'delta_w', 'delta_b_conv_dw': 'delta_w', 'delta_conv_ln_g': 'delta_w', 'delta_conv_ln_b': 'delta_w', 'delta_w_conv_out': 'delta_w', 'delta_hgrn_lb_logits': 'delta_w', 'delta_hgrn_norm_g': 'delta_w', 'delta_w_hgrn_out': 'delta_w', 'delta_w_out': 'delta_w', 'delta_ln1_g': 'delta_w', 'delta_ln1_b': 'delta_w', 'delta_w_ffn_in': 'delta_w', 'delta_w_ffn_dw': 'delta_w', 'delta_b_ffn_dw': 'delta_w', 'delta_w_ffn_out': 'delta_w', 'delta_ln2_g': 'delta_w', 'delta_ln2_b': 'delta_w', 'new_m_w_in': 'new_m', 'new_m_w_conv_dw': 'new_m', 'new_m_b_conv_dw': 'new_m', 'new_m_conv_ln_g': 'new_m', 'new_m_conv_ln_b': 'new_m', 'new_m_w_conv_out': 'new_m', 'new_m_hgrn_lb_logits': 'new_m', 'new_m_hgrn_norm_g': 'new_m', 'new_m_w_hgrn_out': 'new_m', 'new_m_w_out': 'new_m', 'new_m_ln1_g': 'new_m', 'new_m_ln1_b': 'new_m', 'new_m_w_ffn_in': 'new_m', 'new_m_w_ffn_dw': 'new_m', 'new_m_b_ffn_dw': 'new_m', 'new_m_w_ffn_out': 'new_m', 'new_m_ln2_g': 'new_m', 'new_m_ln2_b': 'new_m', 'new_v_w_in': 'new_v', 'new_v_w_conv_dw': 'new_v', 'new_v_b_conv_dw': 'new_v', 'new_v_conv_ln_g': 'new_v', 'new_v_conv_ln_b': 'new_v', 'new_v_w_conv_out': 'new_v', 'new_v_hgrn_lb_logits': 'new_v', 'new_v_hgrn_norm_g': 'new_v', 'new_v_w_hgrn_out': 'new_v', 'new_v_w_out': 'new_v', 'new_v_ln1_g': 'new_v', 'new_v_ln1_b': 'new_v', 'new_v_w_ffn_in': 'new_v', 'new_v_w_ffn_dw': 'new_v', 'new_v_b_ffn_dw': 'new_v', 'new_v_w_ffn_out': 'new_v', 'new_v_ln2_g': 'new_v', 'new_v_ln2_b': 'new_v'}


def _forward(args):
    return _fwd_reference(*[args[k] for k in FWD_PARAMS])


def _output_shape():
    def fwd():
        inp = _fwd_setup_inputs(0)
        return _fwd_reference(*[inp[k] for k in FWD_PARAMS])
    out = _jax.eval_shape(fwd)
    return out.shape, out.dtype

N_MICROBATCH = 1
ADAM_LR = 0.001
ADAM_B1 = 0.9
ADAM_B2 = 0.999
ADAM_EPS = 1e-08
ADAM_WD = 0.01
ADAM_STEP = 10
PER_EXAMPLE_BATCH_AXIS = {'x': 0, 'loss_target': 0}
SHARED_INPUTS = []
_WEIGHT_DTYPES = {'w_in': _jnp.float32, 'w_conv_dw': _jnp.float32, 'b_conv_dw': _jnp.float32, 'conv_ln_g': _jnp.float32, 'conv_ln_b': _jnp.float32, 'w_conv_out': _jnp.float32, 'hgrn_lb_logits': _jnp.float32, 'hgrn_norm_g': _jnp.float32, 'w_hgrn_out': _jnp.float32, 'w_out': _jnp.float32, 'ln1_g': _jnp.float32, 'ln1_b': _jnp.float32, 'w_ffn_in': _jnp.float32, 'w_ffn_dw': _jnp.float32, 'b_ffn_dw': _jnp.float32, 'w_ffn_out': _jnp.float32, 'ln2_g': _jnp.float32, 'ln2_b': _jnp.float32}
MOMENT_SCALE = {'w_in': 2.150782e-02, 'w_conv_dw': 3.580677e-02, 'b_conv_dw': 1.568936e-01, 'conv_ln_g': 6.164446e-02, 'conv_ln_b': 9.697242e-02, 'w_conv_out': 5.592715e-02, 'hgrn_lb_logits': 2.076505e-03, 'hgrn_norm_g': 2.404048e-02, 'w_hgrn_out': 4.075446e-02, 'w_out': 6.892765e-02, 'ln1_g': 2.212172e+00, 'ln1_b': 1.021691e+00, 'w_ffn_in': 3.861381e-02, 'w_ffn_dw': 2.893759e-02, 'b_ffn_dw': 4.709235e-02, 'w_ffn_out': 7.592029e-02, 'ln2_g': 6.406370e+01, 'ln2_b': 2.328231e+00}


def _to_microbatches(a, axis):
    t = _jnp.moveaxis(a, axis, 0)
    t = t.reshape((N_MICROBATCH, t.shape[0] // N_MICROBATCH) + t.shape[1:])
    return _jnp.moveaxis(t, 1, axis + 1)


def setup_inputs(seed: int = 0) -> dict:
    inp = _fwd_setup_inputs(seed)
    key = _jax.random.fold_in(_jax.random.key(seed), 7919)
    shape, _ = _output_shape()
    out = dict(inp)
    out["loss_target"] = _jax.random.normal(_jax.random.fold_in(key, 0), shape, _jnp.float32)
    for i, name in enumerate(TWIN_WEIGHTS):
        w = inp[name].astype(_jnp.float32)
        if MOMENT_SCALE is None:
            s = _jnp.sqrt(_jnp.mean(_jnp.square(w)) + 1e-30)
        else:
            s = MOMENT_SCALE[name]
        km, kv = _jax.random.split(_jax.random.fold_in(key, i + 1))
        out[name] = w
        out["m_" + name] = s * _jax.random.normal(km, w.shape, _jnp.float32)
        out["v_" + name] = (s * s) * _jax.random.uniform(kv, w.shape, _jnp.float32, 0.5, 1.5)
    if N_MICROBATCH > 1:
        for name, axis in PER_EXAMPLE_BATCH_AXIS.items():
            out[name] = _to_microbatches(out[name], axis)
    return {'x': out['x'], 'w_in': out['w_in'], 'w_conv_dw': out['w_conv_dw'], 'b_conv_dw': out['b_conv_dw'], 'conv_ln_g': out['conv_ln_g'], 'conv_ln_b': out['conv_ln_b'], 'w_conv_out': out['w_conv_out'], 'hgrn_lb_logits': out['hgrn_lb_logits'], 'hgrn_norm_g': out['hgrn_norm_g'], 'w_hgrn_out': out['w_hgrn_out'], 'w_out': out['w_out'], 'ln1_g': out['ln1_g'], 'ln1_b': out['ln1_b'], 'w_ffn_in': out['w_ffn_in'], 'w_ffn_dw': out['w_ffn_dw'], 'b_ffn_dw': out['b_ffn_dw'], 'w_ffn_out': out['w_ffn_out'], 'ln2_g': out['ln2_g'], 'ln2_b': out['ln2_b'], 'loss_target': out['loss_target'], 'm_w_in': out['m_w_in'], 'm_w_conv_dw': out['m_w_conv_dw'], 'm_b_conv_dw': out['m_b_conv_dw'], 'm_conv_ln_g': out['m_conv_ln_g'], 'm_conv_ln_b': out['m_conv_ln_b'], 'm_w_conv_out': out['m_w_conv_out'], 'm_hgrn_lb_logits': out['m_hgrn_lb_logits'], 'm_hgrn_norm_g': out['m_hgrn_norm_g'], 'm_w_hgrn_out': out['m_w_hgrn_out'], 'm_w_out': out['m_w_out'], 'm_ln1_g': out['m_ln1_g'], 'm_ln1_b': out['m_ln1_b'], 'm_w_ffn_in': out['m_w_ffn_in'], 'm_w_ffn_dw': out['m_w_ffn_dw'], 'm_b_ffn_dw': out['m_b_ffn_dw'], 'm_w_ffn_out': out['m_w_ffn_out'], 'm_ln2_g': out['m_ln2_g'], 'm_ln2_b': out['m_ln2_b'], 'v_w_in': out['v_w_in'], 'v_w_conv_dw': out['v_w_conv_dw'], 'v_b_conv_dw': out['v_b_conv_dw'], 'v_conv_ln_g': out['v_conv_ln_g'], 'v_conv_ln_b': out['v_conv_ln_b'], 'v_w_conv_out': out['v_w_conv_out'], 'v_hgrn_lb_logits': out['v_hgrn_lb_logits'], 'v_hgrn_norm_g': out['v_hgrn_norm_g'], 'v_w_hgrn_out': out['v_w_hgrn_out'], 'v_w_out': out['v_w_out'], 'v_ln1_g': out['v_ln1_g'], 'v_ln1_b': out['v_ln1_b'], 'v_w_ffn_in': out['v_w_ffn_in'], 'v_w_ffn_dw': out['v_w_ffn_dw'], 'v_b_ffn_dw': out['v_b_ffn_dw'], 'v_w_ffn_out': out['v_w_ffn_out'], 'v_ln2_g': out['v_ln2_g'], 'v_ln2_b': out['v_ln2_b']}


def _loss(weights, diff, rest, loss_target):
    with _jax.named_scope("forward"):
        args = {**rest, TWIN_DIFF_INPUT: diff, **{k: w.astype(_WEIGHT_DTYPES[k]) for k, w in weights.items()}}
        y = _forward(args)
    with _jax.named_scope("loss_head"):
        err = _jnp.square(y.astype(_jnp.float32) - loss_target)
        return 0.5 * _jnp.sum(_jnp.mean(err, axis=-1)) if err.ndim else 0.5 * err


def _adamw(w, g, m, v):
    m = ADAM_B1 * m + (1.0 - ADAM_B1) * g
    v = ADAM_B2 * v + (1.0 - ADAM_B2) * _jnp.square(g)
    m_hat = m / (1.0 - ADAM_B1 ** ADAM_STEP)
    v_hat = v / (1.0 - ADAM_B2 ** ADAM_STEP)
    delta = -ADAM_LR * (m_hat / (_jnp.sqrt(v_hat) + ADAM_EPS) + ADAM_WD * w)
    return delta, m, v


def reference(x, w_in, w_conv_dw, b_conv_dw, conv_ln_g, conv_ln_b, w_conv_out, hgrn_lb_logits, hgrn_norm_g, w_hgrn_out, w_out, ln1_g, ln1_b, w_ffn_in, w_ffn_dw, b_ffn_dw, w_ffn_out, ln2_g, ln2_b, loss_target, m_w_in, m_w_conv_dw, m_b_conv_dw, m_conv_ln_g, m_conv_ln_b, m_w_conv_out, m_hgrn_lb_logits, m_hgrn_norm_g, m_w_hgrn_out, m_w_out, m_ln1_g, m_ln1_b, m_w_ffn_in, m_w_ffn_dw, m_b_ffn_dw, m_w_ffn_out, m_ln2_g, m_ln2_b, v_w_in, v_w_conv_dw, v_b_conv_dw, v_conv_ln_g, v_conv_ln_b, v_w_conv_out, v_hgrn_lb_logits, v_hgrn_norm_g, v_w_hgrn_out, v_w_out, v_ln1_g, v_ln1_b, v_w_ffn_in, v_w_ffn_dw, v_b_ffn_dw, v_w_ffn_out, v_ln2_g, v_ln2_b):
    given = dict(x=x, w_in=w_in, w_conv_dw=w_conv_dw, b_conv_dw=b_conv_dw, conv_ln_g=conv_ln_g, conv_ln_b=conv_ln_b, w_conv_out=w_conv_out, hgrn_lb_logits=hgrn_lb_logits, hgrn_norm_g=hgrn_norm_g, w_hgrn_out=w_hgrn_out, w_out=w_out, ln1_g=ln1_g, ln1_b=ln1_b, w_ffn_in=w_ffn_in, w_ffn_dw=w_ffn_dw, b_ffn_dw=b_ffn_dw, w_ffn_out=w_ffn_out, ln2_g=ln2_g, ln2_b=ln2_b, loss_target=loss_target, m_w_in=m_w_in, m_w_conv_dw=m_w_conv_dw, m_b_conv_dw=m_b_conv_dw, m_conv_ln_g=m_conv_ln_g, m_conv_ln_b=m_conv_ln_b, m_w_conv_out=m_w_conv_out, m_hgrn_lb_logits=m_hgrn_lb_logits, m_hgrn_norm_g=m_hgrn_norm_g, m_w_hgrn_out=m_w_hgrn_out, m_w_out=m_w_out, m_ln1_g=m_ln1_g, m_ln1_b=m_ln1_b, m_w_ffn_in=m_w_ffn_in, m_w_ffn_dw=m_w_ffn_dw, m_b_ffn_dw=m_b_ffn_dw, m_w_ffn_out=m_w_ffn_out, m_ln2_g=m_ln2_g, m_ln2_b=m_ln2_b, v_w_in=v_w_in, v_w_conv_dw=v_w_conv_dw, v_b_conv_dw=v_b_conv_dw, v_conv_ln_g=v_conv_ln_g, v_conv_ln_b=v_conv_ln_b, v_w_conv_out=v_w_conv_out, v_hgrn_lb_logits=v_hgrn_lb_logits, v_hgrn_norm_g=v_hgrn_norm_g, v_w_hgrn_out=v_w_hgrn_out, v_w_out=v_w_out, v_ln1_g=v_ln1_g, v_ln1_b=v_ln1_b, v_w_ffn_in=v_w_ffn_in, v_w_ffn_dw=v_w_ffn_dw, v_b_ffn_dw=v_b_ffn_dw, v_w_ffn_out=v_w_ffn_out, v_ln2_g=v_ln2_g, v_ln2_b=v_ln2_b)
    weights = {n: given[n] for n in TWIN_WEIGHTS}
    shared = {n: given[n] for n in SHARED_INPUTS}
    per_example = {n: given[n] for n in ['x']}
    grad_fn = _jax.value_and_grad(_loss, argnums=(0, 1))

    def one_microbatch(ex, loss_target):
        ex = dict(ex)
        diff = ex.pop(TWIN_DIFF_INPUT)
        return grad_fn(weights, diff, {**shared, **ex}, loss_target)

    if N_MICROBATCH == 1:
        loss, (grad_w, grad_x) = one_microbatch(per_example, given["loss_target"])
    else:
        def body(carry, xs):
            loss_sum, grad_sum = carry
            l_k, (gw_k, gx_k) = one_microbatch(xs[0], xs[1])
            with _jax.named_scope("update"):
                return (loss_sum + l_k, _jax.tree.map(_jnp.add, grad_sum, gw_k)), gx_k

        init = (_jnp.zeros((), _jnp.float32), _jax.tree.map(_jnp.zeros_like, weights))
        (loss, grad_w), grad_x = _jax.lax.scan(body, init, (per_example, given["loss_target"]))
    with _jax.named_scope("update"):
        delta_w, new_m, new_v = {}, {}, {}
        for n in TWIN_WEIGHTS:
            delta_w[n], new_m[n], new_v[n] = _adamw(weights[n], grad_w[n], given["m_" + n], given["v_" + n])
    return (loss, grad_x, *[grad_w[n] for n in TWIN_WEIGHTS], *[delta_w[n] for n in TWIN_WEIGHTS],
            *[new_m[n] for n in TWIN_WEIGHTS], *[new_v[n] for n in TWIN_WEIGHTS])
```

```python
import jax
import jax.numpy as jnp
from jax import lax
from jax.experimental import pallas as pl
from jax.experimental.pallas import tpu as pltpu

F32 = jnp.float32
BF16 = jnp.bfloat16

D_MODEL = 1024
CONV_DIM = 512
CONV_KERNEL = 31
HGRN_DIM = 1024
HGRN_HEADS = 8
HEAD_DIM = 128
CHUNK = 64
SUB = 16
N_SUB = CHUNK // SUB
D_FF = 2816
FFN_KERNEL = 3
IN_COLS = 7168
LN_EPS = 1e-5
RMS_EPS = 1e-6
ALPHA = 2.0 ** 0.25
GELU_C = 0.7978845608028654
GELU_A = 0.044715

ADAM_LR = 0.001
ADAM_B1 = 0.9
ADAM_B2 = 0.999
ADAM_EPS = 1e-08
ADAM_WD = 0.01
ADAM_STEP = 10
ADAM_BC1 = 1.0 - ADAM_B1 ** ADAM_STEP
ADAM_BC2 = 1.0 - ADAM_B2 ** ADAM_STEP

N_CHIPS = 4
N_DEV = 8
ROW_BLOCK = 32
CONV_HALO = 32
FFN_HALO = 8
MESH = pl.DeviceIdType.MESH
ANY = pl.BlockSpec(memory_space=pl.ANY)


def _dot(a, b):
    return jnp.dot(a, b, preferred_element_type=F32)


def _dot_nt(a, b):
    return lax.dot_general(a, b, (((1,), (1,)), ((), ())), preferred_element_type=F32)


def _dot_tn(a, b):
    return lax.dot_general(a, b, (((0,), (0,)), ((), ())), preferred_element_type=F32)


def _sigmoid(z):
    return jax.nn.sigmoid(z)


def _silu_grad(z, s):
    return s * (1.0 + z * (1.0 - s))


def _gelu_and_grad(u):
    inner = GELU_C * (u + GELU_A * u * u * u)
    th = jnp.tanh(inner)
    g = 0.5 * u * (1.0 + th)
    dg = 0.5 * (1.0 + th) + 0.5 * u * (1.0 - th * th) * GELU_C * (1.0 + 3.0 * GELU_A * u * u)
    return g, dg


def _ln_stats(r):
    mu = jnp.mean(r, axis=-1, keepdims=True)
    xc = r - mu
    var = jnp.mean(xc * xc, axis=-1, keepdims=True)
    rstd = lax.rsqrt(var + LN_EPS)
    return xc * rstd, rstd


def _ln_bwd(dy, xhat, rstd, g):
    dxh = dy * g
    m1 = jnp.mean(dxh, axis=-1, keepdims=True)
    m2 = jnp.mean(dxh * xhat, axis=-1, keepdims=True)
    return rstd * (dxh - m1 - xhat * m2)


def _fold8(x):
    acc = x[0:8, :]
    for r in range(8, x.shape[0], 8):
        acc = acc + x[r:r + 8, :]
    return acc


def _call(body, *, name, grid, in_specs, out_specs, out_shape, scratch=(), vmem_mb=32, aliases=None):
    return pl.pallas_call(
        body, name=name, grid=grid, in_specs=in_specs, out_specs=out_specs, out_shape=out_shape,
        scratch_shapes=list(scratch), input_output_aliases=aliases or {},
        compiler_params=pltpu.CompilerParams(
            dimension_semantics=("arbitrary",) * len(grid), vmem_limit_bytes=vmem_mb * 2 ** 20))


def _sds(shape, dtype):
    return jax.ShapeDtypeStruct(shape, dtype)


def _proj(x, w):
    t = x.shape[0]
    tm, tn = 512, 512

    def body(x_ref, w_ref, p_ref, xb_ref):
        @pl.when(pl.program_id(1) == 0)
        def _():
            xb_ref[...] = x_ref[...].astype(BF16)
        p_ref[...] = _dot(xb_ref[...], w_ref[...])

    return _call(
        body, name="proj", grid=(t // tm, IN_COLS // tn),
        in_specs=[pl.BlockSpec((tm, D_MODEL), lambda i, j: (i, 0)),
                  pl.BlockSpec((D_MODEL, tn), lambda i, j: (0, j))],
        out_specs=[pl.BlockSpec((tm, tn), lambda i, j: (i, j)),
                   pl.BlockSpec((tm, D_MODEL), lambda i, j: (i, 0))],
        out_shape=[_sds((t, IN_COLS), F32), _sds((t, D_MODEL), BF16)])(x, w)


def _mm_nn(a, w, *, tm, tn, name):
    t, k = a.shape
    n = w.shape[1]

    def body(a_ref, w_ref, o_ref):
        o_ref[...] = _dot(a_ref[...], w_ref[...])

    return _call(
        body, name=name, grid=(t // tm, n // tn),
        in_specs=[pl.BlockSpec((tm, k), lambda i, j: (i, 0)), pl.BlockSpec((k, tn), lambda i, j: (0, j))],
        out_specs=pl.BlockSpec((tm, tn), lambda i, j: (i, j)),
        out_shape=_sds((t, n), F32))(a, w)


def _views(*arrs):
    out = []
    for a in arrs:
        if a.ndim == 2:
            out.append((a, None))
        else:
            out.extend((a, p) for p in range(a.shape[0]))
    return out


def _piece_layout(views, tile):
    starts, counts, total = [], [], 0
    for arr, _ in views:
        width = arr.shape[-1]
        assert width % tile == 0
        starts.append(total)
        counts.append(width // tile)
        total += width // tile
    return starts, counts, total


def _mm_tn(a, views, *, tn, name, tt=512, vmem_mb=32):
    t, m = a.shape
    starts, counts, nj = _piece_layout(views, tn)
    n_views = len(views)

    def body(a_ref, *refs):
        b_refs, o_ref = refs[:n_views], refs[n_views]
        j = pl.program_id(0)

        @pl.when(pl.program_id(1) == 0)
        def _():
            o_ref[...] = jnp.zeros_like(o_ref)

        for b_ref, st, nb, (_, p) in zip(b_refs, starts, counts, views):
            @pl.when((j >= st) & (j < st + nb))
            def _(b_ref=b_ref, p=p):
                blk = b_ref[...] if p is None else b_ref[0]
                o_ref[...] += _dot_tn(a_ref[...], blk)

    def b_spec(st, nb, p):
        def rows(j, k):
            return jnp.where((j >= st) & (j < st + nb), k, 0)

        def cols(j):
            return jnp.clip(j - st, 0, nb - 1)

        if p is None:
            return pl.BlockSpec((tt, tn), lambda j, k: (rows(j, k), cols(j)))
        return pl.BlockSpec((1, tt, tn), lambda j, k: (p, rows(j, k), cols(j)))

    return _call(
        body, name=name, grid=(nj, t // tt),
        in_specs=[pl.BlockSpec((tt, m), lambda j, k: (k, 0))]
        + [b_spec(st, nb, p) for st, nb, (_, p) in zip(starts, counts, views)],
        out_specs=pl.BlockSpec((m, tn), lambda j, k: (0, j)),
        out_shape=_sds((m, nj * tn), F32), vmem_mb=vmem_mb)(a, *[arr for arr, _ in views])


def _mm_nt(views, w, add, *, add_scale, tk, name, tm=512, vmem_mb=32):
    t = views[0][0].shape[-2]
    kout = w.shape[0]
    starts, counts, nk = _piece_layout(views, tk)
    n_views = len(views)

    def body(add_ref, *refs):
        b_refs, w_ref, o_ref = refs[:n_views], refs[n_views], refs[n_views + 1]
        k = pl.program_id(1)

        @pl.when(k == 0)
        def _():
            o_ref[...] = add_scale * add_ref[...]

        for b_ref, st, nb, (_, p) in zip(b_refs, starts, counts, views):
            @pl.when((k >= st) & (k < st + nb))
            def _(b_ref=b_ref, p=p):
                blk = b_ref[...] if p is None else b_ref[0]
                o_ref[...] += _dot_nt(blk, w_ref[...])

    def b_spec(st, nb, p):
        def cols(k):
            return jnp.clip(k - st, 0, nb - 1)

        if p is None:
            return pl.BlockSpec((tm, tk), lambda i, k: (i, cols(k)))
        return pl.BlockSpec((1, tm, tk), lambda i, k: (p, i, cols(k)))

    return _call(
        body, name=name, grid=(t // tm, nk),
        in_specs=[pl.BlockSpec((tm, kout), lambda i, k: (i, 0))]
        + [b_spec(st, nb, p) for st, nb, (_, p) in zip(starts, counts, views)]
        + [pl.BlockSpec((kout, tk), lambda i, k: (0, k))],
        out_specs=pl.BlockSpec((tm, kout), lambda i, k: (i, 0)),
        out_shape=_sds((t, kout), F32), vmem_mb=vmem_mb)(add, *[arr for arr, _ in views], w)


def _conv_fwd(proj, wcd, bcd, lng, lnb):
    t = proj.shape[0]
    tm = 512

    def body(cv_ref, cg_ref, w_ref, b_ref, g_ref, be_ref, cc_ref, cs_ref, ext_ref):
        i = pl.program_id(0)

        @pl.when(i == 0)
        def _():
            ext_ref[0:CONV_HALO, :] = jnp.zeros((CONV_HALO, CONV_DIM), F32)

        @pl.when(i > 0)
        def _():
            ext_ref[0:CONV_HALO, :] = ext_ref[tm:tm + CONV_HALO, :]

        ext_ref[CONV_HALO:CONV_HALO + tm, :] = cv_ref[...] * _sigmoid(cg_ref[...])

        def block(r, carry):
            r0 = pl.multiple_of(r * ROW_BLOCK, ROW_BLOCK)
            win = ext_ref[pl.ds(r0, ROW_BLOCK + CONV_HALO), :]
            acc = jnp.broadcast_to(b_ref[...], (ROW_BLOCK, CONV_DIM))
            for k in range(CONV_KERNEL):
                acc = acc + w_ref[k:k + 1, :] * win[2 + k:2 + k + ROW_BLOCK, :]
            cc_ref[pl.ds(r0, ROW_BLOCK), :] = acc
            xhat, _ = _ln_stats(acc)
            a = xhat * g_ref[...] + be_ref[...]
            cs_ref[pl.ds(r0, ROW_BLOCK), :] = (a * _sigmoid(a)).astype(BF16)
            return carry

        lax.fori_loop(0, tm // ROW_BLOCK, block, 0)

    vec = pl.BlockSpec((1, CONV_DIM), lambda i: (0, 0))
    return _call(
        body, name="conv_fwd", grid=(t // tm,),
        in_specs=[pl.BlockSpec((tm, CONV_DIM), lambda i: (i, 0)), pl.BlockSpec((tm, CONV_DIM), lambda i: (i, 1)),
                  pl.BlockSpec((CONV_KERNEL, CONV_DIM), lambda i: (0, 0)), vec, vec, vec],
        out_specs=[pl.BlockSpec((tm, CONV_DIM), lambda i: (i, 0)), pl.BlockSpec((tm, CONV_DIM), lambda i: (i, 0))],
        out_shape=[_sds((t, CONV_DIM), F32), _sds((t, CONV_DIM), BF16)],
        scratch=[pltpu.VMEM((tm + CONV_HALO, CONV_DIM), F32)])(proj, proj, wcd, bcd, lng, lnb)


def _conv_bwd_a(dcs, cc, proj, lng, lnb):
    t = proj.shape[0]
    tm = 512
    nt = t // tm

    def body(dcs_ref, cc_ref, cv_ref, cg_ref, g_ref, be_ref,
             dcc_ref, dw_ref, db_ref, dg_ref, dbe_ref, ext_ref, accw_ref, acc3_ref):
        i = pl.program_id(0)

        @pl.when(i == 0)
        def _():
            ext_ref[0:CONV_HALO, :] = jnp.zeros((CONV_HALO, CONV_DIM), F32)
            accw_ref[...] = jnp.zeros_like(accw_ref)
            acc3_ref[...] = jnp.zeros_like(acc3_ref)

        @pl.when(i > 0)
        def _():
            ext_ref[0:CONV_HALO, :] = ext_ref[tm:tm + CONV_HALO, :]

        ext_ref[CONV_HALO:CONV_HALO + tm, :] = cv_ref[...] * _sigmoid(cg_ref[...])

        def block(r, carry):
            r0 = pl.multiple_of(r * ROW_BLOCK, ROW_BLOCK)
            rows = pl.ds(r0, ROW_BLOCK)
            xhat, rstd = _ln_stats(cc_ref[rows, :])
            a = xhat * g_ref[...] + be_ref[...]
            sg = _sigmoid(a)
            da = dcs_ref[rows, :] * _silu_grad(a, sg)
            acc3_ref[8:16, :] += _fold8(da * xhat)
            acc3_ref[16:24, :] += _fold8(da)
            dcc = _ln_bwd(da, xhat, rstd, g_ref[...])
            dcc_ref[rows, :] = dcc
            acc3_ref[0:8, :] += _fold8(dcc)
            win = ext_ref[pl.ds(r0, ROW_BLOCK + CONV_HALO), :]
            for k in range(CONV_KERNEL):
                accw_ref[8 * k:8 * k + 8, :] += _fold8(dcc * win[2 + k:2 + k + ROW_BLOCK, :])
            return carry

        lax.fori_loop(0, tm // ROW_BLOCK, block, 0)

        @pl.when(i == nt - 1)
        def _():
            for k in range(CONV_KERNEL):
                dw_ref[k:k + 1, :] = jnp.sum(accw_ref[8 * k:8 * k + 8, :], axis=0, keepdims=True)
            db_ref[...] = jnp.sum(acc3_ref[0:8, :], axis=0, keepdims=True)
            dg_ref[...] = jnp.sum(acc3_ref[8:16, :], axis=0, keepdims=True)
            dbe_ref[...] = jnp.sum(acc3_ref[16:24, :], axis=0, keepdims=True)

    vec = pl.BlockSpec((1, CONV_DIM), lambda i: (0, 0))
    tile = pl.BlockSpec((tm, CONV_DIM), lambda i: (i, 0))
    return _call(
        body, name="conv_bwd_a", grid=(nt,),
        in_specs=[tile, tile, tile, pl.BlockSpec((tm, CONV_DIM), lambda i: (i, 1)), vec, vec],
        out_specs=[tile, pl.BlockSpec((CONV_KERNEL, CONV_DIM), lambda i: (0, 0)), vec, vec, vec],
        out_shape=[_sds((t, CONV_DIM), F32), _sds((CONV_KERNEL, CONV_DIM), F32),
                   _sds((1, CONV_DIM), F32), _sds((1, CONV_DIM), F32), _sds((1, CONV_DIM), F32)],
        scratch=[pltpu.VMEM((tm + CONV_HALO, CONV_DIM), F32),
                 pltpu.VMEM((8 * CONV_KERNEL, CONV_DIM), F32),
                 pltpu.VMEM((24, CONV_DIM), F32)])(dcs, cc, proj, proj, lng, lnb)


def _conv_bwd_b(dcc, proj, wcd):
    t = proj.shape[0]
    tm = 512
    nt = t // tm

    def body(dcc_ref, cv_ref, cg_ref, w_ref, out_ref, ext_ref):
        i = pl.program_id(0)

        @pl.when(i == 0)
        def _():
            ext_ref[tm:tm + CONV_HALO, :] = jnp.zeros((CONV_HALO, CONV_DIM), F32)

        @pl.when(i > 0)
        def _():
            ext_ref[tm:tm + CONV_HALO, :] = ext_ref[0:CONV_HALO, :]

        ext_ref[0:tm, :] = dcc_ref[...]

        def block(r, carry):
            r0 = pl.multiple_of(r * ROW_BLOCK, ROW_BLOCK)
            rows = pl.ds(r0, ROW_BLOCK)
            win = ext_ref[pl.ds(r0, ROW_BLOCK + CONV_HALO), :]
            acc = jnp.zeros((ROW_BLOCK, CONV_DIM), F32)
            for k in range(CONV_KERNEL):
                off = CONV_KERNEL - 1 - k
                acc = acc + w_ref[k:k + 1, :] * win[off:off + ROW_BLOCK, :]
            sg = _sigmoid(cg_ref[rows, :])
            out_ref[rows, 0:CONV_DIM] = (acc * sg).astype(BF16)
            out_ref[rows, CONV_DIM:2 * CONV_DIM] = (acc * cv_ref[rows, :] * sg * (1.0 - sg)).astype(BF16)
            return carry

        lax.fori_loop(0, tm // ROW_BLOCK, block, 0)

    rev = lambda i: (nt - 1 - i, 0)
    return _call(
        body, name="conv_bwd_b", grid=(nt,),
        in_specs=[pl.BlockSpec((tm, CONV_DIM), rev), pl.BlockSpec((tm, CONV_DIM), rev),
                  pl.BlockSpec((tm, CONV_DIM), lambda i: (nt - 1 - i, 1)),
                  pl.BlockSpec((CONV_KERNEL, CONV_DIM), lambda i: (0, 0))],
        out_specs=pl.BlockSpec((tm, 2 * CONV_DIM), rev),
        out_shape=_sds((t, 2 * CONV_DIM), BF16),
        scratch=[pltpu.VMEM((tm + CONV_HALO, CONV_DIM), F32)])(dcc, proj, proj, wcd)


def _lower_bound(lg_ref):
    a0, a1 = lg_ref[0:1, :], lg_ref[1:2, :]
    m = jnp.maximum(a0, a1)
    e0, e1 = jnp.exp(a0 - m), jnp.exp(a1 - m)
    return e0 / (e0 + e1)


def _block_tri(n, upper):
    r = lax.broadcasted_iota(jnp.int32, (n, n), 0)
    c = lax.broadcasted_iota(jnp.int32, (n, n), 1)
    same = (r >> 6) == (c >> 6)
    tri = (c >= r) if upper else (c <= r)
    return jnp.where(same & tri, 1.0, 0.0).astype(BF16)


def _block_cumsum(x, upper):
    n, w = x.shape
    hi = x.astype(BF16)
    r1 = x - hi.astype(F32)
    mid = r1.astype(BF16)
    lo = (r1 - mid.astype(F32)).astype(BF16)
    y = _dot(_block_tri(n, upper), jnp.concatenate([hi, mid, lo], axis=1))
    return y[:, 0:w] + y[:, w:2 * w] + y[:, 2 * w:3 * w]


def _pick_row(x, row_ids, r):
    return jnp.sum(jnp.where(row_ids == r, x, 0.0), axis=0, keepdims=True)


def _chunk_terms(qc, kc, bc):
    row = lax.broadcasted_iota(jnp.int32, (CHUNK, 1), 0)
    blk = row >> 4
    betas = [jnp.zeros((1, HEAD_DIM), F32)] + [_pick_row(bc, row, SUB * i - 1) for i in range(1, N_SUB)]
    brow = jnp.zeros_like(bc)
    for i in range(1, N_SUB):
        brow = jnp.where(blk == i, betas[i], brow)
    qscale = jnp.exp(bc - brow)
    qs = qc * qscale
    qcat = jnp.concatenate([jnp.where(blk == i, qs, 0.0) for i in range(N_SUB)], axis=1)
    kscales = []
    for i in range(N_SUB):
        valid = row < SUB * (i + 1)
        kscales.append(jnp.where(valid, jnp.exp(jnp.where(valid, betas[i] - bc, 0.0)), 0.0))
    kcat = jnp.concatenate([kc * ks for ks in kscales], axis=1)
    b_last = _pick_row(bc, row, CHUNK - 1)
    return dict(row=row, blk=blk, qscale=qscale, qcat=qcat, kscales=kscales, kcat=kcat,
                eb=jnp.exp(bc), e_last=jnp.exp(b_last), ktscale=jnp.exp(b_last - bc))


def _causal(shape_rows_first):
    r = lax.broadcasted_iota(jnp.int32, (CHUNK, CHUNK), 0)
    c = lax.broadcasted_iota(jnp.int32, (CHUNK, CHUNK), 1)
    return (c <= r) if shape_rows_first else (r <= c)


def _hgrn_specs(tm, tile_of):
    col = lambda base: (lambda h, i: (tile_of(i), base + h))
    return [pl.BlockSpec((tm, HEAD_DIM), col(8)), pl.BlockSpec((tm, HEAD_DIM), col(16)),
            pl.BlockSpec((tm, HEAD_DIM), col(24)), pl.BlockSpec((tm, HEAD_DIM), col(32)),
            pl.BlockSpec((2, HEAD_DIM), lambda h, i: (0, h)), pl.BlockSpec((1, HEAD_DIM), lambda h, i: (0, h))]


def _hgrn_fwd(proj, logits, ng):
    t = proj.shape[0]
    tm = 512
    nc = tm // CHUNK
    nt = t // tm

    def body(zq_ref, zf_ref, v_ref, zg_ref, lg_ref, ng_ref, o_ref, og_ref, st_ref,
             s_scr, q_scr, k_scr, b_scr):
        @pl.when(pl.program_id(1) == 0)
        def _():
            s_scr[...] = jnp.zeros_like(s_scr)

        lb = _lower_bound(lg_ref)
        zf = zf_ref[...]
        f = lb + (1.0 - lb) * _sigmoid(zf)
        k_scr[...] = (1.0 - lb) * _sigmoid(-zf)
        zq = zq_ref[...]
        q_scr[...] = zq * _sigmoid(zq)
        b_scr[...] = _block_cumsum(jnp.log(f), upper=False)

        def chunk(c, carry):
            rows = pl.ds(pl.multiple_of(c * CHUNK, CHUNK), CHUNK)
            qc, kc, bc, vc = q_scr[rows, :], k_scr[rows, :], b_scr[rows, :], v_ref[rows, :]
            st = s_scr[...]
            st_ref[0, c] = st
            tr = _chunk_terms(qc, kc, bc)
            a = jnp.where(_causal(True), _dot_nt(tr["qcat"].astype(BF16), tr["kcat"].astype(BF16)), 0.0)
            vb = vc.astype(BF16)
            o_ref[rows, :] = _dot(a.astype(BF16), vb) + _dot_nt((qc * tr["eb"]).astype(BF16), st.astype(BF16))
            s_scr[...] = st * tr["e_last"] + _dot_tn(vb, (kc * tr["ktscale"]).astype(BF16))
            return carry

        lax.fori_loop(0, nc, chunk, 0)

        o = o_ref[...]
        rinv = lax.rsqrt(jnp.mean(o * o, axis=-1, keepdims=True) + RMS_EPS)
        zg = zg_ref[...]
        og_ref[...] = (o * rinv * ng_ref[...] * (zg * _sigmoid(zg))).astype(BF16)

    tile = pl.BlockSpec((tm, HEAD_DIM), lambda h, i: (i, h))
    return _call(
        body, name="hgrn_fwd", grid=(HGRN_HEADS, nt),
        in_specs=_hgrn_specs(tm, lambda i: i),
        out_specs=[tile, tile, pl.BlockSpec((1, nc, HEAD_DIM, HEAD_DIM), lambda h, i: (h, i, 0, 0))],
        out_shape=[_sds((t, HGRN_DIM), F32), _sds((t, HGRN_DIM), BF16),
                   _sds((HGRN_HEADS, t // CHUNK, HEAD_DIM, HEAD_DIM), F32)],
        scratch=[pltpu.VMEM((HEAD_DIM, HEAD_DIM), F32)] + [pltpu.VMEM((tm, HEAD_DIM), F32)] * 3,
    )(proj, proj, proj, proj, logits, ng)


def _hgrn_bwd(dog, o, states, proj, logits, ng):
    t = proj.shape[0]
    tm = 512
    nc = tm // CHUNK
    nt = t // tm

    def body(dog_ref, o_ref, st_ref, zq_ref, zf_ref, v_ref, zg_ref, lg_ref, ng_ref,
             dp_ref, dlg_ref, dng_ref,
             ds_scr, q_scr, k_scr, b_scr, do_scr, dq_scr, dk_scr, dv_scr, db_scr, dlb_scr):
        i = pl.program_id(1)

        @pl.when(i == 0)
        def _():
            ds_scr[...] = jnp.zeros_like(ds_scr)
            dlb_scr[...] = jnp.zeros_like(dlb_scr)
            dng_ref[...] = jnp.zeros_like(dng_ref)

        lb = _lower_bound(lg_ref)
        ng_row = ng_ref[...]
        o = o_ref[...]
        rinv = lax.rsqrt(jnp.mean(o * o, axis=-1, keepdims=True) + RMS_EPS)
        ohat = o * rinv
        zg = zg_ref[...]
        sg = _sigmoid(zg)
        dog_v = dog_ref[...]
        don = dog_v * (zg * sg)
        dp_ref[3] = (dog_v * (ohat * ng_row) * _silu_grad(zg, sg)).astype(BF16)
        dng_ref[...] += jnp.sum(don * ohat, axis=0, keepdims=True)
        dohat = don * ng_row
        do_scr[...] = rinv * (dohat - ohat * jnp.mean(dohat * ohat, axis=-1, keepdims=True))

        zf = zf_ref[...]
        s = _sigmoid(zf)
        s_neg = _sigmoid(-zf)
        f = lb + (1.0 - lb) * s
        k_scr[...] = (1.0 - lb) * s_neg
        zq = zq_ref[...]
        sq = _sigmoid(zq)
        q_scr[...] = zq * sq
        b_scr[...] = _block_cumsum(jnp.log(f), upper=False)

        def chunk(cc, carry):
            c = nc - 1 - cc
            rows = pl.ds(pl.multiple_of(c * CHUNK, CHUNK), CHUNK)
            qc, kc, bc, vc, doc = q_scr[rows, :], k_scr[rows, :], b_scr[rows, :], v_ref[rows, :], do_scr[rows, :]
            st = st_ref[0, c]
            dst = ds_scr[...]
            tr = _chunk_terms(qc, kc, bc)
            qcb, kcb = tr["qcat"].astype(BF16), tr["kcat"].astype(BF16)
            dob, vb, dstb = doc.astype(BF16), vc.astype(BF16), dst.astype(BF16)
            a_t = jnp.where(_causal(False), _dot_nt(kcb, qcb), 0.0)
            da = jnp.where(_causal(True), _dot_nt(dob, vb), 0.0)
            da_t = jnp.where(_causal(False), _dot_nt(vb, dob), 0.0)
            dqcat = _dot(da.astype(BF16), kcb)
            dkcat = _dot(da_t.astype(BF16), qcb)
            kt = kc * tr["ktscale"]
            dv_scr[rows, :] = _dot(a_t.astype(BF16), dob) + _dot_nt(kt.astype(BF16), dstb)
            dq = jnp.zeros_like(qc)
            dk = jnp.zeros_like(kc)
            db = jnp.zeros_like(bc)
            for n in range(N_SUB):
                lanes = slice(n * HEAD_DIM, (n + 1) * HEAD_DIM)
                dq = dq + jnp.where(tr["blk"] == n, dqcat[:, lanes], 0.0)
                dk = dk + dkcat[:, lanes] * tr["kscales"][n]
                db = db + (qcb[:, lanes].astype(F32) * dqcat[:, lanes] - kcb[:, lanes].astype(F32) * dkcat[:, lanes])
            dq_inter = _dot(dob, st.astype(BF16)) * tr["eb"]
            dkt = _dot(vb, dstb)
            dk_inter = dkt * tr["ktscale"]
            extra = (jnp.sum(dkt * kt, axis=0, keepdims=True)
                     + tr["e_last"] * jnp.sum(dst * st, axis=0, keepdims=True))
            dq_scr[rows, :] = dq * tr["qscale"] + dq_inter
            dk_scr[rows, :] = dk + dk_inter
            db_scr[rows, :] = (db + qc * dq_inter - kc * dk_inter
                               + jnp.where(tr["row"] == CHUNK - 1, extra, 0.0))
            ds_scr[...] = dst * tr["e_last"] + _dot_tn(dob, (qc * tr["eb"]).astype(BF16))
            return carry

        lax.fori_loop(0, nc, chunk, 0)

        dlogf = _block_cumsum(db_scr[...], upper=True)
        df = dlogf / f - dk_scr[...]
        dp_ref[0] = (dq_scr[...] * _silu_grad(zq, sq)).astype(BF16)
        dp_ref[1] = (df * (1.0 - lb) * s * (1.0 - s)).astype(BF16)
        dp_ref[2] = dv_scr[...].astype(BF16)
        dlb_scr[...] += jnp.sum(df * s_neg, axis=0, keepdims=True)

        @pl.when(i == nt - 1)
        def _():
            dlogit = dlb_scr[...] * lb * (1.0 - lb)
            dlg_ref[0:1, :] = dlogit
            dlg_ref[1:2, :] = -dlogit

    rev = lambda i: nt - 1 - i
    tile = pl.BlockSpec((tm, HEAD_DIM), lambda h, i: (rev(i), h))
    return _call(
        body, name="hgrn_bwd", grid=(HGRN_HEADS, nt),
        in_specs=[tile, tile, pl.BlockSpec((1, nc, HEAD_DIM, HEAD_DIM), lambda h, i: (h, rev(i), 0, 0))]
        + _hgrn_specs(tm, rev),
        out_specs=[pl.BlockSpec((4, tm, HEAD_DIM), lambda h, i: (0, rev(i), h)),
                   pl.BlockSpec((2, HEAD_DIM), lambda h, i: (0, h)),
                   pl.BlockSpec((1, HEAD_DIM), lambda h, i: (0, h))],
        out_shape=[_sds((4, t, HGRN_DIM), BF16), _sds((2, HGRN_DIM), F32), _sds((1, HGRN_DIM), F32)],
        scratch=[pltpu.VMEM((HEAD_DIM, HEAD_DIM), F32)] + [pltpu.VMEM((tm, HEAD_DIM), F32)] * 8
        + [pltpu.VMEM((1, HEAD_DIM), F32)],
    )(dog, o, states, proj, proj, proj, proj, logits, ng)


def _merge_fwd(cs, og, proj, x, wco, wh, wo, g1, b1):
    t = x.shape[0]
    tm = 256

    def body(cs_ref, og_ref, m0_ref, m1_ref, x_ref, wco_ref, wh_ref, wo_ref, g_ref, b_ref,
             y_ref, mixed_ref, r1_ref, x1_ref, x1b_ref):
        yc = _dot(cs_ref[...], wco_ref[...])
        yh = _dot(og_ref[...], wh_ref[...])
        y_ref[0] = yc
        y_ref[1] = yh
        mixed = (_sigmoid(m0_ref[...]) * yc + _sigmoid(m1_ref[...]) * yh).astype(BF16)
        mixed_ref[...] = mixed
        r1 = ALPHA * x_ref[...] + _dot(mixed, wo_ref[...])
        r1_ref[...] = r1
        xhat, _ = _ln_stats(r1)
        x1 = xhat * g_ref[...] + b_ref[...]
        x1_ref[...] = x1
        x1b_ref[...] = x1.astype(BF16)

    row = lambda w: pl.BlockSpec((tm, w), lambda i: (i, 0))
    full = lambda a: pl.BlockSpec(a.shape, lambda i: (0, 0))
    return _call(
        body, name="merge_fwd", grid=(t // tm,),
        in_specs=[row(CONV_DIM), row(HGRN_DIM),
                  pl.BlockSpec((tm, D_MODEL), lambda i: (i, 5)), pl.BlockSpec((tm, D_MODEL), lambda i: (i, 6)),
                  row(D_MODEL), full(wco), full(wh), full(wo), full(g1), full(b1)],
        out_specs=[pl.BlockSpec((2, tm, D_MODEL), lambda i: (0, i, 0)), row(D_MODEL), row(D_MODEL),
                   row(D_MODEL), row(D_MODEL)],
        out_shape=[_sds((2, t, D_MODEL), F32), _sds((t, D_MODEL), BF16), _sds((t, D_MODEL), F32),
                   _sds((t, D_MODEL), F32), _sds((t, D_MODEL), BF16)],
        vmem_mb=48)(cs, og, proj, proj, x, wco, wh, wo, g1, b1)


def _merge_bwd(dr1b, ycat, proj, wo, wco, wh):
    t = dr1b.shape[0]
    tm = 256

    def body(dr_ref, y_ref, m0_ref, m1_ref, wo_ref, wco_ref, wh_ref, dpm_ref, dy_ref, dcs_ref, dog_ref):
        dmixed = _dot_nt(dr_ref[...], wo_ref[...])
        g0 = _sigmoid(m0_ref[...])
        g1 = _sigmoid(m1_ref[...])
        dpm_ref[0] = (dmixed * y_ref[0] * g0 * (1.0 - g0)).astype(BF16)
        dpm_ref[1] = (dmixed * y_ref[1] * g1 * (1.0 - g1)).astype(BF16)
        dyc = (dmixed * g0).astype(BF16)
        dyh = (dmixed * g1).astype(BF16)
        dy_ref[0] = dyc
        dy_ref[1] = dyh
        dcs_ref[...] = _dot_nt(dyc, wco_ref[...])
        dog_ref[...] = _dot_nt(dyh, wh_ref[...])

    row = lambda w: pl.BlockSpec((tm, w), lambda i: (i, 0))
    pair = pl.BlockSpec((2, tm, D_MODEL), lambda i: (0, i, 0))
    full = lambda a: pl.BlockSpec(a.shape, lambda i: (0, 0))
    return _call(
        body, name="merge_bwd", grid=(t // tm,),
        in_specs=[row(D_MODEL), pair,
                  pl.BlockSpec((tm, D_MODEL), lambda i: (i, 5)), pl.BlockSpec((tm, D_MODEL), lambda i: (i, 6)),
                  full(wo), full(wco), full(wh)],
        out_specs=[pair, pair, row(CONV_DIM), row(HGRN_DIM)],
        out_shape=[_sds((2, t, D_MODEL), BF16), _sds((2, t, D_MODEL), BF16),
                   _sds((t, CONV_DIM), F32), _sds((t, HGRN_DIM), F32)],
        vmem_mb=48)(dr1b, ycat, proj, proj, wo, wco, wh)


def _ffn_conv3(win, w_ref, off):
    return (w_ref[0:1, :] * win[off:off + ROW_BLOCK, :] + w_ref[1:2, :] * win[off + 1:off + 1 + ROW_BLOCK, :]
            + w_ref[2:3, :] * win[off + 2:off + 2 + ROW_BLOCK, :])


def _ffn_mid(z, wfd, bfd):
    t = z.shape[0]
    tm = 256

    def body(u_ref, gv_ref, w_ref, b_ref, h_ref, ext_ref):
        i = pl.program_id(0)

        @pl.when(i == 0)
        def _():
            ext_ref[0:FFN_HALO, :] = jnp.zeros((FFN_HALO, D_FF), F32)

        @pl.when(i > 0)
        def _():
            ext_ref[0:FFN_HALO, :] = ext_ref[tm:tm + FFN_HALO, :]

        ext_ref[FFN_HALO:FFN_HALO + tm, :] = u_ref[...]

        def block(r, carry):
            r0 = pl.multiple_of(r * ROW_BLOCK, ROW_BLOCK)
            rows = pl.ds(r0, ROW_BLOCK)
            win = ext_ref[pl.ds(r0, ROW_BLOCK + FFN_HALO), :]
            uc = _ffn_conv3(win, w_ref, FFN_HALO - 2) + b_ref[...]
            g, _ = _gelu_and_grad(uc)
            h_ref[rows, :] = (g * gv_ref[rows, :]).astype(BF16)
            return carry

        lax.fori_loop(0, tm // ROW_BLOCK, block, 0)

    return _call(
        body, name="ffn_mid", grid=(t // tm,),
        in_specs=[pl.BlockSpec((tm, D_FF), lambda i: (i, 0)), pl.BlockSpec((tm, D_FF), lambda i: (i, 1)),
                  pl.BlockSpec((FFN_KERNEL, D_FF), lambda i: (0, 0)), pl.BlockSpec((1, D_FF), lambda i: (0, 0))],
        out_specs=pl.BlockSpec((tm, D_FF), lambda i: (i, 0)),
        out_shape=_sds((t, D_FF), BF16),
        scratch=[pltpu.VMEM((tm + FFN_HALO, D_FF), F32)], vmem_mb=40)(z, z, wfd, bfd)


def _ffn_out_loss(hmid, x1, target, wfo, g2, b2):
    t = x1.shape[0]
    tm = 256
    inv_n = 1.0 / D_MODEL

    def body(h_ref, x1_ref, tg_ref, w_ref, g_ref, b_ref, dr_ref, drb_ref, loss_ref, dg_ref, db_ref):
        @pl.when(pl.program_id(0) == 0)
        def _():
            loss_ref[...] = jnp.zeros_like(loss_ref)
            dg_ref[...] = jnp.zeros_like(dg_ref)
            db_ref[...] = jnp.zeros_like(db_ref)

        r2 = ALPHA * x1_ref[...] + _dot(h_ref[...], w_ref[...])
        xhat, rstd = _ln_stats(r2)
        err = xhat * g_ref[...] + b_ref[...] - tg_ref[...]
        loss_ref[...] += 0.5 * inv_n * jnp.sum(err * err)
        dy = err * inv_n
        dg_ref[...] += jnp.sum(dy * xhat, axis=0, keepdims=True)
        db_ref[...] += jnp.sum(dy, axis=0, keepdims=True)
        dr = _ln_bwd(dy, xhat, rstd, g_ref[...])
        dr_ref[...] = dr
        drb_ref[...] = dr.astype(BF16)

    row = lambda w: pl.BlockSpec((tm, w), lambda i: (i, 0))
    vec = pl.BlockSpec((1, D_MODEL), lambda i: (0, 0))
    return _call(
        body, name="ffn_out_loss", grid=(t // tm,),
        in_specs=[row(D_FF), row(D_MODEL), row(D_MODEL), pl.BlockSpec((D_FF, D_MODEL), lambda i: (0, 0)), vec, vec],
        out_specs=[row(D_MODEL), row(D_MODEL), pl.BlockSpec((1, 128), lambda i: (0, 0)), vec, vec],
        out_shape=[_sds((t, D_MODEL), F32), _sds((t, D_MODEL), BF16), _sds((1, 128), F32),
                   _sds((1, D_MODEL), F32), _sds((1, D_MODEL), F32)],
        vmem_mb=40)(hmid, x1, target, wfo, g2, b2)


def _ffn_bwd_a(dr2b, z, wfo, wfd, bfd):
    t = z.shape[0]
    tm = 256
    nt = t // tm

    def body(dr_ref, u_ref, gv_ref, wfo_ref, w_ref, b_ref, dgv_ref, duc_ref, dw_ref, db_ref,
             ext_ref, dh_ref, acc_ref):
        i = pl.program_id(0)

        @pl.when(i == 0)
        def _():
            ext_ref[0:FFN_HALO, :] = jnp.zeros((FFN_HALO, D_FF), F32)
            acc_ref[...] = jnp.zeros_like(acc_ref)

        @pl.when(i > 0)
        def _():
            ext_ref[0:FFN_HALO, :] = ext_ref[tm:tm + FFN_HALO, :]

        ext_ref[FFN_HALO:FFN_HALO + tm, :] = u_ref[...]
        dh_ref[...] = _dot_nt(dr_ref[...], wfo_ref[...])

        def block(r, carry):
            r0 = pl.multiple_of(r * ROW_BLOCK, ROW_BLOCK)
            rows = pl.ds(r0, ROW_BLOCK)
            win = ext_ref[pl.ds(r0, ROW_BLOCK + FFN_HALO), :]
            off = FFN_HALO - 2
            uc = _ffn_conv3(win, w_ref, off) + b_ref[...]
            g, dg = _gelu_and_grad(uc)
            dh = dh_ref[rows, :]
            dgv_ref[rows, :] = (dh * g).astype(BF16)
            duc = dh * gv_ref[rows, :] * dg
            duc_ref[rows, :] = duc
            acc_ref[0:8, :] += _fold8(duc)
            for k in range(FFN_KERNEL):
                acc_ref[8 + 8 * k:16 + 8 * k, :] += _fold8(duc * win[off + k:off + k + ROW_BLOCK, :])
            return carry

        lax.fori_loop(0, tm // ROW_BLOCK, block, 0)

        @pl.when(i == nt - 1)
        def _():
            db_ref[...] = jnp.sum(acc_ref[0:8, :], axis=0, keepdims=True)
            for k in range(FFN_KERNEL):
                dw_ref[k:k + 1, :] = jnp.sum(acc_ref[8 + 8 * k:16 + 8 * k, :], axis=0, keepdims=True)

    tile = pl.BlockSpec((tm, D_FF), lambda i: (i, 0))
    return _call(
        body, name="ffn_bwd_a", grid=(nt,),
        in_specs=[pl.BlockSpec((tm, D_MODEL), lambda i: (i, 0)), tile, pl.BlockSpec((tm, D_FF), lambda i: (i, 1)),
                  pl.BlockSpec((D_FF, D_MODEL), lambda i: (0, 0)),
                  pl.BlockSpec((FFN_KERNEL, D_FF), lambda i: (0, 0)), pl.BlockSpec((1, D_FF), lambda i: (0, 0))],
        out_specs=[tile, tile, pl.BlockSpec((FFN_KERNEL, D_FF), lambda i: (0, 0)),
                   pl.BlockSpec((1, D_FF), lambda i: (0, 0))],
        out_shape=[_sds((t, D_FF), BF16), _sds((t, D_FF), F32), _sds((FFN_KERNEL, D_FF), F32), _sds((1, D_FF), F32)],
        scratch=[pltpu.VMEM((tm + FFN_HALO, D_FF), F32), pltpu.VMEM((tm, D_FF), F32),
                 pltpu.VMEM((8 + 8 * FFN_KERNEL, D_FF), F32)],
        vmem_mb=56)(dr2b, z, z, wfo, wfd, bfd)


def _ffn_bwd_b(duc, wfd):
    t = duc.shape[0]
    tm = 256
    nt = t // tm

    def body(duc_ref, w_ref, du_ref, ext_ref):
        i = pl.program_id(0)

        @pl.when(i == 0)
        def _():
            ext_ref[tm:tm + FFN_HALO, :] = jnp.zeros((FFN_HALO, D_FF), F32)

        @pl.when(i > 0)
        def _():
            ext_ref[tm:tm + FFN_HALO, :] = ext_ref[0:FFN_HALO, :]

        ext_ref[0:tm, :] = duc_ref[...]

        def block(r, carry):
            r0 = pl.multiple_of(r * ROW_BLOCK, ROW_BLOCK)
            win = ext_ref[pl.ds(r0, ROW_BLOCK + FFN_HALO), :]
            du = (w_ref[2:3, :] * win[0:ROW_BLOCK, :] + w_ref[1:2, :] * win[1:1 + ROW_BLOCK, :]
                  + w_ref[0:1, :] * win[2:2 + ROW_BLOCK, :])
            du_ref[pl.ds(r0, ROW_BLOCK), :] = du.astype(BF16)
            return carry

        lax.fori_loop(0, tm // ROW_BLOCK, block, 0)

    rev = lambda i: (nt - 1 - i, 0)
    return _call(
        body, name="ffn_bwd_b", grid=(nt,),
        in_specs=[pl.BlockSpec((tm, D_FF), rev), pl.BlockSpec((FFN_KERNEL, D_FF), lambda i: (0, 0))],
        out_specs=pl.BlockSpec((tm, D_FF), rev),
        out_shape=_sds((t, D_FF), BF16),
        scratch=[pltpu.VMEM((tm + FFN_HALO, D_FF), F32)], vmem_mb=40)(duc, wfd)


def _ffn_in_bwd(dr2, dub, dgvb, wfi, r1, g1):
    t = dr2.shape[0]
    tm = 256

    def body(dr2_ref, du_ref, dgv_ref, wu_ref, wg_ref, r1_ref, g_ref, dr1_ref, dr1b_ref, dg_ref, db_ref):
        @pl.when(pl.program_id(0) == 0)
        def _():
            dg_ref[...] = jnp.zeros_like(dg_ref)
            db_ref[...] = jnp.zeros_like(db_ref)

        dx1 = ALPHA * dr2_ref[...] + _dot_nt(du_ref[...], wu_ref[...]) + _dot_nt(dgv_ref[...], wg_ref[...])
        xhat, rstd = _ln_stats(r1_ref[...])
        dg_ref[...] += jnp.sum(dx1 * xhat, axis=0, keepdims=True)
        db_ref[...] += jnp.sum(dx1, axis=0, keepdims=True)
        dr1 = _ln_bwd(dx1, xhat, rstd, g_ref[...])
        dr1_ref[...] = dr1
        dr1b_ref[...] = dr1.astype(BF16)

    row = lambda w: pl.BlockSpec((tm, w), lambda i: (i, 0))
    vec = pl.BlockSpec((1, D_MODEL), lambda i: (0, 0))
    return _call(
        body, name="ffn_in_bwd", grid=(t // tm,),
        in_specs=[row(D_MODEL), row(D_FF), row(D_FF),
                  pl.BlockSpec((D_MODEL, D_FF), lambda i: (0, 0)), pl.BlockSpec((D_MODEL, D_FF), lambda i: (0, 1)),
                  row(D_MODEL), vec],
        out_specs=[row(D_MODEL), row(D_MODEL), vec, vec],
        out_shape=[_sds((t, D_MODEL), F32), _sds((t, D_MODEL), BF16), _sds((1, D_MODEL), F32), _sds((1, D_MODEL), F32)],
        vmem_mb=56)(dr2, dub, dgvb, wfi, wfi, r1, g1)


def _local_step(x, target, wi, wco, wh, wo, wfi, wfo, wcd, bcd, clg, clb, logits, ng, g1, b1, wfd, bfd, g2, b2):
    proj, xb = _proj(x, wi)
    cc, cs = _conv_fwd(proj, wcd, bcd, clg, clb)
    o, og, states = _hgrn_fwd(proj, logits, ng)
    ycat, mixed, r1, x1, x1b = _merge_fwd(cs, og, proj, x, wco, wh, wo, g1, b1)
    z = _mm_nn(x1b, wfi, tm=512, tn=1408, name="ffn_in")
    hmid = _ffn_mid(z, wfd, bfd)
    dr2, dr2b, loss, d_g2, d_b2 = _ffn_out_loss(hmid, x1, target, wfo, g2, b2)

    g_wfo = _mm_tn(hmid, _views(dr2b), tn=512, name="grad_w_ffn_out", vmem_mb=48)
    dgvb, duc, d_wfd, d_bfd = _ffn_bwd_a(dr2b, z, wfo, wfd, bfd)
    dub = _ffn_bwd_b(duc, wfd)
    g_wfi = _mm_tn(x1b, _views(dub, dgvb), tn=1408, name="grad_w_ffn_in", vmem_mb=48)
    dr1, dr1b, d_g1, d_b1 = _ffn_in_bwd(dr2, dub, dgvb, wfi, r1, g1)

    g_wo = _mm_tn(mixed, _views(dr1b), tn=512, name="grad_w_out")
    dpm, dyb, dcs, dog = _merge_bwd(dr1b, ycat, proj, wo, wco, wh)
    g_wco = _mm_tn(cs, [(dyb, 0)], tn=512, name="grad_w_conv_out")
    g_wh = _mm_tn(og, [(dyb, 1)], tn=512, name="grad_w_hgrn_out")

    dcc, d_wcd, d_bcd, d_clg, d_clb = _conv_bwd_a(dcs, cc, proj, clg, clb)
    dpc = _conv_bwd_b(dcc, proj, wcd)
    dph, d_logits, d_ng = _hgrn_bwd(dog, o, states, proj, logits, ng)

    pieces = _views(dpc, dph, dpm)
    g_wi = _mm_tn(xb, pieces, tn=512, name="grad_w_in")
    grad_x = _mm_nt(pieces, wi, dr1, add_scale=ALPHA, tk=1024, name="grad_x", vmem_mb=48)

    small = dict(w_conv_dw=d_wcd, b_conv_dw=d_bcd, conv_ln_g=d_clg, conv_ln_b=d_clb, hgrn_lb_logits=d_logits,
                 hgrn_norm_g=d_ng, ln1_g=d_g1, ln1_b=d_b1, w_ffn_dw=d_wfd, b_ffn_dw=d_bfd, ln2_g=d_g2, ln2_b=d_b2)
    return loss, grad_x, (g_wi, g_wco, g_wh, g_wo, g_wfi, g_wfo), small


ELEMENTWISE_BLOCK_ELEMS = 256 * 1024


def _row_tile(rows, cols):
    cap = max(16, ELEMENTWISE_BLOCK_ELEMS // cols)
    if rows <= cap:
        return rows
    best = None
    for cand in range(16, cap + 1, 16):
        if rows % cand == 0:
            best = cand
    assert best is not None
    return best


def _elementwise(fn, ins, out_dtypes, *, name):
    r, c = ins[0].shape
    tr = _row_tile(r, c)

    def body(*refs):
        outs = fn(*[ref[...] for ref in refs[:len(ins)]])
        for ref, val in zip(refs[len(ins):], outs):
            ref[...] = val.astype(ref.dtype)

    spec = pl.BlockSpec((tr, c), lambda i: (i, 0))
    return _call(
        body, name=name, grid=(r // tr,), in_specs=[spec] * len(ins), out_specs=[spec] * len(out_dtypes),
        out_shape=[_sds((r, c), dt) for dt in out_dtypes])(*ins)


def _adamw(w, g, m, v, *, name):
    def fn(w_, g_, m_, v_):
        m_new = ADAM_B1 * m_ + (1.0 - ADAM_B1) * g_
        v_new = ADAM_B2 * v_ + (1.0 - ADAM_B2) * (g_ * g_)
        m_hat = m_new / ADAM_BC1
        v_hat = v_new / ADAM_BC2
        delta = -ADAM_LR * (m_hat / (jnp.sqrt(v_hat) + ADAM_EPS) + ADAM_WD * w_)
        return delta, m_new, v_new

    return _elementwise(fn, [w, g, m, v], [F32, F32, F32], name=name)


def _place():
    return lax.axis_index("x"), lax.axis_index("y"), lax.axis_index("c")


def _other_chips(x, y):
    return [(1 - x, y), (x, 1 - y), (1 - x, 1 - y)]


SHARD_XOR = (2, 1, 3)


def _run_copies(local_ops, remote_ops, lsem, ssem, rsem):
    local = [pltpu.make_async_copy(src, dst, lsem.at[n]) for n, (src, dst) in enumerate(local_ops)]
    remote = [pltpu.make_async_remote_copy(src_ref=src, dst_ref=dst, send_sem=ssem.at[n], recv_sem=rsem.at[n],
                                           device_id=dev, device_id_type=MESH)
              for n, (src, dst, dev) in enumerate(remote_ops)]
    for cp in local + remote:
        cp.start()
    for cp in remote:
        cp.wait_recv()
    for cp in remote:
        cp.wait_send()
    for cp in local:
        cp.wait()


def _comm_call(body, *, name, n_in, out_shape, n_local, n_remote):
    return pl.pallas_call(
        body, name=name, in_specs=[ANY] * n_in, out_specs=[ANY] * len(out_shape), out_shape=out_shape,
        scratch_shapes=[pltpu.SemaphoreType.DMA((max(n_local, 1),)), pltpu.SemaphoreType.DMA((n_remote,)),
                        pltpu.SemaphoreType.DMA((n_remote,))])


BIG = (("w_in", D_MODEL, IN_COLS, 1), ("w_conv_out", CONV_DIM, D_MODEL, 1), ("w_hgrn_out", HGRN_DIM, D_MODEL, 0),
       ("w_out", D_MODEL, D_MODEL, 0), ("w_ffn_in", D_MODEL, 2 * D_FF, 1), ("w_ffn_out", D_FF, D_MODEL, 0))


def _shard_slice(ref, rows, cols, axis, k):
    if axis == 1:
        w = cols // N_CHIPS
        return ref.at[:, pl.ds(k * w, w)]
    h = rows // N_CHIPS
    return ref.at[pl.ds(k * h, h), :]


def _half_slice(ref, rows, cols, axis, hc):
    if axis == 1:
        return ref.at[pl.ds(hc * (rows // 2), rows // 2), :]
    return ref.at[:, pl.ds(hc * (cols // 2), cols // 2)]


def _half_shape(rows, cols, axis):
    return (rows // 2, cols) if axis == 1 else (rows, cols // 2)


def _gather_weights(shards, small):
    n_arr = len(shards) + len(small)
    out_shape = ([_sds((r, c), BF16) for _, r, c, _ in BIG]
                 + [_sds((N_CHIPS,) + a.shape, F32) for a in small])

    def body(*refs):
        ins, outs = refs[:n_arr], refs[n_arr:2 * n_arr]
        lsem, ssem, rsem = refs[2 * n_arr:]
        x, y, c = _place()
        me = 2 * x + y
        for k in range(N_CHIPS):
            @pl.when(me == k)
            def _(k=k):
                dsts = [_shard_slice(o, r, cc, ax, k) for o, (_, r, cc, ax) in zip(outs, BIG)]
                dsts += [o.at[k] for o in outs[len(BIG):]]
                local_ops = list(zip(ins, dsts))
                remote_ops = [(src, dst, (cx, cy, c)) for cx, cy in _other_chips(x, y)
                              for src, dst in zip(ins, dsts)]
                _run_copies(local_ops, remote_ops, lsem, ssem, rsem)

    return _comm_call(body, name="gather_weights", n_in=n_arr, out_shape=out_shape,
                      n_local=n_arr, n_remote=3 * n_arr)(*shards, *small)


def _sibling_exchange(grads):
    n = len(BIG)
    shapes = [_sds(_half_shape(r, c, ax), F32) for _, r, c, ax in BIG]

    def body(*refs):
        ins, kept, got = refs[:n], refs[n:2 * n], refs[2 * n:3 * n]
        lsem, ssem, rsem = refs[3 * n:]
        x, y, c = _place()
        for k in range(2):
            @pl.when(c == k)
            def _(k=k):
                local_ops = [(_half_slice(g, r, cc, ax, k), dst) for g, dst, (_, r, cc, ax) in zip(ins, kept, BIG)]
                remote_ops = [(_half_slice(g, r, cc, ax, 1 - k), dst, (x, y, 1 - c))
                              for g, dst, (_, r, cc, ax) in zip(ins, got, BIG)]
                _run_copies(local_ops, remote_ops, lsem, ssem, rsem)

    outs = _comm_call(body, name="grad_sibling_exchange", n_in=n, out_shape=shapes + shapes,
                      n_local=n, n_remote=n)(*grads)
    return outs[:n], outs[n:]


def _chip_exchange(chip_sums):
    n = len(BIG)
    half = [_half_shape(r, c, ax) for _, r, c, ax in BIG]
    piece = [(hr, hc // N_CHIPS) if ax == 1 else (hr // N_CHIPS, hc) for (hr, hc), (_, _, _, ax) in zip(half, BIG)]
    out_shape = [_sds(p, BF16) for p in piece] + [_sds((3,) + p, BF16) for p in piece]

    def body(*refs):
        ins, own, got = refs[:n], refs[n:2 * n], refs[2 * n:3 * n]
        lsem, ssem, rsem = refs[3 * n:]
        x, y, c = _place()
        me = 2 * x + y
        for k in range(N_CHIPS):
            @pl.when(me == k)
            def _(k=k):
                def sl(ref, idx, s):
                    (hr, hc), ax = half[idx], BIG[idx][3]
                    return _shard_slice(ref, hr, hc, ax, s)

                local_ops = [(sl(g, idx, k), dst) for idx, (g, dst) in enumerate(zip(ins, own))]
                remote_ops = [(sl(g, idx, k ^ SHARD_XOR[j]), dst.at[j], (cx, cy, c))
                              for j, (cx, cy) in enumerate(_other_chips(x, y))
                              for idx, (g, dst) in enumerate(zip(ins, got))]
                _run_copies(local_ops, remote_ops, lsem, ssem, rsem)

    outs = _comm_call(body, name="grad_chip_exchange", n_in=n, out_shape=out_shape,
                      n_local=n, n_remote=3 * n)(*chip_sums)
    return outs[:n], outs[n:]


def _sibling_assemble(totals):
    n = len(BIG)
    shard = [(r, c // N_CHIPS) if ax == 1 else (r // N_CHIPS, c) for _, r, c, ax in BIG]
    out_shape = [_sds(s, F32) for s in shard]

    def body(*refs):
        ins, outs = refs[:n], refs[n:2 * n]
        lsem, ssem, rsem = refs[2 * n:]
        x, y, c = _place()
        for k in range(2):
            @pl.when(c == k)
            def _(k=k):
                dsts = [_half_slice(o, sr, sc, ax, k) for o, (sr, sc), (_, _, _, ax) in zip(outs, shard, BIG)]
                local_ops = list(zip(ins, dsts))
                remote_ops = [(src, dst, (x, y, 1 - c)) for src, dst in zip(ins, dsts)]
                _run_copies(local_ops, remote_ops, lsem, ssem, rsem)

    return _comm_call(body, name="grad_sibling_assemble", n_in=n, out_shape=out_shape,
                      n_local=n, n_remote=n)(*totals)


def _all_reduce_small(packed):
    r, w = packed.shape

    def body(in_ref, out_ref, slots, lsem, ssem, rsem):
        x, y, c = _place()
        me = 4 * x + 2 * y + c
        peers = [(x ^ (m >> 2), y ^ ((m >> 1) & 1), c ^ (m & 1)) for m in range(1, N_DEV)]
        _run_copies([(in_ref, slots.at[me])], [(in_ref, slots.at[me], dev) for dev in peers], lsem, ssem, rsem)
        total = slots[0]
        for d in range(1, N_DEV):
            total = total + slots[d]
        out_ref[...] = total

    vmem = pl.BlockSpec(memory_space=pltpu.VMEM)
    return pl.pallas_call(
        body, name="small_all_reduce", in_specs=[vmem], out_specs=vmem, out_shape=_sds((r, w), F32),
        scratch_shapes=[pltpu.VMEM((N_DEV, r, w), F32), pltpu.SemaphoreType.DMA((1,)),
                        pltpu.SemaphoreType.DMA((N_DEV - 1,)), pltpu.SemaphoreType.DMA((N_DEV - 1,))])(packed)


SMALL_ORDER = ("w_conv_dw", "b_conv_dw", "conv_ln_g", "conv_ln_b", "hgrn_lb_logits", "hgrn_norm_g",
               "ln1_g", "ln1_b", "w_ffn_dw", "b_ffn_dw", "ln2_g", "ln2_b")
REPLICATED_SMALL = tuple(n for n in SMALL_ORDER if n not in ("w_conv_dw", "w_ffn_dw"))
WEIGHT_ORDER = ("w_in", "w_conv_dw", "b_conv_dw", "conv_ln_g", "conv_ln_b", "w_conv_out", "hgrn_lb_logits",
                "hgrn_norm_g", "w_hgrn_out", "w_out", "ln1_g", "ln1_b", "w_ffn_in", "w_ffn_dw", "b_ffn_dw",
                "w_ffn_out", "ln2_g", "ln2_b")


def _pack(arrs):
    flat = jnp.concatenate([a.reshape(-1) for a in arrs])
    assert flat.shape[0] % 128 == 0
    return flat.reshape(-1, 128)


def _unpack(packed, shapes):
    flat = packed.reshape(-1)
    out, pos = [], 0
    for shp in shapes:
        size = 1
        for d in shp:
            size *= d
        out.append(flat[pos:pos + size].reshape(shp))
        pos += size
    return out


def kernel(x, w_in, w_conv_dw, b_conv_dw, conv_ln_g, conv_ln_b, w_conv_out, hgrn_lb_logits, hgrn_norm_g, w_hgrn_out, w_out, ln1_g, ln1_b, w_ffn_in, w_ffn_dw, b_ffn_dw, w_ffn_out, ln2_g, ln2_b, loss_target, m_w_in, m_w_conv_dw, m_b_conv_dw, m_conv_ln_g, m_conv_ln_b, m_w_conv_out, m_hgrn_lb_logits, m_hgrn_norm_g, m_w_hgrn_out, m_w_out, m_ln1_g, m_ln1_b, m_w_ffn_in, m_w_ffn_dw, m_b_ffn_dw, m_w_ffn_out, m_ln2_g, m_ln2_b, v_w_in, v_w_conv_dw, v_b_conv_dw, v_conv_ln_g, v_conv_ln_b, v_w_conv_out, v_hgrn_lb_logits, v_hgrn_norm_g, v_w_hgrn_out, v_w_out, v_ln1_g, v_ln1_b, v_w_ffn_in, v_w_ffn_dw, v_b_ffn_dw, v_w_ffn_out, v_ln2_g, v_ln2_b):
    w = dict(w_in=w_in, w_conv_dw=w_conv_dw, b_conv_dw=b_conv_dw, conv_ln_g=conv_ln_g, conv_ln_b=conv_ln_b,
             w_conv_out=w_conv_out, hgrn_lb_logits=hgrn_lb_logits, hgrn_norm_g=hgrn_norm_g, w_hgrn_out=w_hgrn_out,
             w_out=w_out, ln1_g=ln1_g, ln1_b=ln1_b, w_ffn_in=w_ffn_in, w_ffn_dw=w_ffn_dw, b_ffn_dw=b_ffn_dw,
             w_ffn_out=w_ffn_out, ln2_g=ln2_g, ln2_b=ln2_b)
    m = dict(w_in=m_w_in, w_conv_dw=m_w_conv_dw, b_conv_dw=m_b_conv_dw, conv_ln_g=m_conv_ln_g, conv_ln_b=m_conv_ln_b,
             w_conv_out=m_w_conv_out, hgrn_lb_logits=m_hgrn_lb_logits, hgrn_norm_g=m_hgrn_norm_g,
             w_hgrn_out=m_w_hgrn_out, w_out=m_w_out, ln1_g=m_ln1_g, ln1_b=m_ln1_b, w_ffn_in=m_w_ffn_in,
             w_ffn_dw=m_w_ffn_dw, b_ffn_dw=m_b_ffn_dw, w_ffn_out=m_w_ffn_out, ln2_g=m_ln2_g, ln2_b=m_ln2_b)
    v = dict(w_in=v_w_in, w_conv_dw=v_w_conv_dw, b_conv_dw=v_b_conv_dw, conv_ln_g=v_conv_ln_g, conv_ln_b=v_conv_ln_b,
             w_conv_out=v_w_conv_out, hgrn_lb_logits=v_hgrn_lb_logits, hgrn_norm_g=v_hgrn_norm_g,
             w_hgrn_out=v_w_hgrn_out, w_out=v_w_out, ln1_g=v_ln1_g, ln1_b=v_ln1_b, w_ffn_in=v_w_ffn_in,
             w_ffn_dw=v_w_ffn_dw, b_ffn_dw=v_b_ffn_dw, w_ffn_out=v_w_ffn_out, ln2_g=v_ln2_g, ln2_b=v_ln2_b)
    big_names = [n for n, _, _, _ in BIG]
    w2 = {n: a[0] if a.ndim == 3 else a for n, a in w.items()}
    m2 = {n: a[0] if a.ndim == 3 else a for n, a in m.items()}
    v2 = {n: a[0] if a.ndim == 3 else a for n, a in v.items()}

    shards_bf16 = [_elementwise(lambda a: (a,), [w2[n]], [BF16], name="cast_" + n)[0] for n in big_names]
    gathered = _gather_weights(shards_bf16, [w2["w_conv_dw"], w2["w_ffn_dw"]])
    wi, wco, wh, wo, wfi, wfo = gathered[:6]
    wcd = jnp.transpose(gathered[6], (1, 0, 2)).reshape(CONV_KERNEL, CONV_DIM)
    wfd = jnp.transpose(gathered[7], (1, 0, 2)).reshape(FFN_KERNEL, D_FF)

    loss_part, grad_x, big_grads, small_grads = _local_step(
        x[0], loss_target[0], wi, wco, wh, wo, wfi, wfo, wcd, w2["b_conv_dw"], w2["conv_ln_g"], w2["conv_ln_b"],
        w2["hgrn_lb_logits"], w2["hgrn_norm_g"], w2["ln1_g"], w2["ln1_b"], wfd, w2["b_ffn_dw"],
        w2["ln2_g"], w2["ln2_b"])
    loss = lax.psum(loss_part[0, 0], ("x", "y", "c"))

    kept, got = _sibling_exchange(big_grads)
    chip_sums = [_elementwise(lambda a, b: (a + b,), [k_, g_], [BF16], name="chip_sum_" + n)[0]
                 for n, k_, g_ in zip(big_names, kept, got)]
    own, recv = _chip_exchange(chip_sums)
    totals = [_elementwise(lambda a, b0, b1, b2: (a.astype(F32) + b0.astype(F32) + b1.astype(F32) + b2.astype(F32),),
                           [o_, r_[0], r_[1], r_[2]], [F32], name="shard_sum_" + n)[0]
              for n, o_, r_ in zip(big_names, own, recv)]
    shard_grads = dict(zip(big_names, _sibling_assemble(totals)))

    small_shapes = [small_grads[n].shape for n in SMALL_ORDER]
    reduced = dict(zip(SMALL_ORDER, _unpack(_all_reduce_small(_pack([small_grads[n] for n in SMALL_ORDER])),
                                            small_shapes)))
    shard = 2 * lax.axis_index("x") + lax.axis_index("y")
    grads = dict(shard_grads)
    for n in REPLICATED_SMALL:
        grads[n] = reduced[n]
    grads["w_conv_dw"] = lax.dynamic_slice_in_dim(reduced["w_conv_dw"], shard * (CONV_DIM // N_CHIPS),
                                                  CONV_DIM // N_CHIPS, axis=1)
    grads["w_ffn_dw"] = lax.dynamic_slice_in_dim(reduced["w_ffn_dw"], shard * (D_FF // N_CHIPS),
                                                 D_FF // N_CHIPS, axis=1)

    delta, new_m, new_v = {}, {}, {}
    for n in big_names + ["w_conv_dw", "w_ffn_dw"]:
        delta[n], new_m[n], new_v[n] = _adamw(w2[n], grads[n], m2[n], v2[n], name="adamw_" + n)
    rep_shapes = [w2[n].shape for n in REPLICATED_SMALL]
    packed = _adamw(*[_pack([src[n] for n in REPLICATED_SMALL]) for src in (w2, grads, m2, v2)], name="adamw_small")
    for dst, pk in zip((delta, new_m, new_v), packed):
        for n, a in zip(REPLICATED_SMALL, _unpack(pk, rep_shapes)):
            dst[n] = a

    def shaped(d):
        return [d[n].reshape(w[n].shape) for n in WEIGHT_ORDER]

    return (loss, grad_x[None], *shaped(grads), *shaped(delta), *shaped(new_m), *shaped(new_v))
```

```python
import jax
import jax.numpy as jnp
from jax import lax
from jax.experimental import pallas as pl
from jax.experimental.pallas import tpu as pltpu

F32 = jnp.float32
BF16 = jnp.bfloat16

D_MODEL = 1024
CONV_DIM = 512
CONV_KERNEL = 31
HGRN_DIM = 1024
HGRN_HEADS = 8
HEAD_DIM = 128
CHUNK = 64
SUB = 16
N_SUB = CHUNK // SUB
D_FF = 2816
FFN_KERNEL = 3
IN_COLS = 7168
LN_EPS = 1e-5
RMS_EPS = 1e-6
ALPHA = 2.0 ** 0.25
GELU_C = 0.7978845608028654
GELU_A = 0.044715

ADAM_LR = 0.001
ADAM_B1 = 0.9
ADAM_B2 = 0.999
ADAM_EPS = 1e-08
ADAM_WD = 0.01
ADAM_STEP = 10
ADAM_BC1 = 1.0 - ADAM_B1 ** ADAM_STEP
ADAM_BC2 = 1.0 - ADAM_B2 ** ADAM_STEP

N_CHIPS = 4
N_DEV = 8
ROW_BLOCK = 32
CONV_HALO = 32
FFN_HALO = 8
MESH = pl.DeviceIdType.MESH
ANY = pl.BlockSpec(memory_space=pl.ANY)


def _dot(a, b):
    return jnp.dot(a, b, preferred_element_type=F32)


def _dot_nt(a, b):
    return lax.dot_general(a, b, (((1,), (1,)), ((), ())), preferred_element_type=F32)


def _dot_tn(a, b):
    return lax.dot_general(a, b, (((0,), (0,)), ((), ())), preferred_element_type=F32)


def _sigmoid(z):
    return jax.nn.sigmoid(z)


def _silu_grad(z, s):
    return s * (1.0 + z * (1.0 - s))


def _gelu_and_grad(u):
    inner = GELU_C * (u + GELU_A * u * u * u)
    th = jnp.tanh(inner)
    g = 0.5 * u * (1.0 + th)
    dg = 0.5 * (1.0 + th) + 0.5 * u * (1.0 - th * th) * GELU_C * (1.0 + 3.0 * GELU_A * u * u)
    return g, dg


def _ln_stats(r):
    mu = jnp.mean(r, axis=-1, keepdims=True)
    xc = r - mu
    var = jnp.mean(xc * xc, axis=-1, keepdims=True)
    rstd = lax.rsqrt(var + LN_EPS)
    return xc * rstd, rstd


def _ln_bwd(dy, xhat, rstd, g):
    dxh = dy * g
    m1 = jnp.mean(dxh, axis=-1, keepdims=True)
    m2 = jnp.mean(dxh * xhat, axis=-1, keepdims=True)
    return rstd * (dxh - m1 - xhat * m2)


def _fold8(x):
    acc = x[0:8, :]
    for r in range(8, x.shape[0], 8):
        acc = acc + x[r:r + 8, :]
    return acc


def _call(body, *, name, grid, in_specs, out_specs, out_shape, scratch=(), vmem_mb=32, aliases=None):
    return pl.pallas_call(
        body, name=name, grid=grid, in_specs=in_specs, out_specs=out_specs, out_shape=out_shape,
        scratch_shapes=list(scratch), input_output_aliases=aliases or {},
        compiler_params=pltpu.CompilerParams(
            dimension_semantics=("arbitrary",) * len(grid), vmem_limit_bytes=vmem_mb * 2 ** 20))


def _sds(shape, dtype):
    return jax.ShapeDtypeStruct(shape, dtype)


def _proj(x, w):
    t = x.shape[0]
    tm, tn = 512, 512

    def body(x_ref, w_ref, p_ref, xb_ref):
        @pl.when(pl.program_id(1) == 0)
        def _():
            xb_ref[...] = x_ref[...].astype(BF16)
        p_ref[...] = _dot(xb_ref[...], w_ref[...])

    return _call(
        body, name="proj", grid=(t // tm, IN_COLS // tn),
        in_specs=[pl.BlockSpec((tm, D_MODEL), lambda i, j: (i, 0)),
                  pl.BlockSpec((D_MODEL, tn), lambda i, j: (0, j))],
        out_specs=[pl.BlockSpec((tm, tn), lambda i, j: (i, j)),
                   pl.BlockSpec((tm, D_MODEL), lambda i, j: (i, 0))],
        out_shape=[_sds((t, IN_COLS), F32), _sds((t, D_MODEL), BF16)])(x, w)


def _mm_nn(a, w, *, tm, tn, name):
    t, k = a.shape
    n = w.shape[1]

    def body(a_ref, w_ref, o_ref):
        o_ref[...] = _dot(a_ref[...], w_ref[...])

    return _call(
        body, name=name, grid=(t // tm, n // tn),
        in_specs=[pl.BlockSpec((tm, k), lambda i, j: (i, 0)), pl.BlockSpec((k, tn), lambda i, j: (0, j))],
        out_specs=pl.BlockSpec((tm, tn), lambda i, j: (i, j)),
        out_shape=_sds((t, n), F32))(a, w)


def _views(*arrs):
    out = []
    for a in arrs:
        if a.ndim == 2:
            out.append((a, None))
        else:
            out.extend((a, p) for p in range(a.shape[0]))
    return out


def _piece_layout(views, tile):
    starts, counts, total = [], [], 0
    for arr, _ in views:
        width = arr.shape[-1]
        assert width % tile == 0
        starts.append(total)
        counts.append(width // tile)
        total += width // tile
    return starts, counts, total


def _mm_tn(a, views, *, tn, name, tt=512, vmem_mb=32):
    t, m = a.shape
    starts, counts, nj = _piece_layout(views, tn)
    n_views = len(views)

    def body(a_ref, *refs):
        b_refs, o_ref = refs[:n_views], refs[n_views]
        j = pl.program_id(0)

        @pl.when(pl.program_id(1) == 0)
        def _():
            o_ref[...] = jnp.zeros_like(o_ref)

        for b_ref, st, nb, (_, p) in zip(b_refs, starts, counts, views):
            @pl.when((j >= st) & (j < st + nb))
            def _(b_ref=b_ref, p=p):
                blk = b_ref[...] if p is None else b_ref[0]
                o_ref[...] += _dot_tn(a_ref[...], blk)

    def b_spec(st, nb, p):
        def rows(j, k):
            return jnp.where((j >= st) & (j < st + nb), k, 0)

        def cols(j):
            return jnp.clip(j - st, 0, nb - 1)

        if p is None:
            return pl.BlockSpec((tt, tn), lambda j, k: (rows(j, k), cols(j)))
        return pl.BlockSpec((1, tt, tn), lambda j, k: (p, rows(j, k), cols(j)))

    return _call(
        body, name=name, grid=(nj, t // tt),
        in_specs=[pl.BlockSpec((tt, m), lambda j, k: (k, 0))]
        + [b_spec(st, nb, p) for st, nb, (_, p) in zip(starts, counts, views)],
        out_specs=pl.BlockSpec((m, tn), lambda j, k: (0, j)),
        out_shape=_sds((m, nj * tn), F32), vmem_mb=vmem_mb)(a, *[arr for arr, _ in views])


def _mm_nt(views, w, add, *, add_scale, tk, name, tm=512, vmem_mb=32):
    t = views[0][0].shape[-2]
    kout = w.shape[0]
    starts, counts, nk = _piece_layout(views, tk)
    n_views = len(views)

    def body(add_ref, *refs):
        b_refs, w_ref, o_ref = refs[:n_views], refs[n_views], refs[n_views + 1]
        k = pl.program_id(1)

        @pl.when(k == 0)
        def _():
            o_ref[...] = add_scale * add_ref[...]

        for b_ref, st, nb, (_, p) in zip(b_refs, starts, counts, views):
            @pl.when((k >= st) & (k < st + nb))
            def _(b_ref=b_ref, p=p):
                blk = b_ref[...] if p is None else b_ref[0]
                o_ref[...] += _dot_nt(blk, w_ref[...])

    def b_spec(st, nb, p):
        def cols(k):
            return jnp.clip(k - st, 0, nb - 1)

        if p is None:
            return pl.BlockSpec((tm, tk), lambda i, k: (i, cols(k)))
        return pl.BlockSpec((1, tm, tk), lambda i, k: (p, i, cols(k)))

    return _call(
        body, name=name, grid=(t // tm, nk),
        in_specs=[pl.BlockSpec((tm, kout), lambda i, k: (i, 0))]
        + [b_spec(st, nb, p) for st, nb, (_, p) in zip(starts, counts, views)]
        + [pl.BlockSpec((kout, tk), lambda i, k: (0, k))],
        out_specs=pl.BlockSpec((tm, kout), lambda i, k: (i, 0)),
        out_shape=_sds((t, kout), F32), vmem_mb=vmem_mb)(add, *[arr for arr, _ in views], w)


def _conv_fwd(proj, wcd, bcd, lng, lnb):
    t = proj.shape[0]
    tm = 512

    def body(cv_ref, cg_ref, w_ref, b_ref, g_ref, be_ref, cc_ref, cs_ref, ext_ref):
        i = pl.program_id(0)

        @pl.when(i == 0)
        def _():
            ext_ref[0:CONV_HALO, :] = jnp.zeros((CONV_HALO, CONV_DIM), F32)

        @pl.when(i > 0)
        def _():
            ext_ref[0:CONV_HALO, :] = ext_ref[tm:tm + CONV_HALO, :]

        ext_ref[CONV_HALO:CONV_HALO + tm, :] = cv_ref[...] * _sigmoid(cg_ref[...])

        def block(r, carry):
            r0 = pl.multiple_of(r * ROW_BLOCK, ROW_BLOCK)
            win = ext_ref[pl.ds(r0, ROW_BLOCK + CONV_HALO), :]
            acc = jnp.broadcast_to(b_ref[...], (ROW_BLOCK, CONV_DIM))
            for k in range(CONV_KERNEL):
                acc = acc + w_ref[k:k + 1, :] * win[2 + k:2 + k + ROW_BLOCK, :]
            cc_ref[pl.ds(r0, ROW_BLOCK), :] = acc
            xhat, _ = _ln_stats(acc)
            a = xhat * g_ref[...] + be_ref[...]
            cs_ref[pl.ds(r0, ROW_BLOCK), :] = (a * _sigmoid(a)).astype(BF16)
            return carry

        lax.fori_loop(0, tm // ROW_BLOCK, block, 0)

    vec = pl.BlockSpec((1, CONV_DIM), lambda i: (0, 0))
    return _call(
        body, name="conv_fwd", grid=(t // tm,),
        in_specs=[pl.BlockSpec((tm, CONV_DIM), lambda i: (i, 0)), pl.BlockSpec((tm, CONV_DIM), lambda i: (i, 1)),
                  pl.BlockSpec((CONV_KERNEL, CONV_DIM), lambda i: (0, 0)), vec, vec, vec],
        out_specs=[pl.BlockSpec((tm, CONV_DIM), lambda i: (i, 0)), pl.BlockSpec((tm, CONV_DIM), lambda i: (i, 0))],
        out_shape=[_sds((t, CONV_DIM), F32), _sds((t, CONV_DIM), BF16)],
        scratch=[pltpu.VMEM((tm + CONV_HALO, CONV_DIM), F32)])(proj, proj, wcd, bcd, lng, lnb)


def _conv_bwd_a(dcs, cc, proj, lng, lnb):
    t = proj.shape[0]
    tm = 512
    nt = t // tm

    def body(dcs_ref, cc_ref, cv_ref, cg_ref, g_ref, be_ref,
             dcc_ref, dw_ref, db_ref, dg_ref, dbe_ref, ext_ref, accw_ref, acc3_ref):
        i = pl.program_id(0)

        @pl.when(i == 0)
        def _():
            ext_ref[0:CONV_HALO, :] = jnp.zeros((CONV_HALO, CONV_DIM), F32)
            accw_ref[...] = jnp.zeros_like(accw_ref)
            acc3_ref[...] = jnp.zeros_like(acc3_ref)

        @pl.when(i > 0)
        def _():
            ext_ref[0:CONV_HALO, :] = ext_ref[tm:tm + CONV_HALO, :]

        ext_ref[CONV_HALO:CONV_HALO + tm, :] = cv_ref[...] * _sigmoid(cg_ref[...])

        def block(r, carry):
            r0 = pl.multiple_of(r * ROW_BLOCK, ROW_BLOCK)
            rows = pl.ds(r0, ROW_BLOCK)
            xhat, rstd = _ln_stats(cc_ref[rows, :])
            a = xhat * g_ref[...] + be_ref[...]
            sg = _sigmoid(a)
            da = dcs_ref[rows, :] * _silu_grad(a, sg)
            acc3_ref[8:16, :] += _fold8(da * xhat)
            acc3_ref[16:24, :] += _fold8(da)
            dcc = _ln_bwd(da, xhat, rstd, g_ref[...])
            dcc_ref[rows, :] = dcc
            acc3_ref[0:8, :] += _fold8(dcc)
            win = ext_ref[pl.ds(r0, ROW_BLOCK + CONV_HALO), :]
            for k in range(CONV_KERNEL):
                accw_ref[8 * k:8 * k + 8, :] += _fold8(dcc * win[2 + k:2 + k + ROW_BLOCK, :])
            return carry

        lax.fori_loop(0, tm // ROW_BLOCK, block, 0)

        @pl.when(i == nt - 1)
        def _():
            for k in range(CONV_KERNEL):
                dw_ref[k:k + 1, :] = jnp.sum(accw_ref[8 * k:8 * k + 8, :], axis=0, keepdims=True)
            db_ref[...] = jnp.sum(acc3_ref[0:8, :], axis=0, keepdims=True)
            dg_ref[...] = jnp.sum(acc3_ref[8:16, :], axis=0, keepdims=True)
            dbe_ref[...] = jnp.sum(acc3_ref[16:24, :], axis=0, keepdims=True)

    vec = pl.BlockSpec((1, CONV_DIM), lambda i: (0, 0))
    tile = pl.BlockSpec((tm, CONV_DIM), lambda i: (i, 0))
    return _call(
        body, name="conv_bwd_a", grid=(nt,),
        in_specs=[tile, tile, tile, pl.BlockSpec((tm, CONV_DIM), lambda i: (i, 1)), vec, vec],
        out_specs=[tile, pl.BlockSpec((CONV_KERNEL, CONV_DIM), lambda i: (0, 0)), vec, vec, vec],
        out_shape=[_sds((t, CONV_DIM), F32), _sds((CONV_KERNEL, CONV_DIM), F32),
                   _sds((1, CONV_DIM), F32), _sds((1, CONV_DIM), F32), _sds((1, CONV_DIM), F32)],
        scratch=[pltpu.VMEM((tm + CONV_HALO, CONV_DIM), F32),
                 pltpu.VMEM((8 * CONV_KERNEL, CONV_DIM), F32),
                 pltpu.VMEM((24, CONV_DIM), F32)])(dcs, cc, proj, proj, lng, lnb)


def _conv_bwd_b(dcc, proj, wcd):
    t = proj.shape[0]
    tm = 512
    nt = t // tm

    def body(dcc_ref, cv_ref, cg_ref, w_ref, out_ref, ext_ref):
        i = pl.program_id(0)

        @pl.when(i == 0)
        def _():
            ext_ref[tm:tm + CONV_HALO, :] = jnp.zeros((CONV_HALO, CONV_DIM), F32)

        @pl.when(i > 0)
        def _():
            ext_ref[tm:tm + CONV_HALO, :] = ext_ref[0:CONV_HALO, :]

        ext_ref[0:tm, :] = dcc_ref[...]

        def block(r, carry):
            r0 = pl.multiple_of(r * ROW_BLOCK, ROW_BLOCK)
            rows = pl.ds(r0, ROW_BLOCK)
            win = ext_ref[pl.ds(r0, ROW_BLOCK + CONV_HALO), :]
            acc = jnp.zeros((ROW_BLOCK, CONV_DIM), F32)
            for k in range(CONV_KERNEL):
                off = CONV_KERNEL - 1 - k
                acc = acc + w_ref[k:k + 1, :] * win[off:off + ROW_BLOCK, :]
            sg = _sigmoid(cg_ref[rows, :])
            out_ref[rows, 0:CONV_DIM] = (acc * sg).astype(BF16)
            out_ref[rows, CONV_DIM:2 * CONV_DIM] = (acc * cv_ref[rows, :] * sg * (1.0 - sg)).astype(BF16)
            return carry

        lax.fori_loop(0, tm // ROW_BLOCK, block, 0)

    rev = lambda i: (nt - 1 - i, 0)
    return _call(
        body, name="conv_bwd_b", grid=(nt,),
        in_specs=[pl.BlockSpec((tm, CONV_DIM), rev), pl.BlockSpec((tm, CONV_DIM), rev),
                  pl.BlockSpec((tm, CONV_DIM), lambda i: (nt - 1 - i, 1)),
                  pl.BlockSpec((CONV_KERNEL, CONV_DIM), lambda i: (0, 0))],
        out_specs=pl.BlockSpec((tm, 2 * CONV_DIM), rev),
        out_shape=_sds((t, 2 * CONV_DIM), BF16),
        scratch=[pltpu.VMEM((tm + CONV_HALO, CONV_DIM), F32)])(dcc, proj, proj, wcd)


def _lower_bound(lg_ref):
    a0, a1 = lg_ref[0:1, :], lg_ref[1:2, :]
    m = jnp.maximum(a0, a1)
    e0, e1 = jnp.exp(a0 - m), jnp.exp(a1 - m)
    return e0 / (e0 + e1)


def _block_tri(n, upper):
    r = lax.broadcasted_iota(jnp.int32, (n, n), 0)
    c = lax.broadcasted_iota(jnp.int32, (n, n), 1)
    same = (r >> 6) == (c >> 6)
    tri = (c >= r) if upper else (c <= r)
    return jnp.where(same & tri, 1.0, 0.0).astype(BF16)


def _block_cumsum(x, upper):
    n, w = x.shape
    hi = x.astype(BF16)
    r1 = x - hi.astype(F32)
    mid = r1.astype(BF16)
    lo = (r1 - mid.astype(F32)).astype(BF16)
    y = _dot(_block_tri(n, upper), jnp.concatenate([hi, mid, lo], axis=1))
    return y[:, 0:w] + y[:, w:2 * w] + y[:, 2 * w:3 * w]


def _pick_row(x, row_ids, r):
    return jnp.sum(jnp.where(row_ids == r, x, 0.0), axis=0, keepdims=True)


def _chunk_terms(qc, kc, bc):
    row = lax.broadcasted_iota(jnp.int32, (CHUNK, 1), 0)
    blk = row >> 4
    betas = [jnp.zeros((1, HEAD_DIM), F32)] + [_pick_row(bc, row, SUB * i - 1) for i in range(1, N_SUB)]
    brow = jnp.zeros_like(bc)
    for i in range(1, N_SUB):
        brow = jnp.where(blk == i, betas[i], brow)
    qscale = jnp.exp(bc - brow)
    qs = qc * qscale
    qcat = jnp.concatenate([jnp.where(blk == i, qs, 0.0) for i in range(N_SUB)], axis=1)
    kscales = []
    for i in range(N_SUB):
        valid = row < SUB * (i + 1)
        kscales.append(jnp.where(valid, jnp.exp(jnp.where(valid, betas[i] - bc, 0.0)), 0.0))
    kcat = jnp.concatenate([kc * ks for ks in kscales], axis=1)
    b_last = _pick_row(bc, row, CHUNK - 1)
    return dict(row=row, blk=blk, qscale=qscale, qcat=qcat, kscales=kscales, kcat=kcat,
                eb=jnp.exp(bc), e_last=jnp.exp(b_last), ktscale=jnp.exp(b_last - bc))


def _causal(shape_rows_first):
    r = lax.broadcasted_iota(jnp.int32, (CHUNK, CHUNK), 0)
    c = lax.broadcasted_iota(jnp.int32, (CHUNK, CHUNK), 1)
    return (c <= r) if shape_rows_first else (r <= c)


def _hgrn_specs(tm, tile_of):
    col = lambda base: (lambda h, i: (tile_of(i), base + h))
    return [pl.BlockSpec((tm, HEAD_DIM), col(8)), pl.BlockSpec((tm, HEAD_DIM), col(16)),
            pl.BlockSpec((tm, HEAD_DIM), col(24)), pl.BlockSpec((tm, HEAD_DIM), col(32)),
            pl.BlockSpec((2, HEAD_DIM), lambda h, i: (0, h)), pl.BlockSpec((1, HEAD_DIM), lambda h, i: (0, h))]


def _hgrn_fwd(proj, logits, ng):
    t = proj.shape[0]
    tm = 512
    nc = tm // CHUNK
    nt = t // tm

    def body(zq_ref, zf_ref, v_ref, zg_ref, lg_ref, ng_ref, o_ref, og_ref, st_ref,
             s_scr, q_scr, k_scr, b_scr):
        @pl.when(pl.program_id(1) == 0)
        def _():
            s_scr[...] = jnp.zeros_like(s_scr)

        lb = _lower_bound(lg_ref)
        zf = zf_ref[...]
        f = lb + (1.0 - lb) * _sigmoid(zf)
        k_scr[...] = (1.0 - lb) * _sigmoid(-zf)
        zq = zq_ref[...]
        q_scr[...] = zq * _sigmoid(zq)
        b_scr[...] = _block_cumsum(jnp.log(f), upper=False)

        def chunk(c, carry):
            rows = pl.ds(pl.multiple_of(c * CHUNK, CHUNK), CHUNK)
            qc, kc, bc, vc = q_scr[rows, :], k_scr[rows, :], b_scr[rows, :], v_ref[rows, :]
            st = s_scr[...]
            st_ref[0, c] = st
            tr = _chunk_terms(qc, kc, bc)
            a = jnp.where(_causal(True), _dot_nt(tr["qcat"].astype(BF16), tr["kcat"].astype(BF16)), 0.0)
            vb = vc.astype(BF16)
            o_ref[rows, :] = _dot(a.astype(BF16), vb) + _dot_nt((qc * tr["eb"]).astype(BF16), st.astype(BF16))
            s_scr[...] = st * tr["e_last"] + _dot_tn(vb, (kc * tr["ktscale"]).astype(BF16))
            return carry

        lax.fori_loop(0, nc, chunk, 0)

        o = o_ref[...]
        rinv = lax.rsqrt(jnp.mean(o * o, axis=-1, keepdims=True) + RMS_EPS)
        zg = zg_ref[...]
        og_ref[...] = (o * rinv * ng_ref[...] * (zg * _sigmoid(zg))).astype(BF16)

    tile = pl.BlockSpec((tm, HEAD_DIM), lambda h, i: (i, h))
    return _call(
        body, name="hgrn_fwd", grid=(HGRN_HEADS, nt),
        in_specs=_hgrn_specs(tm, lambda i: i),
        out_specs=[tile, tile, pl.BlockSpec((1, nc, HEAD_DIM, HEAD_DIM), lambda h, i: (h, i, 0, 0))],
        out_shape=[_sds((t, HGRN_DIM), F32), _sds((t, HGRN_DIM), BF16),
                   _sds((HGRN_HEADS, t // CHUNK, HEAD_DIM, HEAD_DIM), F32)],
        scratch=[pltpu.VMEM((HEAD_DIM, HEAD_DIM), F32)] + [pltpu.VMEM((tm, HEAD_DIM), F32)] * 3,
    )(proj, proj, proj, proj, logits, ng)


def _hgrn_bwd(dog, o, states, proj, logits, ng):
    t = proj.shape[0]
    tm = 512
    nc = tm // CHUNK
    nt = t // tm

    def body(dog_ref, o_ref, st_ref, zq_ref, zf_ref, v_ref, zg_ref, lg_ref, ng_ref,
             dp_ref, dlg_ref, dng_ref,
             ds_scr, q_scr, k_scr, b_scr, do_scr, dq_scr, dk_scr, dv_scr, db_scr, dlb_scr):
        i = pl.program_id(1)

        @pl.when(i == 0)
        def _():
            ds_scr[...] = jnp.zeros_like(ds_scr)
            dlb_scr[...] = jnp.zeros_like(dlb_scr)
            dng_ref[...] = jnp.zeros_like(dng_ref)

        lb = _lower_bound(lg_ref)
        ng_row = ng_ref[...]
        o = o_ref[...]
        rinv = lax.rsqrt(jnp.mean(o * o, axis=-1, keepdims=True) + RMS_EPS)
        ohat = o * rinv
        zg = zg_ref[...]
        sg = _sigmoid(zg)
        dog_v = dog_ref[...]
        don = dog_v * (zg * sg)
        dp_ref[3] = (dog_v * (ohat * ng_row) * _silu_grad(zg, sg)).astype(BF16)
        dng_ref[...] += jnp.sum(don * ohat, axis=0, keepdims=True)
        dohat = don * ng_row
        do_scr[...] = rinv * (dohat - ohat * jnp.mean(dohat * ohat, axis=-1, keepdims=True))

        zf = zf_ref[...]
        s = _sigmoid(zf)
        s_neg = _sigmoid(-zf)
        f = lb + (1.0 - lb) * s
        k_scr[...] = (1.0 - lb) * s_neg
        zq = zq_ref[...]
        sq = _sigmoid(zq)
        q_scr[...] = zq * sq
        b_scr[...] = _block_cumsum(jnp.log(f), upper=False)

        def chunk(cc, carry):
            c = nc - 1 - cc
            rows = pl.ds(pl.multiple_of(c * CHUNK, CHUNK), CHUNK)
            qc, kc, bc, vc, doc = q_scr[rows, :], k_scr[rows, :], b_scr[rows, :], v_ref[rows, :], do_scr[rows, :]
            st = st_ref[0, c]
            dst = ds_scr[...]
            tr = _chunk_terms(qc, kc, bc)
            qcb, kcb = tr["qcat"].astype(BF16), tr["kcat"].astype(BF16)
            dob, vb, dstb = doc.astype(BF16), vc.astype(BF16), dst.astype(BF16)
            a_t = jnp.where(_causal(False), _dot_nt(kcb, qcb), 0.0)
            da = jnp.where(_causal(True), _dot_nt(dob, vb), 0.0)
            da_t = jnp.where(_causal(False), _dot_nt(vb, dob), 0.0)
            dqcat = _dot(da.astype(BF16), kcb)
            dkcat = _dot(da_t.astype(BF16), qcb)
            kt = kc * tr["ktscale"]
            dv_scr[rows, :] = _dot(a_t.astype(BF16), dob) + _dot_nt(kt.astype(BF16), dstb)
            dq = jnp.zeros_like(qc)
            dk = jnp.zeros_like(kc)
            db = jnp.zeros_like(bc)
            for n in range(N_SUB):
                lanes = slice(n * HEAD_DIM, (n + 1) * HEAD_DIM)
                dq = dq + jnp.where(tr["blk"] == n, dqcat[:, lanes], 0.0)
                dk = dk + dkcat[:, lanes] * tr["kscales"][n]
                db = db + (qcb[:, lanes].astype(F32) * dqcat[:, lanes] - kcb[:, lanes].astype(F32) * dkcat[:, lanes])
            dq_inter = _dot(dob, st.astype(BF16)) * tr["eb"]
            dkt = _dot(vb, dstb)
            dk_inter = dkt * tr["ktscale"]
            extra = (jnp.sum(dkt * kt, axis=0, keepdims=True)
                     + tr["e_last"] * jnp.sum(dst * st, axis=0, keepdims=True))
            dq_scr[rows, :] = dq * tr["qscale"] + dq_inter
            dk_scr[rows, :] = dk + dk_inter
            db_scr[rows, :] = (db + qc * dq_inter - kc * dk_inter
                               + jnp.where(tr["row"] == CHUNK - 1, extra, 0.0))
            ds_scr[...] = dst * tr["e_last"] + _dot_tn(dob, (qc * tr["eb"]).astype(BF16))
            return carry

        lax.fori_loop(0, nc, chunk, 0)

        dlogf = _block_cumsum(db_scr[...], upper=True)
        df = dlogf / f - dk_scr[...]
        dp_ref[0] = (dq_scr[...] * _silu_grad(zq, sq)).astype(BF16)
        dp_ref[1] = (df * (1.0 - lb) * s * (1.0 - s)).astype(BF16)
        dp_ref[2] = dv_scr[...].astype(BF16)
        dlb_scr[...] += jnp.sum(df * s_neg, axis=0, keepdims=True)

        @pl.when(i == nt - 1)
        def _():
            dlogit = dlb_scr[...] * lb * (1.0 - lb)
            dlg_ref[0:1, :] = dlogit
            dlg_ref[1:2, :] = -dlogit

    rev = lambda i: nt - 1 - i
    tile = pl.BlockSpec((tm, HEAD_DIM), lambda h, i: (rev(i), h))
    return _call(
        body, name="hgrn_bwd", grid=(HGRN_HEADS, nt),
        in_specs=[tile, tile, pl.BlockSpec((1, nc, HEAD_DIM, HEAD_DIM), lambda h, i: (h, rev(i), 0, 0))]
        + _hgrn_specs(tm, rev),
        out_specs=[pl.BlockSpec((4, tm, HEAD_DIM), lambda h, i: (0, rev(i), h)),
                   pl.BlockSpec((2, HEAD_DIM), lambda h, i: (0, h)),
                   pl.BlockSpec((1, HEAD_DIM), lambda h, i: (0, h))],
        out_shape=[_sds((4, t, HGRN_DIM), BF16), _sds((2, HGRN_DIM), F32), _sds((1, HGRN_DIM), F32)],
        scratch=[pltpu.VMEM((HEAD_DIM, HEAD_DIM), F32)] + [pltpu.VMEM((tm, HEAD_DIM), F32)] * 8
        + [pltpu.VMEM((1, HEAD_DIM), F32)],
    )(dog, o, states, proj, proj, proj, proj, logits, ng)


def _merge_fwd(cs, og, proj, x, wco, wh, wo, g1, b1):
    t = x.shape[0]
    tm = 256

    def body(cs_ref, og_ref, m0_ref, m1_ref, x_ref, wco_ref, wh_ref, wo_ref, g_ref, b_ref,
             y_ref, mixed_ref, r1_ref, x1_ref, x1b_ref):
        yc = _dot(cs_ref[...], wco_ref[...])
        yh = _dot(og_ref[...], wh_ref[...])
        y_ref[0] = yc
        y_ref[1] = yh
        mixed = (_sigmoid(m0_ref[...]) * yc + _sigmoid(m1_ref[...]) * yh).astype(BF16)
        mixed_ref[...] = mixed
        r1 = ALPHA * x_ref[...] + _dot(mixed, wo_ref[...])
        r1_ref[...] = r1
        xhat, _ = _ln_stats(r1)
        x1 = xhat * g_ref[...] + b_ref[...]
        x1_ref[...] = x1
        x1b_ref[...] = x1.astype(BF16)

    row = lambda w: pl.BlockSpec((tm, w), lambda i: (i, 0))
    full = lambda a: pl.BlockSpec(a.shape, lambda i: (0, 0))
    return _call(
        body, name="merge_fwd", grid=(t // tm,),
        in_specs=[row(CONV_DIM), row(HGRN_DIM),
                  pl.BlockSpec((tm, D_MODEL), lambda i: (i, 5)), pl.BlockSpec((tm, D_MODEL), lambda i: (i, 6)),
                  row(D_MODEL), full(wco), full(wh), full(wo), full(g1), full(b1)],
        out_specs=[pl.BlockSpec((2, tm, D_MODEL), lambda i: (0, i, 0)), row(D_MODEL), row(D_MODEL),
                   row(D_MODEL), row(D_MODEL)],
        out_shape=[_sds((2, t, D_MODEL), F32), _sds((t, D_MODEL), BF16), _sds((t, D_MODEL), F32),
                   _sds((t, D_MODEL), F32), _sds((t, D_MODEL), BF16)],
        vmem_mb=48)(cs, og, proj, proj, x, wco, wh, wo, g1, b1)


def _merge_bwd(dr1b, ycat, proj, wo, wco, wh):
    t = dr1b.shape[0]
    tm = 256

    def body(dr_ref, y_ref, m0_ref, m1_ref, wo_ref, wco_ref, wh_ref, dpm_ref, dy_ref, dcs_ref, dog_ref):
        dmixed = _dot_nt(dr_ref[...], wo_ref[...])
        g0 = _sigmoid(m0_ref[...])
        g1 = _sigmoid(m1_ref[...])
        dpm_ref[0] = (dmixed * y_ref[0] * g0 * (1.0 - g0)).astype(BF16)
        dpm_ref[1] = (dmixed * y_ref[1] * g1 * (1.0 - g1)).astype(BF16)
        dyc = (dmixed * g0).astype(BF16)
        dyh = (dmixed * g1).astype(BF16)
        dy_ref[0] = dyc
        dy_ref[1] = dyh
        dcs_ref[...] = _dot_nt(dyc, wco_ref[...])
        dog_ref[...] = _dot_nt(dyh, wh_ref[...])

    row = lambda w: pl.BlockSpec((tm, w), lambda i: (i, 0))
    pair = pl.BlockSpec((2, tm, D_MODEL), lambda i: (0, i, 0))
    full = lambda a: pl.BlockSpec(a.shape, lambda i: (0, 0))
    return _call(
        body, name="merge_bwd", grid=(t // tm,),
        in_specs=[row(D_MODEL), pair,
                  pl.BlockSpec((tm, D_MODEL), lambda i: (i, 5)), pl.BlockSpec((tm, D_MODEL), lambda i: (i, 6)),
                  full(wo), full(wco), full(wh)],
        out_specs=[pair, pair, row(CONV_DIM), row(HGRN_DIM)],
        out_shape=[_sds((2, t, D_MODEL), BF16), _sds((2, t, D_MODEL), BF16),
                   _sds((t, CONV_DIM), F32), _sds((t, HGRN_DIM), F32)],
        vmem_mb=48)(dr1b, ycat, proj, proj, wo, wco, wh)


def _ffn_conv3(win, w_ref, off):
    return (w_ref[0:1, :] * win[off:off + ROW_BLOCK, :] + w_ref[1:2, :] * win[off + 1:off + 1 + ROW_BLOCK, :]
            + w_ref[2:3, :] * win[off + 2:off + 2 + ROW_BLOCK, :])


def _ffn_mid(z, wfd, bfd):
    t = z.shape[0]
    tm = 256

    def body(u_ref, gv_ref, w_ref, b_ref, h_ref, ext_ref):
        i = pl.program_id(0)

        @pl.when(i == 0)
        def _():
            ext_ref[0:FFN_HALO, :] = jnp.zeros((FFN_HALO, D_FF), F32)

        @pl.when(i > 0)
        def _():
            ext_ref[0:FFN_HALO, :] = ext_ref[tm:tm + FFN_HALO, :]

        ext_ref[FFN_HALO:FFN_HALO + tm, :] = u_ref[...]

        def block(r, carry):
            r0 = pl.multiple_of(r * ROW_BLOCK, ROW_BLOCK)
            rows = pl.ds(r0, ROW_BLOCK)
            win = ext_ref[pl.ds(r0, ROW_BLOCK + FFN_HALO), :]
            uc = _ffn_conv3(win, w_ref, FFN_HALO - 2) + b_ref[...]
            g, _ = _gelu_and_grad(uc)
            h_ref[rows, :] = (g * gv_ref[rows, :]).astype(BF16)
            return carry

        lax.fori_loop(0, tm // ROW_BLOCK, block, 0)

    return _call(
        body, name="ffn_mid", grid=(t // tm,),
        in_specs=[pl.BlockSpec((tm, D_FF), lambda i: (i, 0)), pl.BlockSpec((tm, D_FF), lambda i: (i, 1)),
                  pl.BlockSpec((FFN_KERNEL, D_FF), lambda i: (0, 0)), pl.BlockSpec((1, D_FF), lambda i: (0, 0))],
        out_specs=pl.BlockSpec((tm, D_FF), lambda i: (i, 0)),
        out_shape=_sds((t, D_FF), BF16),
        scratch=[pltpu.VMEM((tm + FFN_HALO, D_FF), F32)], vmem_mb=40)(z, z, wfd, bfd)


def _ffn_out_loss(hmid, x1, target, wfo, g2, b2):
    t = x1.shape[0]
    tm = 256
    inv_n = 1.0 / D_MODEL

    def body(h_ref, x1_ref, tg_ref, w_ref, g_ref, b_ref, dr_ref, drb_ref, loss_ref, dg_ref, db_ref):
        @pl.when(pl.program_id(0) == 0)
        def _():
            loss_ref[...] = jnp.zeros_like(loss_ref)
            dg_ref[...] = jnp.zeros_like(dg_ref)
            db_ref[...] = jnp.zeros_like(db_ref)

        r2 = ALPHA * x1_ref[...] + _dot(h_ref[...], w_ref[...])
        xhat, rstd = _ln_stats(r2)
        err = xhat * g_ref[...] + b_ref[...] - tg_ref[...]
        loss_ref[...] += 0.5 * inv_n * jnp.sum(err * err)
        dy = err * inv_n
        dg_ref[...] += jnp.sum(dy * xhat, axis=0, keepdims=True)
        db_ref[...] += jnp.sum(dy, axis=0, keepdims=True)
        dr = _ln_bwd(dy, xhat, rstd, g_ref[...])
        dr_ref[...] = dr
        drb_ref[...] = dr.astype(BF16)

    row = lambda w: pl.BlockSpec((tm, w), lambda i: (i, 0))
    vec = pl.BlockSpec((1, D_MODEL), lambda i: (0, 0))
    return _call(
        body, name="ffn_out_loss", grid=(t // tm,),
        in_specs=[row(D_FF), row(D_MODEL), row(D_MODEL), pl.BlockSpec((D_FF, D_MODEL), lambda i: (0, 0)), vec, vec],
        out_specs=[row(D_MODEL), row(D_MODEL), pl.BlockSpec((1, 128), lambda i: (0, 0)), vec, vec],
        out_shape=[_sds((t, D_MODEL), F32), _sds((t, D_MODEL), BF16), _sds((1, 128), F32),
                   _sds((1, D_MODEL), F32), _sds((1, D_MODEL), F32)],
        vmem_mb=40)(hmid, x1, target, wfo, g2, b2)


def _ffn_bwd_a(dr2b, z, wfo, wfd, bfd):
    t = z.shape[0]
    tm = 256
    nt = t // tm

    def body(dr_ref, u_ref, gv_ref, wfo_ref, w_ref, b_ref, dgv_ref, duc_ref, dw_ref, db_ref,
             ext_ref, dh_ref, acc_ref):
        i = pl.program_id(0)

        @pl.when(i == 0)
        def _():
            ext_ref[0:FFN_HALO, :] = jnp.zeros((FFN_HALO, D_FF), F32)
            acc_ref[...] = jnp.zeros_like(acc_ref)

        @pl.when(i > 0)
        def _():
            ext_ref[0:FFN_HALO, :] = ext_ref[tm:tm + FFN_HALO, :]

        ext_ref[FFN_HALO:FFN_HALO + tm, :] = u_ref[...]
        dh_ref[...] = _dot_nt(dr_ref[...], wfo_ref[...])

        def block(r, carry):
            r0 = pl.multiple_of(r * ROW_BLOCK, ROW_BLOCK)
            rows = pl.ds(r0, ROW_BLOCK)
            win = ext_ref[pl.ds(r0, ROW_BLOCK + FFN_HALO), :]
            off = FFN_HALO - 2
            uc = _ffn_conv3(win, w_ref, off) + b_ref[...]
            g, dg = _gelu_and_grad(uc)
            dh = dh_ref[rows, :]
            dgv_ref[rows, :] = (dh * g).astype(BF16)
            duc = dh * gv_ref[rows, :] * dg
            duc_ref[rows, :] = duc
            acc_ref[0:8, :] += _fold8(duc)
            for k in range(FFN_KERNEL):
                acc_ref[8 + 8 * k:16 + 8 * k, :] += _fold8(duc * win[off + k:off + k + ROW_BLOCK, :])
            return carry

        lax.fori_loop(0, tm // ROW_BLOCK, block, 0)

        @pl.when(i == nt - 1)
        def _():
            db_ref[...] = jnp.sum(acc_ref[0:8, :], axis=0, keepdims=True)
            for k in range(FFN_KERNEL):
                dw_ref[k:k + 1, :] = jnp.sum(acc_ref[8 + 8 * k:16 + 8 * k, :], axis=0, keepdims=True)

    tile = pl.BlockSpec((tm, D_FF), lambda i: (i, 0))
    return _call(
        body, name="ffn_bwd_a", grid=(nt,),
        in_specs=[pl.BlockSpec((tm, D_MODEL), lambda i: (i, 0)), tile, pl.BlockSpec((tm, D_FF), lambda i: (i, 1)),
                  pl.BlockSpec((D_FF, D_MODEL), lambda i: (0, 0)),
                  pl.BlockSpec((FFN_KERNEL, D_FF), lambda i: (0, 0)), pl.BlockSpec((1, D_FF), lambda i: (0, 0))],
        out_specs=[tile, tile, pl.BlockSpec((FFN_KERNEL, D_FF), lambda i: (0, 0)),
                   pl.BlockSpec((1, D_FF), lambda i: (0, 0))],
        out_shape=[_sds((t, D_FF), BF16), _sds((t, D_FF), F32), _sds((FFN_KERNEL, D_FF), F32), _sds((1, D_FF), F32)],
        scratch=[pltpu.VMEM((tm + FFN_HALO, D_FF), F32), pltpu.VMEM((tm, D_FF), F32),
                 pltpu.VMEM((8 + 8 * FFN_KERNEL, D_FF), F32)],
        vmem_mb=56)(dr2b, z, z, wfo, wfd, bfd)


def _ffn_bwd_b(duc, wfd):
    t = duc.shape[0]
    tm = 256
    nt = t // tm

    def body(duc_ref, w_ref, du_ref, ext_ref):
        i = pl.program_id(0)

        @pl.when(i == 0)
        def _():
            ext_ref[tm:tm + FFN_HALO, :] = jnp.zeros((FFN_HALO, D_FF), F32)

        @pl.when(i > 0)
        def _():
            ext_ref[tm:tm + FFN_HALO, :] = ext_ref[0:FFN_HALO, :]

        ext_ref[0:tm, :] = duc_ref[...]

        def block(r, carry):
            r0 = pl.multiple_of(r * ROW_BLOCK, ROW_BLOCK)
            win = ext_ref[pl.ds(r0, ROW_BLOCK + FFN_HALO), :]
            du = (w_ref[2:3, :] * win[0:ROW_BLOCK, :] + w_ref[1:2, :] * win[1:1 + ROW_BLOCK, :]
                  + w_ref[0:1, :] * win[2:2 + ROW_BLOCK, :])
            du_ref[pl.ds(r0, ROW_BLOCK), :] = du.astype(BF16)
            return carry

        lax.fori_loop(0, tm // ROW_BLOCK, block, 0)

    rev = lambda i: (nt - 1 - i, 0)
    return _call(
        body, name="ffn_bwd_b", grid=(nt,),
        in_specs=[pl.BlockSpec((tm, D_FF), rev), pl.BlockSpec((FFN_KERNEL, D_FF), lambda i: (0, 0))],
        out_specs=pl.BlockSpec((tm, D_FF), rev),
        out_shape=_sds((t, D_FF), BF16),
        scratch=[pltpu.VMEM((tm + FFN_HALO, D_FF), F32)], vmem_mb=40)(duc, wfd)


def _ffn_in_bwd(dr2, dub, dgvb, wfi, r1, g1):
    t = dr2.shape[0]
    tm = 256

    def body(dr2_ref, du_ref, dgv_ref, wu_ref, wg_ref, r1_ref, g_ref, dr1_ref, dr1b_ref, dg_ref, db_ref):
        @pl.when(pl.program_id(0) == 0)
        def _():
            dg_ref[...] = jnp.zeros_like(dg_ref)
            db_ref[...] = jnp.zeros_like(db_ref)

        dx1 = ALPHA * dr2_ref[...] + _dot_nt(du_ref[...], wu_ref[...]) + _dot_nt(dgv_ref[...], wg_ref[...])
        xhat, rstd = _ln_stats(r1_ref[...])
        dg_ref[...] += jnp.sum(dx1 * xhat, axis=0, keepdims=True)
        db_ref[...] += jnp.sum(dx1, axis=0, keepdims=True)
        dr1 = _ln_bwd(dx1, xhat, rstd, g_ref[...])
        dr1_ref[...] = dr1
        dr1b_ref[...] = dr1.astype(BF16)

    row = lambda w: pl.BlockSpec((tm, w), lambda i: (i, 0))
    vec = pl.BlockSpec((1, D_MODEL), lambda i: (0, 0))
    return _call(
        body, name="ffn_in_bwd", grid=(t // tm,),
        in_specs=[row(D_MODEL), row(D_FF), row(D_FF),
                  pl.BlockSpec((D_MODEL, D_FF), lambda i: (0, 0)), pl.BlockSpec((D_MODEL, D_FF), lambda i: (0, 1)),
                  row(D_MODEL), vec],
        out_specs=[row(D_MODEL), row(D_MODEL), vec, vec],
        out_shape=[_sds((t, D_MODEL), F32), _sds((t, D_MODEL), BF16), _sds((1, D_MODEL), F32), _sds((1, D_MODEL), F32)],
        vmem_mb=56)(dr2, dub, dgvb, wfi, wfi, r1, g1)


def _local_step(x, target, wi, wco, wh, wo, wfi, wfo, wcd, bcd, clg, clb, logits, ng, g1, b1, wfd, bfd, g2, b2):
    proj, xb = _proj(x, wi)
    cc, cs = _conv_fwd(proj, wcd, bcd, clg, clb)
    o, og, states = _hgrn_fwd(proj, logits, ng)
    ycat, mixed, r1, x1, x1b = _merge_fwd(cs, og, proj, x, wco, wh, wo, g1, b1)
    z = _mm_nn(x1b, wfi, tm=512, tn=1408, name="ffn_in")
    hmid = _ffn_mid(z, wfd, bfd)
    dr2, dr2b, loss, d_g2, d_b2 = _ffn_out_loss(hmid, x1, target, wfo, g2, b2)

    g_wfo = _mm_tn(hmid, _views(dr2b), tn=512, name="grad_w_ffn_out", vmem_mb=48)
    dgvb, duc, d_wfd, d_bfd = _ffn_bwd_a(dr2b, z, wfo, wfd, bfd)
    dub = _ffn_bwd_b(duc, wfd)
    g_wfi = _mm_tn(x1b, _views(dub, dgvb), tn=1408, name="grad_w_ffn_in", vmem_mb=48)
    dr1, dr1b, d_g1, d_b1 = _ffn_in_bwd(dr2, dub, dgvb, wfi, r1, g1)

    g_wo = _mm_tn(mixed, _views(dr1b), tn=512, name="grad_w_out")
    dpm, dyb, dcs, dog = _merge_bwd(dr1b, ycat, proj, wo, wco, wh)
    g_wco = _mm_tn(cs, [(dyb, 0)], tn=512, name="grad_w_conv_out")
    g_wh = _mm_tn(og, [(dyb, 1)], tn=512, name="grad_w_hgrn_out")

    dcc, d_wcd, d_bcd, d_clg, d_clb = _conv_bwd_a(dcs, cc, proj, clg, clb)
    dpc = _conv_bwd_b(dcc, proj, wcd)
    dph, d_logits, d_ng = _hgrn_bwd(dog, o, states, proj, logits, ng)

    pieces = _views(dpc, dph, dpm)
    g_wi = _mm_tn(xb, pieces, tn=512, name="grad_w_in")
    grad_x = _mm_nt(pieces, wi, dr1, add_scale=ALPHA, tk=1024, name="grad_x", vmem_mb=48)

    small = dict(w_conv_dw=d_wcd, b_conv_dw=d_bcd, conv_ln_g=d_clg, conv_ln_b=d_clb, hgrn_lb_logits=d_logits,
                 hgrn_norm_g=d_ng, ln1_g=d_g1, ln1_b=d_b1, w_ffn_dw=d_wfd, b_ffn_dw=d_bfd, ln2_g=d_g2, ln2_b=d_b2)
    return loss, grad_x, (g_wi, g_wco, g_wh, g_wo, g_wfi, g_wfo), small


ELEMENTWISE_BLOCK_ELEMS = 256 * 1024


def _row_tile(rows, cols):
    cap = max(16, ELEMENTWISE_BLOCK_ELEMS // cols)
    if rows <= cap:
        return rows
    best = None
    for cand in range(16, cap + 1, 16):
        if rows % cand == 0:
            best = cand
    assert best is not None
    return best


def _elementwise(fn, ins, out_dtypes, *, name):
    r, c = ins[0].shape
    tr = _row_tile(r, c)

    def body(*refs):
        outs = fn(*[ref[...] for ref in refs[:len(ins)]])
        for ref, val in zip(refs[len(ins):], outs):
            ref[...] = val.astype(ref.dtype)

    spec = pl.BlockSpec((tr, c), lambda i: (i, 0))
    return _call(
        body, name=name, grid=(r // tr,), in_specs=[spec] * len(ins), out_specs=[spec] * len(out_dtypes),
        out_shape=[_sds((r, c), dt) for dt in out_dtypes])(*ins)


def _adamw(w, g, m, v, *, name):
    def fn(w_, g_, m_, v_):
        m_new = ADAM_B1 * m_ + (1.0 - ADAM_B1) * g_
        v_new = ADAM_B2 * v_ + (1.0 - ADAM_B2) * (g_ * g_)
        m_hat = m_new / ADAM_BC1
        v_hat = v_new / ADAM_BC2
        delta = -ADAM_LR * (m_hat / (jnp.sqrt(v_hat) + ADAM_EPS) + ADAM_WD * w_)
        return delta, m_new, v_new

    return _elementwise(fn, [w, g, m, v], [F32, F32, F32], name=name)


def _place():
    return lax.axis_index("x"), lax.axis_index("y"), lax.axis_index("c")


def _other_chips(x, y):
    return [(1 - x, y), (x, 1 - y), (1 - x, 1 - y)]


SHARD_XOR = (2, 1, 3)


DMA_CHUNK_BYTES = 512 * 1024


def _n_chunks(ref):
    rows = ref.shape[0]
    total = ref.dtype.itemsize
    for d in ref.shape:
        total *= d
    best = 1
    for cand in range(2, min(rows, total // DMA_CHUNK_BYTES) + 1):
        if rows % cand == 0 and (rows // cand) % 16 == 0:
            best = cand
    return best


class _Copy:
    def __init__(self, src, dst, sems, dev=None):
        if dev is None:
            make = lambda s_, d_: pltpu.make_async_copy(s_, d_, sems[0])
        else:
            make = lambda s_, d_: pltpu.make_async_remote_copy(
                src_ref=s_, dst_ref=d_, send_sem=sems[0], recv_sem=sems[1], device_id=dev, device_id_type=MESH)
        self.local = dev is None
        self.whole = make(src, dst)
        n = _n_chunks(src)
        step = src.shape[0] // n
        self.parts = ([self.whole] if n == 1 else
                      [make(src.at[pl.ds(i * step, step)], dst.at[pl.ds(i * step, step)]) for i in range(n)])

    def start(self):
        for part in self.parts:
            part.start()

    def wait_recv(self):
        self.whole.wait_recv()

    def wait_send(self):
        self.whole.wait_send()

    def wait(self):
        self.whole.wait()


def _run_copies(local_ops, remote_ops, lsem, ssem, rsem):
    local = [_Copy(src, dst, (lsem.at[n],)) for n, (src, dst) in enumerate(local_ops)]
    remote = [_Copy(src, dst, (ssem.at[n], rsem.at[n]), dev) for n, (src, dst, dev) in enumerate(remote_ops)]
    for cp in local + remote:
        cp.start()
    for cp in remote:
        cp.wait_recv()
    for cp in remote:
        cp.wait_send()
    for cp in local:
        cp.wait()


def _comm_call(body, *, name, n_in, out_shape, n_local, n_remote):
    return pl.pallas_call(
        body, name=name, in_specs=[ANY] * n_in, out_specs=[ANY] * len(out_shape), out_shape=out_shape,
        scratch_shapes=[pltpu.SemaphoreType.DMA((max(n_local, 1),)), pltpu.SemaphoreType.DMA((n_remote,)),
                        pltpu.SemaphoreType.DMA((n_remote,))])


BIG = (("w_in", D_MODEL, IN_COLS, 1), ("w_conv_out", CONV_DIM, D_MODEL, 1), ("w_hgrn_out", HGRN_DIM, D_MODEL, 0),
       ("w_out", D_MODEL, D_MODEL, 0), ("w_ffn_in", D_MODEL, 2 * D_FF, 1), ("w_ffn_out", D_FF, D_MODEL, 0))


def _shard_slice(ref, rows, cols, axis, k):
    if axis == 1:
        w = cols // N_CHIPS
        return ref.at[:, pl.ds(k * w, w)]
    h = rows // N_CHIPS
    return ref.at[pl.ds(k * h, h), :]


def _half_slice(ref, rows, cols, axis, hc):
    if axis == 1:
        return ref.at[pl.ds(hc * (rows // 2), rows // 2), :]
    return ref.at[:, pl.ds(hc * (cols // 2), cols // 2)]


def _half_shape(rows, cols, axis):
    return (rows // 2, cols) if axis == 1 else (rows, cols // 2)


def _gather_weights(shards, small):
    n_big, n_small = len(shards), len(small)
    n_arr = n_big + n_small
    out_shape = ([_sds((r, c), BF16) for _, r, c, _ in BIG]
                 + [_sds((N_CHIPS,) + a.shape, F32) for a in small])
    shard_shape = [(r, c // N_CHIPS) if ax == 1 else (r // N_CHIPS, c) for _, r, c, ax in BIG]

    def body(*refs):
        ins, outs = refs[:n_arr], refs[n_arr:2 * n_arr]
        lsem, ssem, rsem, fsem_s, fsem_r = refs[2 * n_arr:]
        x, y, c = _place()
        me = 2 * x + y
        chips = _other_chips(x, y)
        sibling = (x, y, 1 - c)

        def region(idx, k, hc):
            (_, r, cc, ax), (sr, _) = BIG[idx], shard_shape[idx]
            return _shard_slice(outs[idx], r, cc, ax, k).at[pl.ds(hc * (sr // 2), sr // 2)]

        for k in range(N_CHIPS):
            for hc in range(2):
                @pl.when((me == k) & (c == hc))
                def _(k=k, hc=hc):
                    local = [_Copy(ins[i], _shard_slice(outs[i], BIG[i][1], BIG[i][2], BIG[i][3], k), (lsem.at[i],))
                             for i in range(n_big)]
                    local += [_Copy(ins[n_big + i], outs[n_big + i].at[k], (lsem.at[n_big + i],))
                              for i in range(n_small)]
                    sends, fwds = [], []
                    for j, (cx, cy) in enumerate(chips):
                        for i in range(n_big):
                            n = j * n_arr + i
                            half = ins[i].at[pl.ds(hc * (shard_shape[i][0] // 2), shard_shape[i][0] // 2)]
                            sends.append(_Copy(half, region(i, k, hc), (ssem.at[n], rsem.at[n]), (cx, cy, c)))
                            reg = region(i, k ^ SHARD_XOR[j], hc)
                            fwds.append(_Copy(reg, reg, (fsem_s.at[j * n_big + i], fsem_r.at[j * n_big + i]), sibling))
                        for i in range(n_small):
                            n = j * n_arr + n_big + i
                            sends.append(_Copy(ins[n_big + i], outs[n_big + i].at[k], (ssem.at[n], rsem.at[n]),
                                               (cx, cy, c)))
                    for cp in local + sends:
                        cp.start()
                    for j in range(3):
                        for i in range(n_big):
                            sends[j * n_arr + i].wait_recv()
                        for i in range(n_big):
                            fwds[j * n_big + i].start()
                    for j in range(3):
                        for i in range(n_small):
                            sends[j * n_arr + n_big + i].wait_recv()
                    for cp in fwds:
                        cp.wait_recv()
                    for cp in sends + fwds:
                        cp.wait_send()
                    for cp in local:
                        cp.wait()

    return pl.pallas_call(
        body, name="gather_weights", in_specs=[ANY] * n_arr, out_specs=[ANY] * n_arr, out_shape=out_shape,
        scratch_shapes=[pltpu.SemaphoreType.DMA((n_arr,)), pltpu.SemaphoreType.DMA((3 * n_arr,)),
                        pltpu.SemaphoreType.DMA((3 * n_arr,)), pltpu.SemaphoreType.DMA((3 * n_big,)),
                        pltpu.SemaphoreType.DMA((3 * n_big,))])(*shards, *small)


def _sibling_exchange(grads):
    n = len(BIG)
    shapes = [_sds(_half_shape(r, c, ax), F32) for _, r, c, ax in BIG]

    def body(*refs):
        ins, kept, got = refs[:n], refs[n:2 * n], refs[2 * n:3 * n]
        lsem, ssem, rsem = refs[3 * n:]
        x, y, c = _place()
        for k in range(2):
            @pl.when(c == k)
            def _(k=k):
                local_ops = [(_half_slice(g, r, cc, ax, k), dst) for g, dst, (_, r, cc, ax) in zip(ins, kept, BIG)]
                remote_ops = [(_half_slice(g, r, cc, ax, 1 - k), dst, (x, y, 1 - c))
                              for g, dst, (_, r, cc, ax) in zip(ins, got, BIG)]
                _run_copies(local_ops, remote_ops, lsem, ssem, rsem)

    outs = _comm_call(body, name="grad_sibling_exchange", n_in=n, out_shape=shapes + shapes,
                      n_local=n, n_remote=n)(*grads)
    return outs[:n], outs[n:]


def _chip_exchange(chip_sums):
    n = len(BIG)
    half = [_half_shape(r, c, ax) for _, r, c, ax in BIG]
    piece = [(hr, hc // N_CHIPS) if ax == 1 else (hr // N_CHIPS, hc) for (hr, hc), (_, _, _, ax) in zip(half, BIG)]
    out_shape = [_sds(p, BF16) for p in piece] + [_sds((3,) + p, BF16) for p in piece]

    def body(*refs):
        ins, own, got = refs[:n], refs[n:2 * n], refs[2 * n:3 * n]
        lsem, ssem, rsem = refs[3 * n:]
        x, y, c = _place()
        me = 2 * x + y
        for k in range(N_CHIPS):
            @pl.when(me == k)
            def _(k=k):
                def sl(ref, idx, s):
                    (hr, hc), ax = half[idx], BIG[idx][3]
                    return _shard_slice(ref, hr, hc, ax, s)

                local_ops = [(sl(g, idx, k), dst) for idx, (g, dst) in enumerate(zip(ins, own))]
                remote_ops = [(sl(g, idx, k ^ SHARD_XOR[j]), dst.at[j], (cx, cy, c))
                              for j, (cx, cy) in enumerate(_other_chips(x, y))
                              for idx, (g, dst) in enumerate(zip(ins, got))]
                _run_copies(local_ops, remote_ops, lsem, ssem, rsem)

    outs = _comm_call(body, name="grad_chip_exchange", n_in=n, out_shape=out_shape,
                      n_local=n, n_remote=3 * n)(*chip_sums)
    return outs[:n], outs[n:]


def _sibling_assemble(totals):
    n = len(BIG)
    shard = [(r, c // N_CHIPS) if ax == 1 else (r // N_CHIPS, c) for _, r, c, ax in BIG]
    out_shape = [_sds(s, F32) for s in shard]

    def body(*refs):
        ins, outs = refs[:n], refs[n:2 * n]
        lsem, ssem, rsem = refs[2 * n:]
        x, y, c = _place()
        for k in range(2):
            @pl.when(c == k)
            def _(k=k):
                dsts = [_half_slice(o, sr, sc, ax, k) for o, (sr, sc), (_, _, _, ax) in zip(outs, shard, BIG)]
                local_ops = list(zip(ins, dsts))
                remote_ops = [(src, dst, (x, y, 1 - c)) for src, dst in zip(ins, dsts)]
                _run_copies(local_ops, remote_ops, lsem, ssem, rsem)

    return _comm_call(body, name="grad_sibling_assemble", n_in=n, out_shape=out_shape,
                      n_local=n, n_remote=n)(*totals)


def _all_reduce_small(packed):
    r, w = packed.shape

    def body(in_ref, out_ref, slots, lsem, ssem, rsem):
        x, y, c = _place()
        me = 4 * x + 2 * y + c
        peers = [(x ^ (m >> 2), y ^ ((m >> 1) & 1), c ^ (m & 1)) for m in range(1, N_DEV)]
        _run_copies([(in_ref, slots.at[me])], [(in_ref, slots.at[me], dev) for dev in peers], lsem, ssem, rsem)
        total = slots[0]
        for d in range(1, N_DEV):
            total = total + slots[d]
        out_ref[...] = total

    vmem = pl.BlockSpec(memory_space=pltpu.VMEM)
    return pl.pallas_call(
        body, name="small_all_reduce", in_specs=[vmem], out_specs=vmem, out_shape=_sds((r, w), F32),
        scratch_shapes=[pltpu.VMEM((N_DEV, r, w), F32), pltpu.SemaphoreType.DMA((1,)),
                        pltpu.SemaphoreType.DMA((N_DEV - 1,)), pltpu.SemaphoreType.DMA((N_DEV - 1,))])(packed)


SMALL_ORDER = ("w_conv_dw", "b_conv_dw", "conv_ln_g", "conv_ln_b", "hgrn_lb_logits", "hgrn_norm_g",
               "ln1_g", "ln1_b", "w_ffn_dw", "b_ffn_dw", "ln2_g", "ln2_b")
REPLICATED_SMALL = tuple(n for n in SMALL_ORDER if n not in ("w_conv_dw", "w_ffn_dw"))
WEIGHT_ORDER = ("w_in", "w_conv_dw", "b_conv_dw", "conv_ln_g", "conv_ln_b", "w_conv_out", "hgrn_lb_logits",
                "hgrn_norm_g", "w_hgrn_out", "w_out", "ln1_g", "ln1_b", "w_ffn_in", "w_ffn_dw", "b_ffn_dw",
                "w_ffn_out", "ln2_g", "ln2_b")


def _pack(arrs):
    flat = jnp.concatenate([a.reshape(-1) for a in arrs])
    assert flat.shape[0] % 128 == 0
    return flat.reshape(-1, 128)


def _unpack(packed, shapes):
    flat = packed.reshape(-1)
    out, pos = [], 0
    for shp in shapes:
        size = 1
        for d in shp:
            size *= d
        out.append(flat[pos:pos + size].reshape(shp))
        pos += size
    return out


def kernel(x, w_in, w_conv_dw, b_conv_dw, conv_ln_g, conv_ln_b, w_conv_out, hgrn_lb_logits, hgrn_norm_g, w_hgrn_out, w_out, ln1_g, ln1_b, w_ffn_in, w_ffn_dw, b_ffn_dw, w_ffn_out, ln2_g, ln2_b, loss_target, m_w_in, m_w_conv_dw, m_b_conv_dw, m_conv_ln_g, m_conv_ln_b, m_w_conv_out, m_hgrn_lb_logits, m_hgrn_norm_g, m_w_hgrn_out, m_w_out, m_ln1_g, m_ln1_b, m_w_ffn_in, m_w_ffn_dw, m_b_ffn_dw, m_w_ffn_out, m_ln2_g, m_ln2_b, v_w_in, v_w_conv_dw, v_b_conv_dw, v_conv_ln_g, v_conv_ln_b, v_w_conv_out, v_hgrn_lb_logits, v_hgrn_norm_g, v_w_hgrn_out, v_w_out, v_ln1_g, v_ln1_b, v_w_ffn_in, v_w_ffn_dw, v_b_ffn_dw, v_w_ffn_out, v_ln2_g, v_ln2_b):
    w = dict(w_in=w_in, w_conv_dw=w_conv_dw, b_conv_dw=b_conv_dw, conv_ln_g=conv_ln_g, conv_ln_b=conv_ln_b,
             w_conv_out=w_conv_out, hgrn_lb_logits=hgrn_lb_logits, hgrn_norm_g=hgrn_norm_g, w_hgrn_out=w_hgrn_out,
             w_out=w_out, ln1_g=ln1_g, ln1_b=ln1_b, w_ffn_in=w_ffn_in, w_ffn_dw=w_ffn_dw, b_ffn_dw=b_ffn_dw,
             w_ffn_out=w_ffn_out, ln2_g=ln2_g, ln2_b=ln2_b)
    m = dict(w_in=m_w_in, w_conv_dw=m_w_conv_dw, b_conv_dw=m_b_conv_dw, conv_ln_g=m_conv_ln_g, conv_ln_b=m_conv_ln_b,
             w_conv_out=m_w_conv_out, hgrn_lb_logits=m_hgrn_lb_logits, hgrn_norm_g=m_hgrn_norm_g,
             w_hgrn_out=m_w_hgrn_out, w_out=m_w_out, ln1_g=m_ln1_g, ln1_b=m_ln1_b, w_ffn_in=m_w_ffn_in,
             w_ffn_dw=m_w_ffn_dw, b_ffn_dw=m_b_ffn_dw, w_ffn_out=m_w_ffn_out, ln2_g=m_ln2_g, ln2_b=m_ln2_b)
    v = dict(w_in=v_w_in, w_conv_dw=v_w_conv_dw, b_conv_dw=v_b_conv_dw, conv_ln_g=v_conv_ln_g, conv_ln_b=v_conv_ln_b,
             w_conv_out=v_w_conv_out, hgrn_lb_logits=v_hgrn_lb_logits, hgrn_norm_g=v_hgrn_norm_g,
             w_hgrn_out=v_w_hgrn_out, w_out=v_w_out, ln1_g=v_ln1_g, ln1_b=v_ln1_b, w_ffn_in=v_w_ffn_in,
             w_ffn_dw=v_w_ffn_dw, b_ffn_dw=v_b_ffn_dw, w_ffn_out=v_w_ffn_out, ln2_g=v_ln2_g, ln2_b=v_ln2_b)
    big_names = [n for n, _, _, _ in BIG]
    w2 = {n: a[0] if a.ndim == 3 else a for n, a in w.items()}
    m2 = {n: a[0] if a.ndim == 3 else a for n, a in m.items()}
    v2 = {n: a[0] if a.ndim == 3 else a for n, a in v.items()}

    shards_bf16 = [_elementwise(lambda a: (a,), [w2[n]], [BF16], name="cast_" + n)[0] for n in big_names]
    gathered = _gather_weights(shards_bf16, [w2["w_conv_dw"], w2["w_ffn_dw"]])
    wi, wco, wh, wo, wfi, wfo = gathered[:6]
    wcd = jnp.transpose(gathered[6], (1, 0, 2)).reshape(CONV_KERNEL, CONV_DIM)
    wfd = jnp.transpose(gathered[7], (1, 0, 2)).reshape(FFN_KERNEL, D_FF)

    loss_part, grad_x, big_grads, small_grads = _local_step(
        x[0], loss_target[0], wi, wco, wh, wo, wfi, wfo, wcd, w2["b_conv_dw"], w2["conv_ln_g"], w2["conv_ln_b"],
        w2["hgrn_lb_logits"], w2["hgrn_norm_g"], w2["ln1_g"], w2["ln1_b"], wfd, w2["b_ffn_dw"],
        w2["ln2_g"], w2["ln2_b"])
    loss = lax.psum(loss_part[0, 0], ("x", "y", "c"))

    kept, got = _sibling_exchange(big_grads)
    chip_sums = [_elementwise(lambda a, b: (a + b,), [k_, g_], [BF16], name="chip_sum_" + n)[0]
                 for n, k_, g_ in zip(big_names, kept, got)]
    own, recv = _chip_exchange(chip_sums)
    totals = [_elementwise(lambda a, b0, b1, b2: (a.astype(F32) + b0.astype(F32) + b1.astype(F32) + b2.astype(F32),),
                           [o_, r_[0], r_[1], r_[2]], [F32], name="shard_sum_" + n)[0]
              for n, o_, r_ in zip(big_names, own, recv)]
    shard_grads = dict(zip(big_names, _sibling_assemble(totals)))

    small_shapes = [small_grads[n].shape for n in SMALL_ORDER]
    reduced = dict(zip(SMALL_ORDER, _unpack(_all_reduce_small(_pack([small_grads[n] for n in SMALL_ORDER])),
                                            small_shapes)))
    shard = 2 * lax.axis_index("x") + lax.axis_index("y")
    grads = dict(shard_grads)
    for n in REPLICATED_SMALL:
        grads[n] = reduced[n]
    grads["w_conv_dw"] = lax.dynamic_slice_in_dim(reduced["w_conv_dw"], shard * (CONV_DIM // N_CHIPS),
                                                  CONV_DIM // N_CHIPS, axis=1)
    grads["w_ffn_dw"] = lax.dynamic_slice_in_dim(reduced["w_ffn_dw"], shard * (D_FF // N_CHIPS),
                                                 D_FF // N_CHIPS, axis=1)

    delta, new_m, new_v = {}, {}, {}
    for n in big_names + ["w_conv_dw", "w_ffn_dw"]:
        delta[n], new_m[n], new_v[n] = _adamw(w2[n], grads[n], m2[n], v2[n], name="adamw_" + n)
    rep_shapes = [w2[n].shape for n in REPLICATED_SMALL]
    packed = _adamw(*[_pack([src[n] for n in REPLICATED_SMALL]) for src in (w2, grads, m2, v2)], name="adamw_small")
    for dst, pk in zip((delta, new_m, new_v), packed):
        for n, a in zip(REPLICATED_SMALL, _unpack(pk, rep_shapes)):
            dst[n] = a

    def shaped(d):
        return [d[n].reshape(w[n].shape) for n in WEIGHT_ORDER]

    return (loss, grad_x[None], *shaped(grads), *shaped(delta), *shaped(new_m), *shaped(new_v))
```

```python
import jax
import jax.numpy as jnp
from jax import lax
from jax.experimental import pallas as pl
from jax.experimental.pallas import tpu as pltpu

F32 = jnp.float32
BF16 = jnp.bfloat16

D_MODEL = 1024
CONV_DIM = 512
CONV_KERNEL = 31
HGRN_DIM = 1024
HGRN_HEADS = 8
HEAD_DIM = 128
CHUNK = 64
SUB = 16
N_SUB = CHUNK // SUB
D_FF = 2816
FFN_KERNEL = 3
IN_COLS = 7168
LN_EPS = 1e-5
RMS_EPS = 1e-6
ALPHA = 2.0 ** 0.25
GELU_C = 0.7978845608028654
GELU_A = 0.044715

ADAM_LR = 0.001
ADAM_B1 = 0.9
ADAM_B2 = 0.999
ADAM_EPS = 1e-08
ADAM_WD = 0.01
ADAM_STEP = 10
ADAM_BC1 = 1.0 - ADAM_B1 ** ADAM_STEP
ADAM_BC2 = 1.0 - ADAM_B2 ** ADAM_STEP

N_CHIPS = 4
N_DEV = 8
ROW_BLOCK = 32
CONV_HALO = 32
FFN_HALO = 8
MESH = pl.DeviceIdType.MESH
ANY = pl.BlockSpec(memory_space=pl.ANY)


def _dot(a, b):
    return jnp.dot(a, b, preferred_element_type=F32)


def _dot_nt(a, b):
    return lax.dot_general(a, b, (((1,), (1,)), ((), ())), preferred_element_type=F32)


def _dot_tn(a, b):
    return lax.dot_general(a, b, (((0,), (0,)), ((), ())), preferred_element_type=F32)


def _sigmoid(z):
    return jax.nn.sigmoid(z)


def _silu_grad(z, s):
    return s * (1.0 + z * (1.0 - s))


def _gelu_and_grad(u):
    inner = GELU_C * (u + GELU_A * u * u * u)
    th = jnp.tanh(inner)
    g = 0.5 * u * (1.0 + th)
    dg = 0.5 * (1.0 + th) + 0.5 * u * (1.0 - th * th) * GELU_C * (1.0 + 3.0 * GELU_A * u * u)
    return g, dg


def _ln_stats(r):
    mu = jnp.mean(r, axis=-1, keepdims=True)
    xc = r - mu
    var = jnp.mean(xc * xc, axis=-1, keepdims=True)
    rstd = lax.rsqrt(var + LN_EPS)
    return xc * rstd, rstd


def _ln_bwd(dy, xhat, rstd, g):
    dxh = dy * g
    m1 = jnp.mean(dxh, axis=-1, keepdims=True)
    m2 = jnp.mean(dxh * xhat, axis=-1, keepdims=True)
    return rstd * (dxh - m1 - xhat * m2)


def _fold8(x):
    acc = x[0:8, :]
    for r in range(8, x.shape[0], 8):
        acc = acc + x[r:r + 8, :]
    return acc


def _call(body, *, name, grid, in_specs, out_specs, out_shape, scratch=(), vmem_mb=32, aliases=None):
    return pl.pallas_call(
        body, name=name, grid=grid, in_specs=in_specs, out_specs=out_specs, out_shape=out_shape,
        scratch_shapes=list(scratch), input_output_aliases=aliases or {},
        compiler_params=pltpu.CompilerParams(
            dimension_semantics=("arbitrary",) * len(grid), vmem_limit_bytes=vmem_mb * 2 ** 20))


def _sds(shape, dtype):
    return jax.ShapeDtypeStruct(shape, dtype)


def _proj(x, w):
    t = x.shape[0]
    tm, tn = 512, 512

    def body(x_ref, w_ref, p_ref, xb_ref):
        @pl.when(pl.program_id(1) == 0)
        def _():
            xb_ref[...] = x_ref[...].astype(BF16)
        p_ref[...] = _dot(xb_ref[...], w_ref[...])

    return _call(
        body, name="proj", grid=(t // tm, IN_COLS // tn),
        in_specs=[pl.BlockSpec((tm, D_MODEL), lambda i, j: (i, 0)),
                  pl.BlockSpec((D_MODEL, tn), lambda i, j: (0, j))],
        out_specs=[pl.BlockSpec((tm, tn), lambda i, j: (i, j)),
                   pl.BlockSpec((tm, D_MODEL), lambda i, j: (i, 0))],
        out_shape=[_sds((t, IN_COLS), F32), _sds((t, D_MODEL), BF16)])(x, w)


def _mm_nn(a, w, *, tm, tn, name):
    t, k = a.shape
    n = w.shape[1]

    def body(a_ref, w_ref, o_ref):
        o_ref[...] = _dot(a_ref[...], w_ref[...])

    return _call(
        body, name=name, grid=(t // tm, n // tn),
        in_specs=[pl.BlockSpec((tm, k), lambda i, j: (i, 0)), pl.BlockSpec((k, tn), lambda i, j: (0, j))],
        out_specs=pl.BlockSpec((tm, tn), lambda i, j: (i, j)),
        out_shape=_sds((t, n), F32))(a, w)


def _views(*arrs):
    out = []
    for a in arrs:
        if a.ndim == 2:
            out.append((a, None))
        else:
            out.extend((a, p) for p in range(a.shape[0]))
    return out


def _piece_layout(views, tile):
    starts, counts, total = [], [], 0
    for arr, _ in views:
        width = arr.shape[-1]
        assert width % tile == 0
        starts.append(total)
        counts.append(width // tile)
        total += width // tile
    return starts, counts, total


def _mm_tn(a, views, *, tn, name, tt=512, vmem_mb=32):
    t, m = a.shape
    starts, counts, nj = _piece_layout(views, tn)
    n_views = len(views)

    def body(a_ref, *refs):
        b_refs, o_ref = refs[:n_views], refs[n_views]
        j = pl.program_id(0)

        @pl.when(pl.program_id(1) == 0)
        def _():
            o_ref[...] = jnp.zeros_like(o_ref)

        for b_ref, st, nb, (_, p) in zip(b_refs, starts, counts, views):
            @pl.when((j >= st) & (j < st + nb))
            def _(b_ref=b_ref, p=p):
                blk = b_ref[...] if p is None else b_ref[0]
                o_ref[...] += _dot_tn(a_ref[...], blk)

    def b_spec(st, nb, p):
        def rows(j, k):
            return jnp.where((j >= st) & (j < st + nb), k, 0)

        def cols(j):
            return jnp.clip(j - st, 0, nb - 1)

        if p is None:
            return pl.BlockSpec((tt, tn), lambda j, k: (rows(j, k), cols(j)))
        return pl.BlockSpec((1, tt, tn), lambda j, k: (p, rows(j, k), cols(j)))

    return _call(
        body, name=name, grid=(nj, t // tt),
        in_specs=[pl.BlockSpec((tt, m), lambda j, k: (k, 0))]
        + [b_spec(st, nb, p) for st, nb, (_, p) in zip(starts, counts, views)],
        out_specs=pl.BlockSpec((m, tn), lambda j, k: (0, j)),
        out_shape=_sds((m, nj * tn), F32), vmem_mb=vmem_mb)(a, *[arr for arr, _ in views])


def _mm_nt(views, w, add, *, add_scale, tk, name, tm=512, vmem_mb=32):
    t = views[0][0].shape[-2]
    kout = w.shape[0]
    starts, counts, nk = _piece_layout(views, tk)
    n_views = len(views)

    def body(add_ref, *refs):
        b_refs, w_ref, o_ref = refs[:n_views], refs[n_views], refs[n_views + 1]
        k = pl.program_id(1)

        @pl.when(k == 0)
        def _():
            o_ref[...] = add_scale * add_ref[...]

        for b_ref, st, nb, (_, p) in zip(b_refs, starts, counts, views):
            @pl.when((k >= st) & (k < st + nb))
            def _(b_ref=b_ref, p=p):
                blk = b_ref[...] if p is None else b_ref[0]
                o_ref[...] += _dot_nt(blk, w_ref[...])

    def b_spec(st, nb, p):
        def cols(k):
            return jnp.clip(k - st, 0, nb - 1)

        if p is None:
            return pl.BlockSpec((tm, tk), lambda i, k: (i, cols(k)))
        return pl.BlockSpec((1, tm, tk), lambda i, k: (p, i, cols(k)))

    return _call(
        body, name=name, grid=(t // tm, nk),
        in_specs=[pl.BlockSpec((tm, kout), lambda i, k: (i, 0))]
        + [b_spec(st, nb, p) for st, nb, (_, p) in zip(starts, counts, views)]
        + [pl.BlockSpec((kout, tk), lambda i, k: (0, k))],
        out_specs=pl.BlockSpec((tm, kout), lambda i, k: (i, 0)),
        out_shape=_sds((t, kout), F32), vmem_mb=vmem_mb)(add, *[arr for arr, _ in views], w)


def _conv_fwd(proj, wcd, bcd, lng, lnb):
    t = proj.shape[0]
    tm = 512

    def body(cv_ref, cg_ref, w_ref, b_ref, g_ref, be_ref, cc_ref, cs_ref, ext_ref):
        i = pl.program_id(0)

        @pl.when(i == 0)
        def _():
            ext_ref[0:CONV_HALO, :] = jnp.zeros((CONV_HALO, CONV_DIM), F32)

        @pl.when(i > 0)
        def _():
            ext_ref[0:CONV_HALO, :] = ext_ref[tm:tm + CONV_HALO, :]

        ext_ref[CONV_HALO:CONV_HALO + tm, :] = cv_ref[...] * _sigmoid(cg_ref[...])

        def block(r, carry):
            r0 = pl.multiple_of(r * ROW_BLOCK, ROW_BLOCK)
            win = ext_ref[pl.ds(r0, ROW_BLOCK + CONV_HALO), :]
            acc = jnp.broadcast_to(b_ref[...], (ROW_BLOCK, CONV_DIM))
            for k in range(CONV_KERNEL):
                acc = acc + w_ref[k:k + 1, :] * win[2 + k:2 + k + ROW_BLOCK, :]
            cc_ref[pl.ds(r0, ROW_BLOCK), :] = acc
            xhat, _ = _ln_stats(acc)
            a = xhat * g_ref[...] + be_ref[...]
            cs_ref[pl.ds(r0, ROW_BLOCK), :] = (a * _sigmoid(a)).astype(BF16)
            return carry

        lax.fori_loop(0, tm // ROW_BLOCK, block, 0)

    vec = pl.BlockSpec((1, CONV_DIM), lambda i: (0, 0))
    return _call(
        body, name="conv_fwd", grid=(t // tm,),
        in_specs=[pl.BlockSpec((tm, CONV_DIM), lambda i: (i, 0)), pl.BlockSpec((tm, CONV_DIM), lambda i: (i, 1)),
                  pl.BlockSpec((CONV_KERNEL, CONV_DIM), lambda i: (0, 0)), vec, vec, vec],
        out_specs=[pl.BlockSpec((tm, CONV_DIM), lambda i: (i, 0)), pl.BlockSpec((tm, CONV_DIM), lambda i: (i, 0))],
        out_shape=[_sds((t, CONV_DIM), F32), _sds((t, CONV_DIM), BF16)],
        scratch=[pltpu.VMEM((tm + CONV_HALO, CONV_DIM), F32)])(proj, proj, wcd, bcd, lng, lnb)


def _conv_bwd_a(dcs, cc, proj, lng, lnb):
    t = proj.shape[0]
    tm = 512
    nt = t // tm

    def body(dcs_ref, cc_ref, cv_ref, cg_ref, g_ref, be_ref,
             dcc_ref, dw_ref, db_ref, dg_ref, dbe_ref, ext_ref, accw_ref, acc3_ref):
        i = pl.program_id(0)

        @pl.when(i == 0)
        def _():
            ext_ref[0:CONV_HALO, :] = jnp.zeros((CONV_HALO, CONV_DIM), F32)
            accw_ref[...] = jnp.zeros_like(accw_ref)
            acc3_ref[...] = jnp.zeros_like(acc3_ref)

        @pl.when(i > 0)
        def _():
            ext_ref[0:CONV_HALO, :] = ext_ref[tm:tm + CONV_HALO, :]

        ext_ref[CONV_HALO:CONV_HALO + tm, :] = cv_ref[...] * _sigmoid(cg_ref[...])

        def block(r, carry):
            r0 = pl.multiple_of(r * ROW_BLOCK, ROW_BLOCK)
            rows = pl.ds(r0, ROW_BLOCK)
            xhat, rstd = _ln_stats(cc_ref[rows, :])
            a = xhat * g_ref[...] + be_ref[...]
            sg = _sigmoid(a)
            da = dcs_ref[rows, :] * _silu_grad(a, sg)
            acc3_ref[8:16, :] += _fold8(da * xhat)
            acc3_ref[16:24, :] += _fold8(da)
            dcc = _ln_bwd(da, xhat, rstd, g_ref[...])
            dcc_ref[rows, :] = dcc
            acc3_ref[0:8, :] += _fold8(dcc)
            win = ext_ref[pl.ds(r0, ROW_BLOCK + CONV_HALO), :]
            for k in range(CONV_KERNEL):
                accw_ref[8 * k:8 * k + 8, :] += _fold8(dcc * win[2 + k:2 + k + ROW_BLOCK, :])
            return carry

        lax.fori_loop(0, tm // ROW_BLOCK, block, 0)

        @pl.when(i == nt - 1)
        def _():
            for k in range(CONV_KERNEL):
                dw_ref[k:k + 1, :] = jnp.sum(accw_ref[8 * k:8 * k + 8, :], axis=0, keepdims=True)
            db_ref[...] = jnp.sum(acc3_ref[0:8, :], axis=0, keepdims=True)
            dg_ref[...] = jnp.sum(acc3_ref[8:16, :], axis=0, keepdims=True)
            dbe_ref[...] = jnp.sum(acc3_ref[16:24, :], axis=0, keepdims=True)

    vec = pl.BlockSpec((1, CONV_DIM), lambda i: (0, 0))
    tile = pl.BlockSpec((tm, CONV_DIM), lambda i: (i, 0))
    return _call(
        body, name="conv_bwd_a", grid=(nt,),
        in_specs=[tile, tile, tile, pl.BlockSpec((tm, CONV_DIM), lambda i: (i, 1)), vec, vec],
        out_specs=[tile, pl.BlockSpec((CONV_KERNEL, CONV_DIM), lambda i: (0, 0)), vec, vec, vec],
        out_shape=[_sds((t, CONV_DIM), F32), _sds((CONV_KERNEL, CONV_DIM), F32),
                   _sds((1, CONV_DIM), F32), _sds((1, CONV_DIM), F32), _sds((1, CONV_DIM), F32)],
        scratch=[pltpu.VMEM((tm + CONV_HALO, CONV_DIM), F32),
                 pltpu.VMEM((8 * CONV_KERNEL, CONV_DIM), F32),
                 pltpu.VMEM((24, CONV_DIM), F32)])(dcs, cc, proj, proj, lng, lnb)


def _conv_bwd_b(dcc, proj, wcd):
    t = proj.shape[0]
    tm = 512
    nt = t // tm

    def body(dcc_ref, cv_ref, cg_ref, w_ref, out_ref, ext_ref):
        i = pl.program_id(0)

        @pl.when(i == 0)
        def _():
            ext_ref[tm:tm + CONV_HALO, :] = jnp.zeros((CONV_HALO, CONV_DIM), F32)

        @pl.when(i > 0)
        def _():
            ext_ref[tm:tm + CONV_HALO, :] = ext_ref[0:CONV_HALO, :]

        ext_ref[0:tm, :] = dcc_ref[...]

        def block(r, carry):
            r0 = pl.multiple_of(r * ROW_BLOCK, ROW_BLOCK)
            rows = pl.ds(r0, ROW_BLOCK)
            win = ext_ref[pl.ds(r0, ROW_BLOCK + CONV_HALO), :]
            acc = jnp.zeros((ROW_BLOCK, CONV_DIM), F32)
            for k in range(CONV_KERNEL):
                off = CONV_KERNEL - 1 - k
                acc = acc + w_ref[k:k + 1, :] * win[off:off + ROW_BLOCK, :]
            sg = _sigmoid(cg_ref[rows, :])
            out_ref[rows, 0:CONV_DIM] = (acc * sg).astype(BF16)
            out_ref[rows, CONV_DIM:2 * CONV_DIM] = (acc * cv_ref[rows, :] * sg * (1.0 - sg)).astype(BF16)
            return carry

        lax.fori_loop(0, tm // ROW_BLOCK, block, 0)

    rev = lambda i: (nt - 1 - i, 0)
    return _call(
        body, name="conv_bwd_b", grid=(nt,),
        in_specs=[pl.BlockSpec((tm, CONV_DIM), rev), pl.BlockSpec((tm, CONV_DIM), rev),
                  pl.BlockSpec((tm, CONV_DIM), lambda i: (nt - 1 - i, 1)),
                  pl.BlockSpec((CONV_KERNEL, CONV_DIM), lambda i: (0, 0))],
        out_specs=pl.BlockSpec((tm, 2 * CONV_DIM), rev),
        out_shape=_sds((t, 2 * CONV_DIM), BF16),
        scratch=[pltpu.VMEM((tm + CONV_HALO, CONV_DIM), F32)])(dcc, proj, proj, wcd)


def _lower_bound(lg_ref):
    a0, a1 = lg_ref[0:1, :], lg_ref[1:2, :]
    m = jnp.maximum(a0, a1)
    e0, e1 = jnp.exp(a0 - m), jnp.exp(a1 - m)
    return e0 / (e0 + e1)


def _block_tri(n, upper):
    r = lax.broadcasted_iota(jnp.int32, (n, n), 0)
    c = lax.broadcasted_iota(jnp.int32, (n, n), 1)
    same = (r >> 6) == (c >> 6)
    tri = (c >= r) if upper else (c <= r)
    return jnp.where(same & tri, 1.0, 0.0).astype(BF16)


def _block_cumsum(x, upper):
    n, w = x.shape
    hi = x.astype(BF16)
    r1 = x - hi.astype(F32)
    mid = r1.astype(BF16)
    lo = (r1 - mid.astype(F32)).astype(BF16)
    y = _dot(_block_tri(n, upper), jnp.concatenate([hi, mid, lo], axis=1))
    return y[:, 0:w] + y[:, w:2 * w] + y[:, 2 * w:3 * w]


def _pick_row(x, row_ids, r):
    return jnp.sum(jnp.where(row_ids == r, x, 0.0), axis=0, keepdims=True)


def _chunk_terms(qc, kc, bc):
    row = lax.broadcasted_iota(jnp.int32, (CHUNK, 1), 0)
    blk = row >> 4
    betas = [jnp.zeros((1, HEAD_DIM), F32)] + [_pick_row(bc, row, SUB * i - 1) for i in range(1, N_SUB)]
    brow = jnp.zeros_like(bc)
    for i in range(1, N_SUB):
        brow = jnp.where(blk == i, betas[i], brow)
    qscale = jnp.exp(bc - brow)
    qs = qc * qscale
    qcat = jnp.concatenate([jnp.where(blk == i, qs, 0.0) for i in range(N_SUB)], axis=1)
    kscales = []
    for i in range(N_SUB):
        valid = row < SUB * (i + 1)
        kscales.append(jnp.where(valid, jnp.exp(jnp.where(valid, betas[i] - bc, 0.0)), 0.0))
    kcat = jnp.concatenate([kc * ks for ks in kscales], axis=1)
    b_last = _pick_row(bc, row, CHUNK - 1)
    return dict(row=row, blk=blk, qscale=qscale, qcat=qcat, kscales=kscales, kcat=kcat,
                eb=jnp.exp(bc), e_last=jnp.exp(b_last), ktscale=jnp.exp(b_last - bc))


def _causal(shape_rows_first):
    r = lax.broadcasted_iota(jnp.int32, (CHUNK, CHUNK), 0)
    c = lax.broadcasted_iota(jnp.int32, (CHUNK, CHUNK), 1)
    return (c <= r) if shape_rows_first else (r <= c)


def _hgrn_specs(tm, tile_of):
    col = lambda base: (lambda h, i: (tile_of(i), base + h))
    return [pl.BlockSpec((tm, HEAD_DIM), col(8)), pl.BlockSpec((tm, HEAD_DIM), col(16)),
            pl.BlockSpec((tm, HEAD_DIM), col(24)), pl.BlockSpec((tm, HEAD_DIM), col(32)),
            pl.BlockSpec((2, HEAD_DIM), lambda h, i: (0, h)), pl.BlockSpec((1, HEAD_DIM), lambda h, i: (0, h))]


def _hgrn_fwd(proj, logits, ng):
    t = proj.shape[0]
    tm = 512
    nc = tm // CHUNK
    nt = t // tm

    def body(zq_ref, zf_ref, v_ref, zg_ref, lg_ref, ng_ref, o_ref, og_ref, st_ref,
             s_scr, q_scr, k_scr, b_scr):
        @pl.when(pl.program_id(1) == 0)
        def _():
            s_scr[...] = jnp.zeros_like(s_scr)

        lb = _lower_bound(lg_ref)
        zf = zf_ref[...]
        f = lb + (1.0 - lb) * _sigmoid(zf)
        k_scr[...] = (1.0 - lb) * _sigmoid(-zf)
        zq = zq_ref[...]
        q_scr[...] = zq * _sigmoid(zq)
        b_scr[...] = _block_cumsum(jnp.log(f), upper=False)

        def chunk(c, carry):
            rows = pl.ds(pl.multiple_of(c * CHUNK, CHUNK), CHUNK)
            qc, kc, bc, vc = q_scr[rows, :], k_scr[rows, :], b_scr[rows, :], v_ref[rows, :]
            st = s_scr[...]
            st_ref[0, c] = st
            tr = _chunk_terms(qc, kc, bc)
            a = jnp.where(_causal(True), _dot_nt(tr["qcat"].astype(BF16), tr["kcat"].astype(BF16)), 0.0)
            vb = vc.astype(BF16)
            o_ref[rows, :] = _dot(a.astype(BF16), vb) + _dot_nt((qc * tr["eb"]).astype(BF16), st.astype(BF16))
            s_scr[...] = st * tr["e_last"] + _dot_tn(vb, (kc * tr["ktscale"]).astype(BF16))
            return carry

        lax.fori_loop(0, nc, chunk, 0)

        o = o_ref[...]
        rinv = lax.rsqrt(jnp.mean(o * o, axis=-1, keepdims=True) + RMS_EPS)
        zg = zg_ref[...]
        og_ref[...] = (o * rinv * ng_ref[...] * (zg * _sigmoid(zg))).astype(BF16)

    tile = pl.BlockSpec((tm, HEAD_DIM), lambda h, i: (i, h))
    return _call(
        body, name="hgrn_fwd", grid=(HGRN_HEADS, nt),
        in_specs=_hgrn_specs(tm, lambda i: i),
        out_specs=[tile, tile, pl.BlockSpec((1, nc, HEAD_DIM, HEAD_DIM), lambda h, i: (h, i, 0, 0))],
        out_shape=[_sds((t, HGRN_DIM), F32), _sds((t, HGRN_DIM), BF16),
                   _sds((HGRN_HEADS, t // CHUNK, HEAD_DIM, HEAD_DIM), F32)],
        scratch=[pltpu.VMEM((HEAD_DIM, HEAD_DIM), F32)] + [pltpu.VMEM((tm, HEAD_DIM), F32)] * 3,
    )(proj, proj, proj, proj, logits, ng)


def _hgrn_bwd(dog, o, states, proj, logits, ng):
    t = proj.shape[0]
    tm = 512
    nc = tm // CHUNK
    nt = t // tm

    def body(dog_ref, o_ref, st_ref, zq_ref, zf_ref, v_ref, zg_ref, lg_ref, ng_ref,
             dp_ref, dlg_ref, dng_ref,
             ds_scr, q_scr, k_scr, b_scr, do_scr, dq_scr, dk_scr, dv_scr, db_scr, dlb_scr):
        i = pl.program_id(1)

        @pl.when(i == 0)
        def _():
            ds_scr[...] = jnp.zeros_like(ds_scr)
            dlb_scr[...] = jnp.zeros_like(dlb_scr)
            dng_ref[...] = jnp.zeros_like(dng_ref)

        lb = _lower_bound(lg_ref)
        ng_row = ng_ref[...]
        o = o_ref[...]
        rinv = lax.rsqrt(jnp.mean(o * o, axis=-1, keepdims=True) + RMS_EPS)
        ohat = o * rinv
        zg = zg_ref[...]
        sg = _sigmoid(zg)
        dog_v = dog_ref[...]
        don = dog_v * (zg * sg)
        dp_ref[3] = (dog_v * (ohat * ng_row) * _silu_grad(zg, sg)).astype(BF16)
        dng_ref[...] += jnp.sum(don * ohat, axis=0, keepdims=True)
        dohat = don * ng_row
        do_scr[...] = rinv * (dohat - ohat * jnp.mean(dohat * ohat, axis=-1, keepdims=True))

        zf = zf_ref[...]
        s = _sigmoid(zf)
        s_neg = _sigmoid(-zf)
        f = lb + (1.0 - lb) * s
        k_scr[...] = (1.0 - lb) * s_neg
        zq = zq_ref[...]
        sq = _sigmoid(zq)
        q_scr[...] = zq * sq
        b_scr[...] = _block_cumsum(jnp.log(f), upper=False)

        def chunk(cc, carry):
            c = nc - 1 - cc
            rows = pl.ds(pl.multiple_of(c * CHUNK, CHUNK), CHUNK)
            qc, kc, bc, vc, doc = q_scr[rows, :], k_scr[rows, :], b_scr[rows, :], v_ref[rows, :], do_scr[rows, :]
            st = st_ref[0, c]
            dst = ds_scr[...]
            tr = _chunk_terms(qc, kc, bc)
            qcb, kcb = tr["qcat"].astype(BF16), tr["kcat"].astype(BF16)
            dob, vb, dstb = doc.astype(BF16), vc.astype(BF16), dst.astype(BF16)
            a_t = jnp.where(_causal(False), _dot_nt(kcb, qcb), 0.0)
            da = jnp.where(_causal(True), _dot_nt(dob, vb), 0.0)
            da_t = jnp.where(_causal(False), _dot_nt(vb, dob), 0.0)
            dqcat = _dot(da.astype(BF16), kcb)
            dkcat = _dot(da_t.astype(BF16), qcb)
            kt = kc * tr["ktscale"]
            dv_scr[rows, :] = _dot(a_t.astype(BF16), dob) + _dot_nt(kt.astype(BF16), dstb)
            dq = jnp.zeros_like(qc)
            dk = jnp.zeros_like(kc)
            db = jnp.zeros_like(bc)
            for n in range(N_SUB):
                lanes = slice(n * HEAD_DIM, (n + 1) * HEAD_DIM)
                dq = dq + jnp.where(tr["blk"] == n, dqcat[:, lanes], 0.0)
                dk = dk + dkcat[:, lanes] * tr["kscales"][n]
                db = db + (qcb[:, lanes].astype(F32) * dqcat[:, lanes] - kcb[:, lanes].astype(F32) * dkcat[:, lanes])
            dq_inter = _dot(dob, st.astype(BF16)) * tr["eb"]
            dkt = _dot(vb, dstb)
            dk_inter = dkt * tr["ktscale"]
            extra = (jnp.sum(dkt * kt, axis=0, keepdims=True)
                     + tr["e_last"] * jnp.sum(dst * st, axis=0, keepdims=True))
            dq_scr[rows, :] = dq * tr["qscale"] + dq_inter
            dk_scr[rows, :] = dk + dk_inter
            db_scr[rows, :] = (db + qc * dq_inter - kc * dk_inter
                               + jnp.where(tr["row"] == CHUNK - 1, extra, 0.0))
            ds_scr[...] = dst * tr["e_last"] + _dot_tn(dob, (qc * tr["eb"]).astype(BF16))
            return carry

        lax.fori_loop(0, nc, chunk, 0)

        dlogf = _block_cumsum(db_scr[...], upper=True)
        df = dlogf / f - dk_scr[...]
        dp_ref[0] = (dq_scr[...] * _silu_grad(zq, sq)).astype(BF16)
        dp_ref[1] = (df * (1.0 - lb) * s * (1.0 - s)).astype(BF16)
        dp_ref[2] = dv_scr[...].astype(BF16)
        dlb_scr[...] += jnp.sum(df * s_neg, axis=0, keepdims=True)

        @pl.when(i == nt - 1)
        def _():
            dlogit = dlb_scr[...] * lb * (1.0 - lb)
            dlg_ref[0:1, :] = dlogit
            dlg_ref[1:2, :] = -dlogit

    rev = lambda i: nt - 1 - i
    tile = pl.BlockSpec((tm, HEAD_DIM), lambda h, i: (rev(i), h))
    return _call(
        body, name="hgrn_bwd", grid=(HGRN_HEADS, nt),
        in_specs=[tile, tile, pl.BlockSpec((1, nc, HEAD_DIM, HEAD_DIM), lambda h, i: (h, rev(i), 0, 0))]
        + _hgrn_specs(tm, rev),
        out_specs=[pl.BlockSpec((4, tm, HEAD_DIM), lambda h, i: (0, rev(i), h)),
                   pl.BlockSpec((2, HEAD_DIM), lambda h, i: (0, h)),
                   pl.BlockSpec((1, HEAD_DIM), lambda h, i: (0, h))],
        out_shape=[_sds((4, t, HGRN_DIM), BF16), _sds((2, HGRN_DIM), F32), _sds((1, HGRN_DIM), F32)],
        scratch=[pltpu.VMEM((HEAD_DIM, HEAD_DIM), F32)] + [pltpu.VMEM((tm, HEAD_DIM), F32)] * 8
        + [pltpu.VMEM((1, HEAD_DIM), F32)],
    )(dog, o, states, proj, proj, proj, proj, logits, ng)


def _merge_fwd(cs, og, proj, x, wco, wh, wo, g1, b1):
    t = x.shape[0]
    tm = 256

    def body(cs_ref, og_ref, m0_ref, m1_ref, x_ref, wco_ref, wh_ref, wo_ref, g_ref, b_ref,
             y_ref, mixed_ref, r1_ref, x1_ref, x1b_ref):
        yc = _dot(cs_ref[...], wco_ref[...])
        yh = _dot(og_ref[...], wh_ref[...])
        y_ref[0] = yc
        y_ref[1] = yh
        mixed = (_sigmoid(m0_ref[...]) * yc + _sigmoid(m1_ref[...]) * yh).astype(BF16)
        mixed_ref[...] = mixed
        r1 = ALPHA * x_ref[...] + _dot(mixed, wo_ref[...])
        r1_ref[...] = r1
        xhat, _ = _ln_stats(r1)
        x1 = xhat * g_ref[...] + b_ref[...]
        x1_ref[...] = x1
        x1b_ref[...] = x1.astype(BF16)

    row = lambda w: pl.BlockSpec((tm, w), lambda i: (i, 0))
    full = lambda a: pl.BlockSpec(a.shape, lambda i: (0, 0))
    return _call(
        body, name="merge_fwd", grid=(t // tm,),
        in_specs=[row(CONV_DIM), row(HGRN_DIM),
                  pl.BlockSpec((tm, D_MODEL), lambda i: (i, 5)), pl.BlockSpec((tm, D_MODEL), lambda i: (i, 6)),
                  row(D_MODEL), full(wco), full(wh), full(wo), full(g1), full(b1)],
        out_specs=[pl.BlockSpec((2, tm, D_MODEL), lambda i: (0, i, 0)), row(D_MODEL), row(D_MODEL),
                   row(D_MODEL), row(D_MODEL)],
        out_shape=[_sds((2, t, D_MODEL), F32), _sds((t, D_MODEL), BF16), _sds((t, D_MODEL), F32),
                   _sds((t, D_MODEL), F32), _sds((t, D_MODEL), BF16)],
        vmem_mb=48)(cs, og, proj, proj, x, wco, wh, wo, g1, b1)


def _merge_bwd(dr1b, ycat, proj, wo, wco, wh):
    t = dr1b.shape[0]
    tm = 256

    def body(dr_ref, y_ref, m0_ref, m1_ref, wo_ref, wco_ref, wh_ref, dpm_ref, dy_ref, dcs_ref, dog_ref):
        dmixed = _dot_nt(dr_ref[...], wo_ref[...])
        g0 = _sigmoid(m0_ref[...])
        g1 = _sigmoid(m1_ref[...])
        dpm_ref[0] = (dmixed * y_ref[0] * g0 * (1.0 - g0)).astype(BF16)
        dpm_ref[1] = (dmixed * y_ref[1] * g1 * (1.0 - g1)).astype(BF16)
        dyc = (dmixed * g0).astype(BF16)
        dyh = (dmixed * g1).astype(BF16)
        dy_ref[0] = dyc
        dy_ref[1] = dyh
        dcs_ref[...] = _dot_nt(dyc, wco_ref[...])
        dog_ref[...] = _dot_nt(dyh, wh_ref[...])

    row = lambda w: pl.BlockSpec((tm, w), lambda i: (i, 0))
    pair = pl.BlockSpec((2, tm, D_MODEL), lambda i: (0, i, 0))
    full = lambda a: pl.BlockSpec(a.shape, lambda i: (0, 0))
    return _call(
        body, name="merge_bwd", grid=(t // tm,),
        in_specs=[row(D_MODEL), pair,
                  pl.BlockSpec((tm, D_MODEL), lambda i: (i, 5)), pl.BlockSpec((tm, D_MODEL), lambda i: (i, 6)),
                  full(wo), full(wco), full(wh)],
        out_specs=[pair, pair, row(CONV_DIM), row(HGRN_DIM)],
        out_shape=[_sds((2, t, D_MODEL), BF16), _sds((2, t, D_MODEL), BF16),
                   _sds((t, CONV_DIM), F32), _sds((t, HGRN_DIM), F32)],
        vmem_mb=48)(dr1b, ycat, proj, proj, wo, wco, wh)


def _ffn_conv3(win, w_ref, off):
    return (w_ref[0:1, :] * win[off:off + ROW_BLOCK, :] + w_ref[1:2, :] * win[off + 1:off + 1 + ROW_BLOCK, :]
            + w_ref[2:3, :] * win[off + 2:off + 2 + ROW_BLOCK, :])


def _ffn_mid(z, wfd, bfd):
    t = z.shape[0]
    tm = 256

    def body(u_ref, gv_ref, w_ref, b_ref, h_ref, ext_ref):
        i = pl.program_id(0)

        @pl.when(i == 0)
        def _():
            ext_ref[0:FFN_HALO, :] = jnp.zeros((FFN_HALO, D_FF), F32)

        @pl.when(i > 0)
        def _():
            ext_ref[0:FFN_HALO, :] = ext_ref[tm:tm + FFN_HALO, :]

        ext_ref[FFN_HALO:FFN_HALO + tm, :] = u_ref[...]

        def block(r, carry):
            r0 = pl.multiple_of(r * ROW_BLOCK, ROW_BLOCK)
            rows = pl.ds(r0, ROW_BLOCK)
            win = ext_ref[pl.ds(r0, ROW_BLOCK + FFN_HALO), :]
            uc = _ffn_conv3(win, w_ref, FFN_HALO - 2) + b_ref[...]
            g, _ = _gelu_and_grad(uc)
            h_ref[rows, :] = (g * gv_ref[rows, :]).astype(BF16)
            return carry

        lax.fori_loop(0, tm // ROW_BLOCK, block, 0)

    return _call(
        body, name="ffn_mid", grid=(t // tm,),
        in_specs=[pl.BlockSpec((tm, D_FF), lambda i: (i, 0)), pl.BlockSpec((tm, D_FF), lambda i: (i, 1)),
                  pl.BlockSpec((FFN_KERNEL, D_FF), lambda i: (0, 0)), pl.BlockSpec((1, D_FF), lambda i: (0, 0))],
        out_specs=pl.BlockSpec((tm, D_FF), lambda i: (i, 0)),
        out_shape=_sds((t, D_FF), BF16),
        scratch=[pltpu.VMEM((tm + FFN_HALO, D_FF), F32)], vmem_mb=40)(z, z, wfd, bfd)


def _ffn_out_loss(hmid, x1, target, wfo, g2, b2):
    t = x1.shape[0]
    tm = 256
    inv_n = 1.0 / D_MODEL

    def body(h_ref, x1_ref, tg_ref, w_ref, g_ref, b_ref, dr_ref, drb_ref, loss_ref, dg_ref, db_ref):
        @pl.when(pl.program_id(0) == 0)
        def _():
            loss_ref[...] = jnp.zeros_like(loss_ref)
            dg_ref[...] = jnp.zeros_like(dg_ref)
            db_ref[...] = jnp.zeros_like(db_ref)

        r2 = ALPHA * x1_ref[...] + _dot(h_ref[...], w_ref[...])
        xhat, rstd = _ln_stats(r2)
        err = xhat * g_ref[...] + b_ref[...] - tg_ref[...]
        loss_ref[...] += 0.5 * inv_n * jnp.sum(err * err)
        dy = err * inv_n
        dg_ref[...] += jnp.sum(dy * xhat, axis=0, keepdims=True)
        db_ref[...] += jnp.sum(dy, axis=0, keepdims=True)
        dr = _ln_bwd(dy, xhat, rstd, g_ref[...])
        dr_ref[...] = dr
        drb_ref[...] = dr.astype(BF16)

    row = lambda w: pl.BlockSpec((tm, w), lambda i: (i, 0))
    vec = pl.BlockSpec((1, D_MODEL), lambda i: (0, 0))
    return _call(
        body, name="ffn_out_loss", grid=(t // tm,),
        in_specs=[row(D_FF), row(D_MODEL), row(D_MODEL), pl.BlockSpec((D_FF, D_MODEL), lambda i: (0, 0)), vec, vec],
        out_specs=[row(D_MODEL), row(D_MODEL), pl.BlockSpec((1, 128), lambda i: (0, 0)), vec, vec],
        out_shape=[_sds((t, D_MODEL), F32), _sds((t, D_MODEL), BF16), _sds((1, 128), F32),
                   _sds((1, D_MODEL), F32), _sds((1, D_MODEL), F32)],
        vmem_mb=40)(hmid, x1, target, wfo, g2, b2)


def _ffn_bwd_a(dr2b, z, wfo, wfd, bfd):
    t = z.shape[0]
    tm = 256
    nt = t // tm

    def body(dr_ref, u_ref, gv_ref, wfo_ref, w_ref, b_ref, dgv_ref, duc_ref, dw_ref, db_ref,
             ext_ref, dh_ref, acc_ref):
        i = pl.program_id(0)

        @pl.when(i == 0)
        def _():
            ext_ref[0:FFN_HALO, :] = jnp.zeros((FFN_HALO, D_FF), F32)
            acc_ref[...] = jnp.zeros_like(acc_ref)

        @pl.when(i > 0)
        def _():
            ext_ref[0:FFN_HALO, :] = ext_ref[tm:tm + FFN_HALO, :]

        ext_ref[FFN_HALO:FFN_HALO + tm, :] = u_ref[...]
        dh_ref[...] = _dot_nt(dr_ref[...], wfo_ref[...])

        def block(r, carry):
            r0 = pl.multiple_of(r * ROW_BLOCK, ROW_BLOCK)
            rows = pl.ds(r0, ROW_BLOCK)
            win = ext_ref[pl.ds(r0, ROW_BLOCK + FFN_HALO), :]
            off = FFN_HALO - 2
            uc = _ffn_conv3(win, w_ref, off) + b_ref[...]
            g, dg = _gelu_and_grad(uc)
            dh = dh_ref[rows, :]
            dgv_ref[rows, :] = (dh * g).astype(BF16)
            duc = dh * gv_ref[rows, :] * dg
            duc_ref[rows, :] = duc
            acc_ref[0:8, :] += _fold8(duc)
            for k in range(FFN_KERNEL):
                acc_ref[8 + 8 * k:16 + 8 * k, :] += _fold8(duc * win[off + k:off + k + ROW_BLOCK, :])
            return carry

        lax.fori_loop(0, tm // ROW_BLOCK, block, 0)

        @pl.when(i == nt - 1)
        def _():
            db_ref[...] = jnp.sum(acc_ref[0:8, :], axis=0, keepdims=True)
            for k in range(FFN_KERNEL):
                dw_ref[k:k + 1, :] = jnp.sum(acc_ref[8 + 8 * k:16 + 8 * k, :], axis=0, keepdims=True)

    tile = pl.BlockSpec((tm, D_FF), lambda i: (i, 0))
    return _call(
        body, name="ffn_bwd_a", grid=(nt,),
        in_specs=[pl.BlockSpec((tm, D_MODEL), lambda i: (i, 0)), tile, pl.BlockSpec((tm, D_FF), lambda i: (i, 1)),
                  pl.BlockSpec((D_FF, D_MODEL), lambda i: (0, 0)),
                  pl.BlockSpec((FFN_KERNEL, D_FF), lambda i: (0, 0)), pl.BlockSpec((1, D_FF), lambda i: (0, 0))],
        out_specs=[tile, tile, pl.BlockSpec((FFN_KERNEL, D_FF), lambda i: (0, 0)),
                   pl.BlockSpec((1, D_FF), lambda i: (0, 0))],
        out_shape=[_sds((t, D_FF), BF16), _sds((t, D_FF), F32), _sds((FFN_KERNEL, D_FF), F32), _sds((1, D_FF), F32)],
        scratch=[pltpu.VMEM((tm + FFN_HALO, D_FF), F32), pltpu.VMEM((tm, D_FF), F32),
                 pltpu.VMEM((8 + 8 * FFN_KERNEL, D_FF), F32)],
        vmem_mb=56)(dr2b, z, z, wfo, wfd, bfd)


def _ffn_bwd_b(duc, wfd):
    t = duc.shape[0]
    tm = 256
    nt = t // tm

    def body(duc_ref, w_ref, du_ref, ext_ref):
        i = pl.program_id(0)

        @pl.when(i == 0)
        def _():
            ext_ref[tm:tm + FFN_HALO, :] = jnp.zeros((FFN_HALO, D_FF), F32)

        @pl.when(i > 0)
        def _():
            ext_ref[tm:tm + FFN_HALO, :] = ext_ref[0:FFN_HALO, :]

        ext_ref[0:tm, :] = duc_ref[...]

        def block(r, carry):
            r0 = pl.multiple_of(r * ROW_BLOCK, ROW_BLOCK)
            win = ext_ref[pl.ds(r0, ROW_BLOCK + FFN_HALO), :]
            du = (w_ref[2:3, :] * win[0:ROW_BLOCK, :] + w_ref[1:2, :] * win[1:1 + ROW_BLOCK, :]
                  + w_ref[0:1, :] * win[2:2 + ROW_BLOCK, :])
            du_ref[pl.ds(r0, ROW_BLOCK), :] = du.astype(BF16)
            return carry

        lax.fori_loop(0, tm // ROW_BLOCK, block, 0)

    rev = lambda i: (nt - 1 - i, 0)
    return _call(
        body, name="ffn_bwd_b", grid=(nt,),
        in_specs=[pl.BlockSpec((tm, D_FF), rev), pl.BlockSpec((FFN_KERNEL, D_FF), lambda i: (0, 0))],
        out_specs=pl.BlockSpec((tm, D_FF), rev),
        out_shape=_sds((t, D_FF), BF16),
        scratch=[pltpu.VMEM((tm + FFN_HALO, D_FF), F32)], vmem_mb=40)(duc, wfd)


def _ffn_in_bwd(dr2, dub, dgvb, wfi, r1, g1):
    t = dr2.shape[0]
    tm = 256

    def body(dr2_ref, du_ref, dgv_ref, wu_ref, wg_ref, r1_ref, g_ref, dr1_ref, dr1b_ref, dg_ref, db_ref):
        @pl.when(pl.program_id(0) == 0)
        def _():
            dg_ref[...] = jnp.zeros_like(dg_ref)
            db_ref[...] = jnp.zeros_like(db_ref)

        dx1 = ALPHA * dr2_ref[...] + _dot_nt(du_ref[...], wu_ref[...]) + _dot_nt(dgv_ref[...], wg_ref[...])
        xhat, rstd = _ln_stats(r1_ref[...])
        dg_ref[...] += jnp.sum(dx1 * xhat, axis=0, keepdims=True)
        db_ref[...] += jnp.sum(dx1, axis=0, keepdims=True)
        dr1 = _ln_bwd(dx1, xhat, rstd, g_ref[...])
        dr1_ref[...] = dr1
        dr1b_ref[...] = dr1.astype(BF16)

    row = lambda w: pl.BlockSpec((tm, w), lambda i: (i, 0))
    vec = pl.BlockSpec((1, D_MODEL), lambda i: (0, 0))
    return _call(
        body, name="ffn_in_bwd", grid=(t // tm,),
        in_specs=[row(D_MODEL), row(D_FF), row(D_FF),
                  pl.BlockSpec((D_MODEL, D_FF), lambda i: (0, 0)), pl.BlockSpec((D_MODEL, D_FF), lambda i: (0, 1)),
                  row(D_MODEL), vec],
        out_specs=[row(D_MODEL), row(D_MODEL), vec, vec],
        out_shape=[_sds((t, D_MODEL), F32), _sds((t, D_MODEL), BF16), _sds((1, D_MODEL), F32), _sds((1, D_MODEL), F32)],
        vmem_mb=56)(dr2, dub, dgvb, wfi, wfi, r1, g1)


def _local_step(x, target, wi, wco, wh, wo, wfi, wfo, wcd, bcd, clg, clb, logits, ng, g1, b1, wfd, bfd, g2, b2):
    proj, xb = _proj(x, wi)
    cc, cs = _conv_fwd(proj, wcd, bcd, clg, clb)
    o, og, states = _hgrn_fwd(proj, logits, ng)
    ycat, mixed, r1, x1, x1b = _merge_fwd(cs, og, proj, x, wco, wh, wo, g1, b1)
    z = _mm_nn(x1b, wfi, tm=512, tn=1408, name="ffn_in")
    hmid = _ffn_mid(z, wfd, bfd)
    dr2, dr2b, loss, d_g2, d_b2 = _ffn_out_loss(hmid, x1, target, wfo, g2, b2)

    g_wfo = _mm_tn(hmid, _views(dr2b), tn=512, name="grad_w_ffn_out", vmem_mb=48)
    dgvb, duc, d_wfd, d_bfd = _ffn_bwd_a(dr2b, z, wfo, wfd, bfd)
    dub = _ffn_bwd_b(duc, wfd)
    g_wfi = _mm_tn(x1b, _views(dub, dgvb), tn=1408, name="grad_w_ffn_in", vmem_mb=48)
    dr1, dr1b, d_g1, d_b1 = _ffn_in_bwd(dr2, dub, dgvb, wfi, r1, g1)

    g_wo = _mm_tn(mixed, _views(dr1b), tn=512, name="grad_w_out")
    dpm, dyb, dcs, dog = _merge_bwd(dr1b, ycat, proj, wo, wco, wh)
    g_wco = _mm_tn(cs, [(dyb, 0)], tn=512, name="grad_w_conv_out")
    g_wh = _mm_tn(og, [(dyb, 1)], tn=512, name="grad_w_hgrn_out")

    dcc, d_wcd, d_bcd, d_clg, d_clb = _conv_bwd_a(dcs, cc, proj, clg, clb)
    dpc = _conv_bwd_b(dcc, proj, wcd)
    dph, d_logits, d_ng = _hgrn_bwd(dog, o, states, proj, logits, ng)

    pieces = _views(dpc, dph, dpm)
    g_wi = _mm_tn(xb, pieces, tn=512, name="grad_w_in")
    grad_x = _mm_nt(pieces, wi, dr1, add_scale=ALPHA, tk=1024, name="grad_x", vmem_mb=48)

    small = dict(w_conv_dw=d_wcd, b_conv_dw=d_bcd, conv_ln_g=d_clg, conv_ln_b=d_clb, hgrn_lb_logits=d_logits,
                 hgrn_norm_g=d_ng, ln1_g=d_g1, ln1_b=d_b1, w_ffn_dw=d_wfd, b_ffn_dw=d_bfd, ln2_g=d_g2, ln2_b=d_b2)
    return loss, grad_x, (g_wi, g_wco, g_wh, g_wo, g_wfi, g_wfo), small


ELEMENTWISE_BLOCK_ELEMS = 256 * 1024


def _row_tile(rows, cols):
    cap = max(16, ELEMENTWISE_BLOCK_ELEMS // cols)
    if rows <= cap:
        return rows
    best = None
    for cand in range(16, cap + 1, 16):
        if rows % cand == 0:
            best = cand
    assert best is not None
    return best


def _elementwise(fn, ins, out_dtypes, *, name):
    r, c = ins[0].shape
    tr = _row_tile(r, c)

    def body(*refs):
        outs = fn(*[ref[...] for ref in refs[:len(ins)]])
        for ref, val in zip(refs[len(ins):], outs):
            ref[...] = val.astype(ref.dtype)

    spec = pl.BlockSpec((tr, c), lambda i: (i, 0))
    return _call(
        body, name=name, grid=(r // tr,), in_specs=[spec] * len(ins), out_specs=[spec] * len(out_dtypes),
        out_shape=[_sds((r, c), dt) for dt in out_dtypes])(*ins)


def _windowed(fn, sel, ins, outs, *, window, name):
    rows, cols = window
    tr = _row_tile(rows, cols)
    steps = rows // tr

    def spec(where):
        if where is None:
            return pl.BlockSpec((tr, cols), lambda i, s: (i, 0))
        kind, p, _ = where
        if kind == "rows":
            return pl.BlockSpec((tr, cols), lambda i, s: (s[p] * steps + i, 0))
        return pl.BlockSpec((tr, cols), lambda i, s: (i, s[p]))

    n_in = len(ins)

    def body(s_ref, *refs):
        vals = fn(*[ref[...] for ref in refs[:n_in]])
        for ref, val in zip(refs[n_in:], vals):
            ref[...] = val.astype(ref.dtype)

    return pl.pallas_call(
        body, name=name, out_shape=[_sds(shape, dt) for shape, dt, _ in outs],
        grid_spec=pltpu.PrefetchScalarGridSpec(
            num_scalar_prefetch=1, grid=(steps,), in_specs=[spec(where) for _, where in ins],
            out_specs=[spec(where) for _, _, where in outs]),
        compiler_params=pltpu.CompilerParams(dimension_semantics=("arbitrary",), vmem_limit_bytes=32 * 2 ** 20),
    )(sel, *[a for a, _ in ins])


def _adamw(w, g, m, v, *, name):
    def fn(w_, g_, m_, v_):
        m_new = ADAM_B1 * m_ + (1.0 - ADAM_B1) * g_
        v_new = ADAM_B2 * v_ + (1.0 - ADAM_B2) * (g_ * g_)
        m_hat = m_new / ADAM_BC1
        v_hat = v_new / ADAM_BC2
        delta = -ADAM_LR * (m_hat / (jnp.sqrt(v_hat) + ADAM_EPS) + ADAM_WD * w_)
        return delta, m_new, v_new

    return _elementwise(fn, [w, g, m, v], [F32, F32, F32], name=name)


def _place():
    return lax.axis_index("x"), lax.axis_index("y"), lax.axis_index("c")


def _other_chips(x, y):
    return [(1 - x, y), (x, 1 - y), (1 - x, 1 - y)]


SHARD_XOR = (2, 1, 3)


DMA_CHUNK_BYTES = 512 * 1024


def _n_chunks(ref):
    rows = ref.shape[0]
    total = ref.dtype.itemsize
    for d in ref.shape:
        total *= d
    best = 1
    for cand in range(2, min(rows, total // DMA_CHUNK_BYTES) + 1):
        if rows % cand == 0 and (rows // cand) % 16 == 0:
            best = cand
    return best


class _Copy:
    def __init__(self, src, dst, sems, dev=None):
        if dev is None:
            make = lambda s_, d_: pltpu.make_async_copy(s_, d_, sems[0])
        else:
            make = lambda s_, d_: pltpu.make_async_remote_copy(
                src_ref=s_, dst_ref=d_, send_sem=sems[0], recv_sem=sems[1], device_id=dev, device_id_type=MESH)
        self.local = dev is None
        self.whole = make(src, dst)
        n = _n_chunks(src)
        step = src.shape[0] // n
        self.parts = ([self.whole] if n == 1 else
                      [make(src.at[pl.ds(i * step, step)], dst.at[pl.ds(i * step, step)]) for i in range(n)])

    def start(self):
        for part in self.parts:
            part.start()

    def wait_recv(self):
        self.whole.wait_recv()

    def wait_send(self):
        self.whole.wait_send()

    def wait(self):
        self.whole.wait()


def _run_copies(local_ops, remote_ops, lsem, ssem, rsem):
    local = [_Copy(src, dst, (lsem.at[n],)) for n, (src, dst) in enumerate(local_ops)]
    remote = [_Copy(src, dst, (ssem.at[n], rsem.at[n]), dev) for n, (src, dst, dev) in enumerate(remote_ops)]
    for cp in local + remote:
        cp.start()
    for cp in remote:
        cp.wait_recv()
    for cp in remote:
        cp.wait_send()
    for cp in local:
        cp.wait()


def _comm_call(body, *, name, n_in, out_shape, n_local, n_remote):
    return pl.pallas_call(
        body, name=name, in_specs=[ANY] * n_in, out_specs=[ANY] * len(out_shape), out_shape=out_shape,
        scratch_shapes=[pltpu.SemaphoreType.DMA((max(n_local, 1),)), pltpu.SemaphoreType.DMA((n_remote,)),
                        pltpu.SemaphoreType.DMA((n_remote,))])


BIG = (("w_in", D_MODEL, IN_COLS, 1), ("w_conv_out", CONV_DIM, D_MODEL, 1), ("w_hgrn_out", HGRN_DIM, D_MODEL, 0),
       ("w_out", D_MODEL, D_MODEL, 0), ("w_ffn_in", D_MODEL, 2 * D_FF, 1), ("w_ffn_out", D_FF, D_MODEL, 0))


def _shard_slice(ref, rows, cols, axis, k):
    if axis == 1:
        w = cols // N_CHIPS
        return ref.at[:, pl.ds(k * w, w)]
    h = rows // N_CHIPS
    return ref.at[pl.ds(k * h, h), :]


def _half_slice(ref, rows, cols, axis, hc):
    if axis == 1:
        return ref.at[pl.ds(hc * (rows // 2), rows // 2), :]
    return ref.at[:, pl.ds(hc * (cols // 2), cols // 2)]


def _half_shape(rows, cols, axis):
    return (rows // 2, cols) if axis == 1 else (rows, cols // 2)


def _gather_weights(full, small):
    n_big, n_small = len(full), len(small)
    n_arr = n_big + n_small
    out_shape = ([_sds((r, c), BF16) for _, r, c, _ in BIG]
                 + [_sds((N_CHIPS,) + a.shape, F32) for a in small])
    shard_shape = [(r, c // N_CHIPS) if ax == 1 else (r // N_CHIPS, c) for _, r, c, ax in BIG]

    def body(*refs):
        ins, outs = refs[:n_arr], refs[n_arr:2 * n_arr]
        lsem, ssem, rsem, fsem_s, fsem_r = refs[2 * n_arr:]
        x, y, c = _place()
        me = 2 * x + y
        chips = _other_chips(x, y)
        sibling = (x, y, 1 - c)

        def region(idx, k, hc, bufs=outs):
            (_, r, cc, ax), (sr, _) = BIG[idx], shard_shape[idx]
            return _shard_slice(bufs[idx], r, cc, ax, k).at[pl.ds(hc * (sr // 2), sr // 2)]

        for k in range(N_CHIPS):
            for hc in range(2):
                @pl.when((me == k) & (c == hc))
                def _(k=k, hc=hc):
                    local = [_Copy(ins[n_big + i], outs[n_big + i].at[k], (lsem.at[i],)) for i in range(n_small)]
                    sends, fwds = [], []
                    for j, (cx, cy) in enumerate(chips):
                        for i in range(n_big):
                            n = j * n_arr + i
                            sends.append(_Copy(region(i, k, hc, ins), region(i, k, hc), (ssem.at[n], rsem.at[n]),
                                               (cx, cy, c)))
                            reg = region(i, k ^ SHARD_XOR[j], hc)
                            fwds.append(_Copy(reg, reg, (fsem_s.at[j * n_big + i], fsem_r.at[j * n_big + i]), sibling))
                        for i in range(n_small):
                            n = j * n_arr + n_big + i
                            sends.append(_Copy(ins[n_big + i], outs[n_big + i].at[k], (ssem.at[n], rsem.at[n]),
                                               (cx, cy, c)))
                    for cp in local + sends:
                        cp.start()
                    for j in range(3):
                        for i in range(n_big):
                            sends[j * n_arr + i].wait_recv()
                        for i in range(n_big):
                            fwds[j * n_big + i].start()
                    for j in range(3):
                        for i in range(n_small):
                            sends[j * n_arr + n_big + i].wait_recv()
                    for cp in fwds:
                        cp.wait_recv()
                    for cp in sends + fwds:
                        cp.wait_send()
                    for cp in local:
                        cp.wait()

    return pl.pallas_call(
        body, name="gather_weights", in_specs=[ANY] * n_arr, out_specs=[ANY] * n_arr, out_shape=out_shape,
        input_output_aliases={i: i for i in range(n_big)},
        scratch_shapes=[pltpu.SemaphoreType.DMA((n_small,)), pltpu.SemaphoreType.DMA((3 * n_arr,)),
                        pltpu.SemaphoreType.DMA((3 * n_arr,)), pltpu.SemaphoreType.DMA((3 * n_big,)),
                        pltpu.SemaphoreType.DMA((3 * n_big,))])(*full, *small)


def _sibling_exchange(grads):
    n = len(BIG)
    shapes = [_sds(_half_shape(r, c, ax), F32) for _, r, c, ax in BIG]

    def body(*refs):
        ins, got = refs[:n], refs[n:2 * n]
        lsem, ssem, rsem = refs[2 * n:]
        x, y, c = _place()
        for k in range(2):
            @pl.when(c == k)
            def _(k=k):
                remote_ops = [(_half_slice(g, r, cc, ax, 1 - k), dst, (x, y, 1 - c))
                              for g, dst, (_, r, cc, ax) in zip(ins, got, BIG)]
                _run_copies([], remote_ops, lsem, ssem, rsem)

    return _comm_call(body, name="grad_sibling_exchange", n_in=n, out_shape=shapes, n_local=0, n_remote=n)(*grads)


def _piece_shape(rows, cols, axis):
    hr, hc = _half_shape(rows, cols, axis)
    return (hr, hc // N_CHIPS) if axis == 1 else (hr // N_CHIPS, hc)


def _chip_exchange(chip_sums):
    n = len(BIG)
    half = [_half_shape(r, c, ax) for _, r, c, ax in BIG]
    out_shape = [_sds(_piece_shape(r, c, ax), BF16) for _, r, c, ax in BIG for _ in range(3)]

    def body(*refs):
        ins, got = refs[:n], refs[n:4 * n]
        lsem, ssem, rsem = refs[4 * n:]
        x, y, c = _place()
        me = 2 * x + y
        for k in range(N_CHIPS):
            @pl.when(me == k)
            def _(k=k):
                remote_ops = [(_shard_slice(g, half[idx][0], half[idx][1], BIG[idx][3], k ^ SHARD_XOR[j]),
                               got[3 * idx + j], (cx, cy, c))
                              for j, (cx, cy) in enumerate(_other_chips(x, y))
                              for idx, g in enumerate(ins)]
                _run_copies([], remote_ops, lsem, ssem, rsem)

    outs = _comm_call(body, name="grad_chip_exchange", n_in=n, out_shape=out_shape, n_local=0, n_remote=3 * n)(
        *chip_sums)
    return [outs[3 * idx:3 * idx + 3] for idx in range(n)]


def _sibling_assemble(shards):
    n = len(BIG)
    shape = [(r, c // N_CHIPS) if ax == 1 else (r // N_CHIPS, c) for _, r, c, ax in BIG]

    def body(*refs):
        ins, outs = refs[:n], refs[n:2 * n]
        lsem, ssem, rsem = refs[2 * n:]
        x, y, c = _place()
        for k in range(2):
            @pl.when(c == k)
            def _(k=k):
                remote_ops = [(_half_slice(i_, sr, sc, ax, k), _half_slice(o_, sr, sc, ax, k), (x, y, 1 - c))
                              for i_, o_, (sr, sc), (_, _, _, ax) in zip(ins, outs, shape, BIG)]
                _run_copies([], remote_ops, lsem, ssem, rsem)

    return pl.pallas_call(
        body, name="grad_sibling_assemble", in_specs=[ANY] * n, out_specs=[ANY] * n,
        out_shape=[_sds(s, F32) for s in shape], input_output_aliases={i: i for i in range(n)},
        scratch_shapes=[pltpu.SemaphoreType.DMA((1,)), pltpu.SemaphoreType.DMA((n,)),
                        pltpu.SemaphoreType.DMA((n,))])(*shards)


def _all_reduce_small(packed):
    r, w = packed.shape

    def body(in_ref, out_ref, slots, lsem, ssem, rsem):
        x, y, c = _place()
        me = 4 * x + 2 * y + c
        peers = [(x ^ (m >> 2), y ^ ((m >> 1) & 1), c ^ (m & 1)) for m in range(1, N_DEV)]
        _run_copies([(in_ref, slots.at[me])], [(in_ref, slots.at[me], dev) for dev in peers], lsem, ssem, rsem)
        total = slots[0]
        for d in range(1, N_DEV):
            total = total + slots[d]
        out_ref[...] = total

    vmem = pl.BlockSpec(memory_space=pltpu.VMEM)
    return pl.pallas_call(
        body, name="small_all_reduce", in_specs=[vmem], out_specs=vmem, out_shape=_sds((r, w), F32),
        scratch_shapes=[pltpu.VMEM((N_DEV, r, w), F32), pltpu.SemaphoreType.DMA((1,)),
                        pltpu.SemaphoreType.DMA((N_DEV - 1,)), pltpu.SemaphoreType.DMA((N_DEV - 1,))])(packed)


SMALL_ORDER = ("w_conv_dw", "b_conv_dw", "conv_ln_g", "conv_ln_b", "hgrn_lb_logits", "hgrn_norm_g",
               "ln1_g", "ln1_b", "w_ffn_dw", "b_ffn_dw", "ln2_g", "ln2_b")
REPLICATED_SMALL = tuple(n for n in SMALL_ORDER if n not in ("w_conv_dw", "w_ffn_dw"))
WEIGHT_ORDER = ("w_in", "w_conv_dw", "b_conv_dw", "conv_ln_g", "conv_ln_b", "w_conv_out", "hgrn_lb_logits",
                "hgrn_norm_g", "w_hgrn_out", "w_out", "ln1_g", "ln1_b", "w_ffn_in", "w_ffn_dw", "b_ffn_dw",
                "w_ffn_out", "ln2_g", "ln2_b")


def _pack(arrs):
    flat = jnp.concatenate([a.reshape(-1) for a in arrs])
    assert flat.shape[0] % 128 == 0
    return flat.reshape(-1, 128)


def _unpack(packed, shapes):
    flat = packed.reshape(-1)
    out, pos = [], 0
    for shp in shapes:
        size = 1
        for d in shp:
            size *= d
        out.append(flat[pos:pos + size].reshape(shp))
        pos += size
    return out


def kernel(x, w_in, w_conv_dw, b_conv_dw, conv_ln_g, conv_ln_b, w_conv_out, hgrn_lb_logits, hgrn_norm_g, w_hgrn_out, w_out, ln1_g, ln1_b, w_ffn_in, w_ffn_dw, b_ffn_dw, w_ffn_out, ln2_g, ln2_b, loss_target, m_w_in, m_w_conv_dw, m_b_conv_dw, m_conv_ln_g, m_conv_ln_b, m_w_conv_out, m_hgrn_lb_logits, m_hgrn_norm_g, m_w_hgrn_out, m_w_out, m_ln1_g, m_ln1_b, m_w_ffn_in, m_w_ffn_dw, m_b_ffn_dw, m_w_ffn_out, m_ln2_g, m_ln2_b, v_w_in, v_w_conv_dw, v_b_conv_dw, v_conv_ln_g, v_conv_ln_b, v_w_conv_out, v_hgrn_lb_logits, v_hgrn_norm_g, v_w_hgrn_out, v_w_out, v_ln1_g, v_ln1_b, v_w_ffn_in, v_w_ffn_dw, v_b_ffn_dw, v_w_ffn_out, v_ln2_g, v_ln2_b):
    w = dict(w_in=w_in, w_conv_dw=w_conv_dw, b_conv_dw=b_conv_dw, conv_ln_g=conv_ln_g, conv_ln_b=conv_ln_b,
             w_conv_out=w_conv_out, hgrn_lb_logits=hgrn_lb_logits, hgrn_norm_g=hgrn_norm_g, w_hgrn_out=w_hgrn_out,
             w_out=w_out, ln1_g=ln1_g, ln1_b=ln1_b, w_ffn_in=w_ffn_in, w_ffn_dw=w_ffn_dw, b_ffn_dw=b_ffn_dw,
             w_ffn_out=w_ffn_out, ln2_g=ln2_g, ln2_b=ln2_b)
    m = dict(w_in=m_w_in, w_conv_dw=m_w_conv_dw, b_conv_dw=m_b_conv_dw, conv_ln_g=m_conv_ln_g, conv_ln_b=m_conv_ln_b,
             w_conv_out=m_w_conv_out, hgrn_lb_logits=m_hgrn_lb_logits, hgrn_norm_g=m_hgrn_norm_g,
             w_hgrn_out=m_w_hgrn_out, w_out=m_w_out, ln1_g=m_ln1_g, ln1_b=m_ln1_b, w_ffn_in=m_w_ffn_in,
             w_ffn_dw=m_w_ffn_dw, b_ffn_dw=m_b_ffn_dw, w_ffn_out=m_w_ffn_out, ln2_g=m_ln2_g, ln2_b=m_ln2_b)
    v = dict(w_in=v_w_in, w_conv_dw=v_w_conv_dw, b_conv_dw=v_b_conv_dw, conv_ln_g=v_conv_ln_g, conv_ln_b=v_conv_ln_b,
             w_conv_out=v_w_conv_out, hgrn_lb_logits=v_hgrn_lb_logits, hgrn_norm_g=v_hgrn_norm_g,
             w_hgrn_out=v_w_hgrn_out, w_out=v_w_out, ln1_g=v_ln1_g, ln1_b=v_ln1_b, w_ffn_in=v_w_ffn_in,
             w_ffn_dw=v_w_ffn_dw, b_ffn_dw=v_b_ffn_dw, w_ffn_out=v_w_ffn_out, ln2_g=v_ln2_g, ln2_b=v_ln2_b)
    big_names = [n for n, _, _, _ in BIG]
    w2 = {n: a[0] if a.ndim == 3 else a for n, a in w.items()}
    m2 = {n: a[0] if a.ndim == 3 else a for n, a in m.items()}
    v2 = {n: a[0] if a.ndim == 3 else a for n, a in v.items()}

    sel = jnp.stack([2 * lax.axis_index("x") + lax.axis_index("y"), lax.axis_index("c")]).astype(jnp.int32)
    sharded = lambda ax, p, n: ("cols", p, n) if ax == 1 else ("rows", p, n)
    across = lambda ax, p, n: ("rows", p, n) if ax == 1 else ("cols", p, n)

    placed = [_windowed(lambda a: (a,), sel, [(w2[n], None)], [((r, c), BF16, sharded(ax, 0, N_CHIPS))],
                        window=w2[n].shape, name="cast_" + n)[0] for n, r, c, ax in BIG]
    gathered = _gather_weights(placed, [w2["w_conv_dw"], w2["w_ffn_dw"]])
    wi, wco, wh, wo, wfi, wfo = gathered[:6]
    wcd = jnp.transpose(gathered[6], (1, 0, 2)).reshape(CONV_KERNEL, CONV_DIM)
    wfd = jnp.transpose(gathered[7], (1, 0, 2)).reshape(FFN_KERNEL, D_FF)

    loss_part, grad_x, big_grads, small_grads = _local_step(
        x[0], loss_target[0], wi, wco, wh, wo, wfi, wfo, wcd, w2["b_conv_dw"], w2["conv_ln_g"], w2["conv_ln_b"],
        w2["hgrn_lb_logits"], w2["hgrn_norm_g"], w2["ln1_g"], w2["ln1_b"], wfd, w2["b_ffn_dw"],
        w2["ln2_g"], w2["ln2_b"])
    loss = lax.psum(loss_part[0, 0], ("x", "y", "c"))

    got = _sibling_exchange(big_grads)
    chip_sums = [_windowed(lambda a, b: (a + b,), sel, [(g_, across(ax, 1, 2)), (h_, None)],
                           [(_half_shape(r, c, ax), BF16, None)], window=_half_shape(r, c, ax),
                           name="chip_sum_" + n)[0]
                 for (n, r, c, ax), g_, h_ in zip(BIG, big_grads, got)]
    recv = _chip_exchange(chip_sums)
    add4 = lambda a, b0, b1, b2: (a.astype(F32) + b0.astype(F32) + b1.astype(F32) + b2.astype(F32),)
    half_filled = [_windowed(add4, sel, [(cs_, sharded(ax, 0, N_CHIPS))] + [(r_, None) for r_ in recv_],
                             [((r, c // N_CHIPS) if ax == 1 else (r // N_CHIPS, c), F32, across(ax, 1, 2))],
                             window=_piece_shape(r, c, ax), name="shard_sum_" + n)[0]
                   for (n, r, c, ax), cs_, recv_ in zip(BIG, chip_sums, recv)]
    shard_grads = dict(zip(big_names, _sibling_assemble(half_filled)))

    small_shapes = [small_grads[n].shape for n in SMALL_ORDER]
    reduced = dict(zip(SMALL_ORDER, _unpack(_all_reduce_small(_pack([small_grads[n] for n in SMALL_ORDER])),
                                            small_shapes)))
    shard = 2 * lax.axis_index("x") + lax.axis_index("y")
    grads = dict(shard_grads)
    for n in REPLICATED_SMALL:
        grads[n] = reduced[n]
    grads["w_conv_dw"] = lax.dynamic_slice_in_dim(reduced["w_conv_dw"], shard * (CONV_DIM // N_CHIPS),
                                                  CONV_DIM // N_CHIPS, axis=1)
    grads["w_ffn_dw"] = lax.dynamic_slice_in_dim(reduced["w_ffn_dw"], shard * (D_FF // N_CHIPS),
                                                 D_FF // N_CHIPS, axis=1)

    delta, new_m, new_v = {}, {}, {}
    for n in big_names + ["w_conv_dw", "w_ffn_dw"]:
        delta[n], new_m[n], new_v[n] = _adamw(w2[n], grads[n], m2[n], v2[n], name="adamw_" + n)
    rep_shapes = [w2[n].shape for n in REPLICATED_SMALL]
    packed = _adamw(*[_pack([src[n] for n in REPLICATED_SMALL]) for src in (w2, grads, m2, v2)], name="adamw_small")
    for dst, pk in zip((delta, new_m, new_v), packed):
        for n, a in zip(REPLICATED_SMALL, _unpack(pk, rep_shapes)):
            dst[n] = a

    def shaped(d):
        return [d[n].reshape(w[n].shape) for n in WEIGHT_ORDER]

    return (loss, grad_x[None], *shaped(grads), *shaped(delta), *shaped(new_m), *shaped(new_v))
```

```python
import jax
import jax.numpy as jnp
from jax import lax
from jax.experimental import pallas as pl
from jax.experimental.pallas import tpu as pltpu

F32 = jnp.float32
BF16 = jnp.bfloat16

D_MODEL = 1024
CONV_DIM = 512
CONV_KERNEL = 31
HGRN_DIM = 1024
HGRN_HEADS = 8
HEAD_DIM = 128
CHUNK = 64
SUB = 16
N_SUB = CHUNK // SUB
D_FF = 2816
FFN_KERNEL = 3
IN_COLS = 7168
LN_EPS = 1e-5
RMS_EPS = 1e-6
ALPHA = 2.0 ** 0.25
GELU_C = 0.7978845608028654
GELU_A = 0.044715

ADAM_LR = 0.001
ADAM_B1 = 0.9
ADAM_B2 = 0.999
ADAM_EPS = 1e-08
ADAM_WD = 0.01
ADAM_STEP = 10
ADAM_BC1 = 1.0 - ADAM_B1 ** ADAM_STEP
ADAM_BC2 = 1.0 - ADAM_B2 ** ADAM_STEP

N_CHIPS = 4
N_DEV = 8
ROW_BLOCK = 32
CONV_HALO = 32
FFN_HALO = 8
MESH = pl.DeviceIdType.MESH
ANY = pl.BlockSpec(memory_space=pl.ANY)


def _dot(a, b):
    return jnp.dot(a, b, preferred_element_type=F32)


def _dot_nt(a, b):
    return lax.dot_general(a, b, (((1,), (1,)), ((), ())), preferred_element_type=F32)


def _dot_tn(a, b):
    return lax.dot_general(a, b, (((0,), (0,)), ((), ())), preferred_element_type=F32)


def _sigmoid(z):
    return jax.nn.sigmoid(z)


def _silu_grad(z, s):
    return s * (1.0 + z * (1.0 - s))


def _gelu_and_grad(u):
    inner = GELU_C * (u + GELU_A * u * u * u)
    th = jnp.tanh(inner)
    g = 0.5 * u * (1.0 + th)
    dg = 0.5 * (1.0 + th) + 0.5 * u * (1.0 - th * th) * GELU_C * (1.0 + 3.0 * GELU_A * u * u)
    return g, dg


def _ln_stats(r):
    mu = jnp.mean(r, axis=-1, keepdims=True)
    xc = r - mu
    var = jnp.mean(xc * xc, axis=-1, keepdims=True)
    rstd = lax.rsqrt(var + LN_EPS)
    return xc * rstd, rstd


def _ln_bwd(dy, xhat, rstd, g):
    dxh = dy * g
    m1 = jnp.mean(dxh, axis=-1, keepdims=True)
    m2 = jnp.mean(dxh * xhat, axis=-1, keepdims=True)
    return rstd * (dxh - m1 - xhat * m2)


def _fold8(x):
    acc = x[0:8, :]
    for r in range(8, x.shape[0], 8):
        acc = acc + x[r:r + 8, :]
    return acc


def _call(body, *, name, grid, in_specs, out_specs, out_shape, scratch=(), vmem_mb=32, aliases=None):
    return pl.pallas_call(
        body, name=name, grid=grid, in_specs=in_specs, out_specs=out_specs, out_shape=out_shape,
        scratch_shapes=list(scratch), input_output_aliases=aliases or {},
        compiler_params=pltpu.CompilerParams(
            dimension_semantics=("arbitrary",) * len(grid), vmem_limit_bytes=vmem_mb * 2 ** 20))


def _sds(shape, dtype):
    return jax.ShapeDtypeStruct(shape, dtype)


def _proj(x, w):
    t = x.shape[0]
    tm, tn = min(t, 1024), 1024

    def body(x_ref, w_ref, p_ref, xb_ref):
        @pl.when(pl.program_id(1) == 0)
        def _():
            xb_ref[...] = x_ref[...].astype(BF16)
        p_ref[...] = _dot(xb_ref[...], w_ref[...])

    return _call(
        body, name="proj", grid=(t // tm, IN_COLS // tn),
        in_specs=[pl.BlockSpec((tm, D_MODEL), lambda i, j: (i, 0)),
                  pl.BlockSpec((D_MODEL, tn), lambda i, j: (0, j))],
        out_specs=[pl.BlockSpec((tm, tn), lambda i, j: (i, j)),
                   pl.BlockSpec((tm, D_MODEL), lambda i, j: (i, 0))],
        out_shape=[_sds((t, IN_COLS), F32), _sds((t, D_MODEL), BF16)], vmem_mb=48)(x, w)


def _mm_nn(a, w, *, tm, tn, name, vmem_mb=32):
    t, k = a.shape
    n = w.shape[1]
    tm = min(tm, t)

    def body(a_ref, w_ref, o_ref):
        o_ref[...] = _dot(a_ref[...], w_ref[...])

    return _call(
        body, name=name, grid=(t // tm, n // tn),
        in_specs=[pl.BlockSpec((tm, k), lambda i, j: (i, 0)), pl.BlockSpec((k, tn), lambda i, j: (0, j))],
        out_specs=pl.BlockSpec((tm, tn), lambda i, j: (i, j)),
        out_shape=_sds((t, n), F32), vmem_mb=vmem_mb)(a, w)


def _views(*arrs):
    out = []
    for a in arrs:
        if a.ndim == 2:
            out.append((a, None))
        else:
            out.extend((a, p) for p in range(a.shape[0]))
    return out


def _piece_layout(views, tile):
    starts, counts, total = [], [], 0
    for arr, _ in views:
        width = arr.shape[-1]
        assert width % tile == 0
        starts.append(total)
        counts.append(width // tile)
        total += width // tile
    return starts, counts, total


def _mm_tn(a, views, *, tn, name, tt=512, vmem_mb=32):
    t, m = a.shape
    tt = min(tt, t)
    starts, counts, nj = _piece_layout(views, tn)
    n_views = len(views)

    def body(a_ref, *refs):
        b_refs, o_ref = refs[:n_views], refs[n_views]
        j = pl.program_id(0)

        @pl.when(pl.program_id(1) == 0)
        def _():
            o_ref[...] = jnp.zeros_like(o_ref)

        for b_ref, st, nb, (_, p) in zip(b_refs, starts, counts, views):
            @pl.when((j >= st) & (j < st + nb))
            def _(b_ref=b_ref, p=p):
                blk = b_ref[...] if p is None else b_ref[0]
                o_ref[...] += _dot_tn(a_ref[...], blk)

    def b_spec(st, nb, p):
        def rows(j, k):
            return jnp.where((j >= st) & (j < st + nb), k, 0)

        def cols(j):
            return jnp.clip(j - st, 0, nb - 1)

        if p is None:
            return pl.BlockSpec((tt, tn), lambda j, k: (rows(j, k), cols(j)))
        return pl.BlockSpec((1, tt, tn), lambda j, k: (p, rows(j, k), cols(j)))

    return _call(
        body, name=name, grid=(nj, t // tt),
        in_specs=[pl.BlockSpec((tt, m), lambda j, k: (k, 0))]
        + [b_spec(st, nb, p) for st, nb, (_, p) in zip(starts, counts, views)],
        out_specs=pl.BlockSpec((m, tn), lambda j, k: (0, j)),
        out_shape=_sds((m, nj * tn), F32), vmem_mb=vmem_mb)(a, *[arr for arr, _ in views])


def _mm_nt(views, w, add, *, add_scale, tk, name, tm=512, vmem_mb=32):
    t = views[0][0].shape[-2]
    kout = w.shape[0]
    starts, counts, nk = _piece_layout(views, tk)
    n_views = len(views)

    def body(add_ref, *refs):
        b_refs, w_ref, o_ref = refs[:n_views], refs[n_views], refs[n_views + 1]
        k = pl.program_id(1)

        @pl.when(k == 0)
        def _():
            o_ref[...] = add_scale * add_ref[...]

        for b_ref, st, nb, (_, p) in zip(b_refs, starts, counts, views):
            @pl.when((k >= st) & (k < st + nb))
            def _(b_ref=b_ref, p=p):
                blk = b_ref[...] if p is None else b_ref[0]
                o_ref[...] += _dot_nt(blk, w_ref[...])

    def b_spec(st, nb, p):
        def cols(k):
            return jnp.clip(k - st, 0, nb - 1)

        if p is None:
            return pl.BlockSpec((tm, tk), lambda i, k: (i, cols(k)))
        return pl.BlockSpec((1, tm, tk), lambda i, k: (p, i, cols(k)))

    return _call(
        body, name=name, grid=(t // tm, nk),
        in_specs=[pl.BlockSpec((tm, kout), lambda i, k: (i, 0))]
        + [b_spec(st, nb, p) for st, nb, (_, p) in zip(starts, counts, views)]
        + [pl.BlockSpec((kout, tk), lambda i, k: (0, k))],
        out_specs=pl.BlockSpec((tm, kout), lambda i, k: (i, 0)),
        out_shape=_sds((t, kout), F32), vmem_mb=vmem_mb)(add, *[arr for arr, _ in views], w)


LANES = 128
CONV_FWD_OFFSETS = {k: 2 + k for k in range(CONV_KERNEL)}
CONV_BWD_OFFSETS = {k: CONV_KERNEL - 1 - k for k in range(CONV_KERNEL)}


def _conv_taps(win, offsets):
    n = win.shape[0]
    for b in range(8):
        taps = [k for k, o in offsets.items() if o % 8 == b]
        if not taps:
            continue
        shifted = win if b == 0 else pltpu.roll(win, n - b, 0)
        for k in taps:
            first = offsets[k] - b
            yield k, shifted[first:first + ROW_BLOCK, :]


def _conv_fwd(proj, wcd, bcd, lng, lnb):
    t = proj.shape[0]
    tm = 512

    def body(cv_ref, cg_ref, w_ref, b_ref, g_ref, be_ref, cc_ref, cs_ref, ext_ref):
        i = pl.program_id(0)

        @pl.when(i == 0)
        def _():
            ext_ref[0:CONV_HALO, :] = jnp.zeros((CONV_HALO, CONV_DIM), F32)

        @pl.when(i > 0)
        def _():
            ext_ref[0:CONV_HALO, :] = ext_ref[tm:tm + CONV_HALO, :]

        ext_ref[CONV_HALO:CONV_HALO + tm, :] = cv_ref[...] * _sigmoid(cg_ref[...])

        def block(r, carry):
            r0 = pl.multiple_of(r * ROW_BLOCK, ROW_BLOCK)
            groups = []
            for g in range(CONV_DIM // LANES):
                lanes = slice(g * LANES, (g + 1) * LANES)
                win = ext_ref[pl.ds(r0, ROW_BLOCK + CONV_HALO), lanes]
                acc = jnp.broadcast_to(b_ref[:, lanes], (ROW_BLOCK, LANES))
                for k, rows_k in _conv_taps(win, CONV_FWD_OFFSETS):
                    acc = acc + w_ref[k:k + 1, lanes] * rows_k
                groups.append(acc)
            acc = jnp.concatenate(groups, axis=1)
            cc_ref[pl.ds(r0, ROW_BLOCK), :] = acc
            xhat, _ = _ln_stats(acc)
            a = xhat * g_ref[...] + be_ref[...]
            cs_ref[pl.ds(r0, ROW_BLOCK), :] = (a * _sigmoid(a)).astype(BF16)
            return carry

        lax.fori_loop(0, tm // ROW_BLOCK, block, 0)

    vec = pl.BlockSpec((1, CONV_DIM), lambda i: (0, 0))
    return _call(
        body, name="conv_fwd", grid=(t // tm,),
        in_specs=[pl.BlockSpec((tm, CONV_DIM), lambda i: (i, 0)), pl.BlockSpec((tm, CONV_DIM), lambda i: (i, 1)),
                  pl.BlockSpec((CONV_KERNEL, CONV_DIM), lambda i: (0, 0)), vec, vec, vec],
        out_specs=[pl.BlockSpec((tm, CONV_DIM), lambda i: (i, 0)), pl.BlockSpec((tm, CONV_DIM), lambda i: (i, 0))],
        out_shape=[_sds((t, CONV_DIM), F32), _sds((t, CONV_DIM), BF16)],
        scratch=[pltpu.VMEM((tm + CONV_HALO, CONV_DIM), F32)])(proj, proj, wcd, bcd, lng, lnb)


def _conv_bwd_a(dcs, cc, proj, lng, lnb):
    t = proj.shape[0]
    tm = 512
    nt = t // tm

    def body(dcs_ref, cc_ref, cv_ref, cg_ref, g_ref, be_ref,
             dcc_ref, dw_ref, db_ref, dg_ref, dbe_ref, ext_ref, accw_ref, acc3_ref):
        i = pl.program_id(0)

        @pl.when(i == 0)
        def _():
            ext_ref[0:CONV_HALO, :] = jnp.zeros((CONV_HALO, CONV_DIM), F32)
            accw_ref[...] = jnp.zeros_like(accw_ref)
            acc3_ref[...] = jnp.zeros_like(acc3_ref)

        @pl.when(i > 0)
        def _():
            ext_ref[0:CONV_HALO, :] = ext_ref[tm:tm + CONV_HALO, :]

        ext_ref[CONV_HALO:CONV_HALO + tm, :] = cv_ref[...] * _sigmoid(cg_ref[...])

        def block(r, carry):
            r0 = pl.multiple_of(r * ROW_BLOCK, ROW_BLOCK)
            rows = pl.ds(r0, ROW_BLOCK)
            xhat, rstd = _ln_stats(cc_ref[rows, :])
            a = xhat * g_ref[...] + be_ref[...]
            sg = _sigmoid(a)
            da = dcs_ref[rows, :] * _silu_grad(a, sg)
            acc3_ref[8:16, :] += _fold8(da * xhat)
            acc3_ref[16:24, :] += _fold8(da)
            dcc = _ln_bwd(da, xhat, rstd, g_ref[...])
            dcc_ref[rows, :] = dcc
            acc3_ref[0:8, :] += _fold8(dcc)
            for g in range(CONV_DIM // LANES):
                lanes = slice(g * LANES, (g + 1) * LANES)
                win = ext_ref[pl.ds(r0, ROW_BLOCK + CONV_HALO), lanes]
                dcc_g = dcc[:, lanes]
                for k, rows_k in _conv_taps(win, CONV_FWD_OFFSETS):
                    accw_ref[8 * k:8 * k + 8, lanes] += _fold8(dcc_g * rows_k)
            return carry

        lax.fori_loop(0, tm // ROW_BLOCK, block, 0)

        @pl.when(i == nt - 1)
        def _():
            for k in range(CONV_KERNEL):
                dw_ref[k:k + 1, :] = jnp.sum(accw_ref[8 * k:8 * k + 8, :], axis=0, keepdims=True)
            db_ref[...] = jnp.sum(acc3_ref[0:8, :], axis=0, keepdims=True)
            dg_ref[...] = jnp.sum(acc3_ref[8:16, :], axis=0, keepdims=True)
            dbe_ref[...] = jnp.sum(acc3_ref[16:24, :], axis=0, keepdims=True)

    vec = pl.BlockSpec((1, CONV_DIM), lambda i: (0, 0))
    tile = pl.BlockSpec((tm, CONV_DIM), lambda i: (i, 0))
    return _call(
        body, name="conv_bwd_a", grid=(nt,),
        in_specs=[tile, tile, tile, pl.BlockSpec((tm, CONV_DIM), lambda i: (i, 1)), vec, vec],
        out_specs=[tile, pl.BlockSpec((CONV_KERNEL, CONV_DIM), lambda i: (0, 0)), vec, vec, vec],
        out_shape=[_sds((t, CONV_DIM), F32), _sds((CONV_KERNEL, CONV_DIM), F32),
                   _sds((1, CONV_DIM), F32), _sds((1, CONV_DIM), F32), _sds((1, CONV_DIM), F32)],
        scratch=[pltpu.VMEM((tm + CONV_HALO, CONV_DIM), F32),
                 pltpu.VMEM((8 * CONV_KERNEL, CONV_DIM), F32),
                 pltpu.VMEM((24, CONV_DIM), F32)])(dcs, cc, proj, proj, lng, lnb)


def _conv_bwd_b(dcc, proj, wcd):
    t = proj.shape[0]
    tm = 512
    nt = t // tm

    def body(dcc_ref, cv_ref, cg_ref, w_ref, out_ref, ext_ref):
        i = pl.program_id(0)

        @pl.when(i == 0)
        def _():
            ext_ref[tm:tm + CONV_HALO, :] = jnp.zeros((CONV_HALO, CONV_DIM), F32)

        @pl.when(i > 0)
        def _():
            ext_ref[tm:tm + CONV_HALO, :] = ext_ref[0:CONV_HALO, :]

        ext_ref[0:tm, :] = dcc_ref[...]

        def block(r, carry):
            r0 = pl.multiple_of(r * ROW_BLOCK, ROW_BLOCK)
            rows = pl.ds(r0, ROW_BLOCK)
            for g in range(CONV_DIM // LANES):
                lanes = slice(g * LANES, (g + 1) * LANES)
                gate_lanes = slice(CONV_DIM + g * LANES, CONV_DIM + (g + 1) * LANES)
                win = ext_ref[pl.ds(r0, ROW_BLOCK + CONV_HALO), lanes]
                acc = jnp.zeros((ROW_BLOCK, LANES), F32)
                for k, rows_k in _conv_taps(win, CONV_BWD_OFFSETS):
                    acc = acc + w_ref[k:k + 1, lanes] * rows_k
                sg = _sigmoid(cg_ref[rows, lanes])
                out_ref[rows, lanes] = (acc * sg).astype(BF16)
                out_ref[rows, gate_lanes] = (acc * cv_ref[rows, lanes] * sg * (1.0 - sg)).astype(BF16)
            return carry

        lax.fori_loop(0, tm // ROW_BLOCK, block, 0)

    rev = lambda i: (nt - 1 - i, 0)
    return _call(
        body, name="conv_bwd_b", grid=(nt,),
        in_specs=[pl.BlockSpec((tm, CONV_DIM), rev), pl.BlockSpec((tm, CONV_DIM), rev),
                  pl.BlockSpec((tm, CONV_DIM), lambda i: (nt - 1 - i, 1)),
                  pl.BlockSpec((CONV_KERNEL, CONV_DIM), lambda i: (0, 0))],
        out_specs=pl.BlockSpec((tm, 2 * CONV_DIM), rev),
        out_shape=_sds((t, 2 * CONV_DIM), BF16),
        scratch=[pltpu.VMEM((tm + CONV_HALO, CONV_DIM), F32)])(dcc, proj, proj, wcd)


def _lower_bound(lg_ref):
    a0, a1 = lg_ref[0:1, :], lg_ref[1:2, :]
    m = jnp.maximum(a0, a1)
    e0, e1 = jnp.exp(a0 - m), jnp.exp(a1 - m)
    return e0 / (e0 + e1)


def _block_tri(n, upper):
    r = lax.broadcasted_iota(jnp.int32, (n, n), 0)
    c = lax.broadcasted_iota(jnp.int32, (n, n), 1)
    same = (r >> 6) == (c >> 6)
    tri = (c >= r) if upper else (c <= r)
    return jnp.where(same & tri, 1.0, 0.0).astype(BF16)


def _block_cumsum(x, tri):
    w = x.shape[1]
    hi = x.astype(BF16)
    r1 = x - hi.astype(F32)
    mid = r1.astype(BF16)
    lo = (r1 - mid.astype(F32)).astype(BF16)
    y = _dot(tri, jnp.concatenate([hi, mid, lo], axis=1))
    return y[:, 0:w] + y[:, w:2 * w] + y[:, 2 * w:3 * w]


def _first_step():
    return (pl.program_id(0) == 0) & (pl.program_id(1) == 0)


def _pick_row(x, row_ids, r):
    return jnp.sum(jnp.where(row_ids == r, x, 0.0), axis=0, keepdims=True)


def _chunk_terms(qc, kc, bc):
    row = lax.broadcasted_iota(jnp.int32, (CHUNK, 1), 0)
    blk = row >> 4
    betas = [jnp.zeros((1, HEAD_DIM), F32)] + [_pick_row(bc, row, SUB * i - 1) for i in range(1, N_SUB)]
    brow = jnp.zeros_like(bc)
    for i in range(1, N_SUB):
        brow = jnp.where(blk == i, betas[i], brow)
    qscale = jnp.exp(bc - brow)
    qs = qc * qscale
    qcat = jnp.concatenate([jnp.where(blk == i, qs, 0.0) for i in range(N_SUB)], axis=1)
    kscales = []
    for i in range(N_SUB):
        valid = row < SUB * (i + 1)
        kscales.append(jnp.where(valid, jnp.exp(jnp.where(valid, betas[i] - bc, 0.0)), 0.0))
    kcat = jnp.concatenate([kc * ks for ks in kscales], axis=1)
    b_last = _pick_row(bc, row, CHUNK - 1)
    return dict(row=row, blk=blk, qscale=qscale, qcat=qcat, kscales=kscales, kcat=kcat,
                eb=jnp.exp(bc), e_last=jnp.exp(b_last), ktscale=jnp.exp(b_last - bc))


def _causal(shape_rows_first):
    r = lax.broadcasted_iota(jnp.int32, (CHUNK, CHUNK), 0)
    c = lax.broadcasted_iota(jnp.int32, (CHUNK, CHUNK), 1)
    return (c <= r) if shape_rows_first else (r <= c)


def _hgrn_specs(tm, tile_of):
    col = lambda base: (lambda h, i: (tile_of(i), base + h))
    return [pl.BlockSpec((tm, HEAD_DIM), col(8)), pl.BlockSpec((tm, HEAD_DIM), col(16)),
            pl.BlockSpec((tm, HEAD_DIM), col(24)), pl.BlockSpec((tm, HEAD_DIM), col(32)),
            pl.BlockSpec((2, HEAD_DIM), lambda h, i: (0, h)), pl.BlockSpec((1, HEAD_DIM), lambda h, i: (0, h))]


def _hgrn_fwd(proj, logits, ng):
    t = proj.shape[0]
    tm = 512
    nc = tm // CHUNK
    nt = t // tm

    def body(zq_ref, zf_ref, v_ref, zg_ref, lg_ref, ng_ref, o_ref, og_ref, st_ref,
             s_scr, q_scr, k_scr, b_scr, tri_scr):
        @pl.when(_first_step())
        def _():
            tri_scr[...] = _block_tri(tm, upper=False)

        @pl.when(pl.program_id(1) == 0)
        def _():
            s_scr[...] = jnp.zeros_like(s_scr)

        lb = _lower_bound(lg_ref)
        zf = zf_ref[...]
        f = lb + (1.0 - lb) * _sigmoid(zf)
        k_scr[...] = (1.0 - lb) * _sigmoid(-zf)
        zq = zq_ref[...]
        q_scr[...] = zq * _sigmoid(zq)
        b_scr[...] = _block_cumsum(jnp.log(f), tri_scr[...])

        st = s_scr[...]
        for c in range(nc):
            rows = pl.ds(c * CHUNK, CHUNK)
            qc, kc, bc, vc = q_scr[rows, :], k_scr[rows, :], b_scr[rows, :], v_ref[rows, :]
            st_ref[0, c] = st
            tr = _chunk_terms(qc, kc, bc)
            a = jnp.where(_causal(True), _dot_nt(tr["qcat"].astype(BF16), tr["kcat"].astype(BF16)), 0.0)
            vb = vc.astype(BF16)
            o_ref[rows, :] = _dot(a.astype(BF16), vb) + _dot_nt((qc * tr["eb"]).astype(BF16), st.astype(BF16))
            st = st * tr["e_last"] + _dot_tn(vb, (kc * tr["ktscale"]).astype(BF16))
        s_scr[...] = st

        o = o_ref[...]
        rinv = lax.rsqrt(jnp.mean(o * o, axis=-1, keepdims=True) + RMS_EPS)
        zg = zg_ref[...]
        og_ref[...] = (o * rinv * ng_ref[...] * (zg * _sigmoid(zg))).astype(BF16)

    tile = pl.BlockSpec((tm, HEAD_DIM), lambda h, i: (i, h))
    return _call(
        body, name="hgrn_fwd", grid=(HGRN_HEADS, nt),
        in_specs=_hgrn_specs(tm, lambda i: i),
        out_specs=[tile, tile, pl.BlockSpec((1, nc, HEAD_DIM, HEAD_DIM), lambda h, i: (h, i, 0, 0))],
        out_shape=[_sds((t, HGRN_DIM), F32), _sds((t, HGRN_DIM), BF16),
                   _sds((HGRN_HEADS, t // CHUNK, HEAD_DIM, HEAD_DIM), F32)],
        scratch=[pltpu.VMEM((HEAD_DIM, HEAD_DIM), F32)] + [pltpu.VMEM((tm, HEAD_DIM), F32)] * 3
        + [pltpu.VMEM((tm, tm), BF16)],
    )(proj, proj, proj, proj, logits, ng)


def _hgrn_bwd(dog, o, states, proj, logits, ng):
    t = proj.shape[0]
    tm = 512
    nc = tm // CHUNK
    nt = t // tm

    def body(dog_ref, o_ref, st_ref, zq_ref, zf_ref, v_ref, zg_ref, lg_ref, ng_ref,
             dp_ref, dlg_ref, dng_ref,
             ds_scr, q_scr, k_scr, b_scr, do_scr, dq_scr, dk_scr, dv_scr, db_scr, dlb_scr, tri_scr):
        i = pl.program_id(1)

        @pl.when(_first_step())
        def _():
            tri_scr[0] = _block_tri(tm, upper=False)
            tri_scr[1] = _block_tri(tm, upper=True)

        @pl.when(i == 0)
        def _():
            ds_scr[...] = jnp.zeros_like(ds_scr)
            dlb_scr[...] = jnp.zeros_like(dlb_scr)
            dng_ref[...] = jnp.zeros_like(dng_ref)

        lb = _lower_bound(lg_ref)
        ng_row = ng_ref[...]
        o = o_ref[...]
        rinv = lax.rsqrt(jnp.mean(o * o, axis=-1, keepdims=True) + RMS_EPS)
        ohat = o * rinv
        zg = zg_ref[...]
        sg = _sigmoid(zg)
        dog_v = dog_ref[...]
        don = dog_v * (zg * sg)
        dp_ref[3] = (dog_v * (ohat * ng_row) * _silu_grad(zg, sg)).astype(BF16)
        dng_ref[...] += jnp.sum(don * ohat, axis=0, keepdims=True)
        dohat = don * ng_row
        do_scr[...] = rinv * (dohat - ohat * jnp.mean(dohat * ohat, axis=-1, keepdims=True))

        zf = zf_ref[...]
        s = _sigmoid(zf)
        s_neg = _sigmoid(-zf)
        f = lb + (1.0 - lb) * s
        k_scr[...] = (1.0 - lb) * s_neg
        zq = zq_ref[...]
        sq = _sigmoid(zq)
        q_scr[...] = zq * sq
        b_scr[...] = _block_cumsum(jnp.log(f), tri_scr[0])

        dst = ds_scr[...]
        for c in reversed(range(nc)):
            rows = pl.ds(c * CHUNK, CHUNK)
            qc, kc, bc, vc, doc = q_scr[rows, :], k_scr[rows, :], b_scr[rows, :], v_ref[rows, :], do_scr[rows, :]
            st = st_ref[0, c]
            tr = _chunk_terms(qc, kc, bc)
            qcb, kcb = tr["qcat"].astype(BF16), tr["kcat"].astype(BF16)
            dob, vb, dstb = doc.astype(BF16), vc.astype(BF16), dst.astype(BF16)
            a_t = jnp.where(_causal(False), _dot_nt(kcb, qcb), 0.0)
            da = jnp.where(_causal(True), _dot_nt(dob, vb), 0.0)
            da_t = jnp.where(_causal(False), _dot_nt(vb, dob), 0.0)
            dqcat = _dot(da.astype(BF16), kcb)
            dkcat = _dot(da_t.astype(BF16), qcb)
            kt = kc * tr["ktscale"]
            dv_scr[rows, :] = _dot(a_t.astype(BF16), dob) + _dot_nt(kt.astype(BF16), dstb)
            dq = jnp.zeros_like(qc)
            dk = jnp.zeros_like(kc)
            db = jnp.zeros_like(bc)
            for n in range(N_SUB):
                lanes = slice(n * HEAD_DIM, (n + 1) * HEAD_DIM)
                dq = dq + jnp.where(tr["blk"] == n, dqcat[:, lanes], 0.0)
                dk = dk + dkcat[:, lanes] * tr["kscales"][n]
                db = db + (qcb[:, lanes].astype(F32) * dqcat[:, lanes] - kcb[:, lanes].astype(F32) * dkcat[:, lanes])
            dq_inter = _dot(dob, st.astype(BF16)) * tr["eb"]
            dkt = _dot(vb, dstb)
            dk_inter = dkt * tr["ktscale"]
            extra = (jnp.sum(dkt * kt, axis=0, keepdims=True)
                     + tr["e_last"] * jnp.sum(dst * st, axis=0, keepdims=True))
            dq_scr[rows, :] = dq * tr["qscale"] + dq_inter
            dk_scr[rows, :] = dk + dk_inter
            db_scr[rows, :] = (db + qc * dq_inter - kc * dk_inter
                               + jnp.where(tr["row"] == CHUNK - 1, extra, 0.0))
            dst = dst * tr["e_last"] + _dot_tn(dob, (qc * tr["eb"]).astype(BF16))
        ds_scr[...] = dst

        dlogf = _block_cumsum(db_scr[...], tri_scr[1])
        df = dlogf / f - dk_scr[...]
        dp_ref[0] = (dq_scr[...] * _silu_grad(zq, sq)).astype(BF16)
        dp_ref[1] = (df * (1.0 - lb) * s * (1.0 - s)).astype(BF16)
        dp_ref[2] = dv_scr[...].astype(BF16)
        dlb_scr[...] += jnp.sum(df * s_neg, axis=0, keepdims=True)

        @pl.when(i == nt - 1)
        def _():
            dlogit = dlb_scr[...] * lb * (1.0 - lb)
            dlg_ref[0:1, :] = dlogit
            dlg_ref[1:2, :] = -dlogit

    rev = lambda i: nt - 1 - i
    tile = pl.BlockSpec((tm, HEAD_DIM), lambda h, i: (rev(i), h))
    return _call(
        body, name="hgrn_bwd", grid=(HGRN_HEADS, nt),
        in_specs=[tile, tile, pl.BlockSpec((1, nc, HEAD_DIM, HEAD_DIM), lambda h, i: (h, rev(i), 0, 0))]
        + _hgrn_specs(tm, rev),
        out_specs=[pl.BlockSpec((4, tm, HEAD_DIM), lambda h, i: (0, rev(i), h)),
                   pl.BlockSpec((2, HEAD_DIM), lambda h, i: (0, h)),
                   pl.BlockSpec((1, HEAD_DIM), lambda h, i: (0, h))],
        out_shape=[_sds((4, t, HGRN_DIM), BF16), _sds((2, HGRN_DIM), F32), _sds((1, HGRN_DIM), F32)],
        scratch=[pltpu.VMEM((HEAD_DIM, HEAD_DIM), F32)] + [pltpu.VMEM((tm, HEAD_DIM), F32)] * 8
        + [pltpu.VMEM((1, HEAD_DIM), F32), pltpu.VMEM((2, tm, tm), BF16)],
    )(dog, o, states, proj, proj, proj, proj, logits, ng)


def _merge_fwd(cs, og, proj, x, wco, wh, wo, g1, b1):
    t = x.shape[0]
    tm = 256

    def body(cs_ref, og_ref, m0_ref, m1_ref, x_ref, wco_ref, wh_ref, wo_ref, g_ref, b_ref,
             y_ref, mixed_ref, r1_ref, x1_ref, x1b_ref):
        yc = _dot(cs_ref[...], wco_ref[...])
        yh = _dot(og_ref[...], wh_ref[...])
        y_ref[0] = yc
        y_ref[1] = yh
        mixed = (_sigmoid(m0_ref[...]) * yc + _sigmoid(m1_ref[...]) * yh).astype(BF16)
        mixed_ref[...] = mixed
        r1 = ALPHA * x_ref[...] + _dot(mixed, wo_ref[...])
        r1_ref[...] = r1
        xhat, _ = _ln_stats(r1)
        x1 = xhat * g_ref[...] + b_ref[...]
        x1_ref[...] = x1
        x1b_ref[...] = x1.astype(BF16)

    row = lambda w: pl.BlockSpec((tm, w), lambda i: (i, 0))
    full = lambda a: pl.BlockSpec(a.shape, lambda i: (0, 0))
    return _call(
        body, name="merge_fwd", grid=(t // tm,),
        in_specs=[row(CONV_DIM), row(HGRN_DIM),
                  pl.BlockSpec((tm, D_MODEL), lambda i: (i, 5)), pl.BlockSpec((tm, D_MODEL), lambda i: (i, 6)),
                  row(D_MODEL), full(wco), full(wh), full(wo), full(g1), full(b1)],
        out_specs=[pl.BlockSpec((2, tm, D_MODEL), lambda i: (0, i, 0)), row(D_MODEL), row(D_MODEL),
                   row(D_MODEL), row(D_MODEL)],
        out_shape=[_sds((2, t, D_MODEL), F32), _sds((t, D_MODEL), BF16), _sds((t, D_MODEL), F32),
                   _sds((t, D_MODEL), F32), _sds((t, D_MODEL), BF16)],
        vmem_mb=48)(cs, og, proj, proj, x, wco, wh, wo, g1, b1)


def _merge_bwd(dr1b, ycat, proj, wo, wco, wh):
    t = dr1b.shape[0]
    tm = 256

    def body(dr_ref, y_ref, m0_ref, m1_ref, wo_ref, wco_ref, wh_ref, dpm_ref, dy_ref, dcs_ref, dog_ref):
        dmixed = _dot_nt(dr_ref[...], wo_ref[...])
        g0 = _sigmoid(m0_ref[...])
        g1 = _sigmoid(m1_ref[...])
        dpm_ref[0] = (dmixed * y_ref[0] * g0 * (1.0 - g0)).astype(BF16)
        dpm_ref[1] = (dmixed * y_ref[1] * g1 * (1.0 - g1)).astype(BF16)
        dyc = (dmixed * g0).astype(BF16)
        dyh = (dmixed * g1).astype(BF16)
        dy_ref[0] = dyc
        dy_ref[1] = dyh
        dcs_ref[...] = _dot_nt(dyc, wco_ref[...])
        dog_ref[...] = _dot_nt(dyh, wh_ref[...])

    row = lambda w: pl.BlockSpec((tm, w), lambda i: (i, 0))
    pair = pl.BlockSpec((2, tm, D_MODEL), lambda i: (0, i, 0))
    full = lambda a: pl.BlockSpec(a.shape, lambda i: (0, 0))
    return _call(
        body, name="merge_bwd", grid=(t // tm,),
        in_specs=[row(D_MODEL), pair,
                  pl.BlockSpec((tm, D_MODEL), lambda i: (i, 5)), pl.BlockSpec((tm, D_MODEL), lambda i: (i, 6)),
                  full(wo), full(wco), full(wh)],
        out_specs=[pair, pair, row(CONV_DIM), row(HGRN_DIM)],
        out_shape=[_sds((2, t, D_MODEL), BF16), _sds((2, t, D_MODEL), BF16),
                   _sds((t, CONV_DIM), F32), _sds((t, HGRN_DIM), F32)],
        vmem_mb=48)(dr1b, ycat, proj, proj, wo, wco, wh)


def _ffn_conv3(win, w_ref, off):
    return (w_ref[0:1, :] * win[off:off + ROW_BLOCK, :] + w_ref[1:2, :] * win[off + 1:off + 1 + ROW_BLOCK, :]
            + w_ref[2:3, :] * win[off + 2:off + 2 + ROW_BLOCK, :])


def _ffn_mid(z, wfd, bfd):
    t = z.shape[0]
    tm = 256

    def body(u_ref, gv_ref, w_ref, b_ref, h_ref, ext_ref):
        i = pl.program_id(0)

        @pl.when(i == 0)
        def _():
            ext_ref[0:FFN_HALO, :] = jnp.zeros((FFN_HALO, D_FF), F32)

        @pl.when(i > 0)
        def _():
            ext_ref[0:FFN_HALO, :] = ext_ref[tm:tm + FFN_HALO, :]

        ext_ref[FFN_HALO:FFN_HALO + tm, :] = u_ref[...]

        def block(r, carry):
            r0 = pl.multiple_of(r * ROW_BLOCK, ROW_BLOCK)
            rows = pl.ds(r0, ROW_BLOCK)
            win = ext_ref[pl.ds(r0, ROW_BLOCK + FFN_HALO), :]
            uc = _ffn_conv3(win, w_ref, FFN_HALO - 2) + b_ref[...]
            g, _ = _gelu_and_grad(uc)
            h_ref[rows, :] = (g * gv_ref[rows, :]).astype(BF16)
            return carry

        lax.fori_loop(0, tm // ROW_BLOCK, block, 0)

    return _call(
        body, name="ffn_mid", grid=(t // tm,),
        in_specs=[pl.BlockSpec((tm, D_FF), lambda i: (i, 0)), pl.BlockSpec((tm, D_FF), lambda i: (i, 1)),
                  pl.BlockSpec((FFN_KERNEL, D_FF), lambda i: (0, 0)), pl.BlockSpec((1, D_FF), lambda i: (0, 0))],
        out_specs=pl.BlockSpec((tm, D_FF), lambda i: (i, 0)),
        out_shape=_sds((t, D_FF), BF16),
        scratch=[pltpu.VMEM((tm + FFN_HALO, D_FF), F32)], vmem_mb=40)(z, z, wfd, bfd)


def _ffn_out_loss(hmid, x1, target, wfo, g2, b2):
    t = x1.shape[0]
    tm = 256
    inv_n = 1.0 / D_MODEL

    def body(h_ref, x1_ref, tg_ref, w_ref, g_ref, b_ref, dr_ref, drb_ref, loss_ref, dg_ref, db_ref):
        @pl.when(pl.program_id(0) == 0)
        def _():
            loss_ref[...] = jnp.zeros_like(loss_ref)
            dg_ref[...] = jnp.zeros_like(dg_ref)
            db_ref[...] = jnp.zeros_like(db_ref)

        r2 = ALPHA * x1_ref[...] + _dot(h_ref[...], w_ref[...])
        xhat, rstd = _ln_stats(r2)
        err = xhat * g_ref[...] + b_ref[...] - tg_ref[...]
        loss_ref[...] += 0.5 * inv_n * jnp.sum(err * err)
        dy = err * inv_n
        dg_ref[...] += jnp.sum(dy * xhat, axis=0, keepdims=True)
        db_ref[...] += jnp.sum(dy, axis=0, keepdims=True)
        dr = _ln_bwd(dy, xhat, rstd, g_ref[...])
        dr_ref[...] = dr
        drb_ref[...] = dr.astype(BF16)

    row = lambda w: pl.BlockSpec((tm, w), lambda i: (i, 0))
    vec = pl.BlockSpec((1, D_MODEL), lambda i: (0, 0))
    return _call(
        body, name="ffn_out_loss", grid=(t // tm,),
        in_specs=[row(D_FF), row(D_MODEL), row(D_MODEL), pl.BlockSpec((D_FF, D_MODEL), lambda i: (0, 0)), vec, vec],
        out_specs=[row(D_MODEL), row(D_MODEL), pl.BlockSpec((1, 128), lambda i: (0, 0)), vec, vec],
        out_shape=[_sds((t, D_MODEL), F32), _sds((t, D_MODEL), BF16), _sds((1, 128), F32),
                   _sds((1, D_MODEL), F32), _sds((1, D_MODEL), F32)],
        vmem_mb=40)(hmid, x1, target, wfo, g2, b2)


def _ffn_bwd_a(dr2b, z, wfo, wfd, bfd):
    t = z.shape[0]
    tm = 256
    nt = t // tm

    def body(dr_ref, u_ref, gv_ref, wfo_ref, w_ref, b_ref, dgv_ref, duc_ref, dw_ref, db_ref,
             ext_ref, dh_ref, acc_ref):
        i = pl.program_id(0)

        @pl.when(i == 0)
        def _():
            ext_ref[0:FFN_HALO, :] = jnp.zeros((FFN_HALO, D_FF), F32)
            acc_ref[...] = jnp.zeros_like(acc_ref)

        @pl.when(i > 0)
        def _():
            ext_ref[0:FFN_HALO, :] = ext_ref[tm:tm + FFN_HALO, :]

        ext_ref[FFN_HALO:FFN_HALO + tm, :] = u_ref[...]
        dh_ref[...] = _dot_nt(dr_ref[...], wfo_ref[...])

        def block(r, carry):
            r0 = pl.multiple_of(r * ROW_BLOCK, ROW_BLOCK)
            rows = pl.ds(r0, ROW_BLOCK)
            win = ext_ref[pl.ds(r0, ROW_BLOCK + FFN_HALO), :]
            off = FFN_HALO - 2
            uc = _ffn_conv3(win, w_ref, off) + b_ref[...]
            g, dg = _gelu_and_grad(uc)
            dh = dh_ref[rows, :]
            dgv_ref[rows, :] = (dh * g).astype(BF16)
            duc = dh * gv_ref[rows, :] * dg
            duc_ref[rows, :] = duc
            acc_ref[0:8, :] += _fold8(duc)
            for k in range(FFN_KERNEL):
                acc_ref[8 + 8 * k:16 + 8 * k, :] += _fold8(duc * win[off + k:off + k + ROW_BLOCK, :])
            return carry

        lax.fori_loop(0, tm // ROW_BLOCK, block, 0)

        @pl.when(i == nt - 1)
        def _():
            db_ref[...] = jnp.sum(acc_ref[0:8, :], axis=0, keepdims=True)
            for k in range(FFN_KERNEL):
                dw_ref[k:k + 1, :] = jnp.sum(acc_ref[8 + 8 * k:16 + 8 * k, :], axis=0, keepdims=True)

    tile = pl.BlockSpec((tm, D_FF), lambda i: (i, 0))
    return _call(
        body, name="ffn_bwd_a", grid=(nt,),
        in_specs=[pl.BlockSpec((tm, D_MODEL), lambda i: (i, 0)), tile, pl.BlockSpec((tm, D_FF), lambda i: (i, 1)),
                  pl.BlockSpec((D_FF, D_MODEL), lambda i: (0, 0)),
                  pl.BlockSpec((FFN_KERNEL, D_FF), lambda i: (0, 0)), pl.BlockSpec((1, D_FF), lambda i: (0, 0))],
        out_specs=[tile, tile, pl.BlockSpec((FFN_KERNEL, D_FF), lambda i: (0, 0)),
                   pl.BlockSpec((1, D_FF), lambda i: (0, 0))],
        out_shape=[_sds((t, D_FF), BF16), _sds((t, D_FF), F32), _sds((FFN_KERNEL, D_FF), F32), _sds((1, D_FF), F32)],
        scratch=[pltpu.VMEM((tm + FFN_HALO, D_FF), F32), pltpu.VMEM((tm, D_FF), F32),
                 pltpu.VMEM((8 + 8 * FFN_KERNEL, D_FF), F32)],
        vmem_mb=56)(dr2b, z, z, wfo, wfd, bfd)


def _ffn_bwd_b(duc, wfd):
    t = duc.shape[0]
    tm = 256
    nt = t // tm

    def body(duc_ref, w_ref, du_ref, ext_ref):
        i = pl.program_id(0)

        @pl.when(i == 0)
        def _():
            ext_ref[tm:tm + FFN_HALO, :] = jnp.zeros((FFN_HALO, D_FF), F32)

        @pl.when(i > 0)
        def _():
            ext_ref[tm:tm + FFN_HALO, :] = ext_ref[0:FFN_HALO, :]

        ext_ref[0:tm, :] = duc_ref[...]

        def block(r, carry):
            r0 = pl.multiple_of(r * ROW_BLOCK, ROW_BLOCK)
            win = ext_ref[pl.ds(r0, ROW_BLOCK + FFN_HALO), :]
            du = (w_ref[2:3, :] * win[0:ROW_BLOCK, :] + w_ref[1:2, :] * win[1:1 + ROW_BLOCK, :]
                  + w_ref[0:1, :] * win[2:2 + ROW_BLOCK, :])
            du_ref[pl.ds(r0, ROW_BLOCK), :] = du.astype(BF16)
            return carry

        lax.fori_loop(0, tm // ROW_BLOCK, block, 0)

    rev = lambda i: (nt - 1 - i, 0)
    return _call(
        body, name="ffn_bwd_b", grid=(nt,),
        in_specs=[pl.BlockSpec((tm, D_FF), rev), pl.BlockSpec((FFN_KERNEL, D_FF), lambda i: (0, 0))],
        out_specs=pl.BlockSpec((tm, D_FF), rev),
        out_shape=_sds((t, D_FF), BF16),
        scratch=[pltpu.VMEM((tm + FFN_HALO, D_FF), F32)], vmem_mb=40)(duc, wfd)


def _ffn_in_bwd(dr2, dub, dgvb, wfi, r1, g1):
    t = dr2.shape[0]
    tm = 256

    def body(dr2_ref, du_ref, dgv_ref, wu_ref, wg_ref, r1_ref, g_ref, dr1_ref, dr1b_ref, dg_ref, db_ref):
        @pl.when(pl.program_id(0) == 0)
        def _():
            dg_ref[...] = jnp.zeros_like(dg_ref)
            db_ref[...] = jnp.zeros_like(db_ref)

        dx1 = ALPHA * dr2_ref[...] + _dot_nt(du_ref[...], wu_ref[...]) + _dot_nt(dgv_ref[...], wg_ref[...])
        xhat, rstd = _ln_stats(r1_ref[...])
        dg_ref[...] += jnp.sum(dx1 * xhat, axis=0, keepdims=True)
        db_ref[...] += jnp.sum(dx1, axis=0, keepdims=True)
        dr1 = _ln_bwd(dx1, xhat, rstd, g_ref[...])
        dr1_ref[...] = dr1
        dr1b_ref[...] = dr1.astype(BF16)

    row = lambda w: pl.BlockSpec((tm, w), lambda i: (i, 0))
    vec = pl.BlockSpec((1, D_MODEL), lambda i: (0, 0))
    return _call(
        body, name="ffn_in_bwd", grid=(t // tm,),
        in_specs=[row(D_MODEL), row(D_FF), row(D_FF),
                  pl.BlockSpec((D_MODEL, D_FF), lambda i: (0, 0)), pl.BlockSpec((D_MODEL, D_FF), lambda i: (0, 1)),
                  row(D_MODEL), vec],
        out_specs=[row(D_MODEL), row(D_MODEL), vec, vec],
        out_shape=[_sds((t, D_MODEL), F32), _sds((t, D_MODEL), BF16), _sds((1, D_MODEL), F32), _sds((1, D_MODEL), F32)],
        vmem_mb=56)(dr2, dub, dgvb, wfi, wfi, r1, g1)


def _local_step(x, target, wi, wco, wh, wo, wfi, wfo, wcd, bcd, clg, clb, logits, ng, g1, b1, wfd, bfd, g2, b2):
    proj, xb = _proj(x, wi)
    cc, cs = _conv_fwd(proj, wcd, bcd, clg, clb)
    o, og, states = _hgrn_fwd(proj, logits, ng)
    ycat, mixed, r1, x1, x1b = _merge_fwd(cs, og, proj, x, wco, wh, wo, g1, b1)
    z = _mm_nn(x1b, wfi, tm=1024, tn=1408, name="ffn_in", vmem_mb=48)
    hmid = _ffn_mid(z, wfd, bfd)
    dr2, dr2b, loss, d_g2, d_b2 = _ffn_out_loss(hmid, x1, target, wfo, g2, b2)

    g_wfo = _mm_tn(hmid, _views(dr2b), tn=512, tt=1024, name="grad_w_ffn_out", vmem_mb=48)
    dgvb, duc, d_wfd, d_bfd = _ffn_bwd_a(dr2b, z, wfo, wfd, bfd)
    dub = _ffn_bwd_b(duc, wfd)
    g_wfi = _mm_tn(x1b, _views(dub, dgvb), tn=1408, tt=1024, name="grad_w_ffn_in", vmem_mb=48)
    dr1, dr1b, d_g1, d_b1 = _ffn_in_bwd(dr2, dub, dgvb, wfi, r1, g1)

    g_wo = _mm_tn(mixed, _views(dr1b), tn=1024, tt=1024, name="grad_w_out")
    dpm, dyb, dcs, dog = _merge_bwd(dr1b, ycat, proj, wo, wco, wh)
    g_wco = _mm_tn(cs, [(dyb, 0)], tn=1024, tt=1024, name="grad_w_conv_out")
    g_wh = _mm_tn(og, [(dyb, 1)], tn=1024, tt=1024, name="grad_w_hgrn_out")

    dcc, d_wcd, d_bcd, d_clg, d_clb = _conv_bwd_a(dcs, cc, proj, clg, clb)
    dpc = _conv_bwd_b(dcc, proj, wcd)
    dph, d_logits, d_ng = _hgrn_bwd(dog, o, states, proj, logits, ng)

    pieces = _views(dpc, dph, dpm)
    g_wi = _mm_tn(xb, pieces, tn=1024, name="grad_w_in", vmem_mb=48)
    grad_x = _mm_nt(pieces, wi, dr1, add_scale=ALPHA, tk=1024, name="grad_x", vmem_mb=48)

    small = dict(w_conv_dw=d_wcd, b_conv_dw=d_bcd, conv_ln_g=d_clg, conv_ln_b=d_clb, hgrn_lb_logits=d_logits,
                 hgrn_norm_g=d_ng, ln1_g=d_g1, ln1_b=d_b1, w_ffn_dw=d_wfd, b_ffn_dw=d_bfd, ln2_g=d_g2, ln2_b=d_b2)
    return loss, grad_x, (g_wi, g_wco, g_wh, g_wo, g_wfi, g_wfo), small


ELEMENTWISE_BLOCK_ELEMS = 256 * 1024


def _row_tile(rows, cols):
    cap = max(16, ELEMENTWISE_BLOCK_ELEMS // cols)
    if rows <= cap:
        return rows
    best = None
    for cand in range(16, cap + 1, 16):
        if rows % cand == 0:
            best = cand
    assert best is not None
    return best


def _elementwise(fn, ins, out_dtypes, *, name):
    r, c = ins[0].shape
    tr = _row_tile(r, c)

    def body(*refs):
        outs = fn(*[ref[...] for ref in refs[:len(ins)]])
        for ref, val in zip(refs[len(ins):], outs):
            ref[...] = val.astype(ref.dtype)

    spec = pl.BlockSpec((tr, c), lambda i: (i, 0))
    return _call(
        body, name=name, grid=(r // tr,), in_specs=[spec] * len(ins), out_specs=[spec] * len(out_dtypes),
        out_shape=[_sds((r, c), dt) for dt in out_dtypes])(*ins)


def _windowed(fn, sel, ins, outs, *, window, name):
    rows, cols = window
    tr = _row_tile(rows, cols)
    steps = rows // tr

    def spec(where):
        if where is None:
            return pl.BlockSpec((tr, cols), lambda i, s: (i, 0))
        kind, p, _ = where
        if kind == "rows":
            return pl.BlockSpec((tr, cols), lambda i, s: (s[p] * steps + i, 0))
        return pl.BlockSpec((tr, cols), lambda i, s: (i, s[p]))

    n_in = len(ins)

    def body(s_ref, *refs):
        vals = fn(*[ref[...] for ref in refs[:n_in]])
        for ref, val in zip(refs[n_in:], vals):
            ref[...] = val.astype(ref.dtype)

    return pl.pallas_call(
        body, name=name, out_shape=[_sds(shape, dt) for shape, dt, _ in outs],
        grid_spec=pltpu.PrefetchScalarGridSpec(
            num_scalar_prefetch=1, grid=(steps,), in_specs=[spec(where) for _, where in ins],
            out_specs=[spec(where) for _, _, where in outs]),
        compiler_params=pltpu.CompilerParams(dimension_semantics=("arbitrary",), vmem_limit_bytes=32 * 2 ** 20),
    )(sel, *[a for a, _ in ins])


def _adamw(w, g, m, v, *, name):
    def fn(w_, g_, m_, v_):
        m_new = ADAM_B1 * m_ + (1.0 - ADAM_B1) * g_
        v_new = ADAM_B2 * v_ + (1.0 - ADAM_B2) * (g_ * g_)
        m_hat = m_new / ADAM_BC1
        v_hat = v_new / ADAM_BC2
        delta = -ADAM_LR * (m_hat / (jnp.sqrt(v_hat) + ADAM_EPS) + ADAM_WD * w_)
        return delta, m_new, v_new

    return _elementwise(fn, [w, g, m, v], [F32, F32, F32], name=name)


def _place():
    return lax.axis_index("x"), lax.axis_index("y"), lax.axis_index("c")


def _other_chips(x, y):
    return [(1 - x, y), (x, 1 - y), (1 - x, 1 - y)]


SHARD_XOR = (2, 1, 3)


DMA_CHUNK_BYTES = 512 * 1024


def _n_chunks(ref):
    rows = ref.shape[0]
    total = ref.dtype.itemsize
    for d in ref.shape:
        total *= d
    best = 1
    for cand in range(2, min(rows, total // DMA_CHUNK_BYTES) + 1):
        if rows % cand == 0 and (rows // cand) % 16 == 0:
            best = cand
    return best


class _Copy:
    def __init__(self, src, dst, sems, dev=None):
        if dev is None:
            make = lambda s_, d_: pltpu.make_async_copy(s_, d_, sems[0])
        else:
            make = lambda s_, d_: pltpu.make_async_remote_copy(
                src_ref=s_, dst_ref=d_, send_sem=sems[0], recv_sem=sems[1], device_id=dev, device_id_type=MESH)
        self.local = dev is None
        self.whole = make(src, dst)
        n = _n_chunks(src)
        step = src.shape[0] // n
        self.parts = ([self.whole] if n == 1 else
                      [make(src.at[pl.ds(i * step, step)], dst.at[pl.ds(i * step, step)]) for i in range(n)])

    def start(self):
        for part in self.parts:
            part.start()

    def wait_recv(self):
        self.whole.wait_recv()

    def wait_send(self):
        self.whole.wait_send()

    def wait(self):
        self.whole.wait()


def _run_copies(local_ops, remote_ops, lsem, ssem, rsem):
    local = [_Copy(src, dst, (lsem.at[n],)) for n, (src, dst) in enumerate(local_ops)]
    remote = [_Copy(src, dst, (ssem.at[n], rsem.at[n]), dev) for n, (src, dst, dev) in enumerate(remote_ops)]
    for cp in local + remote:
        cp.start()
    for cp in remote:
        cp.wait_recv()
    for cp in remote:
        cp.wait_send()
    for cp in local:
        cp.wait()


def _comm_call(body, *, name, n_in, out_shape, n_local, n_remote):
    return pl.pallas_call(
        body, name=name, in_specs=[ANY] * n_in, out_specs=[ANY] * len(out_shape), out_shape=out_shape,
        scratch_shapes=[pltpu.SemaphoreType.DMA((max(n_local, 1),)), pltpu.SemaphoreType.DMA((n_remote,)),
                        pltpu.SemaphoreType.DMA((n_remote,))])


BIG = (("w_in", D_MODEL, IN_COLS, 1), ("w_conv_out", CONV_DIM, D_MODEL, 1), ("w_hgrn_out", HGRN_DIM, D_MODEL, 0),
       ("w_out", D_MODEL, D_MODEL, 0), ("w_ffn_in", D_MODEL, 2 * D_FF, 1), ("w_ffn_out", D_FF, D_MODEL, 0))


def _shard_slice(ref, rows, cols, axis, k):
    if axis == 1:
        w = cols // N_CHIPS
        return ref.at[:, pl.ds(k * w, w)]
    h = rows // N_CHIPS
    return ref.at[pl.ds(k * h, h), :]


def _half_slice(ref, rows, cols, axis, hc):
    if axis == 1:
        return ref.at[pl.ds(hc * (rows // 2), rows // 2), :]
    return ref.at[:, pl.ds(hc * (cols // 2), cols // 2)]


def _half_shape(rows, cols, axis):
    return (rows // 2, cols) if axis == 1 else (rows, cols // 2)


def _gather_weights(full, small):
    n_big, n_small = len(full), len(small)
    n_arr = n_big + n_small
    out_shape = ([_sds((r, c), BF16) for _, r, c, _ in BIG]
                 + [_sds((N_CHIPS,) + a.shape, F32) for a in small])
    shard_shape = [(r, c // N_CHIPS) if ax == 1 else (r // N_CHIPS, c) for _, r, c, ax in BIG]

    def body(*refs):
        ins, outs = refs[:n_arr], refs[n_arr:2 * n_arr]
        lsem, ssem, rsem, fsem_s, fsem_r = refs[2 * n_arr:]
        x, y, c = _place()
        me = 2 * x + y
        chips = _other_chips(x, y)
        sibling = (x, y, 1 - c)

        def region(idx, k, hc, bufs=outs):
            (_, r, cc, ax), (sr, _) = BIG[idx], shard_shape[idx]
            return _shard_slice(bufs[idx], r, cc, ax, k).at[pl.ds(hc * (sr // 2), sr // 2)]

        for k in range(N_CHIPS):
            for hc in range(2):
                @pl.when((me == k) & (c == hc))
                def _(k=k, hc=hc):
                    local = [_Copy(ins[n_big + i], outs[n_big + i].at[k], (lsem.at[i],)) for i in range(n_small)]
                    sends, fwds = [], []
                    for j, (cx, cy) in enumerate(chips):
                        for i in range(n_big):
                            n = j * n_arr + i
                            sends.append(_Copy(region(i, k, hc, ins), region(i, k, hc), (ssem.at[n], rsem.at[n]),
                                               (cx, cy, c)))
                            reg = region(i, k ^ SHARD_XOR[j], hc)
                            fwds.append(_Copy(reg, reg, (fsem_s.at[j * n_big + i], fsem_r.at[j * n_big + i]), sibling))
                        for i in range(n_small):
                            n = j * n_arr + n_big + i
                            sends.append(_Copy(ins[n_big + i], outs[n_big + i].at[k], (ssem.at[n], rsem.at[n]),
                                               (cx, cy, c)))
                    for cp in local + sends:
                        cp.start()
                    for j in range(3):
                        for i in range(n_big):
                            sends[j * n_arr + i].wait_recv()
                        for i in range(n_big):
                            fwds[j * n_big + i].start()
                    for j in range(3):
                        for i in range(n_small):
                            sends[j * n_arr + n_big + i].wait_recv()
                    for cp in fwds:
                        cp.wait_recv()
                    for cp in sends + fwds:
                        cp.wait_send()
                    for cp in local:
                        cp.wait()

    return pl.pallas_call(
        body, name="gather_weights", in_specs=[ANY] * n_arr, out_specs=[ANY] * n_arr, out_shape=out_shape,
        input_output_aliases={i: i for i in range(n_big)},
        scratch_shapes=[pltpu.SemaphoreType.DMA((n_small,)), pltpu.SemaphoreType.DMA((3 * n_arr,)),
                        pltpu.SemaphoreType.DMA((3 * n_arr,)), pltpu.SemaphoreType.DMA((3 * n_big,)),
                        pltpu.SemaphoreType.DMA((3 * n_big,))])(*full, *small)


def _sibling_exchange(grads):
    n = len(BIG)
    shapes = [_sds(_half_shape(r, c, ax), F32) for _, r, c, ax in BIG]

    def body(*refs):
        ins, got = refs[:n], refs[n:2 * n]
        lsem, ssem, rsem = refs[2 * n:]
        x, y, c = _place()
        for k in range(2):
            @pl.when(c == k)
            def _(k=k):
                remote_ops = [(_half_slice(g, r, cc, ax, 1 - k), dst, (x, y, 1 - c))
                              for g, dst, (_, r, cc, ax) in zip(ins, got, BIG)]
                _run_copies([], remote_ops, lsem, ssem, rsem)

    return _comm_call(body, name="grad_sibling_exchange", n_in=n, out_shape=shapes, n_local=0, n_remote=n)(*grads)


def _piece_shape(rows, cols, axis):
    hr, hc = _half_shape(rows, cols, axis)
    return (hr, hc // N_CHIPS) if axis == 1 else (hr // N_CHIPS, hc)


def _chip_exchange(chip_sums):
    n = len(BIG)
    half = [_half_shape(r, c, ax) for _, r, c, ax in BIG]
    out_shape = [_sds(_piece_shape(r, c, ax), BF16) for _, r, c, ax in BIG for _ in range(3)]

    def body(*refs):
        ins, got = refs[:n], refs[n:4 * n]
        lsem, ssem, rsem = refs[4 * n:]
        x, y, c = _place()
        me = 2 * x + y
        for k in range(N_CHIPS):
            @pl.when(me == k)
            def _(k=k):
                remote_ops = [(_shard_slice(g, half[idx][0], half[idx][1], BIG[idx][3], k ^ SHARD_XOR[j]),
                               got[3 * idx + j], (cx, cy, c))
                              for j, (cx, cy) in enumerate(_other_chips(x, y))
                              for idx, g in enumerate(ins)]
                _run_copies([], remote_ops, lsem, ssem, rsem)

    outs = _comm_call(body, name="grad_chip_exchange", n_in=n, out_shape=out_shape, n_local=0, n_remote=3 * n)(
        *chip_sums)
    return [outs[3 * idx:3 * idx + 3] for idx in range(n)]


def _sibling_assemble(shards):
    n = len(BIG)
    shape = [(r, c // N_CHIPS) if ax == 1 else (r // N_CHIPS, c) for _, r, c, ax in BIG]

    def body(*refs):
        ins, outs = refs[:n], refs[n:2 * n]
        lsem, ssem, rsem = refs[2 * n:]
        x, y, c = _place()
        for k in range(2):
            @pl.when(c == k)
            def _(k=k):
                remote_ops = [(_half_slice(i_, sr, sc, ax, k), _half_slice(o_, sr, sc, ax, k), (x, y, 1 - c))
                              for i_, o_, (sr, sc), (_, _, _, ax) in zip(ins, outs, shape, BIG)]
                _run_copies([], remote_ops, lsem, ssem, rsem)

    return pl.pallas_call(
        body, name="grad_sibling_assemble", in_specs=[ANY] * n, out_specs=[ANY] * n,
        out_shape=[_sds(s, F32) for s in shape], input_output_aliases={i: i for i in range(n)},
        scratch_shapes=[pltpu.SemaphoreType.DMA((1,)), pltpu.SemaphoreType.DMA((n,)),
                        pltpu.SemaphoreType.DMA((n,))])(*shards)


def _all_reduce_small(packed):
    r, w = packed.shape

    def body(in_ref, out_ref, slots, lsem, ssem, rsem):
        x, y, c = _place()
        me = 4 * x + 2 * y + c
        peers = [(x ^ (m >> 2), y ^ ((m >> 1) & 1), c ^ (m & 1)) for m in range(1, N_DEV)]
        _run_copies([(in_ref, slots.at[me])], [(in_ref, slots.at[me], dev) for dev in peers], lsem, ssem, rsem)
        total = slots[0]
        for d in range(1, N_DEV):
            total = total + slots[d]
        out_ref[...] = total

    vmem = pl.BlockSpec(memory_space=pltpu.VMEM)
    return pl.pallas_call(
        body, name="small_all_reduce", in_specs=[vmem], out_specs=vmem, out_shape=_sds((r, w), F32),
        scratch_shapes=[pltpu.VMEM((N_DEV, r, w), F32), pltpu.SemaphoreType.DMA((1,)),
                        pltpu.SemaphoreType.DMA((N_DEV - 1,)), pltpu.SemaphoreType.DMA((N_DEV - 1,))])(packed)


SMALL_ORDER = ("w_conv_dw", "b_conv_dw", "conv_ln_g", "conv_ln_b", "hgrn_lb_logits", "hgrn_norm_g",
               "ln1_g", "ln1_b", "w_ffn_dw", "b_ffn_dw", "ln2_g", "ln2_b")
REPLICATED_SMALL = tuple(n for n in SMALL_ORDER if n not in ("w_conv_dw", "w_ffn_dw"))
WEIGHT_ORDER = ("w_in", "w_conv_dw", "b_conv_dw", "conv_ln_g", "conv_ln_b", "w_conv_out", "hgrn_lb_logits",
                "hgrn_norm_g", "w_hgrn_out", "w_out", "ln1_g", "ln1_b", "w_ffn_in", "w_ffn_dw", "b_ffn_dw",
                "w_ffn_out", "ln2_g", "ln2_b")


def _pack(arrs):
    flat = jnp.concatenate([a.reshape(-1) for a in arrs])
    assert flat.shape[0] % 128 == 0
    return flat.reshape(-1, 128)


def _unpack(packed, shapes):
    flat = packed.reshape(-1)
    out, pos = [], 0
    for shp in shapes:
        size = 1
        for d in shp:
            size *= d
        out.append(flat[pos:pos + size].reshape(shp))
        pos += size
    return out


def kernel(x, w_in, w_conv_dw, b_conv_dw, conv_ln_g, conv_ln_b, w_conv_out, hgrn_lb_logits, hgrn_norm_g, w_hgrn_out, w_out, ln1_g, ln1_b, w_ffn_in, w_ffn_dw, b_ffn_dw, w_ffn_out, ln2_g, ln2_b, loss_target, m_w_in, m_w_conv_dw, m_b_conv_dw, m_conv_ln_g, m_conv_ln_b, m_w_conv_out, m_hgrn_lb_logits, m_hgrn_norm_g, m_w_hgrn_out, m_w_out, m_ln1_g, m_ln1_b, m_w_ffn_in, m_w_ffn_dw, m_b_ffn_dw, m_w_ffn_out, m_ln2_g, m_ln2_b, v_w_in, v_w_conv_dw, v_b_conv_dw, v_conv_ln_g, v_conv_ln_b, v_w_conv_out, v_hgrn_lb_logits, v_hgrn_norm_g, v_w_hgrn_out, v_w_out, v_ln1_g, v_ln1_b, v_w_ffn_in, v_w_ffn_dw, v_b_ffn_dw, v_w_ffn_out, v_ln2_g, v_ln2_b):
    w = dict(w_in=w_in, w_conv_dw=w_conv_dw, b_conv_dw=b_conv_dw, conv_ln_g=conv_ln_g, conv_ln_b=conv_ln_b,
             w_conv_out=w_conv_out, hgrn_lb_logits=hgrn_lb_logits, hgrn_norm_g=hgrn_norm_g, w_hgrn_out=w_hgrn_out,
             w_out=w_out, ln1_g=ln1_g, ln1_b=ln1_b, w_ffn_in=w_ffn_in, w_ffn_dw=w_ffn_dw, b_ffn_dw=b_ffn_dw,
             w_ffn_out=w_ffn_out, ln2_g=ln2_g, ln2_b=ln2_b)
    m = dict(w_in=m_w_in, w_conv_dw=m_w_conv_dw, b_conv_dw=m_b_conv_dw, conv_ln_g=m_conv_ln_g, conv_ln_b=m_conv_ln_b,
             w_conv_out=m_w_conv_out, hgrn_lb_logits=m_hgrn_lb_logits, hgrn_norm_g=m_hgrn_norm_g,
             w_hgrn_out=m_w_hgrn_out, w_out=m_w_out, ln1_g=m_ln1_g, ln1_b=m_ln1_b, w_ffn_in=m_w_ffn_in,
             w_ffn_dw=m_w_ffn_dw, b_ffn_dw=m_b_ffn_dw, w_ffn_out=m_w_ffn_out, ln2_g=m_ln2_g, ln2_b=m_ln2_b)
    v = dict(w_in=v_w_in, w_conv_dw=v_w_conv_dw, b_conv_dw=v_b_conv_dw, conv_ln_g=v_conv_ln_g, conv_ln_b=v_conv_ln_b,
             w_conv_out=v_w_conv_out, hgrn_lb_logits=v_hgrn_lb_logits, hgrn_norm_g=v_hgrn_norm_g,
             w_hgrn_out=v_w_hgrn_out, w_out=v_w_out, ln1_g=v_ln1_g, ln1_b=v_ln1_b, w_ffn_in=v_w_ffn_in,
             w_ffn_dw=v_w_ffn_dw, b_ffn_dw=v_b_ffn_dw, w_ffn_out=v_w_ffn_out, ln2_g=v_ln2_g, ln2_b=v_ln2_b)
    big_names = [n for n, _, _, _ in BIG]
    w2 = {n: a[0] if a.ndim == 3 else a for n, a in w.items()}
    m2 = {n: a[0] if a.ndim == 3 else a for n, a in m.items()}
    v2 = {n: a[0] if a.ndim == 3 else a for n, a in v.items()}

    sel = jnp.stack([2 * lax.axis_index("x") + lax.axis_index("y"), lax.axis_index("c")]).astype(jnp.int32)
    sharded = lambda ax, p, n: ("cols", p, n) if ax == 1 else ("rows", p, n)
    across = lambda ax, p, n: ("rows", p, n) if ax == 1 else ("cols", p, n)

    placed = [_windowed(lambda a: (a,), sel, [(w2[n], None)], [((r, c), BF16, sharded(ax, 0, N_CHIPS))],
                        window=w2[n].shape, name="cast_" + n)[0] for n, r, c, ax in BIG]
    gathered = _gather_weights(placed, [w2["w_conv_dw"], w2["w_ffn_dw"]])
    wi, wco, wh, wo, wfi, wfo = gathered[:6]
    wcd = jnp.transpose(gathered[6], (1, 0, 2)).reshape(CONV_KERNEL, CONV_DIM)
    wfd = jnp.transpose(gathered[7], (1, 0, 2)).reshape(FFN_KERNEL, D_FF)

    loss_part, grad_x, big_grads, small_grads = _local_step(
        x[0], loss_target[0], wi, wco, wh, wo, wfi, wfo, wcd, w2["b_conv_dw"], w2["conv_ln_g"], w2["conv_ln_b"],
        w2["hgrn_lb_logits"], w2["hgrn_norm_g"], w2["ln1_g"], w2["ln1_b"], wfd, w2["b_ffn_dw"],
        w2["ln2_g"], w2["ln2_b"])
    loss = lax.psum(loss_part[0, 0], ("x", "y", "c"))

    got = _sibling_exchange(big_grads)
    chip_sums = [_windowed(lambda a, b: (a + b,), sel, [(g_, across(ax, 1, 2)), (h_, None)],
                           [(_half_shape(r, c, ax), BF16, None)], window=_half_shape(r, c, ax),
                           name="chip_sum_" + n)[0]
                 for (n, r, c, ax), g_, h_ in zip(BIG, big_grads, got)]
    recv = _chip_exchange(chip_sums)
    add4 = lambda a, b0, b1, b2: (a.astype(F32) + b0.astype(F32) + b1.astype(F32) + b2.astype(F32),)
    half_filled = [_windowed(add4, sel, [(cs_, sharded(ax, 0, N_CHIPS))] + [(r_, None) for r_ in recv_],
                             [((r, c // N_CHIPS) if ax == 1 else (r // N_CHIPS, c), F32, across(ax, 1, 2))],
                             window=_piece_shape(r, c, ax), name="shard_sum_" + n)[0]
                   for (n, r, c, ax), cs_, recv_ in zip(BIG, chip_sums, recv)]
    shard_grads = dict(zip(big_names, _sibling_assemble(half_filled)))

    small_shapes = [small_grads[n].shape for n in SMALL_ORDER]
    reduced = dict(zip(SMALL_ORDER, _unpack(_all_reduce_small(_pack([small_grads[n] for n in SMALL_ORDER])),
                                            small_shapes)))
    shard = 2 * lax.axis_index("x") + lax.axis_index("y")
    grads = dict(shard_grads)
    for n in REPLICATED_SMALL:
        grads[n] = reduced[n]
    grads["w_conv_dw"] = lax.dynamic_slice_in_dim(reduced["w_conv_dw"], shard * (CONV_DIM // N_CHIPS),
                                                  CONV_DIM // N_CHIPS, axis=1)
    grads["w_ffn_dw"] = lax.dynamic_slice_in_dim(reduced["w_ffn_dw"], shard * (D_FF // N_CHIPS),
                                                 D_FF // N_CHIPS, axis=1)

    delta, new_m, new_v = {}, {}, {}
    for n in big_names + ["w_conv_dw", "w_ffn_dw"]:
        delta[n], new_m[n], new_v[n] = _adamw(w2[n], grads[n], m2[n], v2[n], name="adamw_" + n)
    rep_shapes = [w2[n].shape for n in REPLICATED_SMALL]
    packed = _adamw(*[_pack([src[n] for n in REPLICATED_SMALL]) for src in (w2, grads, m2, v2)], name="adamw_small")
    for dst, pk in zip((delta, new_m, new_v), packed):
        for n, a in zip(REPLICATED_SMALL, _unpack(pk, rep_shapes)):
            dst[n] = a

    def shaped(d):
        return [d[n].reshape(w[n].shape) for n in WEIGHT_ORDER]

    return (loss, grad_x[None], *shaped(grads), *shaped(delta), *shaped(new_m), *shaped(new_v))
```

```python
import jax
import jax.numpy as jnp
from jax import lax
from jax.experimental import pallas as pl
from jax.experimental.pallas import tpu as pltpu

F32 = jnp.float32
BF16 = jnp.bfloat16

D_MODEL = 1024
CONV_DIM = 512
CONV_KERNEL = 31
HGRN_DIM = 1024
HGRN_HEADS = 8
HEAD_DIM = 128
CHUNK = 64
SUB = 16
N_SUB = CHUNK // SUB
D_FF = 2816
FFN_KERNEL = 3
IN_COLS = 7168
LN_EPS = 1e-5
RMS_EPS = 1e-6
ALPHA = 2.0 ** 0.25
GELU_C = 0.7978845608028654
GELU_A = 0.044715

ADAM_LR = 0.001
ADAM_B1 = 0.9
ADAM_B2 = 0.999
ADAM_EPS = 1e-08
ADAM_WD = 0.01
ADAM_STEP = 10
ADAM_BC1 = 1.0 - ADAM_B1 ** ADAM_STEP
ADAM_BC2 = 1.0 - ADAM_B2 ** ADAM_STEP

N_CHIPS = 4
N_DEV = 8
ROW_BLOCK = 32
CONV_HALO = 32
FFN_HALO = 8
MESH = pl.DeviceIdType.MESH
ANY = pl.BlockSpec(memory_space=pl.ANY)


def _dot(a, b):
    return jnp.dot(a, b, preferred_element_type=F32)


def _dot_nt(a, b):
    return lax.dot_general(a, b, (((1,), (1,)), ((), ())), preferred_element_type=F32)


def _dot_tn(a, b):
    return lax.dot_general(a, b, (((0,), (0,)), ((), ())), preferred_element_type=F32)


def _sigmoid(z):
    return jax.nn.sigmoid(z)


def _silu_grad(z, s):
    return s * (1.0 + z * (1.0 - s))


def _gelu_and_grad(u):
    inner = GELU_C * (u + GELU_A * u * u * u)
    th = jnp.tanh(inner)
    g = 0.5 * u * (1.0 + th)
    dg = 0.5 * (1.0 + th) + 0.5 * u * (1.0 - th * th) * GELU_C * (1.0 + 3.0 * GELU_A * u * u)
    return g, dg


def _ln_stats(r):
    mu = jnp.mean(r, axis=-1, keepdims=True)
    xc = r - mu
    var = jnp.mean(xc * xc, axis=-1, keepdims=True)
    rstd = lax.rsqrt(var + LN_EPS)
    return xc * rstd, rstd


def _ln_bwd(dy, xhat, rstd, g):
    dxh = dy * g
    m1 = jnp.mean(dxh, axis=-1, keepdims=True)
    m2 = jnp.mean(dxh * xhat, axis=-1, keepdims=True)
    return rstd * (dxh - m1 - xhat * m2)


def _fold8(x):
    acc = x[0:8, :]
    for r in range(8, x.shape[0], 8):
        acc = acc + x[r:r + 8, :]
    return acc


def _in_hbm(a):
    return pltpu.with_memory_space_constraint(a, pltpu.HBM)


def _hbm(shape, dtype):
    return pltpu.HBM(shape, dtype)


def _out_hbm(out_shape):
    if isinstance(out_shape, (list, tuple)):
        return [_hbm(s.shape, s.dtype) for s in out_shape]
    return _hbm(out_shape.shape, out_shape.dtype)


def _call(body, *, name, grid, in_specs, out_specs, out_shape, scratch=(), vmem_mb=32, aliases=None):
    call = pl.pallas_call(
        body, name=name, grid=grid, in_specs=in_specs, out_specs=out_specs, out_shape=_out_hbm(out_shape),
        scratch_shapes=list(scratch), input_output_aliases=aliases or {},
        compiler_params=pltpu.CompilerParams(
            dimension_semantics=("arbitrary",) * len(grid), vmem_limit_bytes=vmem_mb * 2 ** 20))
    return lambda *args: call(*[_in_hbm(a) for a in args])


def _sds(shape, dtype):
    return jax.ShapeDtypeStruct(shape, dtype)


def _proj(x, w):
    t = x.shape[0]
    tm, tn = min(t, 1024), 1024

    def body(x_ref, w_ref, p_ref, xb_ref):
        @pl.when(pl.program_id(1) == 0)
        def _():
            xb_ref[...] = x_ref[...].astype(BF16)
        p_ref[...] = _dot(xb_ref[...], w_ref[...])

    return _call(
        body, name="proj", grid=(t // tm, IN_COLS // tn),
        in_specs=[pl.BlockSpec((tm, D_MODEL), lambda i, j: (i, 0)),
                  pl.BlockSpec((D_MODEL, tn), lambda i, j: (0, j))],
        out_specs=[pl.BlockSpec((tm, tn), lambda i, j: (i, j)),
                   pl.BlockSpec((tm, D_MODEL), lambda i, j: (i, 0))],
        out_shape=[_sds((t, IN_COLS), F32), _sds((t, D_MODEL), BF16)], vmem_mb=48)(x, w)


def _mm_nn(a, w, *, tm, tn, name, vmem_mb=32):
    t, k = a.shape
    n = w.shape[1]
    tm = min(tm, t)

    def body(a_ref, w_ref, o_ref):
        o_ref[...] = _dot(a_ref[...], w_ref[...])

    return _call(
        body, name=name, grid=(t // tm, n // tn),
        in_specs=[pl.BlockSpec((tm, k), lambda i, j: (i, 0)), pl.BlockSpec((k, tn), lambda i, j: (0, j))],
        out_specs=pl.BlockSpec((tm, tn), lambda i, j: (i, j)),
        out_shape=_sds((t, n), F32), vmem_mb=vmem_mb)(a, w)


def _views(*arrs):
    out = []
    for a in arrs:
        if a.ndim == 2:
            out.append((a, None))
        else:
            out.extend((a, p) for p in range(a.shape[0]))
    return out


def _piece_layout(views, tile):
    starts, counts, total = [], [], 0
    for arr, _ in views:
        width = arr.shape[-1]
        assert width % tile == 0
        starts.append(total)
        counts.append(width // tile)
        total += width // tile
    return starts, counts, total


def _mm_tn(a, views, *, tn, name, tt=512, vmem_mb=32):
    t, m = a.shape
    tt = min(tt, t)
    starts, counts, nj = _piece_layout(views, tn)
    n_views = len(views)

    def body(a_ref, *refs):
        b_refs, o_ref = refs[:n_views], refs[n_views]
        j = pl.program_id(0)

        @pl.when(pl.program_id(1) == 0)
        def _():
            o_ref[...] = jnp.zeros_like(o_ref)

        for b_ref, st, nb, (_, p) in zip(b_refs, starts, counts, views):
            @pl.when((j >= st) & (j < st + nb))
            def _(b_ref=b_ref, p=p):
                blk = b_ref[...] if p is None else b_ref[0]
                o_ref[...] += _dot_tn(a_ref[...], blk)

    def b_spec(st, nb, p):
        def rows(j, k):
            return jnp.where((j >= st) & (j < st + nb), k, 0)

        def cols(j):
            return jnp.clip(j - st, 0, nb - 1)

        if p is None:
            return pl.BlockSpec((tt, tn), lambda j, k: (rows(j, k), cols(j)))
        return pl.BlockSpec((1, tt, tn), lambda j, k: (p, rows(j, k), cols(j)))

    return _call(
        body, name=name, grid=(nj, t // tt),
        in_specs=[pl.BlockSpec((tt, m), lambda j, k: (k, 0))]
        + [b_spec(st, nb, p) for st, nb, (_, p) in zip(starts, counts, views)],
        out_specs=pl.BlockSpec((m, tn), lambda j, k: (0, j)),
        out_shape=_sds((m, nj * tn), F32), vmem_mb=vmem_mb)(a, *[arr for arr, _ in views])


DPROJ_SLABS = 7
DPROJ_GATE_SLAB = 4
DPROJ_CONV_SLAB = 6


def _slab_cols(s):
    return (s + 1) % DPROJ_SLABS


def _grad_w_in(xb, dstack):
    t = xb.shape[0]
    tt = min(t, 1024)

    def body(a_ref, b_ref, o_ref):
        @pl.when(pl.program_id(1) == 0)
        def _():
            o_ref[...] = jnp.zeros_like(o_ref)

        o_ref[...] += _dot_tn(a_ref[...], b_ref[0])

    return _call(
        body, name="grad_w_in", grid=(DPROJ_SLABS, t // tt),
        in_specs=[pl.BlockSpec((tt, D_MODEL), lambda j, k: (k, 0)),
                  pl.BlockSpec((1, tt, D_MODEL), lambda j, k: (j, k, 0))],
        out_specs=pl.BlockSpec((D_MODEL, D_MODEL), lambda j, k: (0, _slab_cols(j))),
        out_shape=_sds((D_MODEL, IN_COLS), F32), vmem_mb=48)(xb, dstack)


def _grad_x(dstack, wi, dr1):
    t = dr1.shape[0]
    tm = min(t, 1024)

    def body(add_ref, b_ref, w_ref, o_ref):
        @pl.when(pl.program_id(1) == 0)
        def _():
            o_ref[...] = ALPHA * add_ref[...]

        o_ref[...] += _dot_nt(b_ref[0], w_ref[...])

    return _call(
        body, name="grad_x", grid=(t // tm, DPROJ_SLABS),
        in_specs=[pl.BlockSpec((tm, D_MODEL), lambda i, k: (i, 0)),
                  pl.BlockSpec((1, tm, D_MODEL), lambda i, k: (k, i, 0)),
                  pl.BlockSpec((D_MODEL, D_MODEL), lambda i, k: (0, _slab_cols(k)))],
        out_specs=pl.BlockSpec((tm, D_MODEL), lambda i, k: (i, 0)),
        out_shape=_sds((t, D_MODEL), F32), vmem_mb=48)(dr1, dstack, wi)


LANES = 128
CONV_FWD_OFFSETS = {k: 2 + k for k in range(CONV_KERNEL)}
CONV_BWD_OFFSETS = {k: CONV_KERNEL - 1 - k for k in range(CONV_KERNEL)}


def _conv_taps(win, offsets):
    n = win.shape[0]
    for b in range(8):
        taps = [k for k, o in offsets.items() if o % 8 == b]
        if not taps:
            continue
        shifted = win if b == 0 else pltpu.roll(win, n - b, 0)
        for k in taps:
            first = offsets[k] - b
            yield k, shifted[first:first + ROW_BLOCK, :]


def _conv_fwd(proj, wcd, bcd, lng, lnb):
    t = proj.shape[0]
    tm = 512

    def body(cv_ref, cg_ref, w_ref, b_ref, g_ref, be_ref, cc_ref, cs_ref, ext_ref):
        i = pl.program_id(0)

        @pl.when(i == 0)
        def _():
            ext_ref[0:CONV_HALO, :] = jnp.zeros((CONV_HALO, CONV_DIM), F32)

        @pl.when(i > 0)
        def _():
            ext_ref[0:CONV_HALO, :] = ext_ref[tm:tm + CONV_HALO, :]

        ext_ref[CONV_HALO:CONV_HALO + tm, :] = cv_ref[...] * _sigmoid(cg_ref[...])

        def block(r, carry):
            r0 = pl.multiple_of(r * ROW_BLOCK, ROW_BLOCK)
            groups = []
            for g in range(CONV_DIM // LANES):
                lanes = slice(g * LANES, (g + 1) * LANES)
                win = ext_ref[pl.ds(r0, ROW_BLOCK + CONV_HALO), lanes]
                acc = jnp.broadcast_to(b_ref[:, lanes], (ROW_BLOCK, LANES))
                for k, rows_k in _conv_taps(win, CONV_FWD_OFFSETS):
                    acc = acc + w_ref[k:k + 1, lanes] * rows_k
                groups.append(acc)
            acc = jnp.concatenate(groups, axis=1)
            cc_ref[pl.ds(r0, ROW_BLOCK), :] = acc
            xhat, _ = _ln_stats(acc)
            a = xhat * g_ref[...] + be_ref[...]
            cs_ref[pl.ds(r0, ROW_BLOCK), :] = (a * _sigmoid(a)).astype(BF16)
            return carry

        lax.fori_loop(0, tm // ROW_BLOCK, block, 0)

    vec = pl.BlockSpec((1, CONV_DIM), lambda i: (0, 0))
    return _call(
        body, name="conv_fwd", grid=(t // tm,),
        in_specs=[pl.BlockSpec((tm, CONV_DIM), lambda i: (i, 0)), pl.BlockSpec((tm, CONV_DIM), lambda i: (i, 1)),
                  pl.BlockSpec((CONV_KERNEL, CONV_DIM), lambda i: (0, 0)), vec, vec, vec],
        out_specs=[pl.BlockSpec((tm, CONV_DIM), lambda i: (i, 0)), pl.BlockSpec((tm, CONV_DIM), lambda i: (i, 0))],
        out_shape=[_sds((t, CONV_DIM), F32), _sds((t, CONV_DIM), BF16)],
        scratch=[pltpu.VMEM((tm + CONV_HALO, CONV_DIM), F32)])(proj, proj, wcd, bcd, lng, lnb)


def _conv_bwd_a(dcs, cc, proj, lng, lnb):
    t = proj.shape[0]
    tm = 512
    nt = t // tm

    def body(dcs_ref, cc_ref, cv_ref, cg_ref, g_ref, be_ref,
             dcc_ref, dw_ref, db_ref, dg_ref, dbe_ref, ext_ref, accw_ref, acc3_ref):
        i = pl.program_id(0)

        @pl.when(i == 0)
        def _():
            ext_ref[0:CONV_HALO, :] = jnp.zeros((CONV_HALO, CONV_DIM), F32)
            accw_ref[...] = jnp.zeros_like(accw_ref)
            acc3_ref[...] = jnp.zeros_like(acc3_ref)

        @pl.when(i > 0)
        def _():
            ext_ref[0:CONV_HALO, :] = ext_ref[tm:tm + CONV_HALO, :]

        ext_ref[CONV_HALO:CONV_HALO + tm, :] = cv_ref[...] * _sigmoid(cg_ref[...])

        def block(r, carry):
            r0 = pl.multiple_of(r * ROW_BLOCK, ROW_BLOCK)
            rows = pl.ds(r0, ROW_BLOCK)
            xhat, rstd = _ln_stats(cc_ref[rows, :])
            a = xhat * g_ref[...] + be_ref[...]
            sg = _sigmoid(a)
            da = dcs_ref[rows, :] * _silu_grad(a, sg)
            acc3_ref[8:16, :] += _fold8(da * xhat)
            acc3_ref[16:24, :] += _fold8(da)
            dcc = _ln_bwd(da, xhat, rstd, g_ref[...])
            dcc_ref[rows, :] = dcc
            acc3_ref[0:8, :] += _fold8(dcc)
            for g in range(CONV_DIM // LANES):
                lanes = slice(g * LANES, (g + 1) * LANES)
                win = ext_ref[pl.ds(r0, ROW_BLOCK + CONV_HALO), lanes]
                dcc_g = dcc[:, lanes]
                for k, rows_k in _conv_taps(win, CONV_FWD_OFFSETS):
                    accw_ref[8 * k:8 * k + 8, lanes] += _fold8(dcc_g * rows_k)
            return carry

        lax.fori_loop(0, tm // ROW_BLOCK, block, 0)

        @pl.when(i == nt - 1)
        def _():
            for k in range(CONV_KERNEL):
                dw_ref[k:k + 1, :] = jnp.sum(accw_ref[8 * k:8 * k + 8, :], axis=0, keepdims=True)
            db_ref[...] = jnp.sum(acc3_ref[0:8, :], axis=0, keepdims=True)
            dg_ref[...] = jnp.sum(acc3_ref[8:16, :], axis=0, keepdims=True)
            dbe_ref[...] = jnp.sum(acc3_ref[16:24, :], axis=0, keepdims=True)

    vec = pl.BlockSpec((1, CONV_DIM), lambda i: (0, 0))
    tile = pl.BlockSpec((tm, CONV_DIM), lambda i: (i, 0))
    return _call(
        body, name="conv_bwd_a", grid=(nt,),
        in_specs=[tile, tile, tile, pl.BlockSpec((tm, CONV_DIM), lambda i: (i, 1)), vec, vec],
        out_specs=[tile, pl.BlockSpec((CONV_KERNEL, CONV_DIM), lambda i: (0, 0)), vec, vec, vec],
        out_shape=[_sds((t, CONV_DIM), F32), _sds((CONV_KERNEL, CONV_DIM), F32),
                   _sds((1, CONV_DIM), F32), _sds((1, CONV_DIM), F32), _sds((1, CONV_DIM), F32)],
        scratch=[pltpu.VMEM((tm + CONV_HALO, CONV_DIM), F32),
                 pltpu.VMEM((8 * CONV_KERNEL, CONV_DIM), F32),
                 pltpu.VMEM((24, CONV_DIM), F32)])(dcs, cc, proj, proj, lng, lnb)


def _conv_bwd_b(dcc, proj, wcd, dstack):
    t = proj.shape[0]
    tm = 512
    nt = t // tm

    def body(dcc_ref, cv_ref, cg_ref, w_ref, stack_ref, out_ref, ext_ref):
        del stack_ref
        i = pl.program_id(0)

        @pl.when(i == 0)
        def _():
            ext_ref[tm:tm + CONV_HALO, :] = jnp.zeros((CONV_HALO, CONV_DIM), F32)

        @pl.when(i > 0)
        def _():
            ext_ref[tm:tm + CONV_HALO, :] = ext_ref[0:CONV_HALO, :]

        ext_ref[0:tm, :] = dcc_ref[...]

        def block(r, carry):
            r0 = pl.multiple_of(r * ROW_BLOCK, ROW_BLOCK)
            rows = pl.ds(r0, ROW_BLOCK)
            for g in range(CONV_DIM // LANES):
                lanes = slice(g * LANES, (g + 1) * LANES)
                gate_lanes = slice(CONV_DIM + g * LANES, CONV_DIM + (g + 1) * LANES)
                win = ext_ref[pl.ds(r0, ROW_BLOCK + CONV_HALO), lanes]
                acc = jnp.zeros((ROW_BLOCK, LANES), F32)
                for k, rows_k in _conv_taps(win, CONV_BWD_OFFSETS):
                    acc = acc + w_ref[k:k + 1, lanes] * rows_k
                sg = _sigmoid(cg_ref[rows, lanes])
                out_ref[0, rows, lanes] = (acc * sg).astype(BF16)
                out_ref[0, rows, gate_lanes] = (acc * cv_ref[rows, lanes] * sg * (1.0 - sg)).astype(BF16)
            return carry

        lax.fori_loop(0, tm // ROW_BLOCK, block, 0)

    rev = lambda i: (nt - 1 - i, 0)
    return _call(
        body, name="conv_bwd_b", grid=(nt,),
        in_specs=[pl.BlockSpec((tm, CONV_DIM), rev), pl.BlockSpec((tm, CONV_DIM), rev),
                  pl.BlockSpec((tm, CONV_DIM), lambda i: (nt - 1 - i, 1)),
                  pl.BlockSpec((CONV_KERNEL, CONV_DIM), lambda i: (0, 0)), ANY],
        out_specs=pl.BlockSpec((1, tm, 2 * CONV_DIM), lambda i: (DPROJ_CONV_SLAB, nt - 1 - i, 0)),
        out_shape=_sds(dstack.shape, BF16), aliases={4: 0},
        scratch=[pltpu.VMEM((tm + CONV_HALO, CONV_DIM), F32)])(dcc, proj, proj, wcd, dstack)


def _lower_bound(lg_ref):
    a0, a1 = lg_ref[0:1, :], lg_ref[1:2, :]
    m = jnp.maximum(a0, a1)
    e0, e1 = jnp.exp(a0 - m), jnp.exp(a1 - m)
    return e0 / (e0 + e1)


def _block_tri(n, upper):
    r = lax.broadcasted_iota(jnp.int32, (n, n), 0)
    c = lax.broadcasted_iota(jnp.int32, (n, n), 1)
    same = (r >> 6) == (c >> 6)
    tri = (c >= r) if upper else (c <= r)
    return jnp.where(same & tri, 1.0, 0.0).astype(BF16)


def _block_cumsum(x, tri):
    w = x.shape[1]
    hi = x.astype(BF16)
    r1 = x - hi.astype(F32)
    mid = r1.astype(BF16)
    lo = (r1 - mid.astype(F32)).astype(BF16)
    y = _dot(tri, jnp.concatenate([hi, mid, lo], axis=1))
    return y[:, 0:w] + y[:, w:2 * w] + y[:, 2 * w:3 * w]


def _first_step():
    return (pl.program_id(0) == 0) & (pl.program_id(1) == 0)


def _pick_row(x, row_ids, r):
    return jnp.sum(jnp.where(row_ids == r, x, 0.0), axis=0, keepdims=True)


def _chunk_terms(qc, kc, bc):
    row = lax.broadcasted_iota(jnp.int32, (CHUNK, 1), 0)
    blk = row >> 4
    betas = [jnp.zeros((1, HEAD_DIM), F32)] + [_pick_row(bc, row, SUB * i - 1) for i in range(1, N_SUB)]
    brow = jnp.zeros_like(bc)
    for i in range(1, N_SUB):
        brow = jnp.where(blk == i, betas[i], brow)
    qscale = jnp.exp(bc - brow)
    qs = qc * qscale
    qcat = jnp.concatenate([jnp.where(blk == i, qs, 0.0) for i in range(N_SUB)], axis=1)
    kscales = []
    for i in range(N_SUB):
        valid = row < SUB * (i + 1)
        kscales.append(jnp.where(valid, jnp.exp(jnp.where(valid, betas[i] - bc, 0.0)), 0.0))
    kcat = jnp.concatenate([kc * ks for ks in kscales], axis=1)
    b_last = _pick_row(bc, row, CHUNK - 1)
    return dict(row=row, blk=blk, qscale=qscale, qcat=qcat, kscales=kscales, kcat=kcat,
                eb=jnp.exp(bc), e_last=jnp.exp(b_last), ktscale=jnp.exp(b_last - bc))


def _causal(shape_rows_first):
    r = lax.broadcasted_iota(jnp.int32, (CHUNK, CHUNK), 0)
    c = lax.broadcasted_iota(jnp.int32, (CHUNK, CHUNK), 1)
    return (c <= r) if shape_rows_first else (r <= c)


def _hgrn_specs(tm, tile_of):
    col = lambda base: (lambda h, i: (tile_of(i), base + h))
    return [pl.BlockSpec((tm, HEAD_DIM), col(8)), pl.BlockSpec((tm, HEAD_DIM), col(16)),
            pl.BlockSpec((tm, HEAD_DIM), col(24)), pl.BlockSpec((tm, HEAD_DIM), col(32)),
            pl.BlockSpec((2, HEAD_DIM), lambda h, i: (0, h)), pl.BlockSpec((1, HEAD_DIM), lambda h, i: (0, h))]


def _hgrn_fwd(proj, logits, ng):
    t = proj.shape[0]
    tm = 512
    nc = tm // CHUNK
    nt = t // tm

    def body(zq_ref, zf_ref, v_ref, zg_ref, lg_ref, ng_ref, o_ref, og_ref, st_ref,
             s_scr, q_scr, k_scr, b_scr, tri_scr):
        @pl.when(_first_step())
        def _():
            tri_scr[...] = _block_tri(tm, upper=False)

        @pl.when(pl.program_id(1) == 0)
        def _():
            s_scr[...] = jnp.zeros_like(s_scr)

        lb = _lower_bound(lg_ref)
        zf = zf_ref[...]
        f = lb + (1.0 - lb) * _sigmoid(zf)
        k_scr[...] = (1.0 - lb) * _sigmoid(-zf)
        zq = zq_ref[...]
        q_scr[...] = zq * _sigmoid(zq)
        b_scr[...] = _block_cumsum(jnp.log(f), tri_scr[...])

        st = s_scr[...]
        for c in range(nc):
            rows = pl.ds(c * CHUNK, CHUNK)
            qc, kc, bc, vc = q_scr[rows, :], k_scr[rows, :], b_scr[rows, :], v_ref[rows, :]
            st_ref[0, c] = st
            tr = _chunk_terms(qc, kc, bc)
            a = jnp.where(_causal(True), _dot_nt(tr["qcat"].astype(BF16), tr["kcat"].astype(BF16)), 0.0)
            vb = vc.astype(BF16)
            o_ref[rows, :] = _dot(a.astype(BF16), vb) + _dot_nt((qc * tr["eb"]).astype(BF16), st.astype(BF16))
            st = st * tr["e_last"] + _dot_tn(vb, (kc * tr["ktscale"]).astype(BF16))
        s_scr[...] = st

        o = o_ref[...]
        rinv = lax.rsqrt(jnp.mean(o * o, axis=-1, keepdims=True) + RMS_EPS)
        zg = zg_ref[...]
        og_ref[...] = (o * rinv * ng_ref[...] * (zg * _sigmoid(zg))).astype(BF16)

    tile = pl.BlockSpec((tm, HEAD_DIM), lambda h, i: (i, h))
    return _call(
        body, name="hgrn_fwd", grid=(HGRN_HEADS, nt),
        in_specs=_hgrn_specs(tm, lambda i: i),
        out_specs=[tile, tile, pl.BlockSpec((1, nc, HEAD_DIM, HEAD_DIM), lambda h, i: (h, i, 0, 0))],
        out_shape=[_sds((t, HGRN_DIM), F32), _sds((t, HGRN_DIM), BF16),
                   _sds((HGRN_HEADS, t // CHUNK, HEAD_DIM, HEAD_DIM), F32)],
        scratch=[pltpu.VMEM((HEAD_DIM, HEAD_DIM), F32)] + [pltpu.VMEM((tm, HEAD_DIM), F32)] * 3
        + [pltpu.VMEM((tm, tm), BF16)],
    )(proj, proj, proj, proj, logits, ng)


def _hgrn_bwd(dog, o, states, proj, logits, ng, dstack):
    t = proj.shape[0]
    tm = 512
    nc = tm // CHUNK
    nt = t // tm

    def body(dog_ref, o_ref, st_ref, zq_ref, zf_ref, v_ref, zg_ref, lg_ref, ng_ref, stack_ref,
             dp_ref, dlg_ref, dng_ref,
             ds_scr, q_scr, k_scr, b_scr, do_scr, dq_scr, dk_scr, dv_scr, db_scr, dlb_scr, tri_scr):
        i = pl.program_id(1)

        @pl.when(_first_step())
        def _():
            tri_scr[0] = _block_tri(tm, upper=False)
            tri_scr[1] = _block_tri(tm, upper=True)

        @pl.when(i == 0)
        def _():
            ds_scr[...] = jnp.zeros_like(ds_scr)
            dlb_scr[...] = jnp.zeros_like(dlb_scr)
            dng_ref[...] = jnp.zeros_like(dng_ref)

        lb = _lower_bound(lg_ref)
        ng_row = ng_ref[...]
        o = o_ref[...]
        rinv = lax.rsqrt(jnp.mean(o * o, axis=-1, keepdims=True) + RMS_EPS)
        ohat = o * rinv
        zg = zg_ref[...]
        sg = _sigmoid(zg)
        dog_v = dog_ref[...]
        don = dog_v * (zg * sg)
        dp_ref[3] = (dog_v * (ohat * ng_row) * _silu_grad(zg, sg)).astype(BF16)
        dng_ref[...] += jnp.sum(don * ohat, axis=0, keepdims=True)
        dohat = don * ng_row
        do_scr[...] = rinv * (dohat - ohat * jnp.mean(dohat * ohat, axis=-1, keepdims=True))

        zf = zf_ref[...]
        s = _sigmoid(zf)
        s_neg = _sigmoid(-zf)
        f = lb + (1.0 - lb) * s
        k_scr[...] = (1.0 - lb) * s_neg
        zq = zq_ref[...]
        sq = _sigmoid(zq)
        q_scr[...] = zq * sq
        b_scr[...] = _block_cumsum(jnp.log(f), tri_scr[0])

        dst = ds_scr[...]
        for c in reversed(range(nc)):
            rows = pl.ds(c * CHUNK, CHUNK)
            qc, kc, bc, vc, doc = q_scr[rows, :], k_scr[rows, :], b_scr[rows, :], v_ref[rows, :], do_scr[rows, :]
            st = st_ref[0, c]
            tr = _chunk_terms(qc, kc, bc)
            qcb, kcb = tr["qcat"].astype(BF16), tr["kcat"].astype(BF16)
            dob, vb, dstb = doc.astype(BF16), vc.astype(BF16), dst.astype(BF16)
            a_t = jnp.where(_causal(False), _dot_nt(kcb, qcb), 0.0)
            da = jnp.where(_causal(True), _dot_nt(dob, vb), 0.0)
            da_t = jnp.where(_causal(False), _dot_nt(vb, dob), 0.0)
            dqcat = _dot(da.astype(BF16), kcb)
            dkcat = _dot(da_t.astype(BF16), qcb)
            kt = kc * tr["ktscale"]
            dv_scr[rows, :] = _dot(a_t.astype(BF16), dob) + _dot_nt(kt.astype(BF16), dstb)
            dq = jnp.zeros_like(qc)
            dk = jnp.zeros_like(kc)
            db = jnp.zeros_like(bc)
            for n in range(N_SUB):
                lanes = slice(n * HEAD_DIM, (n + 1) * HEAD_DIM)
                dq = dq + jnp.where(tr["blk"] == n, dqcat[:, lanes], 0.0)
                dk = dk + dkcat[:, lanes] * tr["kscales"][n]
                db = db + (qcb[:, lanes].astype(F32) * dqcat[:, lanes] - kcb[:, lanes].astype(F32) * dkcat[:, lanes])
            dq_inter = _dot(dob, st.astype(BF16)) * tr["eb"]
            dkt = _dot(vb, dstb)
            dk_inter = dkt * tr["ktscale"]
            extra = (jnp.sum(dkt * kt, axis=0, keepdims=True)
                     + tr["e_last"] * jnp.sum(dst * st, axis=0, keepdims=True))
            dq_scr[rows, :] = dq * tr["qscale"] + dq_inter
            dk_scr[rows, :] = dk + dk_inter
            db_scr[rows, :] = (db + qc * dq_inter - kc * dk_inter
                               + jnp.where(tr["row"] == CHUNK - 1, extra, 0.0))
            dst = dst * tr["e_last"] + _dot_tn(dob, (qc * tr["eb"]).astype(BF16))
        ds_scr[...] = dst

        dlogf = _block_cumsum(db_scr[...], tri_scr[1])
        df = dlogf / f - dk_scr[...]
        dp_ref[0] = (dq_scr[...] * _silu_grad(zq, sq)).astype(BF16)
        dp_ref[1] = (df * (1.0 - lb) * s * (1.0 - s)).astype(BF16)
        dp_ref[2] = dv_scr[...].astype(BF16)
        dlb_scr[...] += jnp.sum(df * s_neg, axis=0, keepdims=True)

        @pl.when(i == nt - 1)
        def _():
            dlogit = dlb_scr[...] * lb * (1.0 - lb)
            dlg_ref[0:1, :] = dlogit
            dlg_ref[1:2, :] = -dlogit

    rev = lambda i: nt - 1 - i
    tile = pl.BlockSpec((tm, HEAD_DIM), lambda h, i: (rev(i), h))
    return _call(
        body, name="hgrn_bwd", grid=(HGRN_HEADS, nt),
        in_specs=[tile, tile, pl.BlockSpec((1, nc, HEAD_DIM, HEAD_DIM), lambda h, i: (h, rev(i), 0, 0))]
        + _hgrn_specs(tm, rev) + [ANY],
        out_specs=[pl.BlockSpec((4, tm, HEAD_DIM), lambda h, i: (0, rev(i), h)),
                   pl.BlockSpec((2, HEAD_DIM), lambda h, i: (0, h)),
                   pl.BlockSpec((1, HEAD_DIM), lambda h, i: (0, h))],
        out_shape=[_sds(dstack.shape, BF16), _sds((2, HGRN_DIM), F32), _sds((1, HGRN_DIM), F32)],
        aliases={9: 0},
        scratch=[pltpu.VMEM((HEAD_DIM, HEAD_DIM), F32)] + [pltpu.VMEM((tm, HEAD_DIM), F32)] * 8
        + [pltpu.VMEM((1, HEAD_DIM), F32), pltpu.VMEM((2, tm, tm), BF16)],
    )(dog, o, states, proj, proj, proj, proj, logits, ng, dstack)


def _merge_fwd(cs, og, proj, x, wco, wh, wo, g1, b1):
    t = x.shape[0]
    tm = 256

    def body(cs_ref, og_ref, m0_ref, m1_ref, x_ref, wco_ref, wh_ref, wo_ref, g_ref, b_ref,
             y_ref, mixed_ref, r1_ref, x1_ref, x1b_ref):
        yc = _dot(cs_ref[...], wco_ref[...])
        yh = _dot(og_ref[...], wh_ref[...])
        y_ref[0] = yc
        y_ref[1] = yh
        mixed = (_sigmoid(m0_ref[...]) * yc + _sigmoid(m1_ref[...]) * yh).astype(BF16)
        mixed_ref[...] = mixed
        r1 = ALPHA * x_ref[...] + _dot(mixed, wo_ref[...])
        r1_ref[...] = r1
        xhat, _ = _ln_stats(r1)
        x1 = xhat * g_ref[...] + b_ref[...]
        x1_ref[...] = x1
        x1b_ref[...] = x1.astype(BF16)

    row = lambda w: pl.BlockSpec((tm, w), lambda i: (i, 0))
    full = lambda a: pl.BlockSpec(a.shape, lambda i: (0, 0))
    return _call(
        body, name="merge_fwd", grid=(t // tm,),
        in_specs=[row(CONV_DIM), row(HGRN_DIM),
                  pl.BlockSpec((tm, D_MODEL), lambda i: (i, 5)), pl.BlockSpec((tm, D_MODEL), lambda i: (i, 6)),
                  row(D_MODEL), full(wco), full(wh), full(wo), full(g1), full(b1)],
        out_specs=[pl.BlockSpec((2, tm, D_MODEL), lambda i: (0, i, 0)), row(D_MODEL), row(D_MODEL),
                   row(D_MODEL), row(D_MODEL)],
        out_shape=[_sds((2, t, D_MODEL), F32), _sds((t, D_MODEL), BF16), _sds((t, D_MODEL), F32),
                   _sds((t, D_MODEL), F32), _sds((t, D_MODEL), BF16)],
        vmem_mb=48)(cs, og, proj, proj, x, wco, wh, wo, g1, b1)


def _merge_bwd(dr1b, ycat, proj, wo, wco, wh):
    t = dr1b.shape[0]
    tm = 256

    def body(dr_ref, y_ref, m0_ref, m1_ref, wo_ref, wco_ref, wh_ref, dpm_ref, dy_ref, dcs_ref, dog_ref):
        dmixed = _dot_nt(dr_ref[...], wo_ref[...])
        g0 = _sigmoid(m0_ref[...])
        g1 = _sigmoid(m1_ref[...])
        dpm_ref[0] = (dmixed * y_ref[0] * g0 * (1.0 - g0)).astype(BF16)
        dpm_ref[1] = (dmixed * y_ref[1] * g1 * (1.0 - g1)).astype(BF16)
        dyc = (dmixed * g0).astype(BF16)
        dyh = (dmixed * g1).astype(BF16)
        dy_ref[0] = dyc
        dy_ref[1] = dyh
        dcs_ref[...] = _dot_nt(dyc, wco_ref[...])
        dog_ref[...] = _dot_nt(dyh, wh_ref[...])

    row = lambda w: pl.BlockSpec((tm, w), lambda i: (i, 0))
    pair = pl.BlockSpec((2, tm, D_MODEL), lambda i: (0, i, 0))
    full = lambda a: pl.BlockSpec(a.shape, lambda i: (0, 0))
    return _call(
        body, name="merge_bwd", grid=(t // tm,),
        in_specs=[row(D_MODEL), pair,
                  pl.BlockSpec((tm, D_MODEL), lambda i: (i, 5)), pl.BlockSpec((tm, D_MODEL), lambda i: (i, 6)),
                  full(wo), full(wco), full(wh)],
        out_specs=[pl.BlockSpec((2, tm, D_MODEL), lambda i: (DPROJ_GATE_SLAB // 2, i, 0)), pair,
                   row(CONV_DIM), row(HGRN_DIM)],
        out_shape=[_sds((DPROJ_SLABS, t, D_MODEL), BF16), _sds((2, t, D_MODEL), BF16),
                   _sds((t, CONV_DIM), F32), _sds((t, HGRN_DIM), F32)],
        vmem_mb=48)(dr1b, ycat, proj, proj, wo, wco, wh)


def _ffn_conv3(win, w_ref, off):
    return (w_ref[0:1, :] * win[off:off + ROW_BLOCK, :] + w_ref[1:2, :] * win[off + 1:off + 1 + ROW_BLOCK, :]
            + w_ref[2:3, :] * win[off + 2:off + 2 + ROW_BLOCK, :])


def _ffn_mid(z, wfd, bfd):
    t = z.shape[0]
    tm = 256

    def body(u_ref, gv_ref, w_ref, b_ref, h_ref, ext_ref):
        i = pl.program_id(0)

        @pl.when(i == 0)
        def _():
            ext_ref[0:FFN_HALO, :] = jnp.zeros((FFN_HALO, D_FF), F32)

        @pl.when(i > 0)
        def _():
            ext_ref[0:FFN_HALO, :] = ext_ref[tm:tm + FFN_HALO, :]

        ext_ref[FFN_HALO:FFN_HALO + tm, :] = u_ref[...]

        def block(r, carry):
            r0 = pl.multiple_of(r * ROW_BLOCK, ROW_BLOCK)
            rows = pl.ds(r0, ROW_BLOCK)
            win = ext_ref[pl.ds(r0, ROW_BLOCK + FFN_HALO), :]
            uc = _ffn_conv3(win, w_ref, FFN_HALO - 2) + b_ref[...]
            g, _ = _gelu_and_grad(uc)
            h_ref[rows, :] = (g * gv_ref[rows, :]).astype(BF16)
            return carry

        lax.fori_loop(0, tm // ROW_BLOCK, block, 0)

    return _call(
        body, name="ffn_mid", grid=(t // tm,),
        in_specs=[pl.BlockSpec((tm, D_FF), lambda i: (i, 0)), pl.BlockSpec((tm, D_FF), lambda i: (i, 1)),
                  pl.BlockSpec((FFN_KERNEL, D_FF), lambda i: (0, 0)), pl.BlockSpec((1, D_FF), lambda i: (0, 0))],
        out_specs=pl.BlockSpec((tm, D_FF), lambda i: (i, 0)),
        out_shape=_sds((t, D_FF), BF16),
        scratch=[pltpu.VMEM((tm + FFN_HALO, D_FF), F32)], vmem_mb=40)(z, z, wfd, bfd)


def _ffn_out_loss(hmid, x1, target, wfo, g2, b2):
    t = x1.shape[0]
    tm = 256
    inv_n = 1.0 / D_MODEL

    def body(h_ref, x1_ref, tg_ref, w_ref, g_ref, b_ref, dr_ref, drb_ref, loss_ref, dg_ref, db_ref):
        @pl.when(pl.program_id(0) == 0)
        def _():
            loss_ref[...] = jnp.zeros_like(loss_ref)
            dg_ref[...] = jnp.zeros_like(dg_ref)
            db_ref[...] = jnp.zeros_like(db_ref)

        r2 = ALPHA * x1_ref[...] + _dot(h_ref[...], w_ref[...])
        xhat, rstd = _ln_stats(r2)
        err = xhat * g_ref[...] + b_ref[...] - tg_ref[...]
        loss_ref[...] += 0.5 * inv_n * jnp.sum(err * err)
        dy = err * inv_n
        dg_ref[...] += jnp.sum(dy * xhat, axis=0, keepdims=True)
        db_ref[...] += jnp.sum(dy, axis=0, keepdims=True)
        dr = _ln_bwd(dy, xhat, rstd, g_ref[...])
        dr_ref[...] = dr
        drb_ref[...] = dr.astype(BF16)

    row = lambda w: pl.BlockSpec((tm, w), lambda i: (i, 0))
    vec = pl.BlockSpec((1, D_MODEL), lambda i: (0, 0))
    return _call(
        body, name="ffn_out_loss", grid=(t // tm,),
        in_specs=[row(D_FF), row(D_MODEL), row(D_MODEL), pl.BlockSpec((D_FF, D_MODEL), lambda i: (0, 0)), vec, vec],
        out_specs=[row(D_MODEL), row(D_MODEL), pl.BlockSpec((1, 128), lambda i: (0, 0)), vec, vec],
        out_shape=[_sds((t, D_MODEL), F32), _sds((t, D_MODEL), BF16), _sds((1, 128), F32),
                   _sds((1, D_MODEL), F32), _sds((1, D_MODEL), F32)],
        vmem_mb=40)(hmid, x1, target, wfo, g2, b2)


def _ffn_bwd_a(dr2b, z, wfo, wfd, bfd):
    t = z.shape[0]
    tm = 256
    nt = t // tm

    def body(dr_ref, u_ref, gv_ref, wfo_ref, w_ref, b_ref, dgv_ref, duc_ref, dw_ref, db_ref,
             ext_ref, dh_ref, acc_ref):
        i = pl.program_id(0)

        @pl.when(i == 0)
        def _():
            ext_ref[0:FFN_HALO, :] = jnp.zeros((FFN_HALO, D_FF), F32)
            acc_ref[...] = jnp.zeros_like(acc_ref)

        @pl.when(i > 0)
        def _():
            ext_ref[0:FFN_HALO, :] = ext_ref[tm:tm + FFN_HALO, :]

        ext_ref[FFN_HALO:FFN_HALO + tm, :] = u_ref[...]
        dh_ref[...] = _dot_nt(dr_ref[...], wfo_ref[...])

        def block(r, carry):
            r0 = pl.multiple_of(r * ROW_BLOCK, ROW_BLOCK)
            rows = pl.ds(r0, ROW_BLOCK)
            win = ext_ref[pl.ds(r0, ROW_BLOCK + FFN_HALO), :]
            off = FFN_HALO - 2
            uc = _ffn_conv3(win, w_ref, off) + b_ref[...]
            g, dg = _gelu_and_grad(uc)
            dh = dh_ref[rows, :]
            dgv_ref[rows, :] = (dh * g).astype(BF16)
            duc = dh * gv_ref[rows, :] * dg
            duc_ref[rows, :] = duc
            acc_ref[0:8, :] += _fold8(duc)
            for k in range(FFN_KERNEL):
                acc_ref[8 + 8 * k:16 + 8 * k, :] += _fold8(duc * win[off + k:off + k + ROW_BLOCK, :])
            return carry

        lax.fori_loop(0, tm // ROW_BLOCK, block, 0)

        @pl.when(i == nt - 1)
        def _():
            db_ref[...] = jnp.sum(acc_ref[0:8, :], axis=0, keepdims=True)
            for k in range(FFN_KERNEL):
                dw_ref[k:k + 1, :] = jnp.sum(acc_ref[8 + 8 * k:16 + 8 * k, :], axis=0, keepdims=True)

    tile = pl.BlockSpec((tm, D_FF), lambda i: (i, 0))
    return _call(
        body, name="ffn_bwd_a", grid=(nt,),
        in_specs=[pl.BlockSpec((tm, D_MODEL), lambda i: (i, 0)), tile, pl.BlockSpec((tm, D_FF), lambda i: (i, 1)),
                  pl.BlockSpec((D_FF, D_MODEL), lambda i: (0, 0)),
                  pl.BlockSpec((FFN_KERNEL, D_FF), lambda i: (0, 0)), pl.BlockSpec((1, D_FF), lambda i: (0, 0))],
        out_specs=[tile, tile, pl.BlockSpec((FFN_KERNEL, D_FF), lambda i: (0, 0)),
                   pl.BlockSpec((1, D_FF), lambda i: (0, 0))],
        out_shape=[_sds((t, D_FF), BF16), _sds((t, D_FF), F32), _sds((FFN_KERNEL, D_FF), F32), _sds((1, D_FF), F32)],
        scratch=[pltpu.VMEM((tm + FFN_HALO, D_FF), F32), pltpu.VMEM((tm, D_FF), F32),
                 pltpu.VMEM((8 + 8 * FFN_KERNEL, D_FF), F32)],
        vmem_mb=56)(dr2b, z, z, wfo, wfd, bfd)


def _ffn_bwd_b(duc, wfd):
    t = duc.shape[0]
    tm = 256
    nt = t // tm

    def body(duc_ref, w_ref, du_ref, ext_ref):
        i = pl.program_id(0)

        @pl.when(i == 0)
        def _():
            ext_ref[tm:tm + FFN_HALO, :] = jnp.zeros((FFN_HALO, D_FF), F32)

        @pl.when(i > 0)
        def _():
            ext_ref[tm:tm + FFN_HALO, :] = ext_ref[0:FFN_HALO, :]

        ext_ref[0:tm, :] = duc_ref[...]

        def block(r, carry):
            r0 = pl.multiple_of(r * ROW_BLOCK, ROW_BLOCK)
            win = ext_ref[pl.ds(r0, ROW_BLOCK + FFN_HALO), :]
            du = (w_ref[2:3, :] * win[0:ROW_BLOCK, :] + w_ref[1:2, :] * win[1:1 + ROW_BLOCK, :]
                  + w_ref[0:1, :] * win[2:2 + ROW_BLOCK, :])
            du_ref[pl.ds(r0, ROW_BLOCK), :] = du.astype(BF16)
            return carry

        lax.fori_loop(0, tm // ROW_BLOCK, block, 0)

    rev = lambda i: (nt - 1 - i, 0)
    return _call(
        body, name="ffn_bwd_b", grid=(nt,),
        in_specs=[pl.BlockSpec((tm, D_FF), rev), pl.BlockSpec((FFN_KERNEL, D_FF), lambda i: (0, 0))],
        out_specs=pl.BlockSpec((tm, D_FF), rev),
        out_shape=_sds((t, D_FF), BF16),
        scratch=[pltpu.VMEM((tm + FFN_HALO, D_FF), F32)], vmem_mb=40)(duc, wfd)


def _ffn_in_bwd(dr2, dub, dgvb, wfi, r1, g1):
    t = dr2.shape[0]
    tm = 256

    def body(dr2_ref, du_ref, dgv_ref, wu_ref, wg_ref, r1_ref, g_ref, dr1_ref, dr1b_ref, dg_ref, db_ref):
        @pl.when(pl.program_id(0) == 0)
        def _():
            dg_ref[...] = jnp.zeros_like(dg_ref)
            db_ref[...] = jnp.zeros_like(db_ref)

        dx1 = ALPHA * dr2_ref[...] + _dot_nt(du_ref[...], wu_ref[...]) + _dot_nt(dgv_ref[...], wg_ref[...])
        xhat, rstd = _ln_stats(r1_ref[...])
        dg_ref[...] += jnp.sum(dx1 * xhat, axis=0, keepdims=True)
        db_ref[...] += jnp.sum(dx1, axis=0, keepdims=True)
        dr1 = _ln_bwd(dx1, xhat, rstd, g_ref[...])
        dr1_ref[...] = dr1
        dr1b_ref[...] = dr1.astype(BF16)

    row = lambda w: pl.BlockSpec((tm, w), lambda i: (i, 0))
    vec = pl.BlockSpec((1, D_MODEL), lambda i: (0, 0))
    return _call(
        body, name="ffn_in_bwd", grid=(t // tm,),
        in_specs=[row(D_MODEL), row(D_FF), row(D_FF),
                  pl.BlockSpec((D_MODEL, D_FF), lambda i: (0, 0)), pl.BlockSpec((D_MODEL, D_FF), lambda i: (0, 1)),
                  row(D_MODEL), vec],
        out_specs=[row(D_MODEL), row(D_MODEL), vec, vec],
        out_shape=[_sds((t, D_MODEL), F32), _sds((t, D_MODEL), BF16), _sds((1, D_MODEL), F32), _sds((1, D_MODEL), F32)],
        vmem_mb=56)(dr2, dub, dgvb, wfi, wfi, r1, g1)


def _local_step(x, target, wi, wco, wh, wo, wfi, wfo, wcd, bcd, clg, clb, logits, ng, g1, b1, wfd, bfd, g2, b2):
    proj, xb = _proj(x, wi)
    cc, cs = _conv_fwd(proj, wcd, bcd, clg, clb)
    o, og, states = _hgrn_fwd(proj, logits, ng)
    ycat, mixed, r1, x1, x1b = _merge_fwd(cs, og, proj, x, wco, wh, wo, g1, b1)
    z = _mm_nn(x1b, wfi, tm=1024, tn=1408, name="ffn_in", vmem_mb=48)
    hmid = _ffn_mid(z, wfd, bfd)
    dr2, dr2b, loss, d_g2, d_b2 = _ffn_out_loss(hmid, x1, target, wfo, g2, b2)

    g_wfo = _mm_tn(hmid, _views(dr2b), tn=512, tt=1024, name="grad_w_ffn_out", vmem_mb=48)
    dgvb, duc, d_wfd, d_bfd = _ffn_bwd_a(dr2b, z, wfo, wfd, bfd)
    dub = _ffn_bwd_b(duc, wfd)
    g_wfi = _mm_tn(x1b, _views(dub, dgvb), tn=1408, tt=1024, name="grad_w_ffn_in", vmem_mb=48)
    dr1, dr1b, d_g1, d_b1 = _ffn_in_bwd(dr2, dub, dgvb, wfi, r1, g1)

    g_wo = _mm_tn(mixed, _views(dr1b), tn=1024, tt=1024, name="grad_w_out")
    dstack, dyb, dcs, dog = _merge_bwd(dr1b, ycat, proj, wo, wco, wh)
    g_wco = _mm_tn(cs, [(dyb, 0)], tn=1024, tt=1024, name="grad_w_conv_out")
    g_wh = _mm_tn(og, [(dyb, 1)], tn=1024, tt=1024, name="grad_w_hgrn_out")

    dcc, d_wcd, d_bcd, d_clg, d_clb = _conv_bwd_a(dcs, cc, proj, clg, clb)
    dstack = _conv_bwd_b(dcc, proj, wcd, dstack)
    dstack, d_logits, d_ng = _hgrn_bwd(dog, o, states, proj, logits, ng, dstack)

    g_wi = _grad_w_in(xb, dstack)
    grad_x = _grad_x(dstack, wi, dr1)

    small = dict(w_conv_dw=d_wcd, b_conv_dw=d_bcd, conv_ln_g=d_clg, conv_ln_b=d_clb, hgrn_lb_logits=d_logits,
                 hgrn_norm_g=d_ng, ln1_g=d_g1, ln1_b=d_b1, w_ffn_dw=d_wfd, b_ffn_dw=d_bfd, ln2_g=d_g2, ln2_b=d_b2)
    return loss, grad_x, (g_wi, g_wco, g_wh, g_wo, g_wfi, g_wfo), small


ELEMENTWISE_BLOCK_ELEMS = 256 * 1024


def _row_tile(rows, cols):
    cap = max(16, ELEMENTWISE_BLOCK_ELEMS // cols)
    if rows <= cap:
        return rows
    best = None
    for cand in range(16, cap + 1, 16):
        if rows % cand == 0:
            best = cand
    assert best is not None
    return best


def _elementwise(fn, ins, out_dtypes, *, name):
    r, c = ins[0].shape
    tr = _row_tile(r, c)

    def body(*refs):
        outs = fn(*[ref[...] for ref in refs[:len(ins)]])
        for ref, val in zip(refs[len(ins):], outs):
            ref[...] = val.astype(ref.dtype)

    spec = pl.BlockSpec((tr, c), lambda i: (i, 0))
    return _call(
        body, name=name, grid=(r // tr,), in_specs=[spec] * len(ins), out_specs=[spec] * len(out_dtypes),
        out_shape=[_sds((r, c), dt) for dt in out_dtypes])(*ins)


def _windowed(fn, sel, ins, outs, *, window, name):
    rows, cols = window
    tr = _row_tile(rows, cols)
    steps = rows // tr

    def spec(where):
        if where is None:
            return pl.BlockSpec((tr, cols), lambda i, s: (i, 0))
        kind, p, _ = where
        if kind == "rows":
            return pl.BlockSpec((tr, cols), lambda i, s: (s[p] * steps + i, 0))
        return pl.BlockSpec((tr, cols), lambda i, s: (i, s[p]))

    n_in = len(ins)

    def body(s_ref, *refs):
        vals = fn(*[ref[...] for ref in refs[:n_in]])
        for ref, val in zip(refs[n_in:], vals):
            ref[...] = val.astype(ref.dtype)

    return pl.pallas_call(
        body, name=name, out_shape=[_hbm(shape, dt) for shape, dt, _ in outs],
        grid_spec=pltpu.PrefetchScalarGridSpec(
            num_scalar_prefetch=1, grid=(steps,), in_specs=[spec(where) for _, where in ins],
            out_specs=[spec(where) for _, _, where in outs]),
        compiler_params=pltpu.CompilerParams(dimension_semantics=("arbitrary",), vmem_limit_bytes=32 * 2 ** 20),
    )(sel, *[_in_hbm(a) for a, _ in ins])


def _adamw(w, g, m, v, *, name):
    def fn(w_, g_, m_, v_):
        m_new = ADAM_B1 * m_ + (1.0 - ADAM_B1) * g_
        v_new = ADAM_B2 * v_ + (1.0 - ADAM_B2) * (g_ * g_)
        m_hat = m_new / ADAM_BC1
        v_hat = v_new / ADAM_BC2
        delta = -ADAM_LR * (m_hat / (jnp.sqrt(v_hat) + ADAM_EPS) + ADAM_WD * w_)
        return delta, m_new, v_new

    return _elementwise(fn, [w, g, m, v], [F32, F32, F32], name=name)


def _place():
    return lax.axis_index("x"), lax.axis_index("y"), lax.axis_index("c")


def _other_chips(x, y):
    return [(1 - x, y), (x, 1 - y), (1 - x, 1 - y)]


SHARD_XOR = (2, 1, 3)


DMA_CHUNK_BYTES = 512 * 1024


def _n_chunks(ref):
    rows = ref.shape[0]
    total = ref.dtype.itemsize
    for d in ref.shape:
        total *= d
    best = 1
    for cand in range(2, min(rows, total // DMA_CHUNK_BYTES) + 1):
        if rows % cand == 0 and (rows // cand) % 16 == 0:
            best = cand
    return best


class _Copy:
    def __init__(self, src, dst, sems, dev=None):
        if dev is None:
            make = lambda s_, d_: pltpu.make_async_copy(s_, d_, sems[0])
        else:
            make = lambda s_, d_: pltpu.make_async_remote_copy(
                src_ref=s_, dst_ref=d_, send_sem=sems[0], recv_sem=sems[1], device_id=dev, device_id_type=MESH)
        self.local = dev is None
        self.whole = make(src, dst)
        n = _n_chunks(src)
        step = src.shape[0] // n
        self.parts = ([self.whole] if n == 1 else
                      [make(src.at[pl.ds(i * step, step)], dst.at[pl.ds(i * step, step)]) for i in range(n)])

    def start(self):
        for part in self.parts:
            part.start()

    def wait_recv(self):
        self.whole.wait_recv()

    def wait_send(self):
        self.whole.wait_send()

    def wait(self):
        self.whole.wait()


def _run_copies(local_ops, remote_ops, lsem, ssem, rsem):
    local = [_Copy(src, dst, (lsem.at[n],)) for n, (src, dst) in enumerate(local_ops)]
    remote = [_Copy(src, dst, (ssem.at[n], rsem.at[n]), dev) for n, (src, dst, dev) in enumerate(remote_ops)]
    for cp in local + remote:
        cp.start()
    for cp in remote:
        cp.wait_recv()
    for cp in remote:
        cp.wait_send()
    for cp in local:
        cp.wait()


def _comm_call(body, *, name, n_in, out_shape, n_local, n_remote):
    return pl.pallas_call(
        body, name=name, in_specs=[ANY] * n_in, out_specs=[ANY] * len(out_shape), out_shape=out_shape,
        scratch_shapes=[pltpu.SemaphoreType.DMA((max(n_local, 1),)), pltpu.SemaphoreType.DMA((n_remote,)),
                        pltpu.SemaphoreType.DMA((n_remote,))])


BIG = (("w_in", D_MODEL, IN_COLS, 1), ("w_conv_out", CONV_DIM, D_MODEL, 1), ("w_hgrn_out", HGRN_DIM, D_MODEL, 0),
       ("w_out", D_MODEL, D_MODEL, 0), ("w_ffn_in", D_MODEL, 2 * D_FF, 1), ("w_ffn_out", D_FF, D_MODEL, 0))


def _shard_slice(ref, rows, cols, axis, k):
    if axis == 1:
        w = cols // N_CHIPS
        return ref.at[:, pl.ds(k * w, w)]
    h = rows // N_CHIPS
    return ref.at[pl.ds(k * h, h), :]


def _half_slice(ref, rows, cols, axis, hc):
    if axis == 1:
        return ref.at[pl.ds(hc * (rows // 2), rows // 2), :]
    return ref.at[:, pl.ds(hc * (cols // 2), cols // 2)]


def _half_shape(rows, cols, axis):
    return (rows // 2, cols) if axis == 1 else (rows, cols // 2)


def _gather_weights(full, small):
    n_big, n_small = len(full), len(small)
    n_arr = n_big + n_small
    out_shape = ([_sds((r, c), BF16) for _, r, c, _ in BIG]
                 + [_sds((N_CHIPS,) + a.shape, F32) for a in small])
    shard_shape = [(r, c // N_CHIPS) if ax == 1 else (r // N_CHIPS, c) for _, r, c, ax in BIG]

    def body(*refs):
        ins, outs = refs[:n_arr], refs[n_arr:2 * n_arr]
        lsem, ssem, rsem, fsem_s, fsem_r = refs[2 * n_arr:]
        x, y, c = _place()
        me = 2 * x + y
        chips = _other_chips(x, y)
        sibling = (x, y, 1 - c)

        def region(idx, k, hc, bufs=outs):
            (_, r, cc, ax), (sr, _) = BIG[idx], shard_shape[idx]
            return _shard_slice(bufs[idx], r, cc, ax, k).at[pl.ds(hc * (sr // 2), sr // 2)]

        for k in range(N_CHIPS):
            for hc in range(2):
                @pl.when((me == k) & (c == hc))
                def _(k=k, hc=hc):
                    local = [_Copy(ins[n_big + i], outs[n_big + i].at[k], (lsem.at[i],)) for i in range(n_small)]
                    sends, fwds = [], []
                    for j, (cx, cy) in enumerate(chips):
                        for i in range(n_big):
                            n = j * n_arr + i
                            sends.append(_Copy(region(i, k, hc, ins), region(i, k, hc), (ssem.at[n], rsem.at[n]),
                                               (cx, cy, c)))
                            reg = region(i, k ^ SHARD_XOR[j], hc)
                            fwds.append(_Copy(reg, reg, (fsem_s.at[j * n_big + i], fsem_r.at[j * n_big + i]), sibling))
                        for i in range(n_small):
                            n = j * n_arr + n_big + i
                            sends.append(_Copy(ins[n_big + i], outs[n_big + i].at[k], (ssem.at[n], rsem.at[n]),
                                               (cx, cy, c)))
                    for cp in local + sends:
                        cp.start()
                    for j in range(3):
                        for i in range(n_big):
                            sends[j * n_arr + i].wait_recv()
                        for i in range(n_big):
                            fwds[j * n_big + i].start()
                    for j in range(3):
                        for i in range(n_small):
                            sends[j * n_arr + n_big + i].wait_recv()
                    for cp in fwds:
                        cp.wait_recv()
                    for cp in sends + fwds:
                        cp.wait_send()
                    for cp in local:
                        cp.wait()

    return pl.pallas_call(
        body, name="gather_weights", in_specs=[ANY] * n_arr, out_specs=[ANY] * n_arr, out_shape=out_shape,
        input_output_aliases={i: i for i in range(n_big)},
        scratch_shapes=[pltpu.SemaphoreType.DMA((n_small,)), pltpu.SemaphoreType.DMA((3 * n_arr,)),
                        pltpu.SemaphoreType.DMA((3 * n_arr,)), pltpu.SemaphoreType.DMA((3 * n_big,)),
                        pltpu.SemaphoreType.DMA((3 * n_big,))])(*full, *small)


def _sibling_exchange(grads):
    n = len(BIG)
    shapes = [_sds(_half_shape(r, c, ax), F32) for _, r, c, ax in BIG]

    def body(*refs):
        ins, got = refs[:n], refs[n:2 * n]
        lsem, ssem, rsem = refs[2 * n:]
        x, y, c = _place()
        for k in range(2):
            @pl.when(c == k)
            def _(k=k):
                remote_ops = [(_half_slice(g, r, cc, ax, 1 - k), dst, (x, y, 1 - c))
                              for g, dst, (_, r, cc, ax) in zip(ins, got, BIG)]
                _run_copies([], remote_ops, lsem, ssem, rsem)

    return _comm_call(body, name="grad_sibling_exchange", n_in=n, out_shape=shapes, n_local=0, n_remote=n)(*grads)


def _piece_shape(rows, cols, axis):
    hr, hc = _half_shape(rows, cols, axis)
    return (hr, hc // N_CHIPS) if axis == 1 else (hr // N_CHIPS, hc)


def _chip_exchange(chip_sums):
    n = len(BIG)
    half = [_half_shape(r, c, ax) for _, r, c, ax in BIG]
    out_shape = [_sds(_piece_shape(r, c, ax), BF16) for _, r, c, ax in BIG for _ in range(3)]

    def body(*refs):
        ins, got = refs[:n], refs[n:4 * n]
        lsem, ssem, rsem = refs[4 * n:]
        x, y, c = _place()
        me = 2 * x + y
        for k in range(N_CHIPS):
            @pl.when(me == k)
            def _(k=k):
                remote_ops = [(_shard_slice(g, half[idx][0], half[idx][1], BIG[idx][3], k ^ SHARD_XOR[j]),
                               got[3 * idx + j], (cx, cy, c))
                              for j, (cx, cy) in enumerate(_other_chips(x, y))
                              for idx, g in enumerate(ins)]
                _run_copies([], remote_ops, lsem, ssem, rsem)

    outs = _comm_call(body, name="grad_chip_exchange", n_in=n, out_shape=out_shape, n_local=0, n_remote=3 * n)(
        *chip_sums)
    return [outs[3 * idx:3 * idx + 3] for idx in range(n)]


def _sibling_assemble(shards):
    n = len(BIG)
    shape = [(r, c // N_CHIPS) if ax == 1 else (r // N_CHIPS, c) for _, r, c, ax in BIG]

    def body(*refs):
        ins, outs = refs[:n], refs[n:2 * n]
        lsem, ssem, rsem = refs[2 * n:]
        x, y, c = _place()
        for k in range(2):
            @pl.when(c == k)
            def _(k=k):
                remote_ops = [(_half_slice(i_, sr, sc, ax, k), _half_slice(o_, sr, sc, ax, k), (x, y, 1 - c))
                              for i_, o_, (sr, sc), (_, _, _, ax) in zip(ins, outs, shape, BIG)]
                _run_copies([], remote_ops, lsem, ssem, rsem)

    return pl.pallas_call(
        body, name="grad_sibling_assemble", in_specs=[ANY] * n, out_specs=[ANY] * n,
        out_shape=[_sds(s, F32) for s in shape], input_output_aliases={i: i for i in range(n)},
        scratch_shapes=[pltpu.SemaphoreType.DMA((1,)), pltpu.SemaphoreType.DMA((n,)),
                        pltpu.SemaphoreType.DMA((n,))])(*shards)


def _all_reduce_small(packed):
    r, w = packed.shape

    def body(in_ref, out_ref, slots, lsem, ssem, rsem):
        x, y, c = _place()
        me = 4 * x + 2 * y + c
        peers = [(x ^ (m >> 2), y ^ ((m >> 1) & 1), c ^ (m & 1)) for m in range(1, N_DEV)]
        _run_copies([(in_ref, slots.at[me])], [(in_ref, slots.at[me], dev) for dev in peers], lsem, ssem, rsem)
        total = slots[0]
        for d in range(1, N_DEV):
            total = total + slots[d]
        out_ref[...] = total

    vmem = pl.BlockSpec(memory_space=pltpu.VMEM)
    return pl.pallas_call(
        body, name="small_all_reduce", in_specs=[vmem], out_specs=vmem, out_shape=_sds((r, w), F32),
        scratch_shapes=[pltpu.VMEM((N_DEV, r, w), F32), pltpu.SemaphoreType.DMA((1,)),
                        pltpu.SemaphoreType.DMA((N_DEV - 1,)), pltpu.SemaphoreType.DMA((N_DEV - 1,))])(packed)


SMALL_ORDER = ("w_conv_dw", "b_conv_dw", "conv_ln_g", "conv_ln_b", "hgrn_lb_logits", "hgrn_norm_g",
               "ln1_g", "ln1_b", "w_ffn_dw", "b_ffn_dw", "ln2_g", "ln2_b")
REPLICATED_SMALL = tuple(n for n in SMALL_ORDER if n not in ("w_conv_dw", "w_ffn_dw"))
WEIGHT_ORDER = ("w_in", "w_conv_dw", "b_conv_dw", "conv_ln_g", "conv_ln_b", "w_conv_out", "hgrn_lb_logits",
                "hgrn_norm_g", "w_hgrn_out", "w_out", "ln1_g", "ln1_b", "w_ffn_in", "w_ffn_dw", "b_ffn_dw",
                "w_ffn_out", "ln2_g", "ln2_b")


def _pack(arrs):
    flat = jnp.concatenate([a.reshape(-1) for a in arrs])
    assert flat.shape[0] % 128 == 0
    return flat.reshape(-1, 128)


def _unpack(packed, shapes):
    flat = packed.reshape(-1)
    out, pos = [], 0
    for shp in shapes:
        size = 1
        for d in shp:
            size *= d
        out.append(flat[pos:pos + size].reshape(shp))
        pos += size
    return out


def kernel(x, w_in, w_conv_dw, b_conv_dw, conv_ln_g, conv_ln_b, w_conv_out, hgrn_lb_logits, hgrn_norm_g, w_hgrn_out, w_out, ln1_g, ln1_b, w_ffn_in, w_ffn_dw, b_ffn_dw, w_ffn_out, ln2_g, ln2_b, loss_target, m_w_in, m_w_conv_dw, m_b_conv_dw, m_conv_ln_g, m_conv_ln_b, m_w_conv_out, m_hgrn_lb_logits, m_hgrn_norm_g, m_w_hgrn_out, m_w_out, m_ln1_g, m_ln1_b, m_w_ffn_in, m_w_ffn_dw, m_b_ffn_dw, m_w_ffn_out, m_ln2_g, m_ln2_b, v_w_in, v_w_conv_dw, v_b_conv_dw, v_conv_ln_g, v_conv_ln_b, v_w_conv_out, v_hgrn_lb_logits, v_hgrn_norm_g, v_w_hgrn_out, v_w_out, v_ln1_g, v_ln1_b, v_w_ffn_in, v_w_ffn_dw, v_b_ffn_dw, v_w_ffn_out, v_ln2_g, v_ln2_b):
    w = dict(w_in=w_in, w_conv_dw=w_conv_dw, b_conv_dw=b_conv_dw, conv_ln_g=conv_ln_g, conv_ln_b=conv_ln_b,
             w_conv_out=w_conv_out, hgrn_lb_logits=hgrn_lb_logits, hgrn_norm_g=hgrn_norm_g, w_hgrn_out=w_hgrn_out,
             w_out=w_out, ln1_g=ln1_g, ln1_b=ln1_b, w_ffn_in=w_ffn_in, w_ffn_dw=w_ffn_dw, b_ffn_dw=b_ffn_dw,
             w_ffn_out=w_ffn_out, ln2_g=ln2_g, ln2_b=ln2_b)
    m = dict(w_in=m_w_in, w_conv_dw=m_w_conv_dw, b_conv_dw=m_b_conv_dw, conv_ln_g=m_conv_ln_g, conv_ln_b=m_conv_ln_b,
             w_conv_out=m_w_conv_out, hgrn_lb_logits=m_hgrn_lb_logits, hgrn_norm_g=m_hgrn_norm_g,
             w_hgrn_out=m_w_hgrn_out, w_out=m_w_out, ln1_g=m_ln1_g, ln1_b=m_ln1_b, w_ffn_in=m_w_ffn_in,
             w_ffn_dw=m_w_ffn_dw, b_ffn_dw=m_b_ffn_dw, w_ffn_out=m_w_ffn_out, ln2_g=m_ln2_g, ln2_b=m_ln2_b)
    v = dict(w_in=v_w_in, w_conv_dw=v_w_conv_dw, b_conv_dw=v_b_conv_dw, conv_ln_g=v_conv_ln_g, conv_ln_b=v_conv_ln_b,
             w_conv_out=v_w_conv_out, hgrn_lb_logits=v_hgrn_lb_logits, hgrn_norm_g=v_hgrn_norm_g,
             w_hgrn_out=v_w_hgrn_out, w_out=v_w_out, ln1_g=v_ln1_g, ln1_b=v_ln1_b, w_ffn_in=v_w_ffn_in,
             w_ffn_dw=v_w_ffn_dw, b_ffn_dw=v_b_ffn_dw, w_ffn_out=v_w_ffn_out, ln2_g=v_ln2_g, ln2_b=v_ln2_b)
    big_names = [n for n, _, _, _ in BIG]
    w2 = {n: a[0] if a.ndim == 3 else a for n, a in w.items()}
    m2 = {n: a[0] if a.ndim == 3 else a for n, a in m.items()}
    v2 = {n: a[0] if a.ndim == 3 else a for n, a in v.items()}

    sel = jnp.stack([2 * lax.axis_index("x") + lax.axis_index("y"), lax.axis_index("c")]).astype(jnp.int32)
    sharded = lambda ax, p, n: ("cols", p, n) if ax == 1 else ("rows", p, n)
    across = lambda ax, p, n: ("rows", p, n) if ax == 1 else ("cols", p, n)

    placed = [_windowed(lambda a: (a,), sel, [(w2[n], None)], [((r, c), BF16, sharded(ax, 0, N_CHIPS))],
                        window=w2[n].shape, name="cast_" + n)[0] for n, r, c, ax in BIG]
    gathered = _gather_weights(placed, [w2["w_conv_dw"], w2["w_ffn_dw"]])
    wi, wco, wh, wo, wfi, wfo = gathered[:6]
    wcd = jnp.transpose(gathered[6], (1, 0, 2)).reshape(CONV_KERNEL, CONV_DIM)
    wfd = jnp.transpose(gathered[7], (1, 0, 2)).reshape(FFN_KERNEL, D_FF)

    loss_part, grad_x, big_grads, small_grads = _local_step(
        x[0], loss_target[0], wi, wco, wh, wo, wfi, wfo, wcd, w2["b_conv_dw"], w2["conv_ln_g"], w2["conv_ln_b"],
        w2["hgrn_lb_logits"], w2["hgrn_norm_g"], w2["ln1_g"], w2["ln1_b"], wfd, w2["b_ffn_dw"],
        w2["ln2_g"], w2["ln2_b"])
    loss = lax.psum(loss_part[0, 0], ("x", "y", "c"))

    got = _sibling_exchange(big_grads)
    chip_sums = [_windowed(lambda a, b: (a + b,), sel, [(g_, across(ax, 1, 2)), (h_, None)],
                           [(_half_shape(r, c, ax), BF16, None)], window=_half_shape(r, c, ax),
                           name="chip_sum_" + n)[0]
                 for (n, r, c, ax), g_, h_ in zip(BIG, big_grads, got)]
    recv = _chip_exchange(chip_sums)
    add4 = lambda a, b0, b1, b2: (a.astype(F32) + b0.astype(F32) + b1.astype(F32) + b2.astype(F32),)
    half_filled = [_windowed(add4, sel, [(cs_, sharded(ax, 0, N_CHIPS))] + [(r_, None) for r_ in recv_],
                             [((r, c // N_CHIPS) if ax == 1 else (r // N_CHIPS, c), F32, across(ax, 1, 2))],
                             window=_piece_shape(r, c, ax), name="shard_sum_" + n)[0]
                   for (n, r, c, ax), cs_, recv_ in zip(BIG, chip_sums, recv)]
    shard_grads = dict(zip(big_names, _sibling_assemble(half_filled)))

    small_shapes = [small_grads[n].shape for n in SMALL_ORDER]
    reduced = dict(zip(SMALL_ORDER, _unpack(_all_reduce_small(_pack([small_grads[n] for n in SMALL_ORDER])),
                                            small_shapes)))
    shard = 2 * lax.axis_index("x") + lax.axis_index("y")
    grads = dict(shard_grads)
    for n in REPLICATED_SMALL:
        grads[n] = reduced[n]
    grads["w_conv_dw"] = lax.dynamic_slice_in_dim(reduced["w_conv_dw"], shard * (CONV_DIM // N_CHIPS),
                                                  CONV_DIM // N_CHIPS, axis=1)
    grads["w_ffn_dw"] = lax.dynamic_slice_in_dim(reduced["w_ffn_dw"], shard * (D_FF // N_CHIPS),
                                                 D_FF // N_CHIPS, axis=1)

    delta, new_m, new_v = {}, {}, {}
    for n in big_names + ["w_conv_dw", "w_ffn_dw"]:
        delta[n], new_m[n], new_v[n] = _adamw(w2[n], grads[n], m2[n], v2[n], name="adamw_" + n)
    rep_shapes = [w2[n].shape for n in REPLICATED_SMALL]
    packed = _adamw(*[_pack([src[n] for n in REPLICATED_SMALL]) for src in (w2, grads, m2, v2)], name="adamw_small")
    for dst, pk in zip((delta, new_m, new_v), packed):
        for n, a in zip(REPLICATED_SMALL, _unpack(pk, rep_shapes)):
            dst[n] = a

    def shaped(d):
        return [d[n].reshape(w[n].shape) for n in WEIGHT_ORDER]

    return (loss, grad_x[None], *shaped(grads), *shaped(delta), *shaped(new_m), *shaped(new_v))
```

```python
import jax
import jax.numpy as jnp
from jax import lax
from jax.experimental import pallas as pl
from jax.experimental.pallas import tpu as pltpu

F32 = jnp.float32
BF16 = jnp.bfloat16

D_MODEL = 1024
CONV_DIM = 512
CONV_KERNEL = 31
HGRN_DIM = 1024
HGRN_HEADS = 8
HEAD_DIM = 128
CHUNK = 64
SUB = 16
N_SUB = CHUNK // SUB
D_FF = 2816
FFN_KERNEL = 3
IN_COLS = 7168
LN_EPS = 1e-5
RMS_EPS = 1e-6
ALPHA = 2.0 ** 0.25
GELU_C = 0.7978845608028654
GELU_A = 0.044715

ADAM_LR = 0.001
ADAM_B1 = 0.9
ADAM_B2 = 0.999
ADAM_EPS = 1e-08
ADAM_WD = 0.01
ADAM_STEP = 10
ADAM_BC1 = 1.0 - ADAM_B1 ** ADAM_STEP
ADAM_BC2 = 1.0 - ADAM_B2 ** ADAM_STEP

N_CHIPS = 4
N_DEV = 8
ROW_BLOCK = 32
CONV_HALO = 32
FFN_HALO = 8
MESH = pl.DeviceIdType.MESH
ANY = pl.BlockSpec(memory_space=pl.ANY)


def _dot(a, b):
    return jnp.dot(a, b, preferred_element_type=F32)


def _dot_nt(a, b):
    return lax.dot_general(a, b, (((1,), (1,)), ((), ())), preferred_element_type=F32)


def _dot_tn(a, b):
    return lax.dot_general(a, b, (((0,), (0,)), ((), ())), preferred_element_type=F32)


def _sigmoid(z):
    return jax.nn.sigmoid(z)


def _silu_grad(z, s):
    return s * (1.0 + z * (1.0 - s))


def _gelu_and_grad(u):
    u2 = u * u
    th = jnp.tanh(u * (GELU_C + (GELU_C * GELU_A) * u2))
    half = 0.5 + 0.5 * th
    g = u * half
    dg = half + (0.5 * u) * (1.0 - th * th) * (GELU_C + (3.0 * GELU_C * GELU_A) * u2)
    return g, dg


def _ln_stats(r):
    mu = jnp.mean(r, axis=-1, keepdims=True)
    xc = r - mu
    var = jnp.mean(xc * xc, axis=-1, keepdims=True)
    rstd = lax.rsqrt(var + LN_EPS)
    return xc * rstd, rstd


def _ln_bwd(dy, xhat, rstd, g):
    dxh = dy * g
    m1 = jnp.mean(dxh, axis=-1, keepdims=True)
    m2 = jnp.mean(dxh * xhat, axis=-1, keepdims=True)
    return rstd * (dxh - m1 - xhat * m2)


def _fold8(x):
    acc = x[0:8, :]
    for r in range(8, x.shape[0], 8):
        acc = acc + x[r:r + 8, :]
    return acc


def _in_hbm(a):
    return pltpu.with_memory_space_constraint(a, pltpu.HBM)


def _hbm(shape, dtype):
    return pltpu.HBM(shape, dtype)


def _out_hbm(out_shape):
    if isinstance(out_shape, (list, tuple)):
        return [_hbm(s.shape, s.dtype) for s in out_shape]
    return _hbm(out_shape.shape, out_shape.dtype)


def _call(body, *, name, grid, in_specs, out_specs, out_shape, scratch=(), vmem_mb=32, aliases=None):
    call = pl.pallas_call(
        body, name=name, grid=grid, in_specs=in_specs, out_specs=out_specs, out_shape=_out_hbm(out_shape),
        scratch_shapes=list(scratch), input_output_aliases=aliases or {},
        compiler_params=pltpu.CompilerParams(
            dimension_semantics=("arbitrary",) * len(grid), vmem_limit_bytes=vmem_mb * 2 ** 20))
    return lambda *args: call(*[_in_hbm(a) for a in args])


def _sds(shape, dtype):
    return jax.ShapeDtypeStruct(shape, dtype)


def _proj(x, w):
    t = x.shape[0]
    tm, tn = min(t, 1024), 1024

    def body(x_ref, w_ref, p_ref, xb_ref):
        @pl.when(pl.program_id(1) == 0)
        def _():
            xb_ref[...] = x_ref[...].astype(BF16)
        p_ref[...] = _dot(xb_ref[...], w_ref[...])

    return _call(
        body, name="proj", grid=(t // tm, IN_COLS // tn),
        in_specs=[pl.BlockSpec((tm, D_MODEL), lambda i, j: (i, 0)),
                  pl.BlockSpec((D_MODEL, tn), lambda i, j: (0, j))],
        out_specs=[pl.BlockSpec((tm, tn), lambda i, j: (i, j)),
                   pl.BlockSpec((tm, D_MODEL), lambda i, j: (i, 0))],
        out_shape=[_sds((t, IN_COLS), F32), _sds((t, D_MODEL), BF16)], vmem_mb=48)(x, w)


def _mm_nn(a, w, *, tm, tn, name, vmem_mb=32):
    t, k = a.shape
    n = w.shape[1]
    tm = min(tm, t)

    def body(a_ref, w_ref, o_ref):
        o_ref[...] = _dot(a_ref[...], w_ref[...])

    return _call(
        body, name=name, grid=(t // tm, n // tn),
        in_specs=[pl.BlockSpec((tm, k), lambda i, j: (i, 0)), pl.BlockSpec((k, tn), lambda i, j: (0, j))],
        out_specs=pl.BlockSpec((tm, tn), lambda i, j: (i, j)),
        out_shape=_sds((t, n), F32), vmem_mb=vmem_mb)(a, w)


def _views(*arrs):
    out = []
    for a in arrs:
        if a.ndim == 2:
            out.append((a, None))
        else:
            out.extend((a, p) for p in range(a.shape[0]))
    return out


def _piece_layout(views, tile):
    starts, counts, total = [], [], 0
    for arr, _ in views:
        width = arr.shape[-1]
        assert width % tile == 0
        starts.append(total)
        counts.append(width // tile)
        total += width // tile
    return starts, counts, total


def _mm_tn(a, views, *, tn, name, tt=512, vmem_mb=32):
    t, m = a.shape
    tt = min(tt, t)
    starts, counts, nj = _piece_layout(views, tn)
    n_views = len(views)

    def body(a_ref, *refs):
        b_refs, o_ref = refs[:n_views], refs[n_views]
        j = pl.program_id(0)

        @pl.when(pl.program_id(1) == 0)
        def _():
            o_ref[...] = jnp.zeros_like(o_ref)

        for b_ref, st, nb, (_, p) in zip(b_refs, starts, counts, views):
            @pl.when((j >= st) & (j < st + nb))
            def _(b_ref=b_ref, p=p):
                blk = b_ref[...] if p is None else b_ref[0]
                o_ref[...] += _dot_tn(a_ref[...], blk)

    def b_spec(st, nb, p):
        def rows(j, k):
            return jnp.where((j >= st) & (j < st + nb), k, 0)

        def cols(j):
            return jnp.clip(j - st, 0, nb - 1)

        if p is None:
            return pl.BlockSpec((tt, tn), lambda j, k: (rows(j, k), cols(j)))
        return pl.BlockSpec((1, tt, tn), lambda j, k: (p, rows(j, k), cols(j)))

    return _call(
        body, name=name, grid=(nj, t // tt),
        in_specs=[pl.BlockSpec((tt, m), lambda j, k: (k, 0))]
        + [b_spec(st, nb, p) for st, nb, (_, p) in zip(starts, counts, views)],
        out_specs=pl.BlockSpec((m, tn), lambda j, k: (0, j)),
        out_shape=_sds((m, nj * tn), F32), vmem_mb=vmem_mb)(a, *[arr for arr, _ in views])


DPROJ_SLABS = 7
DPROJ_GATE_SLAB = 4
DPROJ_CONV_SLAB = 6


def _slab_cols(s):
    return (s + 1) % DPROJ_SLABS


def _grad_w_in(xb, dstack):
    t = xb.shape[0]
    tt = min(t, 1024)

    def body(a_ref, b_ref, o_ref):
        @pl.when(pl.program_id(1) == 0)
        def _():
            o_ref[...] = jnp.zeros_like(o_ref)

        o_ref[...] += _dot_tn(a_ref[...], b_ref[0])

    return _call(
        body, name="grad_w_in", grid=(DPROJ_SLABS, t // tt),
        in_specs=[pl.BlockSpec((tt, D_MODEL), lambda j, k: (k, 0)),
                  pl.BlockSpec((1, tt, D_MODEL), lambda j, k: (j, k, 0))],
        out_specs=pl.BlockSpec((D_MODEL, D_MODEL), lambda j, k: (0, _slab_cols(j))),
        out_shape=_sds((D_MODEL, IN_COLS), F32), vmem_mb=48)(xb, dstack)


def _grad_x(dstack, wi, dr1):
    t = dr1.shape[0]
    tm = min(t, 1024)

    def body(add_ref, b_ref, w_ref, o_ref):
        @pl.when(pl.program_id(1) == 0)
        def _():
            o_ref[...] = ALPHA * add_ref[...]

        o_ref[...] += _dot_nt(b_ref[0], w_ref[...])

    return _call(
        body, name="grad_x", grid=(t // tm, DPROJ_SLABS),
        in_specs=[pl.BlockSpec((tm, D_MODEL), lambda i, k: (i, 0)),
                  pl.BlockSpec((1, tm, D_MODEL), lambda i, k: (k, i, 0)),
                  pl.BlockSpec((D_MODEL, D_MODEL), lambda i, k: (0, _slab_cols(k)))],
        out_specs=pl.BlockSpec((tm, D_MODEL), lambda i, k: (i, 0)),
        out_shape=_sds((t, D_MODEL), F32), vmem_mb=48)(dr1, dstack, wi)


LANES = 128
CONV_FWD_OFFSETS = {k: 2 + k for k in range(CONV_KERNEL)}
CONV_BWD_OFFSETS = {k: CONV_KERNEL - 1 - k for k in range(CONV_KERNEL)}


def _conv_taps(win, offsets):
    n = win.shape[0]
    for b in range(8):
        taps = [k for k, o in offsets.items() if o % 8 == b]
        if not taps:
            continue
        shifted = win if b == 0 else pltpu.roll(win, n - b, 0)
        for k in taps:
            first = offsets[k] - b
            yield k, shifted[first:first + ROW_BLOCK, :]


def _conv_fwd(proj, wcd, bcd, lng, lnb):
    t = proj.shape[0]
    tm = 512

    def body(cv_ref, cg_ref, w_ref, b_ref, g_ref, be_ref, cc_ref, cs_ref, ext_ref):
        i = pl.program_id(0)

        @pl.when(i == 0)
        def _():
            ext_ref[0:CONV_HALO, :] = jnp.zeros((CONV_HALO, CONV_DIM), F32)

        @pl.when(i > 0)
        def _():
            ext_ref[0:CONV_HALO, :] = ext_ref[tm:tm + CONV_HALO, :]

        ext_ref[CONV_HALO:CONV_HALO + tm, :] = cv_ref[...] * _sigmoid(cg_ref[...])

        def block(r, carry):
            r0 = pl.multiple_of(r * ROW_BLOCK, ROW_BLOCK)
            groups = []
            for g in range(CONV_DIM // LANES):
                lanes = slice(g * LANES, (g + 1) * LANES)
                win = ext_ref[pl.ds(r0, ROW_BLOCK + CONV_HALO), lanes]
                acc = jnp.broadcast_to(b_ref[:, lanes], (ROW_BLOCK, LANES))
                for k, rows_k in _conv_taps(win, CONV_FWD_OFFSETS):
                    acc = acc + w_ref[k:k + 1, lanes] * rows_k
                groups.append(acc)
            acc = jnp.concatenate(groups, axis=1)
            cc_ref[pl.ds(r0, ROW_BLOCK), :] = acc
            xhat, _ = _ln_stats(acc)
            a = xhat * g_ref[...] + be_ref[...]
            cs_ref[pl.ds(r0, ROW_BLOCK), :] = (a * _sigmoid(a)).astype(BF16)
            return carry

        lax.fori_loop(0, tm // ROW_BLOCK, block, 0)

    vec = pl.BlockSpec((1, CONV_DIM), lambda i: (0, 0))
    return _call(
        body, name="conv_fwd", grid=(t // tm,),
        in_specs=[pl.BlockSpec((tm, CONV_DIM), lambda i: (i, 0)), pl.BlockSpec((tm, CONV_DIM), lambda i: (i, 1)),
                  pl.BlockSpec((CONV_KERNEL, CONV_DIM), lambda i: (0, 0)), vec, vec, vec],
        out_specs=[pl.BlockSpec((tm, CONV_DIM), lambda i: (i, 0)), pl.BlockSpec((tm, CONV_DIM), lambda i: (i, 0))],
        out_shape=[_sds((t, CONV_DIM), F32), _sds((t, CONV_DIM), BF16)],
        scratch=[pltpu.VMEM((tm + CONV_HALO, CONV_DIM), F32)])(proj, proj, wcd, bcd, lng, lnb)


def _conv_bwd_a(dcs, cc, proj, lng, lnb):
    t = proj.shape[0]
    tm = 512
    nt = t // tm

    def body(dcs_ref, cc_ref, cv_ref, cg_ref, g_ref, be_ref,
             dcc_ref, dw_ref, db_ref, dg_ref, dbe_ref, ext_ref, accw_ref, acc3_ref):
        i = pl.program_id(0)

        @pl.when(i == 0)
        def _():
            ext_ref[0:CONV_HALO, :] = jnp.zeros((CONV_HALO, CONV_DIM), F32)
            accw_ref[...] = jnp.zeros_like(accw_ref)
            acc3_ref[...] = jnp.zeros_like(acc3_ref)

        @pl.when(i > 0)
        def _():
            ext_ref[0:CONV_HALO, :] = ext_ref[tm:tm + CONV_HALO, :]

        ext_ref[CONV_HALO:CONV_HALO + tm, :] = cv_ref[...] * _sigmoid(cg_ref[...])

        def block(r, carry):
            r0 = pl.multiple_of(r * ROW_BLOCK, ROW_BLOCK)
            rows = pl.ds(r0, ROW_BLOCK)
            xhat, rstd = _ln_stats(cc_ref[rows, :])
            a = xhat * g_ref[...] + be_ref[...]
            sg = _sigmoid(a)
            da = dcs_ref[rows, :] * _silu_grad(a, sg)
            acc3_ref[8:16, :] += _fold8(da * xhat)
            acc3_ref[16:24, :] += _fold8(da)
            dcc = _ln_bwd(da, xhat, rstd, g_ref[...])
            dcc_ref[rows, :] = dcc
            acc3_ref[0:8, :] += _fold8(dcc)
            for g in range(CONV_DIM // LANES):
                lanes = slice(g * LANES, (g + 1) * LANES)
                win = ext_ref[pl.ds(r0, ROW_BLOCK + CONV_HALO), lanes]
                dcc_g = dcc[:, lanes]
                for k, rows_k in _conv_taps(win, CONV_FWD_OFFSETS):
                    accw_ref[8 * k:8 * k + 8, lanes] += _fold8(dcc_g * rows_k)
            return carry

        lax.fori_loop(0, tm // ROW_BLOCK, block, 0)

        @pl.when(i == nt - 1)
        def _():
            for k in range(CONV_KERNEL):
                dw_ref[k:k + 1, :] = jnp.sum(accw_ref[8 * k:8 * k + 8, :], axis=0, keepdims=True)
            db_ref[...] = jnp.sum(acc3_ref[0:8, :], axis=0, keepdims=True)
            dg_ref[...] = jnp.sum(acc3_ref[8:16, :], axis=0, keepdims=True)
            dbe_ref[...] = jnp.sum(acc3_ref[16:24, :], axis=0, keepdims=True)

    vec = pl.BlockSpec((1, CONV_DIM), lambda i: (0, 0))
    tile = pl.BlockSpec((tm, CONV_DIM), lambda i: (i, 0))
    return _call(
        body, name="conv_bwd_a", grid=(nt,),
        in_specs=[tile, tile, tile, pl.BlockSpec((tm, CONV_DIM), lambda i: (i, 1)), vec, vec],
        out_specs=[tile, pl.BlockSpec((CONV_KERNEL, CONV_DIM), lambda i: (0, 0)), vec, vec, vec],
        out_shape=[_sds((t, CONV_DIM), F32), _sds((CONV_KERNEL, CONV_DIM), F32),
                   _sds((1, CONV_DIM), F32), _sds((1, CONV_DIM), F32), _sds((1, CONV_DIM), F32)],
        scratch=[pltpu.VMEM((tm + CONV_HALO, CONV_DIM), F32),
                 pltpu.VMEM((8 * CONV_KERNEL, CONV_DIM), F32),
                 pltpu.VMEM((24, CONV_DIM), F32)])(dcs, cc, proj, proj, lng, lnb)


def _conv_bwd_b(dcc, proj, wcd, dstack):
    t = proj.shape[0]
    tm = 512
    nt = t // tm

    def body(dcc_ref, cv_ref, cg_ref, w_ref, stack_ref, out_ref, ext_ref):
        del stack_ref
        i = pl.program_id(0)

        @pl.when(i == 0)
        def _():
            ext_ref[tm:tm + CONV_HALO, :] = jnp.zeros((CONV_HALO, CONV_DIM), F32)

        @pl.when(i > 0)
        def _():
            ext_ref[tm:tm + CONV_HALO, :] = ext_ref[0:CONV_HALO, :]

        ext_ref[0:tm, :] = dcc_ref[...]

        def block(r, carry):
            r0 = pl.multiple_of(r * ROW_BLOCK, ROW_BLOCK)
            rows = pl.ds(r0, ROW_BLOCK)
            for g in range(CONV_DIM // LANES):
                lanes = slice(g * LANES, (g + 1) * LANES)
                gate_lanes = slice(CONV_DIM + g * LANES, CONV_DIM + (g + 1) * LANES)
                win = ext_ref[pl.ds(r0, ROW_BLOCK + CONV_HALO), lanes]
                acc = jnp.zeros((ROW_BLOCK, LANES), F32)
                for k, rows_k in _conv_taps(win, CONV_BWD_OFFSETS):
                    acc = acc + w_ref[k:k + 1, lanes] * rows_k
                sg = _sigmoid(cg_ref[rows, lanes])
                out_ref[0, rows, lanes] = (acc * sg).astype(BF16)
                out_ref[0, rows, gate_lanes] = (acc * cv_ref[rows, lanes] * sg * (1.0 - sg)).astype(BF16)
            return carry

        lax.fori_loop(0, tm // ROW_BLOCK, block, 0)

    rev = lambda i: (nt - 1 - i, 0)
    return _call(
        body, name="conv_bwd_b", grid=(nt,),
        in_specs=[pl.BlockSpec((tm, CONV_DIM), rev), pl.BlockSpec((tm, CONV_DIM), rev),
                  pl.BlockSpec((tm, CONV_DIM), lambda i: (nt - 1 - i, 1)),
                  pl.BlockSpec((CONV_KERNEL, CONV_DIM), lambda i: (0, 0)), ANY],
        out_specs=pl.BlockSpec((1, tm, 2 * CONV_DIM), lambda i: (DPROJ_CONV_SLAB, nt - 1 - i, 0)),
        out_shape=_sds(dstack.shape, BF16), aliases={4: 0},
        scratch=[pltpu.VMEM((tm + CONV_HALO, CONV_DIM), F32)])(dcc, proj, proj, wcd, dstack)


def _lower_bound(lg_ref):
    a0, a1 = lg_ref[0:1, :], lg_ref[1:2, :]
    m = jnp.maximum(a0, a1)
    e0, e1 = jnp.exp(a0 - m), jnp.exp(a1 - m)
    return e0 / (e0 + e1)


def _block_tri(n, upper):
    r = lax.broadcasted_iota(jnp.int32, (n, n), 0)
    c = lax.broadcasted_iota(jnp.int32, (n, n), 1)
    same = (r >> 6) == (c >> 6)
    tri = (c >= r) if upper else (c <= r)
    return jnp.where(same & tri, 1.0, 0.0).astype(BF16)


def _block_cumsum(x, tri):
    w = x.shape[1]
    hi = x.astype(BF16)
    r1 = x - hi.astype(F32)
    mid = r1.astype(BF16)
    lo = (r1 - mid.astype(F32)).astype(BF16)
    y = _dot(tri, jnp.concatenate([hi, mid, lo], axis=1))
    return y[:, 0:w] + y[:, w:2 * w] + y[:, 2 * w:3 * w]


def _first_step():
    return (pl.program_id(0) == 0) & (pl.program_id(1) == 0)


def _block_rows(i):
    return slice(SUB * i, SUB * (i + 1))


def _chunk_terms(qc, kc, bc, b_ref, first_row):
    betas = [jnp.zeros((1, HEAD_DIM), F32)]
    betas += [b_ref[first_row + SUB * i - 1:first_row + SUB * i, :] for i in range(1, N_SUB)]
    b_last = b_ref[first_row + CHUNK - 1:first_row + CHUNK, :]
    zeros = lambda rows: jnp.zeros((rows, HEAD_DIM), F32)
    qscale = [jnp.exp(bc[_block_rows(i), :] - betas[i]) for i in range(N_SUB)]
    qs = [qc[_block_rows(i), :] * qscale[i] for i in range(N_SUB)]
    kscale = [jnp.exp(betas[n] - bc[0:SUB * (n + 1), :]) for n in range(N_SUB)]
    ks = [kc[0:SUB * (n + 1), :] * kscale[n] for n in range(N_SUB)]

    def tall(parts):
        parts = [p for p in parts if p.shape[0]]
        return parts[0] if len(parts) == 1 else jnp.concatenate(parts, axis=0)

    qcat = jnp.concatenate([tall([zeros(SUB * n), qs[n], zeros(CHUNK - SUB * (n + 1))]) for n in range(N_SUB)],
                           axis=1)
    kcat = jnp.concatenate([tall([ks[n], zeros(CHUNK - SUB * (n + 1))]) for n in range(N_SUB)], axis=1)
    return dict(qscale=qscale, qcat=qcat, kscale=kscale, kcat=kcat,
                eb=jnp.exp(bc), e_last=jnp.exp(b_last), ktscale=jnp.exp(b_last - bc))


def _causal(shape_rows_first):
    r = lax.broadcasted_iota(jnp.int32, (CHUNK, CHUNK), 0)
    c = lax.broadcasted_iota(jnp.int32, (CHUNK, CHUNK), 1)
    return (c <= r) if shape_rows_first else (r <= c)


def _hgrn_specs(tm, tile_of):
    col = lambda base: (lambda h, i: (tile_of(i), base + h))
    return [pl.BlockSpec((tm, HEAD_DIM), col(8)), pl.BlockSpec((tm, HEAD_DIM), col(16)),
            pl.BlockSpec((tm, HEAD_DIM), col(24)), pl.BlockSpec((tm, HEAD_DIM), col(32)),
            pl.BlockSpec((2, HEAD_DIM), lambda h, i: (0, h)), pl.BlockSpec((1, HEAD_DIM), lambda h, i: (0, h))]


def _hgrn_fwd(proj, logits, ng):
    t = proj.shape[0]
    tm = 512
    nc = tm // CHUNK
    nt = t // tm

    def body(zq_ref, zf_ref, v_ref, zg_ref, lg_ref, ng_ref, o_ref, og_ref, st_ref,
             s_scr, q_scr, k_scr, b_scr, tri_scr):
        @pl.when(_first_step())
        def _():
            tri_scr[...] = _block_tri(tm, upper=False)

        @pl.when(pl.program_id(1) == 0)
        def _():
            s_scr[...] = jnp.zeros_like(s_scr)

        lb = _lower_bound(lg_ref)
        zf = zf_ref[...]
        f = lb + (1.0 - lb) * _sigmoid(zf)
        k_scr[...] = (1.0 - lb) * _sigmoid(-zf)
        zq = zq_ref[...]
        q_scr[...] = zq * _sigmoid(zq)
        b_scr[...] = _block_cumsum(jnp.log(f), tri_scr[...])

        st = s_scr[...]
        for c in range(nc):
            rows = pl.ds(c * CHUNK, CHUNK)
            qc, kc, bc, vc = q_scr[rows, :], k_scr[rows, :], b_scr[rows, :], v_ref[rows, :]
            st_ref[0, c] = st
            tr = _chunk_terms(qc, kc, bc, b_scr, c * CHUNK)
            a = jnp.where(_causal(True), _dot_nt(tr["qcat"].astype(BF16), tr["kcat"].astype(BF16)), 0.0)
            vb = vc.astype(BF16)
            o_ref[rows, :] = _dot(a.astype(BF16), vb) + _dot_nt((qc * tr["eb"]).astype(BF16), st.astype(BF16))
            st = st * tr["e_last"] + _dot_tn(vb, (kc * tr["ktscale"]).astype(BF16))
        s_scr[...] = st

        o = o_ref[...]
        rinv = lax.rsqrt(jnp.mean(o * o, axis=-1, keepdims=True) + RMS_EPS)
        zg = zg_ref[...]
        og_ref[...] = (o * rinv * ng_ref[...] * (zg * _sigmoid(zg))).astype(BF16)

    tile = pl.BlockSpec((tm, HEAD_DIM), lambda h, i: (i, h))
    return _call(
        body, name="hgrn_fwd", grid=(HGRN_HEADS, nt),
        in_specs=_hgrn_specs(tm, lambda i: i),
        out_specs=[tile, tile, pl.BlockSpec((1, nc, HEAD_DIM, HEAD_DIM), lambda h, i: (h, i, 0, 0))],
        out_shape=[_sds((t, HGRN_DIM), F32), _sds((t, HGRN_DIM), BF16),
                   _sds((HGRN_HEADS, t // CHUNK, HEAD_DIM, HEAD_DIM), F32)],
        scratch=[pltpu.VMEM((HEAD_DIM, HEAD_DIM), F32)] + [pltpu.VMEM((tm, HEAD_DIM), F32)] * 3
        + [pltpu.VMEM((tm, tm), BF16)],
    )(proj, proj, proj, proj, logits, ng)


def _hgrn_bwd(dog, o, states, proj, logits, ng, dstack):
    t = proj.shape[0]
    tm = 512
    nc = tm // CHUNK
    nt = t // tm

    def body(dog_ref, o_ref, st_ref, zq_ref, zf_ref, v_ref, zg_ref, lg_ref, ng_ref, stack_ref,
             dp_ref, dlg_ref, dng_ref,
             ds_scr, q_scr, k_scr, b_scr, do_scr, dq_scr, dk_scr, dv_scr, db_scr, dlb_scr, tri_scr):
        i = pl.program_id(1)

        @pl.when(_first_step())
        def _():
            tri_scr[0] = _block_tri(tm, upper=False)
            tri_scr[1] = _block_tri(tm, upper=True)

        @pl.when(i == 0)
        def _():
            ds_scr[...] = jnp.zeros_like(ds_scr)
            dlb_scr[...] = jnp.zeros_like(dlb_scr)
            dng_ref[...] = jnp.zeros_like(dng_ref)

        lb = _lower_bound(lg_ref)
        ng_row = ng_ref[...]
        o = o_ref[...]
        rinv = lax.rsqrt(jnp.mean(o * o, axis=-1, keepdims=True) + RMS_EPS)
        ohat = o * rinv
        zg = zg_ref[...]
        sg = _sigmoid(zg)
        dog_v = dog_ref[...]
        don = dog_v * (zg * sg)
        dp_ref[3] = (dog_v * (ohat * ng_row) * _silu_grad(zg, sg)).astype(BF16)
        dng_ref[...] += jnp.sum(don * ohat, axis=0, keepdims=True)
        dohat = don * ng_row
        do_scr[...] = rinv * (dohat - ohat * jnp.mean(dohat * ohat, axis=-1, keepdims=True))

        zf = zf_ref[...]
        s = _sigmoid(zf)
        s_neg = _sigmoid(-zf)
        f = lb + (1.0 - lb) * s
        k_scr[...] = (1.0 - lb) * s_neg
        zq = zq_ref[...]
        sq = _sigmoid(zq)
        q_scr[...] = zq * sq
        b_scr[...] = _block_cumsum(jnp.log(f), tri_scr[0])

        dst = ds_scr[...]
        for c in reversed(range(nc)):
            rows = pl.ds(c * CHUNK, CHUNK)
            qc, kc, bc, vc, doc = q_scr[rows, :], k_scr[rows, :], b_scr[rows, :], v_ref[rows, :], do_scr[rows, :]
            st = st_ref[0, c]
            tr = _chunk_terms(qc, kc, bc, b_scr, c * CHUNK)
            qcb, kcb = tr["qcat"].astype(BF16), tr["kcat"].astype(BF16)
            dob, vb, dstb = doc.astype(BF16), vc.astype(BF16), dst.astype(BF16)
            a_t = jnp.where(_causal(False), _dot_nt(kcb, qcb), 0.0)
            da = jnp.where(_causal(True), _dot_nt(dob, vb), 0.0)
            da_t = jnp.where(_causal(False), _dot_nt(vb, dob), 0.0)
            dqcat = _dot(da.astype(BF16), kcb)
            dkcat = _dot(da_t.astype(BF16), qcb)
            kt = kc * tr["ktscale"]
            dv_scr[rows, :] = _dot(a_t.astype(BF16), dob) + _dot_nt(kt.astype(BF16), dstb)
            dq_blocks, dk_blocks, db_blocks = [], [], []
            for j in range(N_SUB):
                rows_j = _block_rows(j)
                lanes_j = slice(j * HEAD_DIM, (j + 1) * HEAD_DIM)
                dq_j = dqcat[rows_j, lanes_j]
                dq_blocks.append(dq_j * tr["qscale"][j])
                db_j = qcb[rows_j, lanes_j].astype(F32) * dq_j
                dk_j = jnp.zeros((SUB, HEAD_DIM), F32)
                for n in range(j, N_SUB):
                    lanes_n = slice(n * HEAD_DIM, (n + 1) * HEAD_DIM)
                    dk_jn = dkcat[rows_j, lanes_n]
                    dk_j = dk_j + dk_jn * tr["kscale"][n][rows_j, :]
                    db_j = db_j - kcb[rows_j, lanes_n].astype(F32) * dk_jn
                dk_blocks.append(dk_j)
                db_blocks.append(db_j)
            dq_inter = _dot(dob, st.astype(BF16)) * tr["eb"]
            dkt = _dot(vb, dstb)
            dk_inter = dkt * tr["ktscale"]
            extra = (jnp.sum(dkt * kt, axis=0, keepdims=True)
                     + tr["e_last"] * jnp.sum(dst * st, axis=0, keepdims=True))
            dq_scr[rows, :] = jnp.concatenate(dq_blocks, axis=0) + dq_inter
            dk_scr[rows, :] = jnp.concatenate(dk_blocks, axis=0) + dk_inter
            db_scr[rows, :] = jnp.concatenate(db_blocks, axis=0) + qc * dq_inter - kc * dk_inter
            last = c * CHUNK + CHUNK - 1
            db_scr[last:last + 1, :] += extra
            dst = dst * tr["e_last"] + _dot_tn(dob, (qc * tr["eb"]).astype(BF16))
        ds_scr[...] = dst

        dlogf = _block_cumsum(db_scr[...], tri_scr[1])
        df = dlogf / f - dk_scr[...]
        dp_ref[0] = (dq_scr[...] * _silu_grad(zq, sq)).astype(BF16)
        dp_ref[1] = (df * (1.0 - lb) * s * (1.0 - s)).astype(BF16)
        dp_ref[2] = dv_scr[...].astype(BF16)
        dlb_scr[...] += jnp.sum(df * s_neg, axis=0, keepdims=True)

        @pl.when(i == nt - 1)
        def _():
            dlogit = dlb_scr[...] * lb * (1.0 - lb)
            dlg_ref[0:1, :] = dlogit
            dlg_ref[1:2, :] = -dlogit

    rev = lambda i: nt - 1 - i
    tile = pl.BlockSpec((tm, HEAD_DIM), lambda h, i: (rev(i), h))
    return _call(
        body, name="hgrn_bwd", grid=(HGRN_HEADS, nt),
        in_specs=[tile, tile, pl.BlockSpec((1, nc, HEAD_DIM, HEAD_DIM), lambda h, i: (h, rev(i), 0, 0))]
        + _hgrn_specs(tm, rev) + [ANY],
        out_specs=[pl.BlockSpec((4, tm, HEAD_DIM), lambda h, i: (0, rev(i), h)),
                   pl.BlockSpec((2, HEAD_DIM), lambda h, i: (0, h)),
                   pl.BlockSpec((1, HEAD_DIM), lambda h, i: (0, h))],
        out_shape=[_sds(dstack.shape, BF16), _sds((2, HGRN_DIM), F32), _sds((1, HGRN_DIM), F32)],
        aliases={9: 0},
        scratch=[pltpu.VMEM((HEAD_DIM, HEAD_DIM), F32)] + [pltpu.VMEM((tm, HEAD_DIM), F32)] * 8
        + [pltpu.VMEM((1, HEAD_DIM), F32), pltpu.VMEM((2, tm, tm), BF16)],
    )(dog, o, states, proj, proj, proj, proj, logits, ng, dstack)


def _merge_fwd(cs, og, proj, x, wco, wh, wo, g1, b1):
    t = x.shape[0]
    tm = 256

    def body(cs_ref, og_ref, m0_ref, m1_ref, x_ref, wco_ref, wh_ref, wo_ref, g_ref, b_ref,
             y_ref, mixed_ref, r1_ref, x1_ref, x1b_ref):
        yc = _dot(cs_ref[...], wco_ref[...])
        yh = _dot(og_ref[...], wh_ref[...])
        y_ref[0] = yc
        y_ref[1] = yh
        mixed = (_sigmoid(m0_ref[...]) * yc + _sigmoid(m1_ref[...]) * yh).astype(BF16)
        mixed_ref[...] = mixed
        r1 = ALPHA * x_ref[...] + _dot(mixed, wo_ref[...])
        r1_ref[...] = r1
        xhat, _ = _ln_stats(r1)
        x1 = xhat * g_ref[...] + b_ref[...]
        x1_ref[...] = x1
        x1b_ref[...] = x1.astype(BF16)

    row = lambda w: pl.BlockSpec((tm, w), lambda i: (i, 0))
    full = lambda a: pl.BlockSpec(a.shape, lambda i: (0, 0))
    return _call(
        body, name="merge_fwd", grid=(t // tm,),
        in_specs=[row(CONV_DIM), row(HGRN_DIM),
                  pl.BlockSpec((tm, D_MODEL), lambda i: (i, 5)), pl.BlockSpec((tm, D_MODEL), lambda i: (i, 6)),
                  row(D_MODEL), full(wco), full(wh), full(wo), full(g1), full(b1)],
        out_specs=[pl.BlockSpec((2, tm, D_MODEL), lambda i: (0, i, 0)), row(D_MODEL), row(D_MODEL),
                   row(D_MODEL), row(D_MODEL)],
        out_shape=[_sds((2, t, D_MODEL), F32), _sds((t, D_MODEL), BF16), _sds((t, D_MODEL), F32),
                   _sds((t, D_MODEL), F32), _sds((t, D_MODEL), BF16)],
        vmem_mb=48)(cs, og, proj, proj, x, wco, wh, wo, g1, b1)


def _merge_bwd(dr1b, ycat, proj, wo, wco, wh):
    t = dr1b.shape[0]
    tm = 256

    def body(dr_ref, y_ref, m0_ref, m1_ref, wo_ref, wco_ref, wh_ref, dpm_ref, dy_ref, dcs_ref, dog_ref):
        dmixed = _dot_nt(dr_ref[...], wo_ref[...])
        g0 = _sigmoid(m0_ref[...])
        g1 = _sigmoid(m1_ref[...])
        dpm_ref[0] = (dmixed * y_ref[0] * g0 * (1.0 - g0)).astype(BF16)
        dpm_ref[1] = (dmixed * y_ref[1] * g1 * (1.0 - g1)).astype(BF16)
        dyc = (dmixed * g0).astype(BF16)
        dyh = (dmixed * g1).astype(BF16)
        dy_ref[0] = dyc
        dy_ref[1] = dyh
        dcs_ref[...] = _dot_nt(dyc, wco_ref[...])
        dog_ref[...] = _dot_nt(dyh, wh_ref[...])

    row = lambda w: pl.BlockSpec((tm, w), lambda i: (i, 0))
    pair = pl.BlockSpec((2, tm, D_MODEL), lambda i: (0, i, 0))
    full = lambda a: pl.BlockSpec(a.shape, lambda i: (0, 0))
    return _call(
        body, name="merge_bwd", grid=(t // tm,),
        in_specs=[row(D_MODEL), pair,
                  pl.BlockSpec((tm, D_MODEL), lambda i: (i, 5)), pl.BlockSpec((tm, D_MODEL), lambda i: (i, 6)),
                  full(wo), full(wco), full(wh)],
        out_specs=[pl.BlockSpec((2, tm, D_MODEL), lambda i: (DPROJ_GATE_SLAB // 2, i, 0)), pair,
                   row(CONV_DIM), row(HGRN_DIM)],
        out_shape=[_sds((DPROJ_SLABS, t, D_MODEL), BF16), _sds((2, t, D_MODEL), BF16),
                   _sds((t, CONV_DIM), F32), _sds((t, HGRN_DIM), F32)],
        vmem_mb=48)(dr1b, ycat, proj, proj, wo, wco, wh)


def _ffn_taps(win):
    return (pltpu.roll(win, 2, 0)[FFN_HALO:, :], pltpu.roll(win, 1, 0)[FFN_HALO:, :], win[FFN_HALO:, :])


def _ffn_conv3(taps, w_ref):
    return w_ref[0:1, :] * taps[0] + w_ref[1:2, :] * taps[1] + w_ref[2:3, :] * taps[2]


def _ffn_mid(z, wfd, bfd):
    t = z.shape[0]
    tm = 256

    def body(u_ref, gv_ref, w_ref, b_ref, h_ref, ext_ref):
        i = pl.program_id(0)

        @pl.when(i == 0)
        def _():
            ext_ref[0:FFN_HALO, :] = jnp.zeros((FFN_HALO, D_FF), F32)

        @pl.when(i > 0)
        def _():
            ext_ref[0:FFN_HALO, :] = ext_ref[tm:tm + FFN_HALO, :]

        ext_ref[FFN_HALO:FFN_HALO + tm, :] = u_ref[...]

        def block(r, carry):
            r0 = pl.multiple_of(r * ROW_BLOCK, ROW_BLOCK)
            rows = pl.ds(r0, ROW_BLOCK)
            win = ext_ref[pl.ds(r0, ROW_BLOCK + FFN_HALO), :]
            uc = _ffn_conv3(_ffn_taps(win), w_ref) + b_ref[...]
            g, _ = _gelu_and_grad(uc)
            h_ref[rows, :] = (g * gv_ref[rows, :]).astype(BF16)
            return carry

        lax.fori_loop(0, tm // ROW_BLOCK, block, 0)

    return _call(
        body, name="ffn_mid", grid=(t // tm,),
        in_specs=[pl.BlockSpec((tm, D_FF), lambda i: (i, 0)), pl.BlockSpec((tm, D_FF), lambda i: (i, 1)),
                  pl.BlockSpec((FFN_KERNEL, D_FF), lambda i: (0, 0)), pl.BlockSpec((1, D_FF), lambda i: (0, 0))],
        out_specs=pl.BlockSpec((tm, D_FF), lambda i: (i, 0)),
        out_shape=_sds((t, D_FF), BF16),
        scratch=[pltpu.VMEM((tm + FFN_HALO, D_FF), F32)], vmem_mb=40)(z, z, wfd, bfd)


def _ffn_out_loss(hmid, x1, target, wfo, g2, b2):
    t = x1.shape[0]
    tm = 256
    inv_n = 1.0 / D_MODEL

    def body(h_ref, x1_ref, tg_ref, w_ref, g_ref, b_ref, dr_ref, drb_ref, loss_ref, dg_ref, db_ref):
        @pl.when(pl.program_id(0) == 0)
        def _():
            loss_ref[...] = jnp.zeros_like(loss_ref)
            dg_ref[...] = jnp.zeros_like(dg_ref)
            db_ref[...] = jnp.zeros_like(db_ref)

        r2 = ALPHA * x1_ref[...] + _dot(h_ref[...], w_ref[...])
        xhat, rstd = _ln_stats(r2)
        err = xhat * g_ref[...] + b_ref[...] - tg_ref[...]
        loss_ref[...] += 0.5 * inv_n * jnp.sum(err * err)
        dy = err * inv_n
        dg_ref[...] += jnp.sum(dy * xhat, axis=0, keepdims=True)
        db_ref[...] += jnp.sum(dy, axis=0, keepdims=True)
        dr = _ln_bwd(dy, xhat, rstd, g_ref[...])
        dr_ref[...] = dr
        drb_ref[...] = dr.astype(BF16)

    row = lambda w: pl.BlockSpec((tm, w), lambda i: (i, 0))
    vec = pl.BlockSpec((1, D_MODEL), lambda i: (0, 0))
    return _call(
        body, name="ffn_out_loss", grid=(t // tm,),
        in_specs=[row(D_FF), row(D_MODEL), row(D_MODEL), pl.BlockSpec((D_FF, D_MODEL), lambda i: (0, 0)), vec, vec],
        out_specs=[row(D_MODEL), row(D_MODEL), pl.BlockSpec((1, 128), lambda i: (0, 0)), vec, vec],
        out_shape=[_sds((t, D_MODEL), F32), _sds((t, D_MODEL), BF16), _sds((1, 128), F32),
                   _sds((1, D_MODEL), F32), _sds((1, D_MODEL), F32)],
        vmem_mb=40)(hmid, x1, target, wfo, g2, b2)


def _ffn_bwd_a(dr2b, z, wfo, wfd, bfd):
    t = z.shape[0]
    tm = 256
    nt = t // tm

    def body(dr_ref, u_ref, gv_ref, wfo_ref, w_ref, b_ref, dgv_ref, duc_ref, dw_ref, db_ref,
             ext_ref, dh_ref, acc_ref):
        i = pl.program_id(0)

        @pl.when(i == 0)
        def _():
            ext_ref[0:FFN_HALO, :] = jnp.zeros((FFN_HALO, D_FF), F32)
            acc_ref[...] = jnp.zeros_like(acc_ref)

        @pl.when(i > 0)
        def _():
            ext_ref[0:FFN_HALO, :] = ext_ref[tm:tm + FFN_HALO, :]

        ext_ref[FFN_HALO:FFN_HALO + tm, :] = u_ref[...]
        dh_ref[...] = _dot_nt(dr_ref[...], wfo_ref[...])

        def block(r, carry):
            r0 = pl.multiple_of(r * ROW_BLOCK, ROW_BLOCK)
            rows = pl.ds(r0, ROW_BLOCK)
            taps = _ffn_taps(ext_ref[pl.ds(r0, ROW_BLOCK + FFN_HALO), :])
            uc = _ffn_conv3(taps, w_ref) + b_ref[...]
            g, dg = _gelu_and_grad(uc)
            dh = dh_ref[rows, :]
            dgv_ref[rows, :] = (dh * g).astype(BF16)
            duc = dh * gv_ref[rows, :] * dg
            duc_ref[rows, :] = duc
            acc_ref[0:8, :] += _fold8(duc)
            for k in range(FFN_KERNEL):
                acc_ref[8 + 8 * k:16 + 8 * k, :] += _fold8(duc * taps[k])
            return carry

        lax.fori_loop(0, tm // ROW_BLOCK, block, 0)

        @pl.when(i == nt - 1)
        def _():
            db_ref[...] = jnp.sum(acc_ref[0:8, :], axis=0, keepdims=True)
            for k in range(FFN_KERNEL):
                dw_ref[k:k + 1, :] = jnp.sum(acc_ref[8 + 8 * k:16 + 8 * k, :], axis=0, keepdims=True)

    tile = pl.BlockSpec((tm, D_FF), lambda i: (i, 0))
    return _call(
        body, name="ffn_bwd_a", grid=(nt,),
        in_specs=[pl.BlockSpec((tm, D_MODEL), lambda i: (i, 0)), tile, pl.BlockSpec((tm, D_FF), lambda i: (i, 1)),
                  pl.BlockSpec((D_FF, D_MODEL), lambda i: (0, 0)),
                  pl.BlockSpec((FFN_KERNEL, D_FF), lambda i: (0, 0)), pl.BlockSpec((1, D_FF), lambda i: (0, 0))],
        out_specs=[tile, tile, pl.BlockSpec((FFN_KERNEL, D_FF), lambda i: (0, 0)),
                   pl.BlockSpec((1, D_FF), lambda i: (0, 0))],
        out_shape=[_sds((t, D_FF), BF16), _sds((t, D_FF), F32), _sds((FFN_KERNEL, D_FF), F32), _sds((1, D_FF), F32)],
        scratch=[pltpu.VMEM((tm + FFN_HALO, D_FF), F32), pltpu.VMEM((tm, D_FF), F32),
                 pltpu.VMEM((8 + 8 * FFN_KERNEL, D_FF), F32)],
        vmem_mb=56)(dr2b, z, z, wfo, wfd, bfd)


def _ffn_bwd_b(duc, wfd):
    t = duc.shape[0]
    tm = 256
    nt = t // tm

    def body(duc_ref, w_ref, du_ref, ext_ref):
        i = pl.program_id(0)

        @pl.when(i == 0)
        def _():
            ext_ref[tm:tm + FFN_HALO, :] = jnp.zeros((FFN_HALO, D_FF), F32)

        @pl.when(i > 0)
        def _():
            ext_ref[tm:tm + FFN_HALO, :] = ext_ref[0:FFN_HALO, :]

        ext_ref[0:tm, :] = duc_ref[...]

        def block(r, carry):
            r0 = pl.multiple_of(r * ROW_BLOCK, ROW_BLOCK)
            win = ext_ref[pl.ds(r0, ROW_BLOCK + FFN_HALO), :]
            n = ROW_BLOCK + FFN_HALO
            du = (w_ref[2:3, :] * win[0:ROW_BLOCK, :] + w_ref[1:2, :] * pltpu.roll(win, n - 1, 0)[0:ROW_BLOCK, :]
                  + w_ref[0:1, :] * pltpu.roll(win, n - 2, 0)[0:ROW_BLOCK, :])
            du_ref[pl.ds(r0, ROW_BLOCK), :] = du.astype(BF16)
            return carry

        lax.fori_loop(0, tm // ROW_BLOCK, block, 0)

    rev = lambda i: (nt - 1 - i, 0)
    return _call(
        body, name="ffn_bwd_b", grid=(nt,),
        in_specs=[pl.BlockSpec((tm, D_FF), rev), pl.BlockSpec((FFN_KERNEL, D_FF), lambda i: (0, 0))],
        out_specs=pl.BlockSpec((tm, D_FF), rev),
        out_shape=_sds((t, D_FF), BF16),
        scratch=[pltpu.VMEM((tm + FFN_HALO, D_FF), F32)], vmem_mb=40)(duc, wfd)


def _ffn_in_bwd(dr2, dub, dgvb, wfi, r1, g1):
    t = dr2.shape[0]
    tm = 256

    def body(dr2_ref, du_ref, dgv_ref, wu_ref, wg_ref, r1_ref, g_ref, dr1_ref, dr1b_ref, dg_ref, db_ref):
        @pl.when(pl.program_id(0) == 0)
        def _():
            dg_ref[...] = jnp.zeros_like(dg_ref)
            db_ref[...] = jnp.zeros_like(db_ref)

        dx1 = ALPHA * dr2_ref[...] + _dot_nt(du_ref[...], wu_ref[...]) + _dot_nt(dgv_ref[...], wg_ref[...])
        xhat, rstd = _ln_stats(r1_ref[...])
        dg_ref[...] += jnp.sum(dx1 * xhat, axis=0, keepdims=True)
        db_ref[...] += jnp.sum(dx1, axis=0, keepdims=True)
        dr1 = _ln_bwd(dx1, xhat, rstd, g_ref[...])
        dr1_ref[...] = dr1
        dr1b_ref[...] = dr1.astype(BF16)

    row = lambda w: pl.BlockSpec((tm, w), lambda i: (i, 0))
    vec = pl.BlockSpec((1, D_MODEL), lambda i: (0, 0))
    return _call(
        body, name="ffn_in_bwd", grid=(t // tm,),
        in_specs=[row(D_MODEL), row(D_FF), row(D_FF),
                  pl.BlockSpec((D_MODEL, D_FF), lambda i: (0, 0)), pl.BlockSpec((D_MODEL, D_FF), lambda i: (0, 1)),
                  row(D_MODEL), vec],
        out_specs=[row(D_MODEL), row(D_MODEL), vec, vec],
        out_shape=[_sds((t, D_MODEL), F32), _sds((t, D_MODEL), BF16), _sds((1, D_MODEL), F32), _sds((1, D_MODEL), F32)],
        vmem_mb=56)(dr2, dub, dgvb, wfi, wfi, r1, g1)


def _local_step(x, target, wi, wco, wh, wo, wfi, wfo, wcd, bcd, clg, clb, logits, ng, g1, b1, wfd, bfd, g2, b2):
    proj, xb = _proj(x, wi)
    cc, cs = _conv_fwd(proj, wcd, bcd, clg, clb)
    o, og, states = _hgrn_fwd(proj, logits, ng)
    ycat, mixed, r1, x1, x1b = _merge_fwd(cs, og, proj, x, wco, wh, wo, g1, b1)
    z = _mm_nn(x1b, wfi, tm=1024, tn=1408, name="ffn_in", vmem_mb=48)
    hmid = _ffn_mid(z, wfd, bfd)
    dr2, dr2b, loss, d_g2, d_b2 = _ffn_out_loss(hmid, x1, target, wfo, g2, b2)

    g_wfo = _mm_tn(hmid, _views(dr2b), tn=512, tt=1024, name="grad_w_ffn_out", vmem_mb=48)
    dgvb, duc, d_wfd, d_bfd = _ffn_bwd_a(dr2b, z, wfo, wfd, bfd)
    dub = _ffn_bwd_b(duc, wfd)
    g_wfi = _mm_tn(x1b, _views(dub, dgvb), tn=1408, tt=1024, name="grad_w_ffn_in", vmem_mb=48)
    dr1, dr1b, d_g1, d_b1 = _ffn_in_bwd(dr2, dub, dgvb, wfi, r1, g1)

    g_wo = _mm_tn(mixed, _views(dr1b), tn=1024, tt=1024, name="grad_w_out")
    dstack, dyb, dcs, dog = _merge_bwd(dr1b, ycat, proj, wo, wco, wh)
    g_wco = _mm_tn(cs, [(dyb, 0)], tn=1024, tt=1024, name="grad_w_conv_out")
    g_wh = _mm_tn(og, [(dyb, 1)], tn=1024, tt=1024, name="grad_w_hgrn_out")

    dcc, d_wcd, d_bcd, d_clg, d_clb = _conv_bwd_a(dcs, cc, proj, clg, clb)
    dstack = _conv_bwd_b(dcc, proj, wcd, dstack)
    dstack, d_logits, d_ng = _hgrn_bwd(dog, o, states, proj, logits, ng, dstack)

    g_wi = _grad_w_in(xb, dstack)
    grad_x = _grad_x(dstack, wi, dr1)

    small = dict(w_conv_dw=d_wcd, b_conv_dw=d_bcd, conv_ln_g=d_clg, conv_ln_b=d_clb, hgrn_lb_logits=d_logits,
                 hgrn_norm_g=d_ng, ln1_g=d_g1, ln1_b=d_b1, w_ffn_dw=d_wfd, b_ffn_dw=d_bfd, ln2_g=d_g2, ln2_b=d_b2)
    return loss, grad_x, (g_wi, g_wco, g_wh, g_wo, g_wfi, g_wfo), small


ELEMENTWISE_BLOCK_ELEMS = 256 * 1024


def _row_tile(rows, cols):
    cap = max(16, ELEMENTWISE_BLOCK_ELEMS // cols)
    if rows <= cap:
        return rows
    best = None
    for cand in range(16, cap + 1, 16):
        if rows % cand == 0:
            best = cand
    assert best is not None
    return best


def _elementwise(fn, ins, out_dtypes, *, name):
    r, c = ins[0].shape
    tr = _row_tile(r, c)

    def body(*refs):
        outs = fn(*[ref[...] for ref in refs[:len(ins)]])
        for ref, val in zip(refs[len(ins):], outs):
            ref[...] = val.astype(ref.dtype)

    spec = pl.BlockSpec((tr, c), lambda i: (i, 0))
    return _call(
        body, name=name, grid=(r // tr,), in_specs=[spec] * len(ins), out_specs=[spec] * len(out_dtypes),
        out_shape=[_sds((r, c), dt) for dt in out_dtypes])(*ins)


def _windowed(fn, sel, ins, outs, *, window, name):
    rows, cols = window
    tr = _row_tile(rows, cols)
    steps = rows // tr

    def spec(where):
        if where is None:
            return pl.BlockSpec((tr, cols), lambda i, s: (i, 0))
        kind, p, _ = where
        if kind == "rows":
            return pl.BlockSpec((tr, cols), lambda i, s: (s[p] * steps + i, 0))
        return pl.BlockSpec((tr, cols), lambda i, s: (i, s[p]))

    n_in = len(ins)

    def body(s_ref, *refs):
        vals = fn(*[ref[...] for ref in refs[:n_in]])
        for ref, val in zip(refs[n_in:], vals):
            ref[...] = val.astype(ref.dtype)

    return pl.pallas_call(
        body, name=name, out_shape=[_hbm(shape, dt) for shape, dt, _ in outs],
        grid_spec=pltpu.PrefetchScalarGridSpec(
            num_scalar_prefetch=1, grid=(steps,), in_specs=[spec(where) for _, where in ins],
            out_specs=[spec(where) for _, _, where in outs]),
        compiler_params=pltpu.CompilerParams(dimension_semantics=("arbitrary",), vmem_limit_bytes=32 * 2 ** 20),
    )(sel, *[_in_hbm(a) for a, _ in ins])


def _adamw(w, g, m, v, *, name):
    def fn(w_, g_, m_, v_):
        m_new = ADAM_B1 * m_ + (1.0 - ADAM_B1) * g_
        v_new = ADAM_B2 * v_ + (1.0 - ADAM_B2) * (g_ * g_)
        m_hat = m_new / ADAM_BC1
        v_hat = v_new / ADAM_BC2
        delta = -ADAM_LR * (m_hat / (jnp.sqrt(v_hat) + ADAM_EPS) + ADAM_WD * w_)
        return delta, m_new, v_new

    return _elementwise(fn, [w, g, m, v], [F32, F32, F32], name=name)


def _place():
    return lax.axis_index("x"), lax.axis_index("y"), lax.axis_index("c")


def _other_chips(x, y):
    return [(1 - x, y), (x, 1 - y), (1 - x, 1 - y)]


SHARD_XOR = (2, 1, 3)


DMA_CHUNK_BYTES = 512 * 1024


def _n_chunks(ref):
    rows = ref.shape[0]
    total = ref.dtype.itemsize
    for d in ref.shape:
        total *= d
    best = 1
    for cand in range(2, min(rows, total // DMA_CHUNK_BYTES) + 1):
        if rows % cand == 0 and (rows // cand) % 16 == 0:
            best = cand
    return best


class _Copy:
    def __init__(self, src, dst, sems, dev=None):
        if dev is None:
            make = lambda s_, d_: pltpu.make_async_copy(s_, d_, sems[0])
        else:
            make = lambda s_, d_: pltpu.make_async_remote_copy(
                src_ref=s_, dst_ref=d_, send_sem=sems[0], recv_sem=sems[1], device_id=dev, device_id_type=MESH)
        self.local = dev is None
        self.whole = make(src, dst)
        n = _n_chunks(src)
        step = src.shape[0] // n
        self.parts = ([self.whole] if n == 1 else
                      [make(src.at[pl.ds(i * step, step)], dst.at[pl.ds(i * step, step)]) for i in range(n)])

    def start(self):
        for part in self.parts:
            part.start()

    def wait_recv(self):
        self.whole.wait_recv()

    def wait_send(self):
        self.whole.wait_send()

    def wait(self):
        self.whole.wait()


def _run_copies(local_ops, remote_ops, lsem, ssem, rsem):
    local = [_Copy(src, dst, (lsem.at[n],)) for n, (src, dst) in enumerate(local_ops)]
    remote = [_Copy(src, dst, (ssem.at[n], rsem.at[n]), dev) for n, (src, dst, dev) in enumerate(remote_ops)]
    for cp in local + remote:
        cp.start()
    for cp in remote:
        cp.wait_recv()
    for cp in remote:
        cp.wait_send()
    for cp in local:
        cp.wait()


def _comm_call(body, *, name, n_in, out_shape, n_local, n_remote):
    return pl.pallas_call(
        body, name=name, in_specs=[ANY] * n_in, out_specs=[ANY] * len(out_shape), out_shape=out_shape,
        scratch_shapes=[pltpu.SemaphoreType.DMA((max(n_local, 1),)), pltpu.SemaphoreType.DMA((n_remote,)),
                        pltpu.SemaphoreType.DMA((n_remote,))])


BIG = (("w_in", D_MODEL, IN_COLS, 1), ("w_conv_out", CONV_DIM, D_MODEL, 1), ("w_hgrn_out", HGRN_DIM, D_MODEL, 0),
       ("w_out", D_MODEL, D_MODEL, 0), ("w_ffn_in", D_MODEL, 2 * D_FF, 1), ("w_ffn_out", D_FF, D_MODEL, 0))


def _shard_slice(ref, rows, cols, axis, k):
    if axis == 1:
        w = cols // N_CHIPS
        return ref.at[:, pl.ds(k * w, w)]
    h = rows // N_CHIPS
    return ref.at[pl.ds(k * h, h), :]


def _half_slice(ref, rows, cols, axis, hc):
    if axis == 1:
        return ref.at[pl.ds(hc * (rows // 2), rows // 2), :]
    return ref.at[:, pl.ds(hc * (cols // 2), cols // 2)]


def _half_shape(rows, cols, axis):
    return (rows // 2, cols) if axis == 1 else (rows, cols // 2)


def _gather_weights(full, small):
    n_big, n_small = len(full), len(small)
    n_arr = n_big + n_small
    out_shape = ([_sds((r, c), BF16) for _, r, c, _ in BIG]
                 + [_sds((N_CHIPS,) + a.shape, F32) for a in small])
    shard_shape = [(r, c // N_CHIPS) if ax == 1 else (r // N_CHIPS, c) for _, r, c, ax in BIG]

    def body(*refs):
        ins, outs = refs[:n_arr], refs[n_arr:2 * n_arr]
        lsem, ssem, rsem, fsem_s, fsem_r = refs[2 * n_arr:]
        x, y, c = _place()
        me = 2 * x + y
        chips = _other_chips(x, y)
        sibling = (x, y, 1 - c)

        def region(idx, k, hc, bufs=outs):
            (_, r, cc, ax), (sr, _) = BIG[idx], shard_shape[idx]
            return _shard_slice(bufs[idx], r, cc, ax, k).at[pl.ds(hc * (sr // 2), sr // 2)]

        for k in range(N_CHIPS):
            for hc in range(2):
                @pl.when((me == k) & (c == hc))
                def _(k=k, hc=hc):
                    local = [_Copy(ins[n_big + i], outs[n_big + i].at[k], (lsem.at[i],)) for i in range(n_small)]
                    sends, fwds = [], []
                    for j, (cx, cy) in enumerate(chips):
                        for i in range(n_big):
                            n = j * n_arr + i
                            sends.append(_Copy(region(i, k, hc, ins), region(i, k, hc), (ssem.at[n], rsem.at[n]),
                                               (cx, cy, c)))
                            reg = region(i, k ^ SHARD_XOR[j], hc)
                            fwds.append(_Copy(reg, reg, (fsem_s.at[j * n_big + i], fsem_r.at[j * n_big + i]), sibling))
                        for i in range(n_small):
                            n = j * n_arr + n_big + i
                            sends.append(_Copy(ins[n_big + i], outs[n_big + i].at[k], (ssem.at[n], rsem.at[n]),
                                               (cx, cy, c)))
                    for cp in local + sends:
                        cp.start()
                    for j in range(3):
                        for i in range(n_big):
                            sends[j * n_arr + i].wait_recv()
                        for i in range(n_big):
                            fwds[j * n_big + i].start()
                    for j in range(3):
                        for i in range(n_small):
                            sends[j * n_arr + n_big + i].wait_recv()
                    for cp in fwds:
                        cp.wait_recv()
                    for cp in sends + fwds:
                        cp.wait_send()
                    for cp in local:
                        cp.wait()

    return pl.pallas_call(
        body, name="gather_weights", in_specs=[ANY] * n_arr, out_specs=[ANY] * n_arr, out_shape=out_shape,
        input_output_aliases={i: i for i in range(n_big)},
        scratch_shapes=[pltpu.SemaphoreType.DMA((n_small,)), pltpu.SemaphoreType.DMA((3 * n_arr,)),
                        pltpu.SemaphoreType.DMA((3 * n_arr,)), pltpu.SemaphoreType.DMA((3 * n_big,)),
                        pltpu.SemaphoreType.DMA((3 * n_big,))])(*full, *small)


def _sibling_exchange(grads):
    n = len(BIG)
    shapes = [_sds(_half_shape(r, c, ax), F32) for _, r, c, ax in BIG]

    def body(*refs):
        ins, got = refs[:n], refs[n:2 * n]
        lsem, ssem, rsem = refs[2 * n:]
        x, y, c = _place()
        for k in range(2):
            @pl.when(c == k)
            def _(k=k):
                remote_ops = [(_half_slice(g, r, cc, ax, 1 - k), dst, (x, y, 1 - c))
                              for g, dst, (_, r, cc, ax) in zip(ins, got, BIG)]
                _run_copies([], remote_ops, lsem, ssem, rsem)

    return _comm_call(body, name="grad_sibling_exchange", n_in=n, out_shape=shapes, n_local=0, n_remote=n)(*grads)


def _piece_shape(rows, cols, axis):
    hr, hc = _half_shape(rows, cols, axis)
    return (hr, hc // N_CHIPS) if axis == 1 else (hr // N_CHIPS, hc)


def _chip_exchange(chip_sums):
    n = len(BIG)
    half = [_half_shape(r, c, ax) for _, r, c, ax in BIG]
    out_shape = [_sds(_piece_shape(r, c, ax), BF16) for _, r, c, ax in BIG for _ in range(3)]

    def body(*refs):
        ins, got = refs[:n], refs[n:4 * n]
        lsem, ssem, rsem = refs[4 * n:]
        x, y, c = _place()
        me = 2 * x + y
        for k in range(N_CHIPS):
            @pl.when(me == k)
            def _(k=k):
                remote_ops = [(_shard_slice(g, half[idx][0], half[idx][1], BIG[idx][3], k ^ SHARD_XOR[j]),
                               got[3 * idx + j], (cx, cy, c))
                              for j, (cx, cy) in enumerate(_other_chips(x, y))
                              for idx, g in enumerate(ins)]
                _run_copies([], remote_ops, lsem, ssem, rsem)

    outs = _comm_call(body, name="grad_chip_exchange", n_in=n, out_shape=out_shape, n_local=0, n_remote=3 * n)(
        *chip_sums)
    return [outs[3 * idx:3 * idx + 3] for idx in range(n)]


def _sibling_assemble(shards):
    n = len(BIG)
    shape = [(r, c // N_CHIPS) if ax == 1 else (r // N_CHIPS, c) for _, r, c, ax in BIG]

    def body(*refs):
        ins, outs = refs[:n], refs[n:2 * n]
        lsem, ssem, rsem = refs[2 * n:]
        x, y, c = _place()
        for k in range(2):
            @pl.when(c == k)
            def _(k=k):
                remote_ops = [(_half_slice(i_, sr, sc, ax, k), _half_slice(o_, sr, sc, ax, k), (x, y, 1 - c))
                              for i_, o_, (sr, sc), (_, _, _, ax) in zip(ins, outs, shape, BIG)]
                _run_copies([], remote_ops, lsem, ssem, rsem)

    return pl.pallas_call(
        body, name="grad_sibling_assemble", in_specs=[ANY] * n, out_specs=[ANY] * n,
        out_shape=[_sds(s, F32) for s in shape], input_output_aliases={i: i for i in range(n)},
        scratch_shapes=[pltpu.SemaphoreType.DMA((1,)), pltpu.SemaphoreType.DMA((n,)),
                        pltpu.SemaphoreType.DMA((n,))])(*shards)


def _all_reduce_small(packed):
    r, w = packed.shape

    def body(in_ref, out_ref, slots, lsem, ssem, rsem):
        x, y, c = _place()
        me = 4 * x + 2 * y + c
        peers = [(x ^ (m >> 2), y ^ ((m >> 1) & 1), c ^ (m & 1)) for m in range(1, N_DEV)]
        _run_copies([(in_ref, slots.at[me])], [(in_ref, slots.at[me], dev) for dev in peers], lsem, ssem, rsem)
        total = slots[0]
        for d in range(1, N_DEV):
            total = total + slots[d]
        out_ref[...] = total

    vmem = pl.BlockSpec(memory_space=pltpu.VMEM)
    return pl.pallas_call(
        body, name="small_all_reduce", in_specs=[vmem], out_specs=vmem, out_shape=_sds((r, w), F32),
        scratch_shapes=[pltpu.VMEM((N_DEV, r, w), F32), pltpu.SemaphoreType.DMA((1,)),
                        pltpu.SemaphoreType.DMA((N_DEV - 1,)), pltpu.SemaphoreType.DMA((N_DEV - 1,))])(packed)


SMALL_ORDER = ("w_conv_dw", "b_conv_dw", "conv_ln_g", "conv_ln_b", "hgrn_lb_logits", "hgrn_norm_g",
               "ln1_g", "ln1_b", "w_ffn_dw", "b_ffn_dw", "ln2_g", "ln2_b")
REPLICATED_SMALL = tuple(n for n in SMALL_ORDER if n not in ("w_conv_dw", "w_ffn_dw"))
WEIGHT_ORDER = ("w_in", "w_conv_dw", "b_conv_dw", "conv_ln_g", "conv_ln_b", "w_conv_out", "hgrn_lb_logits",
                "hgrn_norm_g", "w_hgrn_out", "w_out", "ln1_g", "ln1_b", "w_ffn_in", "w_ffn_dw", "b_ffn_dw",
                "w_ffn_out", "ln2_g", "ln2_b")


def _pack(arrs):
    flat = jnp.concatenate([a.reshape(-1) for a in arrs])
    assert flat.shape[0] % 128 == 0
    return flat.reshape(-1, 128)


def _unpack(packed, shapes):
    flat = packed.reshape(-1)
    out, pos = [], 0
    for shp in shapes:
        size = 1
        for d in shp:
            size *= d
        out.append(flat[pos:pos + size].reshape(shp))
        pos += size
    return out


def kernel(x, w_in, w_conv_dw, b_conv_dw, conv_ln_g, conv_ln_b, w_conv_out, hgrn_lb_logits, hgrn_norm_g, w_hgrn_out, w_out, ln1_g, ln1_b, w_ffn_in, w_ffn_dw, b_ffn_dw, w_ffn_out, ln2_g, ln2_b, loss_target, m_w_in, m_w_conv_dw, m_b_conv_dw, m_conv_ln_g, m_conv_ln_b, m_w_conv_out, m_hgrn_lb_logits, m_hgrn_norm_g, m_w_hgrn_out, m_w_out, m_ln1_g, m_ln1_b, m_w_ffn_in, m_w_ffn_dw, m_b_ffn_dw, m_w_ffn_out, m_ln2_g, m_ln2_b, v_w_in, v_w_conv_dw, v_b_conv_dw, v_conv_ln_g, v_conv_ln_b, v_w_conv_out, v_hgrn_lb_logits, v_hgrn_norm_g, v_w_hgrn_out, v_w_out, v_ln1_g, v_ln1_b, v_w_ffn_in, v_w_ffn_dw, v_b_ffn_dw, v_w_ffn_out, v_ln2_g, v_ln2_b):
    w = dict(w_in=w_in, w_conv_dw=w_conv_dw, b_conv_dw=b_conv_dw, conv_ln_g=conv_ln_g, conv_ln_b=conv_ln_b,
             w_conv_out=w_conv_out, hgrn_lb_logits=hgrn_lb_logits, hgrn_norm_g=hgrn_norm_g, w_hgrn_out=w_hgrn_out,
             w_out=w_out, ln1_g=ln1_g, ln1_b=ln1_b, w_ffn_in=w_ffn_in, w_ffn_dw=w_ffn_dw, b_ffn_dw=b_ffn_dw,
             w_ffn_out=w_ffn_out, ln2_g=ln2_g, ln2_b=ln2_b)
    m = dict(w_in=m_w_in, w_conv_dw=m_w_conv_dw, b_conv_dw=m_b_conv_dw, conv_ln_g=m_conv_ln_g, conv_ln_b=m_conv_ln_b,
             w_conv_out=m_w_conv_out, hgrn_lb_logits=m_hgrn_lb_logits, hgrn_norm_g=m_hgrn_norm_g,
             w_hgrn_out=m_w_hgrn_out, w_out=m_w_out, ln1_g=m_ln1_g, ln1_b=m_ln1_b, w_ffn_in=m_w_ffn_in,
             w_ffn_dw=m_w_ffn_dw, b_ffn_dw=m_b_ffn_dw, w_ffn_out=m_w_ffn_out, ln2_g=m_ln2_g, ln2_b=m_ln2_b)
    v = dict(w_in=v_w_in, w_conv_dw=v_w_conv_dw, b_conv_dw=v_b_conv_dw, conv_ln_g=v_conv_ln_g, conv_ln_b=v_conv_ln_b,
             w_conv_out=v_w_conv_out, hgrn_lb_logits=v_hgrn_lb_logits, hgrn_norm_g=v_hgrn_norm_g,
             w_hgrn_out=v_w_hgrn_out, w_out=v_w_out, ln1_g=v_ln1_g, ln1_b=v_ln1_b, w_ffn_in=v_w_ffn_in,
             w_ffn_dw=v_w_ffn_dw, b_ffn_dw=v_b_ffn_dw, w_ffn_out=v_w_ffn_out, ln2_g=v_ln2_g, ln2_b=v_ln2_b)
    big_names = [n for n, _, _, _ in BIG]
    w2 = {n: a[0] if a.ndim == 3 else a for n, a in w.items()}
    m2 = {n: a[0] if a.ndim == 3 else a for n, a in m.items()}
    v2 = {n: a[0] if a.ndim == 3 else a for n, a in v.items()}

    sel = jnp.stack([2 * lax.axis_index("x") + lax.axis_index("y"), lax.axis_index("c")]).astype(jnp.int32)
    sharded = lambda ax, p, n: ("cols", p, n) if ax == 1 else ("rows", p, n)
    across = lambda ax, p, n: ("rows", p, n) if ax == 1 else ("cols", p, n)

    placed = [_windowed(lambda a: (a,), sel, [(w2[n], None)], [((r, c), BF16, sharded(ax, 0, N_CHIPS))],
                        window=w2[n].shape, name="cast_" + n)[0] for n, r, c, ax in BIG]
    gathered = _gather_weights(placed, [w2["w_conv_dw"], w2["w_ffn_dw"]])
    wi, wco, wh, wo, wfi, wfo = gathered[:6]
    wcd = jnp.transpose(gathered[6], (1, 0, 2)).reshape(CONV_KERNEL, CONV_DIM)
    wfd = jnp.transpose(gathered[7], (1, 0, 2)).reshape(FFN_KERNEL, D_FF)

    loss_part, grad_x, big_grads, small_grads = _local_step(
        x[0], loss_target[0], wi, wco, wh, wo, wfi, wfo, wcd, w2["b_conv_dw"], w2["conv_ln_g"], w2["conv_ln_b"],
        w2["hgrn_lb_logits"], w2["hgrn_norm_g"], w2["ln1_g"], w2["ln1_b"], wfd, w2["b_ffn_dw"],
        w2["ln2_g"], w2["ln2_b"])
    loss = lax.psum(loss_part[0, 0], ("x", "y", "c"))

    got = _sibling_exchange(big_grads)
    chip_sums = [_windowed(lambda a, b: (a + b,), sel, [(g_, across(ax, 1, 2)), (h_, None)],
                           [(_half_shape(r, c, ax), BF16, None)], window=_half_shape(r, c, ax),
                           name="chip_sum_" + n)[0]
                 for (n, r, c, ax), g_, h_ in zip(BIG, big_grads, got)]
    recv = _chip_exchange(chip_sums)
    add4 = lambda a, b0, b1, b2: (a.astype(F32) + b0.astype(F32) + b1.astype(F32) + b2.astype(F32),)
    half_filled = [_windowed(add4, sel, [(cs_, sharded(ax, 0, N_CHIPS))] + [(r_, None) for r_ in recv_],
                             [((r, c // N_CHIPS) if ax == 1 else (r // N_CHIPS, c), F32, across(ax, 1, 2))],
                             window=_piece_shape(r, c, ax), name="shard_sum_" + n)[0]
                   for (n, r, c, ax), cs_, recv_ in zip(BIG, chip_sums, recv)]
    shard_grads = dict(zip(big_names, _sibling_assemble(half_filled)))

    small_shapes = [small_grads[n].shape for n in SMALL_ORDER]
    reduced = dict(zip(SMALL_ORDER, _unpack(_all_reduce_small(_pack([small_grads[n] for n in SMALL_ORDER])),
                                            small_shapes)))
    shard = 2 * lax.axis_index("x") + lax.axis_index("y")
    grads = dict(shard_grads)
    for n in REPLICATED_SMALL:
        grads[n] = reduced[n]
    grads["w_conv_dw"] = lax.dynamic_slice_in_dim(reduced["w_conv_dw"], shard * (CONV_DIM // N_CHIPS),
                                                  CONV_DIM // N_CHIPS, axis=1)
    grads["w_ffn_dw"] = lax.dynamic_slice_in_dim(reduced["w_ffn_dw"], shard * (D_FF // N_CHIPS),
                                                 D_FF // N_CHIPS, axis=1)

    delta, new_m, new_v = {}, {}, {}
    for n in big_names + ["w_conv_dw", "w_ffn_dw"]:
        delta[n], new_m[n], new_v[n] = _adamw(w2[n], grads[n], m2[n], v2[n], name="adamw_" + n)
    rep_shapes = [w2[n].shape for n in REPLICATED_SMALL]
    packed = _adamw(*[_pack([src[n] for n in REPLICATED_SMALL]) for src in (w2, grads, m2, v2)], name="adamw_small")
    for dst, pk in zip((delta, new_m, new_v), packed):
        for n, a in zip(REPLICATED_SMALL, _unpack(pk, rep_shapes)):
            dst[n] = a

    def shaped(d):
        return [d[n].reshape(w[n].shape) for n in WEIGHT_ORDER]

    return (loss, grad_x[None], *shaped(grads), *shaped(delta), *shaped(new_m), *shaped(new_v))
```

```python
import jax
import jax.numpy as jnp
from jax import lax
from jax.experimental import pallas as pl
from jax.experimental.pallas import tpu as pltpu

F32 = jnp.float32
BF16 = jnp.bfloat16

D_MODEL = 1024
CONV_DIM = 512
CONV_KERNEL = 31
HGRN_DIM = 1024
HGRN_HEADS = 8
HEAD_DIM = 128
CHUNK = 64
SUB = 16
N_SUB = CHUNK // SUB
D_FF = 2816
FFN_KERNEL = 3
IN_COLS = 7168
LN_EPS = 1e-5
RMS_EPS = 1e-6
ALPHA = 2.0 ** 0.25
GELU_C = 0.7978845608028654
GELU_A = 0.044715

ADAM_LR = 0.001
ADAM_B1 = 0.9
ADAM_B2 = 0.999
ADAM_EPS = 1e-08
ADAM_WD = 0.01
ADAM_STEP = 10
ADAM_BC1 = 1.0 - ADAM_B1 ** ADAM_STEP
ADAM_BC2 = 1.0 - ADAM_B2 ** ADAM_STEP

N_CHIPS = 4
N_DEV = 8
ROW_BLOCK = 32
CONV_HALO = 32
FFN_HALO = 8
MESH = pl.DeviceIdType.MESH
ANY = pl.BlockSpec(memory_space=pl.ANY)


def _dot(a, b):
    return jnp.dot(a, b, preferred_element_type=F32)


def _dot_nt(a, b):
    return lax.dot_general(a, b, (((1,), (1,)), ((), ())), preferred_element_type=F32)


def _dot_tn(a, b):
    return lax.dot_general(a, b, (((0,), (0,)), ((), ())), preferred_element_type=F32)


def _sigmoid(z):
    return jax.nn.sigmoid(z)


def _silu_grad(z, s):
    return s * (1.0 + z * (1.0 - s))


def _gelu_and_grad(u):
    u2 = u * u
    th = jnp.tanh(u * (GELU_C + (GELU_C * GELU_A) * u2))
    half = 0.5 + 0.5 * th
    g = u * half
    dg = half + (0.5 * u) * (1.0 - th * th) * (GELU_C + (3.0 * GELU_C * GELU_A) * u2)
    return g, dg


def _ln_stats(r):
    mu = jnp.mean(r, axis=-1, keepdims=True)
    xc = r - mu
    var = jnp.mean(xc * xc, axis=-1, keepdims=True)
    rstd = lax.rsqrt(var + LN_EPS)
    return xc * rstd, rstd


def _ln_bwd(dy, xhat, rstd, g):
    dxh = dy * g
    m1 = jnp.mean(dxh, axis=-1, keepdims=True)
    m2 = jnp.mean(dxh * xhat, axis=-1, keepdims=True)
    return rstd * (dxh - m1 - xhat * m2)


def _fold8(x):
    acc = x[0:8, :]
    for r in range(8, x.shape[0], 8):
        acc = acc + x[r:r + 8, :]
    return acc


def _in_hbm(a):
    return pltpu.with_memory_space_constraint(a, pltpu.HBM)


def _hbm(shape, dtype):
    return pltpu.HBM(shape, dtype)


def _out_hbm(out_shape):
    if isinstance(out_shape, (list, tuple)):
        return [_hbm(s.shape, s.dtype) for s in out_shape]
    return _hbm(out_shape.shape, out_shape.dtype)


def _call(body, *, name, grid, in_specs, out_specs, out_shape, scratch=(), vmem_mb=32, aliases=None):
    call = pl.pallas_call(
        body, name=name, grid=grid, in_specs=in_specs, out_specs=out_specs, out_shape=_out_hbm(out_shape),
        scratch_shapes=list(scratch), input_output_aliases=aliases or {},
        compiler_params=pltpu.CompilerParams(
            dimension_semantics=("arbitrary",) * len(grid), vmem_limit_bytes=vmem_mb * 2 ** 20))
    return lambda *args: call(*[_in_hbm(a) for a in args])


def _sds(shape, dtype):
    return jax.ShapeDtypeStruct(shape, dtype)


def _proj(x, w):
    t = x.shape[0]
    tm, tn = min(t, 1024), 1024

    def body(x_ref, w_ref, p_ref, xb_ref):
        @pl.when(pl.program_id(1) == 0)
        def _():
            xb_ref[...] = x_ref[...].astype(BF16)
        p_ref[...] = _dot(xb_ref[...], w_ref[...])

    return _call(
        body, name="proj", grid=(t // tm, IN_COLS // tn),
        in_specs=[pl.BlockSpec((tm, D_MODEL), lambda i, j: (i, 0)),
                  pl.BlockSpec((D_MODEL, tn), lambda i, j: (0, j))],
        out_specs=[pl.BlockSpec((tm, tn), lambda i, j: (i, j)),
                   pl.BlockSpec((tm, D_MODEL), lambda i, j: (i, 0))],
        out_shape=[_sds((t, IN_COLS), F32), _sds((t, D_MODEL), BF16)], vmem_mb=48)(x, w)


def _mm_nn(a, w, *, tm, tn, name, vmem_mb=32):
    t, k = a.shape
    n = w.shape[1]
    tm = min(tm, t)

    def body(a_ref, w_ref, o_ref):
        o_ref[...] = _dot(a_ref[...], w_ref[...])

    return _call(
        body, name=name, grid=(t // tm, n // tn),
        in_specs=[pl.BlockSpec((tm, k), lambda i, j: (i, 0)), pl.BlockSpec((k, tn), lambda i, j: (0, j))],
        out_specs=pl.BlockSpec((tm, tn), lambda i, j: (i, j)),
        out_shape=_sds((t, n), F32), vmem_mb=vmem_mb)(a, w)


def _views(*arrs):
    out = []
    for a in arrs:
        if a.ndim == 2:
            out.append((a, None))
        else:
            out.extend((a, p) for p in range(a.shape[0]))
    return out


def _piece_layout(views, tile):
    starts, counts, total = [], [], 0
    for arr, _ in views:
        width = arr.shape[-1]
        assert width % tile == 0
        starts.append(total)
        counts.append(width // tile)
        total += width // tile
    return starts, counts, total


def _mm_tn(a, views, *, tn, name, tt=512, vmem_mb=32):
    t, m = a.shape
    tt = min(tt, t)
    starts, counts, nj = _piece_layout(views, tn)
    n_views = len(views)

    def body(a_ref, *refs):
        b_refs, o_ref = refs[:n_views], refs[n_views]
        j = pl.program_id(0)

        @pl.when(pl.program_id(1) == 0)
        def _():
            o_ref[...] = jnp.zeros_like(o_ref)

        for b_ref, st, nb, (_, p) in zip(b_refs, starts, counts, views):
            @pl.when((j >= st) & (j < st + nb))
            def _(b_ref=b_ref, p=p):
                blk = b_ref[...] if p is None else b_ref[0]
                o_ref[...] += _dot_tn(a_ref[...], blk)

    def b_spec(st, nb, p):
        def rows(j, k):
            return jnp.where((j >= st) & (j < st + nb), k, 0)

        def cols(j):
            return jnp.clip(j - st, 0, nb - 1)

        if p is None:
            return pl.BlockSpec((tt, tn), lambda j, k: (rows(j, k), cols(j)))
        return pl.BlockSpec((1, tt, tn), lambda j, k: (p, rows(j, k), cols(j)))

    return _call(
        body, name=name, grid=(nj, t // tt),
        in_specs=[pl.BlockSpec((tt, m), lambda j, k: (k, 0))]
        + [b_spec(st, nb, p) for st, nb, (_, p) in zip(starts, counts, views)],
        out_specs=pl.BlockSpec((m, tn), lambda j, k: (0, j)),
        out_shape=_sds((m, nj * tn), F32), vmem_mb=vmem_mb)(a, *[arr for arr, _ in views])


DPROJ_SLABS = 7
DPROJ_GATE_SLAB = 4
DPROJ_CONV_SLAB = 6


def _slab_cols(s):
    return (s + 1) % DPROJ_SLABS


def _grad_w_in(xb, dstack):
    t = xb.shape[0]
    tt = min(t, 1024)

    def body(a_ref, b_ref, o_ref):
        @pl.when(pl.program_id(1) == 0)
        def _():
            o_ref[...] = jnp.zeros_like(o_ref)

        o_ref[...] += _dot_tn(a_ref[...], b_ref[0])

    return _call(
        body, name="grad_w_in", grid=(DPROJ_SLABS, t // tt),
        in_specs=[pl.BlockSpec((tt, D_MODEL), lambda j, k: (k, 0)),
                  pl.BlockSpec((1, tt, D_MODEL), lambda j, k: (j, k, 0))],
        out_specs=pl.BlockSpec((D_MODEL, D_MODEL), lambda j, k: (0, _slab_cols(j))),
        out_shape=_sds((D_MODEL, IN_COLS), F32), vmem_mb=48)(xb, dstack)


def _grad_x(dstack, wi, dr1):
    t = dr1.shape[0]
    tm = min(t, 1024)

    def body(add_ref, b_ref, w_ref, o_ref):
        @pl.when(pl.program_id(1) == 0)
        def _():
            o_ref[...] = ALPHA * add_ref[...]

        o_ref[...] += _dot_nt(b_ref[0], w_ref[...])

    return _call(
        body, name="grad_x", grid=(t // tm, DPROJ_SLABS),
        in_specs=[pl.BlockSpec((tm, D_MODEL), lambda i, k: (i, 0)),
                  pl.BlockSpec((1, tm, D_MODEL), lambda i, k: (k, i, 0)),
                  pl.BlockSpec((D_MODEL, D_MODEL), lambda i, k: (0, _slab_cols(k)))],
        out_specs=pl.BlockSpec((tm, D_MODEL), lambda i, k: (i, 0)),
        out_shape=_sds((t, D_MODEL), F32), vmem_mb=48)(dr1, dstack, wi)


LANES = 128
CONV_FWD_OFFSETS = {k: 2 + k for k in range(CONV_KERNEL)}
CONV_BWD_OFFSETS = {k: CONV_KERNEL - 1 - k for k in range(CONV_KERNEL)}


def _conv_taps(win, offsets):
    n = win.shape[0]
    for b in range(8):
        taps = [k for k, o in offsets.items() if o % 8 == b]
        if not taps:
            continue
        shifted = win if b == 0 else pltpu.roll(win, n - b, 0)
        for k in taps:
            first = offsets[k] - b
            yield k, shifted[first:first + ROW_BLOCK, :]


def _conv_fwd(proj, wcd, bcd, lng, lnb):
    t = proj.shape[0]
    tm = 512

    def body(cv_ref, cg_ref, w_ref, b_ref, g_ref, be_ref, cc_ref, cs_ref, ext_ref):
        i = pl.program_id(0)

        @pl.when(i == 0)
        def _():
            ext_ref[0:CONV_HALO, :] = jnp.zeros((CONV_HALO, CONV_DIM), F32)

        @pl.when(i > 0)
        def _():
            ext_ref[0:CONV_HALO, :] = ext_ref[tm:tm + CONV_HALO, :]

        ext_ref[CONV_HALO:CONV_HALO + tm, :] = cv_ref[...] * _sigmoid(cg_ref[...])

        def block(r, carry):
            r0 = pl.multiple_of(r * ROW_BLOCK, ROW_BLOCK)
            groups = []
            for g in range(CONV_DIM // LANES):
                lanes = slice(g * LANES, (g + 1) * LANES)
                win = ext_ref[pl.ds(r0, ROW_BLOCK + CONV_HALO), lanes]
                acc = jnp.broadcast_to(b_ref[:, lanes], (ROW_BLOCK, LANES))
                for k, rows_k in _conv_taps(win, CONV_FWD_OFFSETS):
                    acc = acc + w_ref[k:k + 1, lanes] * rows_k
                groups.append(acc)
            acc = jnp.concatenate(groups, axis=1)
            cc_ref[pl.ds(r0, ROW_BLOCK), :] = acc
            xhat, _ = _ln_stats(acc)
            a = xhat * g_ref[...] + be_ref[...]
            cs_ref[pl.ds(r0, ROW_BLOCK), :] = (a * _sigmoid(a)).astype(BF16)
            return carry

        lax.fori_loop(0, tm // ROW_BLOCK, block, 0)

    vec = pl.BlockSpec((1, CONV_DIM), lambda i: (0, 0))
    return _call(
        body, name="conv_fwd", grid=(t // tm,),
        in_specs=[pl.BlockSpec((tm, CONV_DIM), lambda i: (i, 0)), pl.BlockSpec((tm, CONV_DIM), lambda i: (i, 1)),
                  pl.BlockSpec((CONV_KERNEL, CONV_DIM), lambda i: (0, 0)), vec, vec, vec],
        out_specs=[pl.BlockSpec((tm, CONV_DIM), lambda i: (i, 0)), pl.BlockSpec((tm, CONV_DIM), lambda i: (i, 0))],
        out_shape=[_sds((t, CONV_DIM), F32), _sds((t, CONV_DIM), BF16)],
        scratch=[pltpu.VMEM((tm + CONV_HALO, CONV_DIM), F32)])(proj, proj, wcd, bcd, lng, lnb)


def _conv_bwd_a(dcs, cc, proj, lng, lnb):
    t = proj.shape[0]
    tm = 512
    nt = t // tm

    def body(dcs_ref, cc_ref, cv_ref, cg_ref, g_ref, be_ref,
             dcc_ref, dw_ref, db_ref, dg_ref, dbe_ref, ext_ref, accw_ref, acc3_ref):
        i = pl.program_id(0)

        @pl.when(i == 0)
        def _():
            ext_ref[0:CONV_HALO, :] = jnp.zeros((CONV_HALO, CONV_DIM), F32)
            accw_ref[...] = jnp.zeros_like(accw_ref)
            acc3_ref[...] = jnp.zeros_like(acc3_ref)

        @pl.when(i > 0)
        def _():
            ext_ref[0:CONV_HALO, :] = ext_ref[tm:tm + CONV_HALO, :]

        ext_ref[CONV_HALO:CONV_HALO + tm, :] = cv_ref[...] * _sigmoid(cg_ref[...])

        def block(r, carry):
            r0 = pl.multiple_of(r * ROW_BLOCK, ROW_BLOCK)
            rows = pl.ds(r0, ROW_BLOCK)
            xhat, rstd = _ln_stats(cc_ref[rows, :])
            a = xhat * g_ref[...] + be_ref[...]
            sg = _sigmoid(a)
            da = dcs_ref[rows, :] * _silu_grad(a, sg)
            acc3_ref[8:16, :] += _fold8(da * xhat)
            acc3_ref[16:24, :] += _fold8(da)
            dcc = _ln_bwd(da, xhat, rstd, g_ref[...])
            dcc_ref[rows, :] = dcc
            acc3_ref[0:8, :] += _fold8(dcc)
            for g in range(CONV_DIM // LANES):
                lanes = slice(g * LANES, (g + 1) * LANES)
                win = ext_ref[pl.ds(r0, ROW_BLOCK + CONV_HALO), lanes]
                dcc_g = dcc[:, lanes]
                for k, rows_k in _conv_taps(win, CONV_FWD_OFFSETS):
                    accw_ref[8 * k:8 * k + 8, lanes] += _fold8(dcc_g * rows_k)
            return carry

        lax.fori_loop(0, tm // ROW_BLOCK, block, 0)

        @pl.when(i == nt - 1)
        def _():
            for k in range(CONV_KERNEL):
                dw_ref[k:k + 1, :] = jnp.sum(accw_ref[8 * k:8 * k + 8, :], axis=0, keepdims=True)
            db_ref[...] = jnp.sum(acc3_ref[0:8, :], axis=0, keepdims=True)
            dg_ref[...] = jnp.sum(acc3_ref[8:16, :], axis=0, keepdims=True)
            dbe_ref[...] = jnp.sum(acc3_ref[16:24, :], axis=0, keepdims=True)

    vec = pl.BlockSpec((1, CONV_DIM), lambda i: (0, 0))
    tile = pl.BlockSpec((tm, CONV_DIM), lambda i: (i, 0))
    return _call(
        body, name="conv_bwd_a", grid=(nt,),
        in_specs=[tile, tile, tile, pl.BlockSpec((tm, CONV_DIM), lambda i: (i, 1)), vec, vec],
        out_specs=[tile, pl.BlockSpec((CONV_KERNEL, CONV_DIM), lambda i: (0, 0)), vec, vec, vec],
        out_shape=[_sds((t, CONV_DIM), F32), _sds((CONV_KERNEL, CONV_DIM), F32),
                   _sds((1, CONV_DIM), F32), _sds((1, CONV_DIM), F32), _sds((1, CONV_DIM), F32)],
        scratch=[pltpu.VMEM((tm + CONV_HALO, CONV_DIM), F32),
                 pltpu.VMEM((8 * CONV_KERNEL, CONV_DIM), F32),
                 pltpu.VMEM((24, CONV_DIM), F32)])(dcs, cc, proj, proj, lng, lnb)


def _conv_bwd_b(dcc, proj, wcd, dstack):
    t = proj.shape[0]
    tm = 512
    nt = t // tm

    def body(dcc_ref, cv_ref, cg_ref, w_ref, stack_ref, out_ref, ext_ref):
        del stack_ref
        i = pl.program_id(0)

        @pl.when(i == 0)
        def _():
            ext_ref[tm:tm + CONV_HALO, :] = jnp.zeros((CONV_HALO, CONV_DIM), F32)

        @pl.when(i > 0)
        def _():
            ext_ref[tm:tm + CONV_HALO, :] = ext_ref[0:CONV_HALO, :]

        ext_ref[0:tm, :] = dcc_ref[...]

        def block(r, carry):
            r0 = pl.multiple_of(r * ROW_BLOCK, ROW_BLOCK)
            rows = pl.ds(r0, ROW_BLOCK)
            for g in range(CONV_DIM // LANES):
                lanes = slice(g * LANES, (g + 1) * LANES)
                gate_lanes = slice(CONV_DIM + g * LANES, CONV_DIM + (g + 1) * LANES)
                win = ext_ref[pl.ds(r0, ROW_BLOCK + CONV_HALO), lanes]
                acc = jnp.zeros((ROW_BLOCK, LANES), F32)
                for k, rows_k in _conv_taps(win, CONV_BWD_OFFSETS):
                    acc = acc + w_ref[k:k + 1, lanes] * rows_k
                sg = _sigmoid(cg_ref[rows, lanes])
                out_ref[0, rows, lanes] = (acc * sg).astype(BF16)
                out_ref[0, rows, gate_lanes] = (acc * cv_ref[rows, lanes] * sg * (1.0 - sg)).astype(BF16)
            return carry

        lax.fori_loop(0, tm // ROW_BLOCK, block, 0)

    rev = lambda i: (nt - 1 - i, 0)
    return _call(
        body, name="conv_bwd_b", grid=(nt,),
        in_specs=[pl.BlockSpec((tm, CONV_DIM), rev), pl.BlockSpec((tm, CONV_DIM), rev),
                  pl.BlockSpec((tm, CONV_DIM), lambda i: (nt - 1 - i, 1)),
                  pl.BlockSpec((CONV_KERNEL, CONV_DIM), lambda i: (0, 0)), ANY],
        out_specs=pl.BlockSpec((1, tm, 2 * CONV_DIM), lambda i: (DPROJ_CONV_SLAB, nt - 1 - i, 0)),
        out_shape=_sds(dstack.shape, BF16), aliases={4: 0},
        scratch=[pltpu.VMEM((tm + CONV_HALO, CONV_DIM), F32)])(dcc, proj, proj, wcd, dstack)


def _lower_bound(lg_ref):
    a0, a1 = lg_ref[0:1, :], lg_ref[1:2, :]
    m = jnp.maximum(a0, a1)
    e0, e1 = jnp.exp(a0 - m), jnp.exp(a1 - m)
    return e0 / (e0 + e1)


def _block_tri(n, upper):
    r = lax.broadcasted_iota(jnp.int32, (n, n), 0)
    c = lax.broadcasted_iota(jnp.int32, (n, n), 1)
    same = (r >> 6) == (c >> 6)
    tri = (c >= r) if upper else (c <= r)
    return jnp.where(same & tri, 1.0, 0.0).astype(BF16)


def _block_cumsum(x, tri):
    w = x.shape[1]
    hi = x.astype(BF16)
    r1 = x - hi.astype(F32)
    mid = r1.astype(BF16)
    lo = (r1 - mid.astype(F32)).astype(BF16)
    y = _dot(tri, jnp.concatenate([hi, mid, lo], axis=1))
    return y[:, 0:w] + y[:, w:2 * w] + y[:, 2 * w:3 * w]


def _first_step():
    return (pl.program_id(0) == 0) & (pl.program_id(1) == 0)


def _block_rows(i):
    return slice(SUB * i, SUB * (i + 1))


def _chunk_terms(qc, kc, bc, b_ref, first_row):
    betas = [jnp.zeros((1, HEAD_DIM), F32)]
    betas += [b_ref[first_row + SUB * i - 1:first_row + SUB * i, :] for i in range(1, N_SUB)]
    b_last = b_ref[first_row + CHUNK - 1:first_row + CHUNK, :]
    zeros = lambda rows: jnp.zeros((rows, HEAD_DIM), F32)
    qscale = [jnp.exp(bc[_block_rows(i), :] - betas[i]) for i in range(N_SUB)]
    qs = [qc[_block_rows(i), :] * qscale[i] for i in range(N_SUB)]
    kscale = [jnp.exp(betas[n] - bc[0:SUB * (n + 1), :]) for n in range(N_SUB)]
    ks = [kc[0:SUB * (n + 1), :] * kscale[n] for n in range(N_SUB)]

    def tall(parts):
        parts = [p for p in parts if p.shape[0]]
        return parts[0] if len(parts) == 1 else jnp.concatenate(parts, axis=0)

    qcat = jnp.concatenate([tall([zeros(SUB * n), qs[n], zeros(CHUNK - SUB * (n + 1))]) for n in range(N_SUB)],
                           axis=1)
    kcat = jnp.concatenate([tall([ks[n], zeros(CHUNK - SUB * (n + 1))]) for n in range(N_SUB)], axis=1)
    return dict(qscale=qscale, qcat=qcat, kscale=kscale, kcat=kcat,
                eb=jnp.exp(bc), e_last=jnp.exp(b_last), ktscale=jnp.exp(b_last - bc))


def _causal(shape_rows_first):
    r = lax.broadcasted_iota(jnp.int32, (CHUNK, CHUNK), 0)
    c = lax.broadcasted_iota(jnp.int32, (CHUNK, CHUNK), 1)
    return (c <= r) if shape_rows_first else (r <= c)


def _hgrn_specs(tm, tile_of):
    col = lambda base: (lambda h, i: (tile_of(i), base + h))
    return [pl.BlockSpec((tm, HEAD_DIM), col(8)), pl.BlockSpec((tm, HEAD_DIM), col(16)),
            pl.BlockSpec((tm, HEAD_DIM), col(24)), pl.BlockSpec((tm, HEAD_DIM), col(32)),
            pl.BlockSpec((2, HEAD_DIM), lambda h, i: (0, h)), pl.BlockSpec((1, HEAD_DIM), lambda h, i: (0, h))]


def _hgrn_fwd(proj, logits, ng):
    t = proj.shape[0]
    tm = 512
    nc = tm // CHUNK
    nt = t // tm

    def body(zq_ref, zf_ref, v_ref, zg_ref, lg_ref, ng_ref, o_ref, og_ref, st_ref,
             s_scr, q_scr, k_scr, b_scr, tri_scr):
        @pl.when(_first_step())
        def _():
            tri_scr[...] = _block_tri(tm, upper=False)

        @pl.when(pl.program_id(1) == 0)
        def _():
            s_scr[...] = jnp.zeros_like(s_scr)

        lb = _lower_bound(lg_ref)
        zf = zf_ref[...]
        f = lb + (1.0 - lb) * _sigmoid(zf)
        k_scr[...] = (1.0 - lb) * _sigmoid(-zf)
        zq = zq_ref[...]
        q_scr[...] = zq * _sigmoid(zq)
        b_scr[...] = _block_cumsum(jnp.log(f), tri_scr[...])

        st = s_scr[...]
        for c in range(nc):
            rows = pl.ds(c * CHUNK, CHUNK)
            qc, kc, bc, vc = q_scr[rows, :], k_scr[rows, :], b_scr[rows, :], v_ref[rows, :]
            st_ref[0, c] = st
            tr = _chunk_terms(qc, kc, bc, b_scr, c * CHUNK)
            a = jnp.where(_causal(True), _dot_nt(tr["qcat"].astype(BF16), tr["kcat"].astype(BF16)), 0.0)
            vb = vc.astype(BF16)
            o_ref[rows, :] = _dot(a.astype(BF16), vb) + _dot_nt((qc * tr["eb"]).astype(BF16), st.astype(BF16))
            st = st * tr["e_last"] + _dot_tn(vb, (kc * tr["ktscale"]).astype(BF16))
        s_scr[...] = st

        o = o_ref[...]
        rinv = lax.rsqrt(jnp.mean(o * o, axis=-1, keepdims=True) + RMS_EPS)
        zg = zg_ref[...]
        og_ref[...] = (o * rinv * ng_ref[...] * (zg * _sigmoid(zg))).astype(BF16)

    tile = pl.BlockSpec((tm, HEAD_DIM), lambda h, i: (i, h))
    return _call(
        body, name="hgrn_fwd", grid=(HGRN_HEADS, nt),
        in_specs=_hgrn_specs(tm, lambda i: i),
        out_specs=[tile, tile, pl.BlockSpec((1, nc, HEAD_DIM, HEAD_DIM), lambda h, i: (h, i, 0, 0))],
        out_shape=[_sds((t, HGRN_DIM), F32), _sds((t, HGRN_DIM), BF16),
                   _sds((HGRN_HEADS, t // CHUNK, HEAD_DIM, HEAD_DIM), F32)],
        scratch=[pltpu.VMEM((HEAD_DIM, HEAD_DIM), F32)] + [pltpu.VMEM((tm, HEAD_DIM), F32)] * 3
        + [pltpu.VMEM((tm, tm), BF16)],
    )(proj, proj, proj, proj, logits, ng)


def _hgrn_bwd(dog, o, states, proj, logits, ng, dstack):
    t = proj.shape[0]
    tm = 512
    nc = tm // CHUNK
    nt = t // tm

    def body(dog_ref, o_ref, st_ref, zq_ref, zf_ref, v_ref, zg_ref, lg_ref, ng_ref, stack_ref,
             dp_ref, dlg_ref, dng_ref,
             ds_scr, q_scr, k_scr, b_scr, do_scr, dq_scr, dk_scr, dv_scr, db_scr, dlb_scr, tri_scr):
        i = pl.program_id(1)

        @pl.when(_first_step())
        def _():
            tri_scr[0] = _block_tri(tm, upper=False)
            tri_scr[1] = _block_tri(tm, upper=True)

        @pl.when(i == 0)
        def _():
            ds_scr[...] = jnp.zeros_like(ds_scr)
            dlb_scr[...] = jnp.zeros_like(dlb_scr)
            dng_ref[...] = jnp.zeros_like(dng_ref)

        lb = _lower_bound(lg_ref)
        ng_row = ng_ref[...]
        o = o_ref[...]
        rinv = lax.rsqrt(jnp.mean(o * o, axis=-1, keepdims=True) + RMS_EPS)
        ohat = o * rinv
        zg = zg_ref[...]
        sg = _sigmoid(zg)
        dog_v = dog_ref[...]
        don = dog_v * (zg * sg)
        dp_ref[3] = (dog_v * (ohat * ng_row) * _silu_grad(zg, sg)).astype(BF16)
        dng_ref[...] += jnp.sum(don * ohat, axis=0, keepdims=True)
        dohat = don * ng_row
        do_scr[...] = rinv * (dohat - ohat * jnp.mean(dohat * ohat, axis=-1, keepdims=True))

        zf = zf_ref[...]
        s = _sigmoid(zf)
        s_neg = _sigmoid(-zf)
        f = lb + (1.0 - lb) * s
        k_scr[...] = (1.0 - lb) * s_neg
        zq = zq_ref[...]
        sq = _sigmoid(zq)
        q_scr[...] = zq * sq
        b_scr[...] = _block_cumsum(jnp.log(f), tri_scr[0])

        dst = ds_scr[...]
        for c in reversed(range(nc)):
            rows = pl.ds(c * CHUNK, CHUNK)
            qc, kc, bc, vc, doc = q_scr[rows, :], k_scr[rows, :], b_scr[rows, :], v_ref[rows, :], do_scr[rows, :]
            st = st_ref[0, c]
            tr = _chunk_terms(qc, kc, bc, b_scr, c * CHUNK)
            qcb, kcb = tr["qcat"].astype(BF16), tr["kcat"].astype(BF16)
            dob, vb, dstb = doc.astype(BF16), vc.astype(BF16), dst.astype(BF16)
            a_t = jnp.where(_causal(False), _dot_nt(kcb, qcb), 0.0)
            da = jnp.where(_causal(True), _dot_nt(dob, vb), 0.0)
            da_t = jnp.where(_causal(False), _dot_nt(vb, dob), 0.0)
            dqcat = _dot(da.astype(BF16), kcb)
            dkcat = _dot(da_t.astype(BF16), qcb)
            kt = kc * tr["ktscale"]
            dv_scr[rows, :] = _dot(a_t.astype(BF16), dob) + _dot_nt(kt.astype(BF16), dstb)
            dq_blocks, dk_blocks, db_blocks = [], [], []
            for j in range(N_SUB):
                rows_j = _block_rows(j)
                lanes_j = slice(j * HEAD_DIM, (j + 1) * HEAD_DIM)
                dq_j = dqcat[rows_j, lanes_j]
                dq_blocks.append(dq_j * tr["qscale"][j])
                db_j = qcb[rows_j, lanes_j].astype(F32) * dq_j
                dk_j = jnp.zeros((SUB, HEAD_DIM), F32)
                for n in range(j, N_SUB):
                    lanes_n = slice(n * HEAD_DIM, (n + 1) * HEAD_DIM)
                    dk_jn = dkcat[rows_j, lanes_n]
                    dk_j = dk_j + dk_jn * tr["kscale"][n][rows_j, :]
                    db_j = db_j - kcb[rows_j, lanes_n].astype(F32) * dk_jn
                dk_blocks.append(dk_j)
                db_blocks.append(db_j)
            dq_inter = _dot(dob, st.astype(BF16)) * tr["eb"]
            dkt = _dot(vb, dstb)
            dk_inter = dkt * tr["ktscale"]
            extra = (jnp.sum(dkt * kt, axis=0, keepdims=True)
                     + tr["e_last"] * jnp.sum(dst * st, axis=0, keepdims=True))
            dq_scr[rows, :] = jnp.concatenate(dq_blocks, axis=0) + dq_inter
            dk_scr[rows, :] = jnp.concatenate(dk_blocks, axis=0) + dk_inter
            db_scr[rows, :] = jnp.concatenate(db_blocks, axis=0) + qc * dq_inter - kc * dk_inter
            last = c * CHUNK + CHUNK - 1
            db_scr[last:last + 1, :] += extra
            dst = dst * tr["e_last"] + _dot_tn(dob, (qc * tr["eb"]).astype(BF16))
        ds_scr[...] = dst

        dlogf = _block_cumsum(db_scr[...], tri_scr[1])
        df = dlogf / f - dk_scr[...]
        dp_ref[0] = (dq_scr[...] * _silu_grad(zq, sq)).astype(BF16)
        dp_ref[1] = (df * (1.0 - lb) * s * (1.0 - s)).astype(BF16)
        dp_ref[2] = dv_scr[...].astype(BF16)
        dlb_scr[...] += jnp.sum(df * s_neg, axis=0, keepdims=True)

        @pl.when(i == nt - 1)
        def _():
            dlogit = dlb_scr[...] * lb * (1.0 - lb)
            dlg_ref[0:1, :] = dlogit
            dlg_ref[1:2, :] = -dlogit

    rev = lambda i: nt - 1 - i
    tile = pl.BlockSpec((tm, HEAD_DIM), lambda h, i: (rev(i), h))
    return _call(
        body, name="hgrn_bwd", grid=(HGRN_HEADS, nt),
        in_specs=[tile, tile, pl.BlockSpec((1, nc, HEAD_DIM, HEAD_DIM), lambda h, i: (h, rev(i), 0, 0))]
        + _hgrn_specs(tm, rev) + [ANY],
        out_specs=[pl.BlockSpec((4, tm, HEAD_DIM), lambda h, i: (0, rev(i), h)),
                   pl.BlockSpec((2, HEAD_DIM), lambda h, i: (0, h)),
                   pl.BlockSpec((1, HEAD_DIM), lambda h, i: (0, h))],
        out_shape=[_sds(dstack.shape, BF16), _sds((2, HGRN_DIM), F32), _sds((1, HGRN_DIM), F32)],
        aliases={9: 0},
        scratch=[pltpu.VMEM((HEAD_DIM, HEAD_DIM), F32)] + [pltpu.VMEM((tm, HEAD_DIM), F32)] * 8
        + [pltpu.VMEM((1, HEAD_DIM), F32), pltpu.VMEM((2, tm, tm), BF16)],
    )(dog, o, states, proj, proj, proj, proj, logits, ng, dstack)


def _merge_fwd(cs, og, proj, x, wco, wh, wo, g1, b1):
    t = x.shape[0]
    tm = 256

    def body(cs_ref, og_ref, m0_ref, m1_ref, x_ref, wco_ref, wh_ref, wo_ref, g_ref, b_ref,
             y_ref, mixed_ref, r1_ref, x1_ref, x1b_ref):
        yc = _dot(cs_ref[...], wco_ref[...])
        yh = _dot(og_ref[...], wh_ref[...])
        y_ref[0] = yc
        y_ref[1] = yh
        mixed = (_sigmoid(m0_ref[...]) * yc + _sigmoid(m1_ref[...]) * yh).astype(BF16)
        mixed_ref[...] = mixed
        r1 = ALPHA * x_ref[...] + _dot(mixed, wo_ref[...])
        r1_ref[...] = r1
        xhat, _ = _ln_stats(r1)
        x1 = xhat * g_ref[...] + b_ref[...]
        x1_ref[...] = x1
        x1b_ref[...] = x1.astype(BF16)

    row = lambda w: pl.BlockSpec((tm, w), lambda i: (i, 0))
    full = lambda a: pl.BlockSpec(a.shape, lambda i: (0, 0))
    return _call(
        body, name="merge_fwd", grid=(t // tm,),
        in_specs=[row(CONV_DIM), row(HGRN_DIM),
                  pl.BlockSpec((tm, D_MODEL), lambda i: (i, 5)), pl.BlockSpec((tm, D_MODEL), lambda i: (i, 6)),
                  row(D_MODEL), full(wco), full(wh), full(wo), full(g1), full(b1)],
        out_specs=[pl.BlockSpec((2, tm, D_MODEL), lambda i: (0, i, 0)), row(D_MODEL), row(D_MODEL),
                   row(D_MODEL), row(D_MODEL)],
        out_shape=[_sds((2, t, D_MODEL), F32), _sds((t, D_MODEL), BF16), _sds((t, D_MODEL), F32),
                   _sds((t, D_MODEL), F32), _sds((t, D_MODEL), BF16)],
        vmem_mb=48)(cs, og, proj, proj, x, wco, wh, wo, g1, b1)


def _merge_bwd(dr1b, ycat, proj, wo, wco, wh):
    t = dr1b.shape[0]
    tm = 256

    def body(dr_ref, y_ref, m0_ref, m1_ref, wo_ref, wco_ref, wh_ref, dpm_ref, dy_ref, dcs_ref, dog_ref):
        dmixed = _dot_nt(dr_ref[...], wo_ref[...])
        g0 = _sigmoid(m0_ref[...])
        g1 = _sigmoid(m1_ref[...])
        dpm_ref[0] = (dmixed * y_ref[0] * g0 * (1.0 - g0)).astype(BF16)
        dpm_ref[1] = (dmixed * y_ref[1] * g1 * (1.0 - g1)).astype(BF16)
        dyc = (dmixed * g0).astype(BF16)
        dyh = (dmixed * g1).astype(BF16)
        dy_ref[0] = dyc
        dy_ref[1] = dyh
        dcs_ref[...] = _dot_nt(dyc, wco_ref[...])
        dog_ref[...] = _dot_nt(dyh, wh_ref[...])

    row = lambda w: pl.BlockSpec((tm, w), lambda i: (i, 0))
    pair = pl.BlockSpec((2, tm, D_MODEL), lambda i: (0, i, 0))
    full = lambda a: pl.BlockSpec(a.shape, lambda i: (0, 0))
    return _call(
        body, name="merge_bwd", grid=(t // tm,),
        in_specs=[row(D_MODEL), pair,
                  pl.BlockSpec((tm, D_MODEL), lambda i: (i, 5)), pl.BlockSpec((tm, D_MODEL), lambda i: (i, 6)),
                  full(wo), full(wco), full(wh)],
        out_specs=[pl.BlockSpec((2, tm, D_MODEL), lambda i: (DPROJ_GATE_SLAB // 2, i, 0)), pair,
                   row(CONV_DIM), row(HGRN_DIM)],
        out_shape=[_sds((DPROJ_SLABS, t, D_MODEL), BF16), _sds((2, t, D_MODEL), BF16),
                   _sds((t, CONV_DIM), F32), _sds((t, HGRN_DIM), F32)],
        vmem_mb=48)(dr1b, ycat, proj, proj, wo, wco, wh)


def _ffn_taps(win):
    return (pltpu.roll(win, 2, 0)[FFN_HALO:, :], pltpu.roll(win, 1, 0)[FFN_HALO:, :], win[FFN_HALO:, :])


def _ffn_conv3(taps, w_ref):
    return w_ref[0:1, :] * taps[0] + w_ref[1:2, :] * taps[1] + w_ref[2:3, :] * taps[2]


def _ffn_mid(z, wfd, bfd):
    t = z.shape[0]
    tm = 256

    def body(u_ref, gv_ref, w_ref, b_ref, h_ref, ext_ref):
        i = pl.program_id(0)

        @pl.when(i == 0)
        def _():
            ext_ref[0:FFN_HALO, :] = jnp.zeros((FFN_HALO, D_FF), F32)

        @pl.when(i > 0)
        def _():
            ext_ref[0:FFN_HALO, :] = ext_ref[tm:tm + FFN_HALO, :]

        ext_ref[FFN_HALO:FFN_HALO + tm, :] = u_ref[...]

        def block(r, carry):
            r0 = pl.multiple_of(r * ROW_BLOCK, ROW_BLOCK)
            rows = pl.ds(r0, ROW_BLOCK)
            win = ext_ref[pl.ds(r0, ROW_BLOCK + FFN_HALO), :]
            uc = _ffn_conv3(_ffn_taps(win), w_ref) + b_ref[...]
            g, _ = _gelu_and_grad(uc)
            h_ref[rows, :] = (g * gv_ref[rows, :]).astype(BF16)
            return carry

        lax.fori_loop(0, tm // ROW_BLOCK, block, 0)

    return _call(
        body, name="ffn_mid", grid=(t // tm,),
        in_specs=[pl.BlockSpec((tm, D_FF), lambda i: (i, 0)), pl.BlockSpec((tm, D_FF), lambda i: (i, 1)),
                  pl.BlockSpec((FFN_KERNEL, D_FF), lambda i: (0, 0)), pl.BlockSpec((1, D_FF), lambda i: (0, 0))],
        out_specs=pl.BlockSpec((tm, D_FF), lambda i: (i, 0)),
        out_shape=_sds((t, D_FF), BF16),
        scratch=[pltpu.VMEM((tm + FFN_HALO, D_FF), F32)], vmem_mb=40)(z, z, wfd, bfd)


def _ffn_out_loss(hmid, x1, target, wfo, g2, b2):
    t = x1.shape[0]
    tm = 256
    inv_n = 1.0 / D_MODEL

    def body(h_ref, x1_ref, tg_ref, w_ref, g_ref, b_ref, dr_ref, drb_ref, loss_ref, dg_ref, db_ref):
        @pl.when(pl.program_id(0) == 0)
        def _():
            loss_ref[...] = jnp.zeros_like(loss_ref)
            dg_ref[...] = jnp.zeros_like(dg_ref)
            db_ref[...] = jnp.zeros_like(db_ref)

        r2 = ALPHA * x1_ref[...] + _dot(h_ref[...], w_ref[...])
        xhat, rstd = _ln_stats(r2)
        err = xhat * g_ref[...] + b_ref[...] - tg_ref[...]
        loss_ref[...] += 0.5 * inv_n * jnp.sum(err * err)
        dy = err * inv_n
        dg_ref[...] += jnp.sum(dy * xhat, axis=0, keepdims=True)
        db_ref[...] += jnp.sum(dy, axis=0, keepdims=True)
        dr = _ln_bwd(dy, xhat, rstd, g_ref[...])
        dr_ref[...] = dr
        drb_ref[...] = dr.astype(BF16)

    row = lambda w: pl.BlockSpec((tm, w), lambda i: (i, 0))
    vec = pl.BlockSpec((1, D_MODEL), lambda i: (0, 0))
    return _call(
        body, name="ffn_out_loss", grid=(t // tm,),
        in_specs=[row(D_FF), row(D_MODEL), row(D_MODEL), pl.BlockSpec((D_FF, D_MODEL), lambda i: (0, 0)), vec, vec],
        out_specs=[row(D_MODEL), row(D_MODEL), pl.BlockSpec((1, 128), lambda i: (0, 0)), vec, vec],
        out_shape=[_sds((t, D_MODEL), F32), _sds((t, D_MODEL), BF16), _sds((1, 128), F32),
                   _sds((1, D_MODEL), F32), _sds((1, D_MODEL), F32)],
        vmem_mb=40)(hmid, x1, target, wfo, g2, b2)


def _ffn_bwd_a(dr2b, z, wfo, wfd, bfd):
    t = z.shape[0]
    tm = 256
    nt = t // tm

    def body(dr_ref, u_ref, gv_ref, wfo_ref, w_ref, b_ref, dgv_ref, duc_ref, dw_ref, db_ref,
             ext_ref, dh_ref, acc_ref):
        i = pl.program_id(0)

        @pl.when(i == 0)
        def _():
            ext_ref[0:FFN_HALO, :] = jnp.zeros((FFN_HALO, D_FF), F32)
            acc_ref[...] = jnp.zeros_like(acc_ref)

        @pl.when(i > 0)
        def _():
            ext_ref[0:FFN_HALO, :] = ext_ref[tm:tm + FFN_HALO, :]

        ext_ref[FFN_HALO:FFN_HALO + tm, :] = u_ref[...]
        dh_ref[...] = _dot_nt(dr_ref[...], wfo_ref[...])

        def block(r, carry):
            r0 = pl.multiple_of(r * ROW_BLOCK, ROW_BLOCK)
            rows = pl.ds(r0, ROW_BLOCK)
            taps = _ffn_taps(ext_ref[pl.ds(r0, ROW_BLOCK + FFN_HALO), :])
            uc = _ffn_conv3(taps, w_ref) + b_ref[...]
            g, dg = _gelu_and_grad(uc)
            dh = dh_ref[rows, :]
            dgv_ref[rows, :] = (dh * g).astype(BF16)
            duc = dh * gv_ref[rows, :] * dg
            duc_ref[rows, :] = duc
            acc_ref[0:8, :] += _fold8(duc)
            for k in range(FFN_KERNEL):
                acc_ref[8 + 8 * k:16 + 8 * k, :] += _fold8(duc * taps[k])
            return carry

        lax.fori_loop(0, tm // ROW_BLOCK, block, 0)

        @pl.when(i == nt - 1)
        def _():
            db_ref[...] = jnp.sum(acc_ref[0:8, :], axis=0, keepdims=True)
            for k in range(FFN_KERNEL):
                dw_ref[k:k + 1, :] = jnp.sum(acc_ref[8 + 8 * k:16 + 8 * k, :], axis=0, keepdims=True)

    tile = pl.BlockSpec((tm, D_FF), lambda i: (i, 0))
    return _call(
        body, name="ffn_bwd_a", grid=(nt,),
        in_specs=[pl.BlockSpec((tm, D_MODEL), lambda i: (i, 0)), tile, pl.BlockSpec((tm, D_FF), lambda i: (i, 1)),
                  pl.BlockSpec((D_FF, D_MODEL), lambda i: (0, 0)),
                  pl.BlockSpec((FFN_KERNEL, D_FF), lambda i: (0, 0)), pl.BlockSpec((1, D_FF), lambda i: (0, 0))],
        out_specs=[tile, tile, pl.BlockSpec((FFN_KERNEL, D_FF), lambda i: (0, 0)),
                   pl.BlockSpec((1, D_FF), lambda i: (0, 0))],
        out_shape=[_sds((t, D_FF), BF16), _sds((t, D_FF), F32), _sds((FFN_KERNEL, D_FF), F32), _sds((1, D_FF), F32)],
        scratch=[pltpu.VMEM((tm + FFN_HALO, D_FF), F32), pltpu.VMEM((tm, D_FF), F32),
                 pltpu.VMEM((8 + 8 * FFN_KERNEL, D_FF), F32)],
        vmem_mb=56)(dr2b, z, z, wfo, wfd, bfd)


def _ffn_bwd_b(duc, wfd):
    t = duc.shape[0]
    tm = 256
    nt = t // tm

    def body(duc_ref, w_ref, du_ref, ext_ref):
        i = pl.program_id(0)

        @pl.when(i == 0)
        def _():
            ext_ref[tm:tm + FFN_HALO, :] = jnp.zeros((FFN_HALO, D_FF), F32)

        @pl.when(i > 0)
        def _():
            ext_ref[tm:tm + FFN_HALO, :] = ext_ref[0:FFN_HALO, :]

        ext_ref[0:tm, :] = duc_ref[...]

        def block(r, carry):
            r0 = pl.multiple_of(r * ROW_BLOCK, ROW_BLOCK)
            win = ext_ref[pl.ds(r0, ROW_BLOCK + FFN_HALO), :]
            n = ROW_BLOCK + FFN_HALO
            du = (w_ref[2:3, :] * win[0:ROW_BLOCK, :] + w_ref[1:2, :] * pltpu.roll(win, n - 1, 0)[0:ROW_BLOCK, :]
                  + w_ref[0:1, :] * pltpu.roll(win, n - 2, 0)[0:ROW_BLOCK, :])
            du_ref[pl.ds(r0, ROW_BLOCK), :] = du.astype(BF16)
            return carry

        lax.fori_loop(0, tm // ROW_BLOCK, block, 0)

    rev = lambda i: (nt - 1 - i, 0)
    return _call(
        body, name="ffn_bwd_b", grid=(nt,),
        in_specs=[pl.BlockSpec((tm, D_FF), rev), pl.BlockSpec((FFN_KERNEL, D_FF), lambda i: (0, 0))],
        out_specs=pl.BlockSpec((tm, D_FF), rev),
        out_shape=_sds((t, D_FF), BF16),
        scratch=[pltpu.VMEM((tm + FFN_HALO, D_FF), F32)], vmem_mb=40)(duc, wfd)


def _ffn_in_bwd(dr2, dub, dgvb, wfi, r1, g1):
    t = dr2.shape[0]
    tm = 256

    def body(dr2_ref, du_ref, dgv_ref, wu_ref, wg_ref, r1_ref, g_ref, dr1_ref, dr1b_ref, dg_ref, db_ref):
        @pl.when(pl.program_id(0) == 0)
        def _():
            dg_ref[...] = jnp.zeros_like(dg_ref)
            db_ref[...] = jnp.zeros_like(db_ref)

        dx1 = ALPHA * dr2_ref[...] + _dot_nt(du_ref[...], wu_ref[...]) + _dot_nt(dgv_ref[...], wg_ref[...])
        xhat, rstd = _ln_stats(r1_ref[...])
        dg_ref[...] += jnp.sum(dx1 * xhat, axis=0, keepdims=True)
        db_ref[...] += jnp.sum(dx1, axis=0, keepdims=True)
        dr1 = _ln_bwd(dx1, xhat, rstd, g_ref[...])
        dr1_ref[...] = dr1
        dr1b_ref[...] = dr1.astype(BF16)

    row = lambda w: pl.BlockSpec((tm, w), lambda i: (i, 0))
    vec = pl.BlockSpec((1, D_MODEL), lambda i: (0, 0))
    return _call(
        body, name="ffn_in_bwd", grid=(t // tm,),
        in_specs=[row(D_MODEL), row(D_FF), row(D_FF),
                  pl.BlockSpec((D_MODEL, D_FF), lambda i: (0, 0)), pl.BlockSpec((D_MODEL, D_FF), lambda i: (0, 1)),
                  row(D_MODEL), vec],
        out_specs=[row(D_MODEL), row(D_MODEL), vec, vec],
        out_shape=[_sds((t, D_MODEL), F32), _sds((t, D_MODEL), BF16), _sds((1, D_MODEL), F32), _sds((1, D_MODEL), F32)],
        vmem_mb=56)(dr2, dub, dgvb, wfi, wfi, r1, g1)


def _local_step(x, target, wi, late_weights, wcd, bcd, clg, clb, logits, ng, g1, b1, wfd, bfd, g2, b2):
    proj, xb = _proj(x, wi)
    cc, cs = _conv_fwd(proj, wcd, bcd, clg, clb)
    o, og, states = _hgrn_fwd(proj, logits, ng)
    wco, wh, wo, wfi, wfo = late_weights(og)
    ycat, mixed, r1, x1, x1b = _merge_fwd(cs, og, proj, x, wco, wh, wo, g1, b1)
    z = _mm_nn(x1b, wfi, tm=1024, tn=1408, name="ffn_in", vmem_mb=48)
    hmid = _ffn_mid(z, wfd, bfd)
    dr2, dr2b, loss, d_g2, d_b2 = _ffn_out_loss(hmid, x1, target, wfo, g2, b2)

    g_wfo = _mm_tn(hmid, _views(dr2b), tn=512, tt=1024, name="grad_w_ffn_out", vmem_mb=48)
    dgvb, duc, d_wfd, d_bfd = _ffn_bwd_a(dr2b, z, wfo, wfd, bfd)
    dub = _ffn_bwd_b(duc, wfd)
    g_wfi = _mm_tn(x1b, _views(dub, dgvb), tn=1408, tt=1024, name="grad_w_ffn_in", vmem_mb=48)
    dr1, dr1b, d_g1, d_b1 = _ffn_in_bwd(dr2, dub, dgvb, wfi, r1, g1)

    g_wo = _mm_tn(mixed, _views(dr1b), tn=1024, tt=1024, name="grad_w_out")
    dstack, dyb, dcs, dog = _merge_bwd(dr1b, ycat, proj, wo, wco, wh)
    g_wco = _mm_tn(cs, [(dyb, 0)], tn=1024, tt=1024, name="grad_w_conv_out")
    g_wh = _mm_tn(og, [(dyb, 1)], tn=1024, tt=1024, name="grad_w_hgrn_out")

    dcc, d_wcd, d_bcd, d_clg, d_clb = _conv_bwd_a(dcs, cc, proj, clg, clb)
    dstack = _conv_bwd_b(dcc, proj, wcd, dstack)
    dstack, d_logits, d_ng = _hgrn_bwd(dog, o, states, proj, logits, ng, dstack)

    g_wi = _grad_w_in(xb, dstack)
    grad_x = _grad_x(dstack, wi, dr1)

    small = dict(w_conv_dw=d_wcd, b_conv_dw=d_bcd, conv_ln_g=d_clg, conv_ln_b=d_clb, hgrn_lb_logits=d_logits,
                 hgrn_norm_g=d_ng, ln1_g=d_g1, ln1_b=d_b1, w_ffn_dw=d_wfd, b_ffn_dw=d_bfd, ln2_g=d_g2, ln2_b=d_b2)
    return loss, grad_x, (g_wi, g_wco, g_wh, g_wo, g_wfi, g_wfo), small


ELEMENTWISE_BLOCK_ELEMS = 256 * 1024


def _row_tile(rows, cols):
    cap = max(16, ELEMENTWISE_BLOCK_ELEMS // cols)
    if rows <= cap:
        return rows
    best = None
    for cand in range(16, cap + 1, 16):
        if rows % cand == 0:
            best = cand
    assert best is not None
    return best


def _elementwise(fn, ins, out_dtypes, *, name):
    r, c = ins[0].shape
    tr = _row_tile(r, c)

    def body(*refs):
        outs = fn(*[ref[...] for ref in refs[:len(ins)]])
        for ref, val in zip(refs[len(ins):], outs):
            ref[...] = val.astype(ref.dtype)

    spec = pl.BlockSpec((tr, c), lambda i: (i, 0))
    return _call(
        body, name=name, grid=(r // tr,), in_specs=[spec] * len(ins), out_specs=[spec] * len(out_dtypes),
        out_shape=[_sds((r, c), dt) for dt in out_dtypes])(*ins)


def _windowed(fn, sel, ins, outs, *, window, name):
    rows, cols = window
    tr = _row_tile(rows, cols)
    steps = rows // tr

    def spec(where):
        if where is None:
            return pl.BlockSpec((tr, cols), lambda i, s: (i, 0))
        kind, p, _ = where
        if kind == "rows":
            return pl.BlockSpec((tr, cols), lambda i, s: (s[p] * steps + i, 0))
        return pl.BlockSpec((tr, cols), lambda i, s: (i, s[p]))

    n_in = len(ins)

    def body(s_ref, *refs):
        vals = fn(*[ref[...] for ref in refs[:n_in]])
        for ref, val in zip(refs[n_in:], vals):
            ref[...] = val.astype(ref.dtype)

    return pl.pallas_call(
        body, name=name, out_shape=[_hbm(shape, dt) for shape, dt, _ in outs],
        grid_spec=pltpu.PrefetchScalarGridSpec(
            num_scalar_prefetch=1, grid=(steps,), in_specs=[spec(where) for _, where in ins],
            out_specs=[spec(where) for _, _, where in outs]),
        compiler_params=pltpu.CompilerParams(dimension_semantics=("arbitrary",), vmem_limit_bytes=32 * 2 ** 20),
    )(sel, *[_in_hbm(a) for a, _ in ins])


def _adamw(w, g, m, v, *, name):
    def fn(w_, g_, m_, v_):
        m_new = ADAM_B1 * m_ + (1.0 - ADAM_B1) * g_
        v_new = ADAM_B2 * v_ + (1.0 - ADAM_B2) * (g_ * g_)
        m_hat = m_new / ADAM_BC1
        v_hat = v_new / ADAM_BC2
        delta = -ADAM_LR * (m_hat / (jnp.sqrt(v_hat) + ADAM_EPS) + ADAM_WD * w_)
        return delta, m_new, v_new

    return _elementwise(fn, [w, g, m, v], [F32, F32, F32], name=name)


def _place():
    return lax.axis_index("x"), lax.axis_index("y"), lax.axis_index("c")


def _other_chips(x, y):
    return [(1 - x, y), (x, 1 - y), (1 - x, 1 - y)]


SHARD_XOR = (2, 1, 3)


DMA_CHUNK_BYTES = 512 * 1024


def _n_chunks(ref):
    rows = ref.shape[0]
    total = ref.dtype.itemsize
    for d in ref.shape:
        total *= d
    best = 1
    for cand in range(2, min(rows, total // DMA_CHUNK_BYTES) + 1):
        if rows % cand == 0 and (rows // cand) % 16 == 0:
            best = cand
    return best


class _Copy:
    def __init__(self, src, dst, sems, dev=None):
        if dev is None:
            make = lambda s_, d_: pltpu.make_async_copy(s_, d_, sems[0])
        else:
            make = lambda s_, d_: pltpu.make_async_remote_copy(
                src_ref=s_, dst_ref=d_, send_sem=sems[0], recv_sem=sems[1], device_id=dev, device_id_type=MESH)
        self.local = dev is None
        self.whole = make(src, dst)
        n = _n_chunks(src)
        step = src.shape[0] // n
        self.parts = ([self.whole] if n == 1 else
                      [make(src.at[pl.ds(i * step, step)], dst.at[pl.ds(i * step, step)]) for i in range(n)])

    def start(self):
        for part in self.parts:
            part.start()

    def wait_recv(self):
        self.whole.wait_recv()

    def wait_send(self):
        self.whole.wait_send()

    def wait(self):
        self.whole.wait()


def _run_copies(local_ops, remote_ops, lsem, ssem, rsem):
    local = [_Copy(src, dst, (lsem.at[n],)) for n, (src, dst) in enumerate(local_ops)]
    remote = [_Copy(src, dst, (ssem.at[n], rsem.at[n]), dev) for n, (src, dst, dev) in enumerate(remote_ops)]
    for cp in local + remote:
        cp.start()
    for cp in remote:
        cp.wait_recv()
    for cp in remote:
        cp.wait_send()
    for cp in local:
        cp.wait()


def _comm_call(body, *, name, n_in, out_shape, n_local, n_remote):
    return pl.pallas_call(
        body, name=name, in_specs=[ANY] * n_in, out_specs=[ANY] * len(out_shape), out_shape=out_shape,
        scratch_shapes=[pltpu.SemaphoreType.DMA((max(n_local, 1),)), pltpu.SemaphoreType.DMA((n_remote,)),
                        pltpu.SemaphoreType.DMA((n_remote,))])


BIG = (("w_in", D_MODEL, IN_COLS, 1), ("w_conv_out", CONV_DIM, D_MODEL, 1), ("w_hgrn_out", HGRN_DIM, D_MODEL, 0),
       ("w_out", D_MODEL, D_MODEL, 0), ("w_ffn_in", D_MODEL, 2 * D_FF, 1), ("w_ffn_out", D_FF, D_MODEL, 0))


def _shard_slice(ref, rows, cols, axis, k):
    if axis == 1:
        w = cols // N_CHIPS
        return ref.at[:, pl.ds(k * w, w)]
    h = rows // N_CHIPS
    return ref.at[pl.ds(k * h, h), :]


def _half_slice(ref, rows, cols, axis, hc):
    if axis == 1:
        return ref.at[pl.ds(hc * (rows // 2), rows // 2), :]
    return ref.at[:, pl.ds(hc * (cols // 2), cols // 2)]


def _half_shape(rows, cols, axis):
    return (rows // 2, cols) if axis == 1 else (rows, cols // 2)


def _gather_weights(full, small):
    n_big, n_small = len(full), len(small)
    n_arr = n_big + n_small
    out_shape = ([_sds((r, c), BF16) for _, r, c, _ in BIG[:n_big]]
                 + [_sds((N_CHIPS,) + a.shape, F32) for a in small])
    shard_shape = [(r, c // N_CHIPS) if ax == 1 else (r // N_CHIPS, c) for _, r, c, ax in BIG]

    def body(*refs):
        ins, outs = refs[:n_arr], refs[n_arr:2 * n_arr]
        lsem, ssem, rsem, fsem_s, fsem_r = refs[2 * n_arr:]
        x, y, c = _place()
        me = 2 * x + y
        chips = _other_chips(x, y)
        sibling = (x, y, 1 - c)

        def region(idx, k, hc, bufs=outs):
            (_, r, cc, ax), (sr, _) = BIG[idx], shard_shape[idx]
            return _shard_slice(bufs[idx], r, cc, ax, k).at[pl.ds(hc * (sr // 2), sr // 2)]

        for k in range(N_CHIPS):
            for hc in range(2):
                @pl.when((me == k) & (c == hc))
                def _(k=k, hc=hc):
                    local = [_Copy(ins[n_big + i], outs[n_big + i].at[k], (lsem.at[i],)) for i in range(n_small)]
                    sends, fwds = [], []
                    for j, (cx, cy) in enumerate(chips):
                        for i in range(n_big):
                            n = j * n_arr + i
                            sends.append(_Copy(region(i, k, hc, ins), region(i, k, hc), (ssem.at[n], rsem.at[n]),
                                               (cx, cy, c)))
                            reg = region(i, k ^ SHARD_XOR[j], hc)
                            fwds.append(_Copy(reg, reg, (fsem_s.at[j * n_big + i], fsem_r.at[j * n_big + i]), sibling))
                        for i in range(n_small):
                            n = j * n_arr + n_big + i
                            sends.append(_Copy(ins[n_big + i], outs[n_big + i].at[k], (ssem.at[n], rsem.at[n]),
                                               (cx, cy, c)))
                    for cp in local + sends:
                        cp.start()
                    for j in range(3):
                        for i in range(n_big):
                            sends[j * n_arr + i].wait_recv()
                        for i in range(n_big):
                            fwds[j * n_big + i].start()
                    for j in range(3):
                        for i in range(n_small):
                            sends[j * n_arr + n_big + i].wait_recv()
                    for cp in fwds:
                        cp.wait_recv()
                    for cp in sends + fwds:
                        cp.wait_send()
                    for cp in local:
                        cp.wait()

    return pl.pallas_call(
        body, name="gather_weights", in_specs=[ANY] * n_arr, out_specs=[ANY] * n_arr, out_shape=out_shape,
        input_output_aliases={i: i for i in range(n_big)},
        scratch_shapes=[pltpu.SemaphoreType.DMA((n_small,)), pltpu.SemaphoreType.DMA((3 * n_arr,)),
                        pltpu.SemaphoreType.DMA((3 * n_arr,)), pltpu.SemaphoreType.DMA((3 * n_big,)),
                        pltpu.SemaphoreType.DMA((3 * n_big,))])(*full, *small)


SEM = pl.BlockSpec(memory_space=pltpu.SEMAPHORE)
N_EARLY = 1
LATE = BIG[N_EARLY:]


def _late_copies(src_bufs, dst_bufs, ssem, rsem, x, y, c, k):
    copies = []
    for j, (cx, cy) in enumerate(_other_chips(x, y)):
        for i, (src, dst, (_, r, cc, ax)) in enumerate(zip(src_bufs, dst_bufs, LATE)):
            n = j * len(LATE) + i
            copies.append(_Copy(_shard_slice(src, r, cc, ax, k), _shard_slice(dst, r, cc, ax, k),
                                (ssem.at[n], rsem.at[n]), (cx, cy, c)))
    return copies


def _gather_late_start(full, carry):
    n = len(LATE)

    def body(*refs):
        ins, (ssem, rsem), outs = refs[:n], refs[n + 1:n + 3], refs[n + 3:2 * n + 3]
        x, y, c = _place()
        me = 2 * x + y
        for k in range(N_CHIPS):
            @pl.when(me == k)
            def _(k=k):
                for cp in _late_copies(ins, outs, ssem, rsem, x, y, c, k):
                    cp.start()

    bufs = list(full) + [carry]
    return pl.pallas_call(
        body, name="gather_late_start", in_specs=[ANY] * (n + 1), out_specs=[SEM, SEM] + [ANY] * (n + 1),
        out_shape=[pltpu.SemaphoreType.DMA((3 * n,)), pltpu.SemaphoreType.DMA((3 * n,))]
        + [_hbm(a.shape, a.dtype) for a in bufs],
        input_output_aliases={i: 2 + i for i in range(n + 1)},
        compiler_params=pltpu.CompilerParams(has_side_effects=pltpu.SideEffectType.DATAFLOW_SIDE_EFFECTING),
    )(*[_in_hbm(a) for a in bufs])


def _gather_late_wait(ssem, rsem, full, after):
    n = len(LATE)

    def body(*refs):
        ins, ssem_ref, rsem_ref = refs[:n], refs[n], refs[n + 1]
        x, y, c = _place()
        for cp in _late_copies(ins, ins, ssem_ref, rsem_ref, x, y, c, 0):
            cp.wait_send()
            cp.wait_recv()

    return pl.pallas_call(
        body, name="gather_late_wait", in_specs=[ANY] * n + [SEM, SEM, ANY], out_specs=[ANY] * n,
        out_shape=[_hbm(a.shape, a.dtype) for a in full], input_output_aliases={i: i for i in range(n)},
        compiler_params=pltpu.CompilerParams(has_side_effects=pltpu.SideEffectType.DATAFLOW_SIDE_EFFECTING),
    )(*[_in_hbm(a) for a in full], ssem, rsem, after)


def _sibling_exchange(grads):
    n = len(BIG)
    shapes = [_sds(_half_shape(r, c, ax), F32) for _, r, c, ax in BIG]

    def body(*refs):
        ins, got = refs[:n], refs[n:2 * n]
        lsem, ssem, rsem = refs[2 * n:]
        x, y, c = _place()
        for k in range(2):
            @pl.when(c == k)
            def _(k=k):
                remote_ops = [(_half_slice(g, r, cc, ax, 1 - k), dst, (x, y, 1 - c))
                              for g, dst, (_, r, cc, ax) in zip(ins, got, BIG)]
                _run_copies([], remote_ops, lsem, ssem, rsem)

    return _comm_call(body, name="grad_sibling_exchange", n_in=n, out_shape=shapes, n_local=0, n_remote=n)(*grads)


def _piece_shape(rows, cols, axis):
    hr, hc = _half_shape(rows, cols, axis)
    return (hr, hc // N_CHIPS) if axis == 1 else (hr // N_CHIPS, hc)


def _chip_exchange(chip_sums):
    n = len(BIG)
    half = [_half_shape(r, c, ax) for _, r, c, ax in BIG]
    out_shape = [_sds(_piece_shape(r, c, ax), BF16) for _, r, c, ax in BIG for _ in range(3)]

    def body(*refs):
        ins, got = refs[:n], refs[n:4 * n]
        lsem, ssem, rsem = refs[4 * n:]
        x, y, c = _place()
        me = 2 * x + y
        for k in range(N_CHIPS):
            @pl.when(me == k)
            def _(k=k):
                remote_ops = [(_shard_slice(g, half[idx][0], half[idx][1], BIG[idx][3], k ^ SHARD_XOR[j]),
                               got[3 * idx + j], (cx, cy, c))
                              for j, (cx, cy) in enumerate(_other_chips(x, y))
                              for idx, g in enumerate(ins)]
                _run_copies([], remote_ops, lsem, ssem, rsem)

    outs = _comm_call(body, name="grad_chip_exchange", n_in=n, out_shape=out_shape, n_local=0, n_remote=3 * n)(
        *chip_sums)
    return [outs[3 * idx:3 * idx + 3] for idx in range(n)]


def _sibling_assemble(shards):
    n = len(BIG)
    shape = [(r, c // N_CHIPS) if ax == 1 else (r // N_CHIPS, c) for _, r, c, ax in BIG]

    def body(*refs):
        ins, outs = refs[:n], refs[n:2 * n]
        lsem, ssem, rsem = refs[2 * n:]
        x, y, c = _place()
        for k in range(2):
            @pl.when(c == k)
            def _(k=k):
                remote_ops = [(_half_slice(i_, sr, sc, ax, k), _half_slice(o_, sr, sc, ax, k), (x, y, 1 - c))
                              for i_, o_, (sr, sc), (_, _, _, ax) in zip(ins, outs, shape, BIG)]
                _run_copies([], remote_ops, lsem, ssem, rsem)

    return pl.pallas_call(
        body, name="grad_sibling_assemble", in_specs=[ANY] * n, out_specs=[ANY] * n,
        out_shape=[_sds(s, F32) for s in shape], input_output_aliases={i: i for i in range(n)},
        scratch_shapes=[pltpu.SemaphoreType.DMA((1,)), pltpu.SemaphoreType.DMA((n,)),
                        pltpu.SemaphoreType.DMA((n,))])(*shards)


def _all_reduce_small(packed):
    r, w = packed.shape

    def body(in_ref, out_ref, slots, lsem, ssem, rsem):
        x, y, c = _place()
        me = 4 * x + 2 * y + c
        peers = [(x ^ (m >> 2), y ^ ((m >> 1) & 1), c ^ (m & 1)) for m in range(1, N_DEV)]
        _run_copies([(in_ref, slots.at[me])], [(in_ref, slots.at[me], dev) for dev in peers], lsem, ssem, rsem)
        total = slots[0]
        for d in range(1, N_DEV):
            total = total + slots[d]
        out_ref[...] = total

    vmem = pl.BlockSpec(memory_space=pltpu.VMEM)
    return pl.pallas_call(
        body, name="small_all_reduce", in_specs=[vmem], out_specs=vmem, out_shape=_sds((r, w), F32),
        scratch_shapes=[pltpu.VMEM((N_DEV, r, w), F32), pltpu.SemaphoreType.DMA((1,)),
                        pltpu.SemaphoreType.DMA((N_DEV - 1,)), pltpu.SemaphoreType.DMA((N_DEV - 1,))])(packed)


SMALL_ORDER = ("w_conv_dw", "b_conv_dw", "conv_ln_g", "conv_ln_b", "hgrn_lb_logits", "hgrn_norm_g",
               "ln1_g", "ln1_b", "w_ffn_dw", "b_ffn_dw", "ln2_g", "ln2_b")
REPLICATED_SMALL = tuple(n for n in SMALL_ORDER if n not in ("w_conv_dw", "w_ffn_dw"))
WEIGHT_ORDER = ("w_in", "w_conv_dw", "b_conv_dw", "conv_ln_g", "conv_ln_b", "w_conv_out", "hgrn_lb_logits",
                "hgrn_norm_g", "w_hgrn_out", "w_out", "ln1_g", "ln1_b", "w_ffn_in", "w_ffn_dw", "b_ffn_dw",
                "w_ffn_out", "ln2_g", "ln2_b")


def _pack(arrs):
    flat = jnp.concatenate([a.reshape(-1) for a in arrs])
    assert flat.shape[0] % 128 == 0
    return flat.reshape(-1, 128)


def _unpack(packed, shapes):
    flat = packed.reshape(-1)
    out, pos = [], 0
    for shp in shapes:
        size = 1
        for d in shp:
            size *= d
        out.append(flat[pos:pos + size].reshape(shp))
        pos += size
    return out


def kernel(x, w_in, w_conv_dw, b_conv_dw, conv_ln_g, conv_ln_b, w_conv_out, hgrn_lb_logits, hgrn_norm_g, w_hgrn_out, w_out, ln1_g, ln1_b, w_ffn_in, w_ffn_dw, b_ffn_dw, w_ffn_out, ln2_g, ln2_b, loss_target, m_w_in, m_w_conv_dw, m_b_conv_dw, m_conv_ln_g, m_conv_ln_b, m_w_conv_out, m_hgrn_lb_logits, m_hgrn_norm_g, m_w_hgrn_out, m_w_out, m_ln1_g, m_ln1_b, m_w_ffn_in, m_w_ffn_dw, m_b_ffn_dw, m_w_ffn_out, m_ln2_g, m_ln2_b, v_w_in, v_w_conv_dw, v_b_conv_dw, v_conv_ln_g, v_conv_ln_b, v_w_conv_out, v_hgrn_lb_logits, v_hgrn_norm_g, v_w_hgrn_out, v_w_out, v_ln1_g, v_ln1_b, v_w_ffn_in, v_w_ffn_dw, v_b_ffn_dw, v_w_ffn_out, v_ln2_g, v_ln2_b):
    w = dict(w_in=w_in, w_conv_dw=w_conv_dw, b_conv_dw=b_conv_dw, conv_ln_g=conv_ln_g, conv_ln_b=conv_ln_b,
             w_conv_out=w_conv_out, hgrn_lb_logits=hgrn_lb_logits, hgrn_norm_g=hgrn_norm_g, w_hgrn_out=w_hgrn_out,
             w_out=w_out, ln1_g=ln1_g, ln1_b=ln1_b, w_ffn_in=w_ffn_in, w_ffn_dw=w_ffn_dw, b_ffn_dw=b_ffn_dw,
             w_ffn_out=w_ffn_out, ln2_g=ln2_g, ln2_b=ln2_b)
    m = dict(w_in=m_w_in, w_conv_dw=m_w_conv_dw, b_conv_dw=m_b_conv_dw, conv_ln_g=m_conv_ln_g, conv_ln_b=m_conv_ln_b,
             w_conv_out=m_w_conv_out, hgrn_lb_logits=m_hgrn_lb_logits, hgrn_norm_g=m_hgrn_norm_g,
             w_hgrn_out=m_w_hgrn_out, w_out=m_w_out, ln1_g=m_ln1_g, ln1_b=m_ln1_b, w_ffn_in=m_w_ffn_in,
             w_ffn_dw=m_w_ffn_dw, b_ffn_dw=m_b_ffn_dw, w_ffn_out=m_w_ffn_out, ln2_g=m_ln2_g, ln2_b=m_ln2_b)
    v = dict(w_in=v_w_in, w_conv_dw=v_w_conv_dw, b_conv_dw=v_b_conv_dw, conv_ln_g=v_conv_ln_g, conv_ln_b=v_conv_ln_b,
             w_conv_out=v_w_conv_out, hgrn_lb_logits=v_hgrn_lb_logits, hgrn_norm_g=v_hgrn_norm_g,
             w_hgrn_out=v_w_hgrn_out, w_out=v_w_out, ln1_g=v_ln1_g, ln1_b=v_ln1_b, w_ffn_in=v_w_ffn_in,
             w_ffn_dw=v_w_ffn_dw, b_ffn_dw=v_b_ffn_dw, w_ffn_out=v_w_ffn_out, ln2_g=v_ln2_g, ln2_b=v_ln2_b)
    big_names = [n for n, _, _, _ in BIG]
    w2 = {n: a[0] if a.ndim == 3 else a for n, a in w.items()}
    m2 = {n: a[0] if a.ndim == 3 else a for n, a in m.items()}
    v2 = {n: a[0] if a.ndim == 3 else a for n, a in v.items()}

    sel = jnp.stack([2 * lax.axis_index("x") + lax.axis_index("y"), lax.axis_index("c")]).astype(jnp.int32)
    sharded = lambda ax, p, n: ("cols", p, n) if ax == 1 else ("rows", p, n)
    across = lambda ax, p, n: ("rows", p, n) if ax == 1 else ("cols", p, n)

    placed = [_windowed(lambda a: (a,), sel, [(w2[n], None)], [((r, c), BF16, sharded(ax, 0, N_CHIPS))],
                        window=w2[n].shape, name="cast_" + n)[0] for n, r, c, ax in BIG]
    wi, wcd4, wfd4 = _gather_weights(placed[:N_EARLY], [w2["w_conv_dw"], w2["w_ffn_dw"]])
    wcd = jnp.transpose(wcd4, (1, 0, 2)).reshape(CONV_KERNEL, CONV_DIM)
    wfd = jnp.transpose(wfd4, (1, 0, 2)).reshape(FFN_KERNEL, D_FF)
    ssem, rsem, *in_flight = _gather_late_start(placed[N_EARLY:], wi)
    wi = in_flight.pop()

    loss_part, grad_x, big_grads, small_grads = _local_step(
        x[0], loss_target[0], wi, lambda after: _gather_late_wait(ssem, rsem, in_flight, after),
        wcd, w2["b_conv_dw"], w2["conv_ln_g"], w2["conv_ln_b"],
        w2["hgrn_lb_logits"], w2["hgrn_norm_g"], w2["ln1_g"], w2["ln1_b"], wfd, w2["b_ffn_dw"],
        w2["ln2_g"], w2["ln2_b"])
    loss = lax.psum(loss_part[0, 0], ("x", "y", "c"))

    got = _sibling_exchange(big_grads)
    chip_sums = [_windowed(lambda a, b: (a + b,), sel, [(g_, across(ax, 1, 2)), (h_, None)],
                           [(_half_shape(r, c, ax), BF16, None)], window=_half_shape(r, c, ax),
                           name="chip_sum_" + n)[0]
                 for (n, r, c, ax), g_, h_ in zip(BIG, big_grads, got)]
    recv = _chip_exchange(chip_sums)
    add4 = lambda a, b0, b1, b2: (a.astype(F32) + b0.astype(F32) + b1.astype(F32) + b2.astype(F32),)
    half_filled = [_windowed(add4, sel, [(cs_, sharded(ax, 0, N_CHIPS))] + [(r_, None) for r_ in recv_],
                             [((r, c // N_CHIPS) if ax == 1 else (r // N_CHIPS, c), F32, across(ax, 1, 2))],
                             window=_piece_shape(r, c, ax), name="shard_sum_" + n)[0]
                   for (n, r, c, ax), cs_, recv_ in zip(BIG, chip_sums, recv)]
    shard_grads = dict(zip(big_names, _sibling_assemble(half_filled)))

    small_shapes = [small_grads[n].shape for n in SMALL_ORDER]
    reduced = dict(zip(SMALL_ORDER, _unpack(_all_reduce_small(_pack([small_grads[n] for n in SMALL_ORDER])),
                                            small_shapes)))
    shard = 2 * lax.axis_index("x") + lax.axis_index("y")
    grads = dict(shard_grads)
    for n in REPLICATED_SMALL:
        grads[n] = reduced[n]
    grads["w_conv_dw"] = lax.dynamic_slice_in_dim(reduced["w_conv_dw"], shard * (CONV_DIM // N_CHIPS),
                                                  CONV_DIM // N_CHIPS, axis=1)
    grads["w_ffn_dw"] = lax.dynamic_slice_in_dim(reduced["w_ffn_dw"], shard * (D_FF // N_CHIPS),
                                                 D_FF // N_CHIPS, axis=1)

    delta, new_m, new_v = {}, {}, {}
    for n in big_names + ["w_conv_dw", "w_ffn_dw"]:
        delta[n], new_m[n], new_v[n] = _adamw(w2[n], grads[n], m2[n], v2[n], name="adamw_" + n)
    rep_shapes = [w2[n].shape for n in REPLICATED_SMALL]
    packed = _adamw(*[_pack([src[n] for n in REPLICATED_SMALL]) for src in (w2, grads, m2, v2)], name="adamw_small")
    for dst, pk in zip((delta, new_m, new_v), packed):
        for n, a in zip(REPLICATED_SMALL, _unpack(pk, rep_shapes)):
            dst[n] = a

    def shaped(d):
        return [d[n].reshape(w[n].shape) for n in WEIGHT_ORDER]

    return (loss, grad_x[None], *shaped(grads), *shaped(delta), *shaped(new_m), *shaped(new_v))
```

```python
import jax
import jax.numpy as jnp
from jax import lax
from jax.experimental import pallas as pl
from jax.experimental.pallas import tpu as pltpu

F32 = jnp.float32
BF16 = jnp.bfloat16

D_MODEL = 1024
CONV_DIM = 512
CONV_KERNEL = 31
HGRN_DIM = 1024
HGRN_HEADS = 8
HEAD_DIM = 128
CHUNK = 64
SUB = 16
N_SUB = CHUNK // SUB
D_FF = 2816
FFN_KERNEL = 3
IN_COLS = 7168
LN_EPS = 1e-5
RMS_EPS = 1e-6
ALPHA = 2.0 ** 0.25
GELU_C = 0.7978845608028654
GELU_A = 0.044715

ADAM_LR = 0.001
ADAM_B1 = 0.9
ADAM_B2 = 0.999
ADAM_EPS = 1e-08
ADAM_WD = 0.01
ADAM_STEP = 10
ADAM_BC1 = 1.0 - ADAM_B1 ** ADAM_STEP
ADAM_BC2 = 1.0 - ADAM_B2 ** ADAM_STEP

N_CHIPS = 4
N_DEV = 8
ROW_BLOCK = 32
CONV_HALO = 32
FFN_HALO = 8
MESH = pl.DeviceIdType.MESH
ANY = pl.BlockSpec(memory_space=pl.ANY)


def _dot(a, b):
    return jnp.dot(a, b, preferred_element_type=F32)


def _dot_nt(a, b):
    return lax.dot_general(a, b, (((1,), (1,)), ((), ())), preferred_element_type=F32)


def _dot_tn(a, b):
    return lax.dot_general(a, b, (((0,), (0,)), ((), ())), preferred_element_type=F32)


def _sigmoid(z):
    return jax.nn.sigmoid(z)


def _silu_grad(z, s):
    return s * (1.0 + z * (1.0 - s))


def _gelu_and_grad(u):
    u2 = u * u
    th = jnp.tanh(u * (GELU_C + (GELU_C * GELU_A) * u2))
    half = 0.5 + 0.5 * th
    g = u * half
    dg = half + (0.5 * u) * (1.0 - th * th) * (GELU_C + (3.0 * GELU_C * GELU_A) * u2)
    return g, dg


def _ln_stats(r):
    mu = jnp.mean(r, axis=-1, keepdims=True)
    xc = r - mu
    var = jnp.mean(xc * xc, axis=-1, keepdims=True)
    rstd = lax.rsqrt(var + LN_EPS)
    return xc * rstd, rstd


def _ln_bwd(dy, xhat, rstd, g):
    dxh = dy * g
    m1 = jnp.mean(dxh, axis=-1, keepdims=True)
    m2 = jnp.mean(dxh * xhat, axis=-1, keepdims=True)
    return rstd * (dxh - m1 - xhat * m2)


def _fold8(x):
    acc = x[0:8, :]
    for r in range(8, x.shape[0], 8):
        acc = acc + x[r:r + 8, :]
    return acc


def _in_hbm(a):
    return pltpu.with_memory_space_constraint(a, pltpu.HBM)


def _hbm(shape, dtype):
    return pltpu.HBM(shape, dtype)


def _out_hbm(out_shape):
    if isinstance(out_shape, (list, tuple)):
        return [_hbm(s.shape, s.dtype) for s in out_shape]
    return _hbm(out_shape.shape, out_shape.dtype)


def _call(body, *, name, grid, in_specs, out_specs, out_shape, scratch=(), vmem_mb=32, aliases=None):
    call = pl.pallas_call(
        body, name=name, grid=grid, in_specs=in_specs, out_specs=out_specs, out_shape=_out_hbm(out_shape),
        scratch_shapes=list(scratch), input_output_aliases=aliases or {},
        compiler_params=pltpu.CompilerParams(
            dimension_semantics=("arbitrary",) * len(grid), vmem_limit_bytes=vmem_mb * 2 ** 20))
    return lambda *args: call(*[_in_hbm(a) for a in args])


def _sds(shape, dtype):
    return jax.ShapeDtypeStruct(shape, dtype)


def _proj(x, w):
    t = x.shape[0]
    tm, tn = min(t, 1024), 1024

    def body(x_ref, w_ref, p_ref, xb_ref):
        @pl.when(pl.program_id(1) == 0)
        def _():
            xb_ref[...] = x_ref[...].astype(BF16)
        p_ref[...] = _dot(xb_ref[...], w_ref[...])

    return _call(
        body, name="proj", grid=(t // tm, IN_COLS // tn),
        in_specs=[pl.BlockSpec((tm, D_MODEL), lambda i, j: (i, 0)),
                  pl.BlockSpec((D_MODEL, tn), lambda i, j: (0, j))],
        out_specs=[pl.BlockSpec((tm, tn), lambda i, j: (i, j)),
                   pl.BlockSpec((tm, D_MODEL), lambda i, j: (i, 0))],
        out_shape=[_sds((t, IN_COLS), F32), _sds((t, D_MODEL), BF16)], vmem_mb=48)(x, w)


def _mm_nn(a, w, *, tm, tn, name, vmem_mb=32):
    t, k = a.shape
    n = w.shape[1]
    tm = min(tm, t)

    def body(a_ref, w_ref, o_ref):
        o_ref[...] = _dot(a_ref[...], w_ref[...])

    return _call(
        body, name=name, grid=(t // tm, n // tn),
        in_specs=[pl.BlockSpec((tm, k), lambda i, j: (i, 0)), pl.BlockSpec((k, tn), lambda i, j: (0, j))],
        out_specs=pl.BlockSpec((tm, tn), lambda i, j: (i, j)),
        out_shape=_sds((t, n), F32), vmem_mb=vmem_mb)(a, w)


def _views(*arrs):
    out = []
    for a in arrs:
        if a.ndim == 2:
            out.append((a, None))
        else:
            out.extend((a, p) for p in range(a.shape[0]))
    return out


def _piece_layout(views, tile):
    starts, counts, total = [], [], 0
    for arr, _ in views:
        width = arr.shape[-1]
        assert width % tile == 0
        starts.append(total)
        counts.append(width // tile)
        total += width // tile
    return starts, counts, total


def _mm_tn(a, views, *, tn, name, tt=512, vmem_mb=32):
    t, m = a.shape
    tt = min(tt, t)
    starts, counts, nj = _piece_layout(views, tn)
    n_views = len(views)

    def body(a_ref, *refs):
        b_refs, o_ref = refs[:n_views], refs[n_views]
        j = pl.program_id(0)

        @pl.when(pl.program_id(1) == 0)
        def _():
            o_ref[...] = jnp.zeros_like(o_ref)

        for b_ref, st, nb, (_, p) in zip(b_refs, starts, counts, views):
            @pl.when((j >= st) & (j < st + nb))
            def _(b_ref=b_ref, p=p):
                blk = b_ref[...] if p is None else b_ref[0]
                o_ref[...] += _dot_tn(a_ref[...], blk)

    def b_spec(st, nb, p):
        def rows(j, k):
            return jnp.where((j >= st) & (j < st + nb), k, 0)

        def cols(j):
            return jnp.clip(j - st, 0, nb - 1)

        if p is None:
            return pl.BlockSpec((tt, tn), lambda j, k: (rows(j, k), cols(j)))
        return pl.BlockSpec((1, tt, tn), lambda j, k: (p, rows(j, k), cols(j)))

    return _call(
        body, name=name, grid=(nj, t // tt),
        in_specs=[pl.BlockSpec((tt, m), lambda j, k: (k, 0))]
        + [b_spec(st, nb, p) for st, nb, (_, p) in zip(starts, counts, views)],
        out_specs=pl.BlockSpec((m, tn), lambda j, k: (0, j)),
        out_shape=_sds((m, nj * tn), F32), vmem_mb=vmem_mb)(a, *[arr for arr, _ in views])


DPROJ_SLABS = 7
DPROJ_GATE_SLAB = 4
DPROJ_CONV_SLAB = 6


def _slab_cols(s):
    return (s + 1) % DPROJ_SLABS


def _grad_w_in(xb, dstack):
    t = xb.shape[0]
    tt = min(t, 1024)

    def body(a_ref, b_ref, o_ref):
        @pl.when(pl.program_id(1) == 0)
        def _():
            o_ref[...] = jnp.zeros_like(o_ref)

        o_ref[...] += _dot_tn(a_ref[...], b_ref[0])

    return _call(
        body, name="grad_w_in", grid=(DPROJ_SLABS, t // tt),
        in_specs=[pl.BlockSpec((tt, D_MODEL), lambda j, k: (k, 0)),
                  pl.BlockSpec((1, tt, D_MODEL), lambda j, k: (j, k, 0))],
        out_specs=pl.BlockSpec((D_MODEL, D_MODEL), lambda j, k: (0, _slab_cols(j))),
        out_shape=_sds((D_MODEL, IN_COLS), F32), vmem_mb=48)(xb, dstack)


def _grad_x(dstack, wi, dr1):
    t = dr1.shape[0]
    tm = min(t, 1024)

    def body(add_ref, b_ref, w_ref, o_ref):
        @pl.when(pl.program_id(1) == 0)
        def _():
            o_ref[...] = ALPHA * add_ref[...]

        o_ref[...] += _dot_nt(b_ref[0], w_ref[...])

    return _call(
        body, name="grad_x", grid=(t // tm, DPROJ_SLABS),
        in_specs=[pl.BlockSpec((tm, D_MODEL), lambda i, k: (i, 0)),
                  pl.BlockSpec((1, tm, D_MODEL), lambda i, k: (k, i, 0)),
                  pl.BlockSpec((D_MODEL, D_MODEL), lambda i, k: (0, _slab_cols(k)))],
        out_specs=pl.BlockSpec((tm, D_MODEL), lambda i, k: (i, 0)),
        out_shape=_sds((t, D_MODEL), F32), vmem_mb=48)(dr1, dstack, wi)


LANES = 128
CONV_FWD_OFFSETS = {k: 2 + k for k in range(CONV_KERNEL)}
CONV_BWD_OFFSETS = {k: CONV_KERNEL - 1 - k for k in range(CONV_KERNEL)}


def _conv_taps(win, offsets):
    n = win.shape[0]
    for b in range(8):
        taps = [k for k, o in offsets.items() if o % 8 == b]
        if not taps:
            continue
        shifted = win if b == 0 else pltpu.roll(win, n - b, 0)
        for k in taps:
            first = offsets[k] - b
            yield k, shifted[first:first + ROW_BLOCK, :]


def _conv_fwd(proj, wcd, bcd, lng, lnb):
    t = proj.shape[0]
    tm = 512

    def body(cv_ref, cg_ref, w_ref, b_ref, g_ref, be_ref, cc_ref, cs_ref, ext_ref):
        i = pl.program_id(0)

        @pl.when(i == 0)
        def _():
            ext_ref[0:CONV_HALO, :] = jnp.zeros((CONV_HALO, CONV_DIM), F32)

        @pl.when(i > 0)
        def _():
            ext_ref[0:CONV_HALO, :] = ext_ref[tm:tm + CONV_HALO, :]

        ext_ref[CONV_HALO:CONV_HALO + tm, :] = cv_ref[...] * _sigmoid(cg_ref[...])

        def block(r, carry):
            r0 = pl.multiple_of(r * ROW_BLOCK, ROW_BLOCK)
            groups = []
            for g in range(CONV_DIM // LANES):
                lanes = slice(g * LANES, (g + 1) * LANES)
                win = ext_ref[pl.ds(r0, ROW_BLOCK + CONV_HALO), lanes]
                acc = jnp.broadcast_to(b_ref[:, lanes], (ROW_BLOCK, LANES))
                for k, rows_k in _conv_taps(win, CONV_FWD_OFFSETS):
                    acc = acc + w_ref[k:k + 1, lanes] * rows_k
                groups.append(acc)
            acc = jnp.concatenate(groups, axis=1)
            cc_ref[pl.ds(r0, ROW_BLOCK), :] = acc
            xhat, _ = _ln_stats(acc)
            a = xhat * g_ref[...] + be_ref[...]
            cs_ref[pl.ds(r0, ROW_BLOCK), :] = (a * _sigmoid(a)).astype(BF16)
            return carry

        lax.fori_loop(0, tm // ROW_BLOCK, block, 0)

    vec = pl.BlockSpec((1, CONV_DIM), lambda i: (0, 0))
    return _call(
        body, name="conv_fwd", grid=(t // tm,),
        in_specs=[pl.BlockSpec((tm, CONV_DIM), lambda i: (i, 0)), pl.BlockSpec((tm, CONV_DIM), lambda i: (i, 1)),
                  pl.BlockSpec((CONV_KERNEL, CONV_DIM), lambda i: (0, 0)), vec, vec, vec],
        out_specs=[pl.BlockSpec((tm, CONV_DIM), lambda i: (i, 0)), pl.BlockSpec((tm, CONV_DIM), lambda i: (i, 0))],
        out_shape=[_sds((t, CONV_DIM), F32), _sds((t, CONV_DIM), BF16)],
        scratch=[pltpu.VMEM((tm + CONV_HALO, CONV_DIM), F32)])(proj, proj, wcd, bcd, lng, lnb)


def _conv_bwd_a(dcs, cc, proj, lng, lnb):
    t = proj.shape[0]
    tm = 512
    nt = t // tm

    def body(dcs_ref, cc_ref, cv_ref, cg_ref, g_ref, be_ref,
             dcc_ref, dw_ref, db_ref, dg_ref, dbe_ref, ext_ref, accw_ref, acc3_ref):
        i = pl.program_id(0)

        @pl.when(i == 0)
        def _():
            ext_ref[0:CONV_HALO, :] = jnp.zeros((CONV_HALO, CONV_DIM), F32)
            accw_ref[...] = jnp.zeros_like(accw_ref)
            acc3_ref[...] = jnp.zeros_like(acc3_ref)

        @pl.when(i > 0)
        def _():
            ext_ref[0:CONV_HALO, :] = ext_ref[tm:tm + CONV_HALO, :]

        ext_ref[CONV_HALO:CONV_HALO + tm, :] = cv_ref[...] * _sigmoid(cg_ref[...])

        def block(r, carry):
            r0 = pl.multiple_of(r * ROW_BLOCK, ROW_BLOCK)
            rows = pl.ds(r0, ROW_BLOCK)
            xhat, rstd = _ln_stats(cc_ref[rows, :])
            a = xhat * g_ref[...] + be_ref[...]
            sg = _sigmoid(a)
            da = dcs_ref[rows, :] * _silu_grad(a, sg)
            acc3_ref[8:16, :] += _fold8(da * xhat)
            acc3_ref[16:24, :] += _fold8(da)
            dcc = _ln_bwd(da, xhat, rstd, g_ref[...])
            dcc_ref[rows, :] = dcc
            acc3_ref[0:8, :] += _fold8(dcc)
            for g in range(CONV_DIM // LANES):
                lanes = slice(g * LANES, (g + 1) * LANES)
                win = ext_ref[pl.ds(r0, ROW_BLOCK + CONV_HALO), lanes]
                dcc_g = dcc[:, lanes]
                for k, rows_k in _conv_taps(win, CONV_FWD_OFFSETS):
                    accw_ref[8 * k:8 * k + 8, lanes] += _fold8(dcc_g * rows_k)
            return carry

        lax.fori_loop(0, tm // ROW_BLOCK, block, 0)

        @pl.when(i == nt - 1)
        def _():
            for k in range(CONV_KERNEL):
                dw_ref[k:k + 1, :] = jnp.sum(accw_ref[8 * k:8 * k + 8, :], axis=0, keepdims=True)
            db_ref[...] = jnp.sum(acc3_ref[0:8, :], axis=0, keepdims=True)
            dg_ref[...] = jnp.sum(acc3_ref[8:16, :], axis=0, keepdims=True)
            dbe_ref[...] = jnp.sum(acc3_ref[16:24, :], axis=0, keepdims=True)

    vec = pl.BlockSpec((1, CONV_DIM), lambda i: (0, 0))
    tile = pl.BlockSpec((tm, CONV_DIM), lambda i: (i, 0))
    return _call(
        body, name="conv_bwd_a", grid=(nt,),
        in_specs=[tile, tile, tile, pl.BlockSpec((tm, CONV_DIM), lambda i: (i, 1)), vec, vec],
        out_specs=[tile, pl.BlockSpec((CONV_KERNEL, CONV_DIM), lambda i: (0, 0)), vec, vec, vec],
        out_shape=[_sds((t, CONV_DIM), F32), _sds((CONV_KERNEL, CONV_DIM), F32),
                   _sds((1, CONV_DIM), F32), _sds((1, CONV_DIM), F32), _sds((1, CONV_DIM), F32)],
        scratch=[pltpu.VMEM((tm + CONV_HALO, CONV_DIM), F32),
                 pltpu.VMEM((8 * CONV_KERNEL, CONV_DIM), F32),
                 pltpu.VMEM((24, CONV_DIM), F32)])(dcs, cc, proj, proj, lng, lnb)


def _conv_bwd_b(dcc, proj, wcd, dstack):
    t = proj.shape[0]
    tm = 512
    nt = t // tm

    def body(dcc_ref, cv_ref, cg_ref, w_ref, stack_ref, out_ref, ext_ref):
        del stack_ref
        i = pl.program_id(0)

        @pl.when(i == 0)
        def _():
            ext_ref[tm:tm + CONV_HALO, :] = jnp.zeros((CONV_HALO, CONV_DIM), F32)

        @pl.when(i > 0)
        def _():
            ext_ref[tm:tm + CONV_HALO, :] = ext_ref[0:CONV_HALO, :]

        ext_ref[0:tm, :] = dcc_ref[...]

        def block(r, carry):
            r0 = pl.multiple_of(r * ROW_BLOCK, ROW_BLOCK)
            rows = pl.ds(r0, ROW_BLOCK)
            for g in range(CONV_DIM // LANES):
                lanes = slice(g * LANES, (g + 1) * LANES)
                gate_lanes = slice(CONV_DIM + g * LANES, CONV_DIM + (g + 1) * LANES)
                win = ext_ref[pl.ds(r0, ROW_BLOCK + CONV_HALO), lanes]
                acc = jnp.zeros((ROW_BLOCK, LANES), F32)
                for k, rows_k in _conv_taps(win, CONV_BWD_OFFSETS):
                    acc = acc + w_ref[k:k + 1, lanes] * rows_k
                sg = _sigmoid(cg_ref[rows, lanes])
                out_ref[0, rows, lanes] = (acc * sg).astype(BF16)
                out_ref[0, rows, gate_lanes] = (acc * cv_ref[rows, lanes] * sg * (1.0 - sg)).astype(BF16)
            return carry

        lax.fori_loop(0, tm // ROW_BLOCK, block, 0)

    rev = lambda i: (nt - 1 - i, 0)
    return _call(
        body, name="conv_bwd_b", grid=(nt,),
        in_specs=[pl.BlockSpec((tm, CONV_DIM), rev), pl.BlockSpec((tm, CONV_DIM), rev),
                  pl.BlockSpec((tm, CONV_DIM), lambda i: (nt - 1 - i, 1)),
                  pl.BlockSpec((CONV_KERNEL, CONV_DIM), lambda i: (0, 0)), ANY],
        out_specs=pl.BlockSpec((1, tm, 2 * CONV_DIM), lambda i: (DPROJ_CONV_SLAB, nt - 1 - i, 0)),
        out_shape=_sds(dstack.shape, BF16), aliases={4: 0},
        scratch=[pltpu.VMEM((tm + CONV_HALO, CONV_DIM), F32)])(dcc, proj, proj, wcd, dstack)


def _lower_bound(lg_ref):
    a0, a1 = lg_ref[0:1, :], lg_ref[1:2, :]
    m = jnp.maximum(a0, a1)
    e0, e1 = jnp.exp(a0 - m), jnp.exp(a1 - m)
    return e0 / (e0 + e1)


def _block_tri(n, upper):
    r = lax.broadcasted_iota(jnp.int32, (n, n), 0)
    c = lax.broadcasted_iota(jnp.int32, (n, n), 1)
    same = (r >> 6) == (c >> 6)
    tri = (c >= r) if upper else (c <= r)
    return jnp.where(same & tri, 1.0, 0.0).astype(BF16)


def _block_cumsum(x, tri):
    w = x.shape[1]
    hi = x.astype(BF16)
    r1 = x - hi.astype(F32)
    mid = r1.astype(BF16)
    lo = (r1 - mid.astype(F32)).astype(BF16)
    y = _dot(tri, jnp.concatenate([hi, mid, lo], axis=1))
    return y[:, 0:w] + y[:, w:2 * w] + y[:, 2 * w:3 * w]


def _first_step():
    return (pl.program_id(0) == 0) & (pl.program_id(1) == 0)


def _block_rows(i):
    return slice(SUB * i, SUB * (i + 1))


def _chunk_terms(qc, kc, bc, b_ref, first_row):
    betas = [jnp.zeros((1, HEAD_DIM), F32)]
    betas += [b_ref[first_row + SUB * i - 1:first_row + SUB * i, :] for i in range(1, N_SUB)]
    b_last = b_ref[first_row + CHUNK - 1:first_row + CHUNK, :]
    zeros = lambda rows: jnp.zeros((rows, HEAD_DIM), F32)
    qscale = [jnp.exp(bc[_block_rows(i), :] - betas[i]) for i in range(N_SUB)]
    qs = [qc[_block_rows(i), :] * qscale[i] for i in range(N_SUB)]
    kscale = [jnp.exp(betas[n] - bc[0:SUB * (n + 1), :]) for n in range(N_SUB)]
    ks = [kc[0:SUB * (n + 1), :] * kscale[n] for n in range(N_SUB)]

    def tall(parts):
        parts = [p for p in parts if p.shape[0]]
        return parts[0] if len(parts) == 1 else jnp.concatenate(parts, axis=0)

    qcat = jnp.concatenate([tall([zeros(SUB * n), qs[n], zeros(CHUNK - SUB * (n + 1))]) for n in range(N_SUB)],
                           axis=1)
    kcat = jnp.concatenate([tall([ks[n], zeros(CHUNK - SUB * (n + 1))]) for n in range(N_SUB)], axis=1)
    return dict(qscale=qscale, qcat=qcat, kscale=kscale, kcat=kcat,
                eb=jnp.exp(bc), e_last=jnp.exp(b_last), ktscale=jnp.exp(b_last - bc))


def _causal(shape_rows_first):
    r = lax.broadcasted_iota(jnp.int32, (CHUNK, CHUNK), 0)
    c = lax.broadcasted_iota(jnp.int32, (CHUNK, CHUNK), 1)
    return (c <= r) if shape_rows_first else (r <= c)


def _hgrn_specs(tm, tile_of):
    col = lambda base: (lambda h, i: (tile_of(i), base + h))
    return [pl.BlockSpec((tm, HEAD_DIM), col(8)), pl.BlockSpec((tm, HEAD_DIM), col(16)),
            pl.BlockSpec((tm, HEAD_DIM), col(24)), pl.BlockSpec((tm, HEAD_DIM), col(32)),
            pl.BlockSpec((2, HEAD_DIM), lambda h, i: (0, h)), pl.BlockSpec((1, HEAD_DIM), lambda h, i: (0, h))]


def _hgrn_fwd(proj, logits, ng):
    t = proj.shape[0]
    tm = 512
    nc = tm // CHUNK
    nt = t // tm

    def body(zq_ref, zf_ref, v_ref, zg_ref, lg_ref, ng_ref, o_ref, og_ref, st_ref,
             s_scr, q_scr, k_scr, b_scr, tri_scr):
        @pl.when(_first_step())
        def _():
            tri_scr[...] = _block_tri(tm, upper=False)

        @pl.when(pl.program_id(1) == 0)
        def _():
            s_scr[...] = jnp.zeros_like(s_scr)

        lb = _lower_bound(lg_ref)
        zf = zf_ref[...]
        f = lb + (1.0 - lb) * _sigmoid(zf)
        k_scr[...] = (1.0 - lb) * _sigmoid(-zf)
        zq = zq_ref[...]
        q_scr[...] = zq * _sigmoid(zq)
        b_scr[...] = _block_cumsum(jnp.log(f), tri_scr[...])

        st = s_scr[...]
        for c in range(nc):
            rows = pl.ds(c * CHUNK, CHUNK)
            qc, kc, bc, vc = q_scr[rows, :], k_scr[rows, :], b_scr[rows, :], v_ref[rows, :]
            st_ref[0, c] = st
            tr = _chunk_terms(qc, kc, bc, b_scr, c * CHUNK)
            a = jnp.where(_causal(True), _dot_nt(tr["qcat"].astype(BF16), tr["kcat"].astype(BF16)), 0.0)
            vb = vc.astype(BF16)
            o_ref[rows, :] = _dot(a.astype(BF16), vb) + _dot_nt((qc * tr["eb"]).astype(BF16), st.astype(BF16))
            st = st * tr["e_last"] + _dot_tn(vb, (kc * tr["ktscale"]).astype(BF16))
        s_scr[...] = st

        o = o_ref[...]
        rinv = lax.rsqrt(jnp.mean(o * o, axis=-1, keepdims=True) + RMS_EPS)
        zg = zg_ref[...]
        og_ref[...] = (o * rinv * ng_ref[...] * (zg * _sigmoid(zg))).astype(BF16)

    tile = pl.BlockSpec((tm, HEAD_DIM), lambda h, i: (i, h))
    return _call(
        body, name="hgrn_fwd", grid=(HGRN_HEADS, nt),
        in_specs=_hgrn_specs(tm, lambda i: i),
        out_specs=[tile, tile, pl.BlockSpec((1, nc, HEAD_DIM, HEAD_DIM), lambda h, i: (h, i, 0, 0))],
        out_shape=[_sds((t, HGRN_DIM), F32), _sds((t, HGRN_DIM), BF16),
                   _sds((HGRN_HEADS, t // CHUNK, HEAD_DIM, HEAD_DIM), F32)],
        scratch=[pltpu.VMEM((HEAD_DIM, HEAD_DIM), F32)] + [pltpu.VMEM((tm, HEAD_DIM), F32)] * 3
        + [pltpu.VMEM((tm, tm), BF16)],
    )(proj, proj, proj, proj, logits, ng)


def _hgrn_bwd(dog, o, states, proj, logits, ng, dstack):
    t = proj.shape[0]
    tm = 512
    nc = tm // CHUNK
    nt = t // tm

    def body(dog_ref, o_ref, st_ref, zq_ref, zf_ref, v_ref, zg_ref, lg_ref, ng_ref, stack_ref,
             dp_ref, dlg_ref, dng_ref,
             ds_scr, q_scr, k_scr, b_scr, do_scr, dq_scr, dk_scr, dv_scr, db_scr, dlb_scr, tri_scr):
        i = pl.program_id(1)

        @pl.when(_first_step())
        def _():
            tri_scr[0] = _block_tri(tm, upper=False)
            tri_scr[1] = _block_tri(tm, upper=True)

        @pl.when(i == 0)
        def _():
            ds_scr[...] = jnp.zeros_like(ds_scr)
            dlb_scr[...] = jnp.zeros_like(dlb_scr)
            dng_ref[...] = jnp.zeros_like(dng_ref)

        lb = _lower_bound(lg_ref)
        ng_row = ng_ref[...]
        o = o_ref[...]
        rinv = lax.rsqrt(jnp.mean(o * o, axis=-1, keepdims=True) + RMS_EPS)
        ohat = o * rinv
        zg = zg_ref[...]
        sg = _sigmoid(zg)
        dog_v = dog_ref[...]
        don = dog_v * (zg * sg)
        dp_ref[3] = (dog_v * (ohat * ng_row) * _silu_grad(zg, sg)).astype(BF16)
        dng_ref[...] += jnp.sum(don * ohat, axis=0, keepdims=True)
        dohat = don * ng_row
        do_scr[...] = rinv * (dohat - ohat * jnp.mean(dohat * ohat, axis=-1, keepdims=True))

        zf = zf_ref[...]
        s = _sigmoid(zf)
        s_neg = _sigmoid(-zf)
        f = lb + (1.0 - lb) * s
        k_scr[...] = (1.0 - lb) * s_neg
        zq = zq_ref[...]
        sq = _sigmoid(zq)
        q_scr[...] = zq * sq
        b_scr[...] = _block_cumsum(jnp.log(f), tri_scr[0])

        dst = ds_scr[...]
        for c in reversed(range(nc)):
            rows = pl.ds(c * CHUNK, CHUNK)
            qc, kc, bc, vc, doc = q_scr[rows, :], k_scr[rows, :], b_scr[rows, :], v_ref[rows, :], do_scr[rows, :]
            st = st_ref[0, c]
            tr = _chunk_terms(qc, kc, bc, b_scr, c * CHUNK)
            qcb, kcb = tr["qcat"].astype(BF16), tr["kcat"].astype(BF16)
            dob, vb, dstb = doc.astype(BF16), vc.astype(BF16), dst.astype(BF16)
            a_t = jnp.where(_causal(False), _dot_nt(kcb, qcb), 0.0)
            da = jnp.where(_causal(True), _dot_nt(dob, vb), 0.0)
            da_t = jnp.where(_causal(False), _dot_nt(vb, dob), 0.0)
            dqcat = _dot(da.astype(BF16), kcb)
            dkcat = _dot(da_t.astype(BF16), qcb)
            kt = kc * tr["ktscale"]
            dv_scr[rows, :] = _dot(a_t.astype(BF16), dob) + _dot_nt(kt.astype(BF16), dstb)
            dq_blocks, dk_blocks, db_blocks = [], [], []
            for j in range(N_SUB):
                rows_j = _block_rows(j)
                lanes_j = slice(j * HEAD_DIM, (j + 1) * HEAD_DIM)
                dq_j = dqcat[rows_j, lanes_j]
                dq_blocks.append(dq_j * tr["qscale"][j])
                db_j = qcb[rows_j, lanes_j].astype(F32) * dq_j
                dk_j = jnp.zeros((SUB, HEAD_DIM), F32)
                for n in range(j, N_SUB):
                    lanes_n = slice(n * HEAD_DIM, (n + 1) * HEAD_DIM)
                    dk_jn = dkcat[rows_j, lanes_n]
                    dk_j = dk_j + dk_jn * tr["kscale"][n][rows_j, :]
                    db_j = db_j - kcb[rows_j, lanes_n].astype(F32) * dk_jn
                dk_blocks.append(dk_j)
                db_blocks.append(db_j)
            dq_inter = _dot(dob, st.astype(BF16)) * tr["eb"]
            dkt = _dot(vb, dstb)
            dk_inter = dkt * tr["ktscale"]
            extra = (jnp.sum(dkt * kt, axis=0, keepdims=True)
                     + tr["e_last"] * jnp.sum(dst * st, axis=0, keepdims=True))
            dq_scr[rows, :] = jnp.concatenate(dq_blocks, axis=0) + dq_inter
            dk_scr[rows, :] = jnp.concatenate(dk_blocks, axis=0) + dk_inter
            db_scr[rows, :] = jnp.concatenate(db_blocks, axis=0) + qc * dq_inter - kc * dk_inter
            last = c * CHUNK + CHUNK - 1
            db_scr[last:last + 1, :] += extra
            dst = dst * tr["e_last"] + _dot_tn(dob, (qc * tr["eb"]).astype(BF16))
        ds_scr[...] = dst

        dlogf = _block_cumsum(db_scr[...], tri_scr[1])
        df = dlogf / f - dk_scr[...]
        dp_ref[0] = (dq_scr[...] * _silu_grad(zq, sq)).astype(BF16)
        dp_ref[1] = (df * (1.0 - lb) * s * (1.0 - s)).astype(BF16)
        dp_ref[2] = dv_scr[...].astype(BF16)
        dlb_scr[...] += jnp.sum(df * s_neg, axis=0, keepdims=True)

        @pl.when(i == nt - 1)
        def _():
            dlogit = dlb_scr[...] * lb * (1.0 - lb)
            dlg_ref[0:1, :] = dlogit
            dlg_ref[1:2, :] = -dlogit

    rev = lambda i: nt - 1 - i
    tile = pl.BlockSpec((tm, HEAD_DIM), lambda h, i: (rev(i), h))
    return _call(
        body, name="hgrn_bwd", grid=(HGRN_HEADS, nt),
        in_specs=[tile, tile, pl.BlockSpec((1, nc, HEAD_DIM, HEAD_DIM), lambda h, i: (h, rev(i), 0, 0))]
        + _hgrn_specs(tm, rev) + [ANY],
        out_specs=[pl.BlockSpec((4, tm, HEAD_DIM), lambda h, i: (0, rev(i), h)),
                   pl.BlockSpec((2, HEAD_DIM), lambda h, i: (0, h)),
                   pl.BlockSpec((1, HEAD_DIM), lambda h, i: (0, h))],
        out_shape=[_sds(dstack.shape, BF16), _sds((2, HGRN_DIM), F32), _sds((1, HGRN_DIM), F32)],
        aliases={9: 0},
        scratch=[pltpu.VMEM((HEAD_DIM, HEAD_DIM), F32)] + [pltpu.VMEM((tm, HEAD_DIM), F32)] * 8
        + [pltpu.VMEM((1, HEAD_DIM), F32), pltpu.VMEM((2, tm, tm), BF16)],
    )(dog, o, states, proj, proj, proj, proj, logits, ng, dstack)


def _merge_fwd(cs, og, proj, x, wco, wh, wo, g1, b1):
    t = x.shape[0]
    tm = 256

    def body(cs_ref, og_ref, m0_ref, m1_ref, x_ref, wco_ref, wh_ref, wo_ref, g_ref, b_ref,
             y_ref, mixed_ref, r1_ref, x1_ref, x1b_ref):
        yc = _dot(cs_ref[...], wco_ref[...])
        yh = _dot(og_ref[...], wh_ref[...])
        y_ref[0] = yc
        y_ref[1] = yh
        mixed = (_sigmoid(m0_ref[...]) * yc + _sigmoid(m1_ref[...]) * yh).astype(BF16)
        mixed_ref[...] = mixed
        r1 = ALPHA * x_ref[...] + _dot(mixed, wo_ref[...])
        r1_ref[...] = r1
        xhat, _ = _ln_stats(r1)
        x1 = xhat * g_ref[...] + b_ref[...]
        x1_ref[...] = x1
        x1b_ref[...] = x1.astype(BF16)

    row = lambda w: pl.BlockSpec((tm, w), lambda i: (i, 0))
    full = lambda a: pl.BlockSpec(a.shape, lambda i: (0, 0))
    return _call(
        body, name="merge_fwd", grid=(t // tm,),
        in_specs=[row(CONV_DIM), row(HGRN_DIM),
                  pl.BlockSpec((tm, D_MODEL), lambda i: (i, 5)), pl.BlockSpec((tm, D_MODEL), lambda i: (i, 6)),
                  row(D_MODEL), full(wco), full(wh), full(wo), full(g1), full(b1)],
        out_specs=[pl.BlockSpec((2, tm, D_MODEL), lambda i: (0, i, 0)), row(D_MODEL), row(D_MODEL),
                   row(D_MODEL), row(D_MODEL)],
        out_shape=[_sds((2, t, D_MODEL), F32), _sds((t, D_MODEL), BF16), _sds((t, D_MODEL), F32),
                   _sds((t, D_MODEL), F32), _sds((t, D_MODEL), BF16)],
        vmem_mb=48)(cs, og, proj, proj, x, wco, wh, wo, g1, b1)


def _merge_bwd(dr1b, ycat, proj, wo, wco, wh):
    t = dr1b.shape[0]
    tm = 256

    def body(dr_ref, y_ref, m0_ref, m1_ref, wo_ref, wco_ref, wh_ref, dpm_ref, dy_ref, dcs_ref, dog_ref):
        dmixed = _dot_nt(dr_ref[...], wo_ref[...])
        g0 = _sigmoid(m0_ref[...])
        g1 = _sigmoid(m1_ref[...])
        dpm_ref[0] = (dmixed * y_ref[0] * g0 * (1.0 - g0)).astype(BF16)
        dpm_ref[1] = (dmixed * y_ref[1] * g1 * (1.0 - g1)).astype(BF16)
        dyc = (dmixed * g0).astype(BF16)
        dyh = (dmixed * g1).astype(BF16)
        dy_ref[0] = dyc
        dy_ref[1] = dyh
        dcs_ref[...] = _dot_nt(dyc, wco_ref[...])
        dog_ref[...] = _dot_nt(dyh, wh_ref[...])

    row = lambda w: pl.BlockSpec((tm, w), lambda i: (i, 0))
    pair = pl.BlockSpec((2, tm, D_MODEL), lambda i: (0, i, 0))
    full = lambda a: pl.BlockSpec(a.shape, lambda i: (0, 0))
    return _call(
        body, name="merge_bwd", grid=(t // tm,),
        in_specs=[row(D_MODEL), pair,
                  pl.BlockSpec((tm, D_MODEL), lambda i: (i, 5)), pl.BlockSpec((tm, D_MODEL), lambda i: (i, 6)),
                  full(wo), full(wco), full(wh)],
        out_specs=[pl.BlockSpec((2, tm, D_MODEL), lambda i: (DPROJ_GATE_SLAB // 2, i, 0)), pair,
                   row(CONV_DIM), row(HGRN_DIM)],
        out_shape=[_sds((DPROJ_SLABS, t, D_MODEL), BF16), _sds((2, t, D_MODEL), BF16),
                   _sds((t, CONV_DIM), F32), _sds((t, HGRN_DIM), F32)],
        vmem_mb=48)(dr1b, ycat, proj, proj, wo, wco, wh)


def _ffn_taps(win):
    return (pltpu.roll(win, 2, 0)[FFN_HALO:, :], pltpu.roll(win, 1, 0)[FFN_HALO:, :], win[FFN_HALO:, :])


def _ffn_conv3(taps, w_ref):
    return w_ref[0:1, :] * taps[0] + w_ref[1:2, :] * taps[1] + w_ref[2:3, :] * taps[2]


def _ffn_mid(z, wfd, bfd):
    t = z.shape[0]
    tm = 256

    def body(u_ref, gv_ref, w_ref, b_ref, h_ref, ext_ref):
        i = pl.program_id(0)

        @pl.when(i == 0)
        def _():
            ext_ref[0:FFN_HALO, :] = jnp.zeros((FFN_HALO, D_FF), F32)

        @pl.when(i > 0)
        def _():
            ext_ref[0:FFN_HALO, :] = ext_ref[tm:tm + FFN_HALO, :]

        ext_ref[FFN_HALO:FFN_HALO + tm, :] = u_ref[...]

        def block(r, carry):
            r0 = pl.multiple_of(r * ROW_BLOCK, ROW_BLOCK)
            rows = pl.ds(r0, ROW_BLOCK)
            win = ext_ref[pl.ds(r0, ROW_BLOCK + FFN_HALO), :]
            uc = _ffn_conv3(_ffn_taps(win), w_ref) + b_ref[...]
            g, _ = _gelu_and_grad(uc)
            h_ref[rows, :] = (g * gv_ref[rows, :]).astype(BF16)
            return carry

        lax.fori_loop(0, tm // ROW_BLOCK, block, 0)

    return _call(
        body, name="ffn_mid", grid=(t // tm,),
        in_specs=[pl.BlockSpec((tm, D_FF), lambda i: (i, 0)), pl.BlockSpec((tm, D_FF), lambda i: (i, 1)),
                  pl.BlockSpec((FFN_KERNEL, D_FF), lambda i: (0, 0)), pl.BlockSpec((1, D_FF), lambda i: (0, 0))],
        out_specs=pl.BlockSpec((tm, D_FF), lambda i: (i, 0)),
        out_shape=_sds((t, D_FF), BF16),
        scratch=[pltpu.VMEM((tm + FFN_HALO, D_FF), F32)], vmem_mb=40)(z, z, wfd, bfd)


def _ffn_out_loss(hmid, x1, target, wfo, g2, b2):
    t = x1.shape[0]
    tm = 256
    inv_n = 1.0 / D_MODEL

    def body(h_ref, x1_ref, tg_ref, w_ref, g_ref, b_ref, dr_ref, drb_ref, loss_ref, dg_ref, db_ref):
        @pl.when(pl.program_id(0) == 0)
        def _():
            loss_ref[...] = jnp.zeros_like(loss_ref)
            dg_ref[...] = jnp.zeros_like(dg_ref)
            db_ref[...] = jnp.zeros_like(db_ref)

        r2 = ALPHA * x1_ref[...] + _dot(h_ref[...], w_ref[...])
        xhat, rstd = _ln_stats(r2)
        err = xhat * g_ref[...] + b_ref[...] - tg_ref[...]
        loss_ref[...] += 0.5 * inv_n * jnp.sum(err * err)
        dy = err * inv_n
        dg_ref[...] += jnp.sum(dy * xhat, axis=0, keepdims=True)
        db_ref[...] += jnp.sum(dy, axis=0, keepdims=True)
        dr = _ln_bwd(dy, xhat, rstd, g_ref[...])
        dr_ref[...] = dr
        drb_ref[...] = dr.astype(BF16)

    row = lambda w: pl.BlockSpec((tm, w), lambda i: (i, 0))
    vec = pl.BlockSpec((1, D_MODEL), lambda i: (0, 0))
    return _call(
        body, name="ffn_out_loss", grid=(t // tm,),
        in_specs=[row(D_FF), row(D_MODEL), row(D_MODEL), pl.BlockSpec((D_FF, D_MODEL), lambda i: (0, 0)), vec, vec],
        out_specs=[row(D_MODEL), row(D_MODEL), pl.BlockSpec((1, 128), lambda i: (0, 0)), vec, vec],
        out_shape=[_sds((t, D_MODEL), F32), _sds((t, D_MODEL), BF16), _sds((1, 128), F32),
                   _sds((1, D_MODEL), F32), _sds((1, D_MODEL), F32)],
        vmem_mb=40)(hmid, x1, target, wfo, g2, b2)


def _ffn_bwd_a(dr2b, z, wfo, wfd, bfd):
    t = z.shape[0]
    tm = 256
    nt = t // tm

    def body(dr_ref, u_ref, gv_ref, wfo_ref, w_ref, b_ref, dgv_ref, duc_ref, dw_ref, db_ref,
             ext_ref, dh_ref, acc_ref):
        i = pl.program_id(0)

        @pl.when(i == 0)
        def _():
            ext_ref[0:FFN_HALO, :] = jnp.zeros((FFN_HALO, D_FF), F32)
            acc_ref[...] = jnp.zeros_like(acc_ref)

        @pl.when(i > 0)
        def _():
            ext_ref[0:FFN_HALO, :] = ext_ref[tm:tm + FFN_HALO, :]

        ext_ref[FFN_HALO:FFN_HALO + tm, :] = u_ref[...]
        dh_ref[...] = _dot_nt(dr_ref[...], wfo_ref[...])

        def block(r, carry):
            r0 = pl.multiple_of(r * ROW_BLOCK, ROW_BLOCK)
            rows = pl.ds(r0, ROW_BLOCK)
            taps = _ffn_taps(ext_ref[pl.ds(r0, ROW_BLOCK + FFN_HALO), :])
            uc = _ffn_conv3(taps, w_ref) + b_ref[...]
            g, dg = _gelu_and_grad(uc)
            dh = dh_ref[rows, :]
            dgv_ref[rows, :] = (dh * g).astype(BF16)
            duc = dh * gv_ref[rows, :] * dg
            duc_ref[rows, :] = duc
            acc_ref[0:8, :] += _fold8(duc)
            for k in range(FFN_KERNEL):
                acc_ref[8 + 8 * k:16 + 8 * k, :] += _fold8(duc * taps[k])
            return carry

        lax.fori_loop(0, tm // ROW_BLOCK, block, 0)

        @pl.when(i == nt - 1)
        def _():
            db_ref[...] = jnp.sum(acc_ref[0:8, :], axis=0, keepdims=True)
            for k in range(FFN_KERNEL):
                dw_ref[k:k + 1, :] = jnp.sum(acc_ref[8 + 8 * k:16 + 8 * k, :], axis=0, keepdims=True)

    tile = pl.BlockSpec((tm, D_FF), lambda i: (i, 0))
    return _call(
        body, name="ffn_bwd_a", grid=(nt,),
        in_specs=[pl.BlockSpec((tm, D_MODEL), lambda i: (i, 0)), tile, pl.BlockSpec((tm, D_FF), lambda i: (i, 1)),
                  pl.BlockSpec((D_FF, D_MODEL), lambda i: (0, 0)),
                  pl.BlockSpec((FFN_KERNEL, D_FF), lambda i: (0, 0)), pl.BlockSpec((1, D_FF), lambda i: (0, 0))],
        out_specs=[tile, tile, pl.BlockSpec((FFN_KERNEL, D_FF), lambda i: (0, 0)),
                   pl.BlockSpec((1, D_FF), lambda i: (0, 0))],
        out_shape=[_sds((t, D_FF), BF16), _sds((t, D_FF), F32), _sds((FFN_KERNEL, D_FF), F32), _sds((1, D_FF), F32)],
        scratch=[pltpu.VMEM((tm + FFN_HALO, D_FF), F32), pltpu.VMEM((tm, D_FF), F32),
                 pltpu.VMEM((8 + 8 * FFN_KERNEL, D_FF), F32)],
        vmem_mb=56)(dr2b, z, z, wfo, wfd, bfd)


def _ffn_bwd_b(duc, wfd):
    t = duc.shape[0]
    tm = 256
    nt = t // tm

    def body(duc_ref, w_ref, du_ref, ext_ref):
        i = pl.program_id(0)

        @pl.when(i == 0)
        def _():
            ext_ref[tm:tm + FFN_HALO, :] = jnp.zeros((FFN_HALO, D_FF), F32)

        @pl.when(i > 0)
        def _():
            ext_ref[tm:tm + FFN_HALO, :] = ext_ref[0:FFN_HALO, :]

        ext_ref[0:tm, :] = duc_ref[...]

        def block(r, carry):
            r0 = pl.multiple_of(r * ROW_BLOCK, ROW_BLOCK)
            win = ext_ref[pl.ds(r0, ROW_BLOCK + FFN_HALO), :]
            n = ROW_BLOCK + FFN_HALO
            du = (w_ref[2:3, :] * win[0:ROW_BLOCK, :] + w_ref[1:2, :] * pltpu.roll(win, n - 1, 0)[0:ROW_BLOCK, :]
                  + w_ref[0:1, :] * pltpu.roll(win, n - 2, 0)[0:ROW_BLOCK, :])
            du_ref[pl.ds(r0, ROW_BLOCK), :] = du.astype(BF16)
            return carry

        lax.fori_loop(0, tm // ROW_BLOCK, block, 0)

    rev = lambda i: (nt - 1 - i, 0)
    return _call(
        body, name="ffn_bwd_b", grid=(nt,),
        in_specs=[pl.BlockSpec((tm, D_FF), rev), pl.BlockSpec((FFN_KERNEL, D_FF), lambda i: (0, 0))],
        out_specs=pl.BlockSpec((tm, D_FF), rev),
        out_shape=_sds((t, D_FF), BF16),
        scratch=[pltpu.VMEM((tm + FFN_HALO, D_FF), F32)], vmem_mb=40)(duc, wfd)


def _ffn_in_bwd(dr2, dub, dgvb, wfi, r1, g1):
    t = dr2.shape[0]
    tm = 256

    def body(dr2_ref, du_ref, dgv_ref, wu_ref, wg_ref, r1_ref, g_ref, dr1_ref, dr1b_ref, dg_ref, db_ref):
        @pl.when(pl.program_id(0) == 0)
        def _():
            dg_ref[...] = jnp.zeros_like(dg_ref)
            db_ref[...] = jnp.zeros_like(db_ref)

        dx1 = ALPHA * dr2_ref[...] + _dot_nt(du_ref[...], wu_ref[...]) + _dot_nt(dgv_ref[...], wg_ref[...])
        xhat, rstd = _ln_stats(r1_ref[...])
        dg_ref[...] += jnp.sum(dx1 * xhat, axis=0, keepdims=True)
        db_ref[...] += jnp.sum(dx1, axis=0, keepdims=True)
        dr1 = _ln_bwd(dx1, xhat, rstd, g_ref[...])
        dr1_ref[...] = dr1
        dr1b_ref[...] = dr1.astype(BF16)

    row = lambda w: pl.BlockSpec((tm, w), lambda i: (i, 0))
    vec = pl.BlockSpec((1, D_MODEL), lambda i: (0, 0))
    return _call(
        body, name="ffn_in_bwd", grid=(t // tm,),
        in_specs=[row(D_MODEL), row(D_FF), row(D_FF),
                  pl.BlockSpec((D_MODEL, D_FF), lambda i: (0, 0)), pl.BlockSpec((D_MODEL, D_FF), lambda i: (0, 1)),
                  row(D_MODEL), vec],
        out_specs=[row(D_MODEL), row(D_MODEL), vec, vec],
        out_shape=[_sds((t, D_MODEL), F32), _sds((t, D_MODEL), BF16), _sds((1, D_MODEL), F32), _sds((1, D_MODEL), F32)],
        vmem_mb=56)(dr2, dub, dgvb, wfi, wfi, r1, g1)


def _local_step(x, target, wi, late_weights, late_reduce, wcd, bcd, clg, clb, logits, ng, g1, b1, wfd, bfd, g2, b2):
    proj, xb = _proj(x, wi)
    cc, cs = _conv_fwd(proj, wcd, bcd, clg, clb)
    o, og, states = _hgrn_fwd(proj, logits, ng)
    wco, wh, wo, wfi, wfo = late_weights(og)
    ycat, mixed, r1, x1, x1b = _merge_fwd(cs, og, proj, x, wco, wh, wo, g1, b1)
    z = _mm_nn(x1b, wfi, tm=1024, tn=1408, name="ffn_in", vmem_mb=48)
    hmid = _ffn_mid(z, wfd, bfd)
    dr2, dr2b, loss, d_g2, d_b2 = _ffn_out_loss(hmid, x1, target, wfo, g2, b2)

    g_wfo = _mm_tn(hmid, _views(dr2b), tn=512, tt=1024, name="grad_w_ffn_out", vmem_mb=48)
    dgvb, duc, d_wfd, d_bfd = _ffn_bwd_a(dr2b, z, wfo, wfd, bfd)
    dub = _ffn_bwd_b(duc, wfd)
    g_wfi = _mm_tn(x1b, _views(dub, dgvb), tn=1408, tt=1024, name="grad_w_ffn_in", vmem_mb=48)
    dr1, dr1b, d_g1, d_b1 = _ffn_in_bwd(dr2, dub, dgvb, wfi, r1, g1)

    g_wo = _mm_tn(mixed, _views(dr1b), tn=1024, tt=1024, name="grad_w_out")
    dstack, dyb, dcs, dog = _merge_bwd(dr1b, ycat, proj, wo, wco, wh)
    g_wco = _mm_tn(cs, [(dyb, 0)], tn=1024, tt=1024, name="grad_w_conv_out")
    g_wh = _mm_tn(og, [(dyb, 1)], tn=1024, tt=1024, name="grad_w_hgrn_out")

    dcs = late_reduce.begin((g_wco, g_wh, g_wo, g_wfi, g_wfo), dcs)
    dcc, d_wcd, d_bcd, d_clg, d_clb = _conv_bwd_a(dcs, cc, proj, clg, clb)
    dcc = late_reduce.across_chips(dcc)
    dstack = _conv_bwd_b(dcc, proj, wcd, dstack)
    dstack, d_logits, d_ng = _hgrn_bwd(dog, o, states, proj, logits, ng, dstack)
    dstack = late_reduce.assemble(d_ng, dstack)

    g_wi = _grad_w_in(xb, dstack)
    grad_x = _grad_x(dstack, wi, dr1)
    late = late_reduce.finish(grad_x)

    small = dict(w_conv_dw=d_wcd, b_conv_dw=d_bcd, conv_ln_g=d_clg, conv_ln_b=d_clb, hgrn_lb_logits=d_logits,
                 hgrn_norm_g=d_ng, ln1_g=d_g1, ln1_b=d_b1, w_ffn_dw=d_wfd, b_ffn_dw=d_bfd, ln2_g=d_g2, ln2_b=d_b2)
    return loss, grad_x, (g_wi,), late, small


ELEMENTWISE_BLOCK_ELEMS = 256 * 1024


def _row_tile(rows, cols):
    cap = max(16, ELEMENTWISE_BLOCK_ELEMS // cols)
    if rows <= cap:
        return rows
    best = None
    for cand in range(16, cap + 1, 16):
        if rows % cand == 0:
            best = cand
    assert best is not None
    return best


def _elementwise(fn, ins, out_dtypes, *, name):
    r, c = ins[0].shape
    tr = _row_tile(r, c)

    def body(*refs):
        outs = fn(*[ref[...] for ref in refs[:len(ins)]])
        for ref, val in zip(refs[len(ins):], outs):
            ref[...] = val.astype(ref.dtype)

    spec = pl.BlockSpec((tr, c), lambda i: (i, 0))
    return _call(
        body, name=name, grid=(r // tr,), in_specs=[spec] * len(ins), out_specs=[spec] * len(out_dtypes),
        out_shape=[_sds((r, c), dt) for dt in out_dtypes])(*ins)


def _windowed(fn, sel, ins, outs, *, window, name):
    rows, cols = window
    tr = _row_tile(rows, cols)
    steps = rows // tr

    def spec(where):
        if where is None:
            return pl.BlockSpec((tr, cols), lambda i, s: (i, 0))
        kind, p, _ = where
        if kind == "rows":
            return pl.BlockSpec((tr, cols), lambda i, s: (s[p] * steps + i, 0))
        return pl.BlockSpec((tr, cols), lambda i, s: (i, s[p]))

    n_in = len(ins)

    def body(s_ref, *refs):
        vals = fn(*[ref[...] for ref in refs[:n_in]])
        for ref, val in zip(refs[n_in:], vals):
            ref[...] = val.astype(ref.dtype)

    return pl.pallas_call(
        body, name=name, out_shape=[_hbm(shape, dt) for shape, dt, _ in outs],
        grid_spec=pltpu.PrefetchScalarGridSpec(
            num_scalar_prefetch=1, grid=(steps,), in_specs=[spec(where) for _, where in ins],
            out_specs=[spec(where) for _, _, where in outs]),
        compiler_params=pltpu.CompilerParams(dimension_semantics=("arbitrary",), vmem_limit_bytes=32 * 2 ** 20),
    )(sel, *[_in_hbm(a) for a, _ in ins])


def _adamw(w, g, m, v, *, name):
    def fn(w_, g_, m_, v_):
        m_new = ADAM_B1 * m_ + (1.0 - ADAM_B1) * g_
        v_new = ADAM_B2 * v_ + (1.0 - ADAM_B2) * (g_ * g_)
        m_hat = m_new / ADAM_BC1
        v_hat = v_new / ADAM_BC2
        delta = -ADAM_LR * (m_hat / (jnp.sqrt(v_hat) + ADAM_EPS) + ADAM_WD * w_)
        return delta, m_new, v_new

    return _elementwise(fn, [w, g, m, v], [F32, F32, F32], name=name)


def _place():
    return lax.axis_index("x"), lax.axis_index("y"), lax.axis_index("c")


def _other_chips(x, y):
    return [(1 - x, y), (x, 1 - y), (1 - x, 1 - y)]


SHARD_XOR = (2, 1, 3)


DMA_CHUNK_BYTES = 512 * 1024


def _n_chunks(ref):
    rows = ref.shape[0]
    total = ref.dtype.itemsize
    for d in ref.shape:
        total *= d
    best = 1
    for cand in range(2, min(rows, total // DMA_CHUNK_BYTES) + 1):
        if rows % cand == 0 and (rows // cand) % 16 == 0:
            best = cand
    return best


class _Copy:
    def __init__(self, src, dst, sems, dev=None):
        if dev is None:
            make = lambda s_, d_: pltpu.make_async_copy(s_, d_, sems[0])
        else:
            make = lambda s_, d_: pltpu.make_async_remote_copy(
                src_ref=s_, dst_ref=d_, send_sem=sems[0], recv_sem=sems[1], device_id=dev, device_id_type=MESH)
        self.local = dev is None
        self.whole = make(src, dst)
        n = _n_chunks(src)
        step = src.shape[0] // n
        self.parts = ([self.whole] if n == 1 else
                      [make(src.at[pl.ds(i * step, step)], dst.at[pl.ds(i * step, step)]) for i in range(n)])

    def start(self):
        for part in self.parts:
            part.start()

    def wait_recv(self):
        self.whole.wait_recv()

    def wait_send(self):
        self.whole.wait_send()

    def wait(self):
        self.whole.wait()


def _run_copies(local_ops, remote_ops, lsem, ssem, rsem):
    local = [_Copy(src, dst, (lsem.at[n],)) for n, (src, dst) in enumerate(local_ops)]
    remote = [_Copy(src, dst, (ssem.at[n], rsem.at[n]), dev) for n, (src, dst, dev) in enumerate(remote_ops)]
    for cp in local + remote:
        cp.start()
    for cp in remote:
        cp.wait_recv()
    for cp in remote:
        cp.wait_send()
    for cp in local:
        cp.wait()


def _comm_call(body, *, name, n_in, out_shape, n_local, n_remote):
    return pl.pallas_call(
        body, name=name, in_specs=[ANY] * n_in, out_specs=[ANY] * len(out_shape), out_shape=out_shape,
        scratch_shapes=[pltpu.SemaphoreType.DMA((max(n_local, 1),)), pltpu.SemaphoreType.DMA((n_remote,)),
                        pltpu.SemaphoreType.DMA((n_remote,))])


BIG = (("w_in", D_MODEL, IN_COLS, 1), ("w_conv_out", CONV_DIM, D_MODEL, 1), ("w_hgrn_out", HGRN_DIM, D_MODEL, 0),
       ("w_out", D_MODEL, D_MODEL, 0), ("w_ffn_in", D_MODEL, 2 * D_FF, 1), ("w_ffn_out", D_FF, D_MODEL, 0))


def _shard_slice(ref, rows, cols, axis, k):
    if axis == 1:
        w = cols // N_CHIPS
        return ref.at[:, pl.ds(k * w, w)]
    h = rows // N_CHIPS
    return ref.at[pl.ds(k * h, h), :]


def _half_slice(ref, rows, cols, axis, hc):
    if axis == 1:
        return ref.at[pl.ds(hc * (rows // 2), rows // 2), :]
    return ref.at[:, pl.ds(hc * (cols // 2), cols // 2)]


def _half_shape(rows, cols, axis):
    return (rows // 2, cols) if axis == 1 else (rows, cols // 2)


def _gather_weights(full, small):
    n_big, n_small = len(full), len(small)
    n_arr = n_big + n_small
    out_shape = ([_sds((r, c), BF16) for _, r, c, _ in BIG[:n_big]]
                 + [_sds((N_CHIPS,) + a.shape, F32) for a in small])
    shard_shape = [(r, c // N_CHIPS) if ax == 1 else (r // N_CHIPS, c) for _, r, c, ax in BIG]

    def body(*refs):
        ins, outs = refs[:n_arr], refs[n_arr:2 * n_arr]
        lsem, ssem, rsem, fsem_s, fsem_r = refs[2 * n_arr:]
        x, y, c = _place()
        me = 2 * x + y
        chips = _other_chips(x, y)
        sibling = (x, y, 1 - c)

        def region(idx, k, hc, bufs=outs):
            (_, r, cc, ax), (sr, _) = BIG[idx], shard_shape[idx]
            return _shard_slice(bufs[idx], r, cc, ax, k).at[pl.ds(hc * (sr // 2), sr // 2)]

        for k in range(N_CHIPS):
            for hc in range(2):
                @pl.when((me == k) & (c == hc))
                def _(k=k, hc=hc):
                    local = [_Copy(ins[n_big + i], outs[n_big + i].at[k], (lsem.at[i],)) for i in range(n_small)]
                    sends, fwds = [], []
                    for j, (cx, cy) in enumerate(chips):
                        for i in range(n_big):
                            n = j * n_arr + i
                            sends.append(_Copy(region(i, k, hc, ins), region(i, k, hc), (ssem.at[n], rsem.at[n]),
                                               (cx, cy, c)))
                            reg = region(i, k ^ SHARD_XOR[j], hc)
                            fwds.append(_Copy(reg, reg, (fsem_s.at[j * n_big + i], fsem_r.at[j * n_big + i]), sibling))
                        for i in range(n_small):
                            n = j * n_arr + n_big + i
                            sends.append(_Copy(ins[n_big + i], outs[n_big + i].at[k], (ssem.at[n], rsem.at[n]),
                                               (cx, cy, c)))
                    for cp in local + sends:
                        cp.start()
                    for j in range(3):
                        for i in range(n_big):
                            sends[j * n_arr + i].wait_recv()
                        for i in range(n_big):
                            fwds[j * n_big + i].start()
                    for j in range(3):
                        for i in range(n_small):
                            sends[j * n_arr + n_big + i].wait_recv()
                    for cp in fwds:
                        cp.wait_recv()
                    for cp in sends + fwds:
                        cp.wait_send()
                    for cp in local:
                        cp.wait()

    return pl.pallas_call(
        body, name="gather_weights", in_specs=[ANY] * n_arr, out_specs=[ANY] * n_arr, out_shape=out_shape,
        input_output_aliases={i: i for i in range(n_big)},
        scratch_shapes=[pltpu.SemaphoreType.DMA((n_small,)), pltpu.SemaphoreType.DMA((3 * n_arr,)),
                        pltpu.SemaphoreType.DMA((3 * n_arr,)), pltpu.SemaphoreType.DMA((3 * n_big,)),
                        pltpu.SemaphoreType.DMA((3 * n_big,))])(*full, *small)


SEM = pl.BlockSpec(memory_space=pltpu.SEMAPHORE)
N_EARLY = 1
LATE = BIG[N_EARLY:]


def _late_copies(src_bufs, dst_bufs, ssem, rsem, x, y, c, k):
    copies = []
    for j, (cx, cy) in enumerate(_other_chips(x, y)):
        for i, (src, dst, (_, r, cc, ax)) in enumerate(zip(src_bufs, dst_bufs, LATE)):
            n = j * len(LATE) + i
            copies.append(_Copy(_shard_slice(src, r, cc, ax, k), _shard_slice(dst, r, cc, ax, k),
                                (ssem.at[n], rsem.at[n]), (cx, cy, c)))
    return copies


def _late_sibling_copies(src_bufs, dst_bufs, ssem, rsem, x, y, c, k):
    n = len(LATE)
    return [_Copy(_half_slice(src_bufs[i], r, cc, ax, 1 - k), dst_bufs[n + i], (ssem.at[i], rsem.at[i]), (x, y, 1 - c))
            for i, (_, r, cc, ax) in enumerate(LATE)]


def _late_chip_copies(src_bufs, dst_bufs, ssem, rsem, x, y, c, k):
    n = len(LATE)
    copies = []
    for j, (cx, cy) in enumerate(_other_chips(x, y)):
        for i, (_, r, cc, ax) in enumerate(LATE):
            hr, hc = _half_shape(r, cc, ax)
            copies.append(_Copy(_shard_slice(src_bufs[i], hr, hc, ax, k ^ SHARD_XOR[j]), dst_bufs[n + 3 * i + j],
                                (ssem.at[j * n + i], rsem.at[j * n + i]), (cx, cy, c)))
    return copies


def _late_assemble_copies(src_bufs, dst_bufs, ssem, rsem, x, y, c, k):
    copies = []
    for i, (_, r, cc, ax) in enumerate(LATE):
        sr, sc = (r, cc // N_CHIPS) if ax == 1 else (r // N_CHIPS, cc)
        copies.append(_Copy(_half_slice(src_bufs[i], sr, sc, ax, k), _half_slice(dst_bufs[i], sr, sc, ax, k),
                            (ssem.at[i], rsem.at[i]), (x, y, 1 - c)))
    return copies


SIDE_EFFECT = pltpu.CompilerParams(has_side_effects=pltpu.SideEffectType.DATAFLOW_SIDE_EFFECTING)


def _split_start(name, bufs, carry, make_copies, n_copies, key_of, n_keys):
    n = len(bufs)

    def body(*refs):
        ins, (ssem, rsem), outs = refs[:n], refs[n + 1:n + 3], refs[n + 3:2 * n + 3]
        x, y, c = _place()
        key = key_of(x, y, c)
        for k in range(n_keys):
            @pl.when(key == k)
            def _(k=k):
                for cp in make_copies(ins, outs, ssem, rsem, x, y, c, k):
                    cp.start()

    arrays = list(bufs) + [carry]
    outs = pl.pallas_call(
        body, name=name, in_specs=[ANY] * (n + 1), out_specs=[SEM, SEM] + [ANY] * (n + 1),
        out_shape=[pltpu.SemaphoreType.DMA((n_copies,)), pltpu.SemaphoreType.DMA((n_copies,))]
        + [_hbm(a.shape, a.dtype) for a in arrays],
        input_output_aliases={i: 2 + i for i in range(n + 1)}, compiler_params=SIDE_EFFECT,
    )(*[_in_hbm(a) for a in arrays])
    return (outs[0], outs[1]), list(outs[2:2 + n]), outs[2 + n]


def _split_wait(name, sems, bufs, after, make_copies):
    n = len(bufs)

    def body(*refs):
        ins, ssem, rsem = refs[:n], refs[n], refs[n + 1]
        x, y, c = _place()
        for cp in make_copies(ins, ins, ssem, rsem, x, y, c, 0):
            cp.wait_send()
            cp.wait_recv()

    return pl.pallas_call(
        body, name=name, in_specs=[ANY] * n + [SEM, SEM, ANY], out_specs=[ANY] * n,
        out_shape=[_hbm(a.shape, a.dtype) for a in bufs], input_output_aliases={i: i for i in range(n)},
        compiler_params=SIDE_EFFECT,
    )(*[_in_hbm(a) for a in bufs], *sems, after)


def _chip_of(x, y, c):
    return 2 * x + y


def _core_of(x, y, c):
    return c


def _sibling_exchange(grads):
    n = len(grads)
    shapes = [_sds(_half_shape(r, c, ax), F32) for _, r, c, ax in BIG[:n]]

    def body(*refs):
        ins, got = refs[:n], refs[n:2 * n]
        lsem, ssem, rsem = refs[2 * n:]
        x, y, c = _place()
        for k in range(2):
            @pl.when(c == k)
            def _(k=k):
                remote_ops = [(_half_slice(g, r, cc, ax, 1 - k), dst, (x, y, 1 - c))
                              for g, dst, (_, r, cc, ax) in zip(ins, got, BIG)]
                _run_copies([], remote_ops, lsem, ssem, rsem)

    return _comm_call(body, name="grad_sibling_exchange", n_in=n, out_shape=shapes, n_local=0, n_remote=n)(*grads)


def _piece_shape(rows, cols, axis):
    hr, hc = _half_shape(rows, cols, axis)
    return (hr, hc // N_CHIPS) if axis == 1 else (hr // N_CHIPS, hc)


def _chip_exchange(chip_sums):
    n = len(chip_sums)
    half = [_half_shape(r, c, ax) for _, r, c, ax in BIG[:n]]
    out_shape = [_sds(_piece_shape(r, c, ax), BF16) for _, r, c, ax in BIG[:n] for _ in range(3)]

    def body(*refs):
        ins, got = refs[:n], refs[n:4 * n]
        lsem, ssem, rsem = refs[4 * n:]
        x, y, c = _place()
        me = 2 * x + y
        for k in range(N_CHIPS):
            @pl.when(me == k)
            def _(k=k):
                remote_ops = [(_shard_slice(g, half[idx][0], half[idx][1], BIG[idx][3], k ^ SHARD_XOR[j]),
                               got[3 * idx + j], (cx, cy, c))
                              for j, (cx, cy) in enumerate(_other_chips(x, y))
                              for idx, g in enumerate(ins)]
                _run_copies([], remote_ops, lsem, ssem, rsem)

    outs = _comm_call(body, name="grad_chip_exchange", n_in=n, out_shape=out_shape, n_local=0, n_remote=3 * n)(
        *chip_sums)
    return [outs[3 * idx:3 * idx + 3] for idx in range(n)]


def _sibling_assemble(shards):
    n = len(shards)
    shape = [(r, c // N_CHIPS) if ax == 1 else (r // N_CHIPS, c) for _, r, c, ax in BIG[:n]]

    def body(*refs):
        ins, outs = refs[:n], refs[n:2 * n]
        lsem, ssem, rsem = refs[2 * n:]
        x, y, c = _place()
        for k in range(2):
            @pl.when(c == k)
            def _(k=k):
                remote_ops = [(_half_slice(i_, sr, sc, ax, k), _half_slice(o_, sr, sc, ax, k), (x, y, 1 - c))
                              for i_, o_, (sr, sc), (_, _, _, ax) in zip(ins, outs, shape, BIG)]
                _run_copies([], remote_ops, lsem, ssem, rsem)

    return pl.pallas_call(
        body, name="grad_sibling_assemble", in_specs=[ANY] * n, out_specs=[ANY] * n,
        out_shape=[_sds(s, F32) for s in shape], input_output_aliases={i: i for i in range(n)},
        scratch_shapes=[pltpu.SemaphoreType.DMA((1,)), pltpu.SemaphoreType.DMA((n,)),
                        pltpu.SemaphoreType.DMA((n,))])(*shards)


def _sharded(axis, p, n):
    return ("cols", p, n) if axis == 1 else ("rows", p, n)


def _across(axis, p, n):
    return ("rows", p, n) if axis == 1 else ("cols", p, n)


def _shard_shape(rows, cols, axis):
    return (rows, cols // N_CHIPS) if axis == 1 else (rows // N_CHIPS, cols)


def _chip_sum(spec, sel, grad, got):
    name, r, c, ax = spec
    return _windowed(lambda a, b: (a + b,), sel, [(grad, _across(ax, 1, 2)), (got, None)],
                     [(_half_shape(r, c, ax), BF16, None)], window=_half_shape(r, c, ax), name="chip_sum_" + name)[0]


def _shard_sum(spec, sel, chip_sum, received):
    name, r, c, ax = spec
    add4 = lambda a, b0, b1, b2: (a.astype(F32) + b0.astype(F32) + b1.astype(F32) + b2.astype(F32),)
    return _windowed(add4, sel, [(chip_sum, _sharded(ax, 0, N_CHIPS))] + [(r_, None) for r_ in received],
                     [(_shard_shape(r, c, ax), F32, _across(ax, 1, 2))], window=_piece_shape(r, c, ax),
                     name="shard_sum_" + name)[0]


class _LateReduce:
    def __init__(self, sel):
        self.sel = sel

    def begin(self, grads, carry):
        landing = [lax.empty(_half_shape(r, c, ax), F32) for _, r, c, ax in LATE]
        self.sems, self.bufs, carry = _split_start("late_sibling_start", list(grads) + landing, carry,
                                                   _late_sibling_copies, len(LATE), _core_of, 2)
        return carry

    def across_chips(self, carry):
        n = len(LATE)
        bufs = _split_wait("late_sibling_wait", self.sems, self.bufs, carry, _late_sibling_copies)
        chip_sums = [_chip_sum(spec, self.sel, g_, h_) for spec, g_, h_ in zip(LATE, bufs[:n], bufs[n:])]
        landing = [lax.empty(_piece_shape(r, c, ax), BF16) for _, r, c, ax in LATE for _ in range(3)]
        self.sems, self.bufs, carry = _split_start("late_chip_start", chip_sums + landing, carry,
                                                   _late_chip_copies, 3 * n, _chip_of, N_CHIPS)
        return carry

    def assemble(self, after, carry):
        n = len(LATE)
        bufs = _split_wait("late_chip_wait", self.sems, self.bufs, after, _late_chip_copies)
        half_filled = [_shard_sum(spec, self.sel, bufs[i], bufs[n + 3 * i:n + 3 * i + 3]) for i, spec in enumerate(LATE)]
        self.sems, self.bufs, carry = _split_start("late_assemble_start", half_filled, carry,
                                                   _late_assemble_copies, n, _core_of, 2)
        return carry

    def finish(self, after):
        return _split_wait("late_assemble_wait", self.sems, self.bufs, after, _late_assemble_copies)


def _all_reduce_small(packed):
    r, w = packed.shape

    def body(in_ref, out_ref, slots, lsem, ssem, rsem):
        x, y, c = _place()
        me = 4 * x + 2 * y + c
        peers = [(x ^ (m >> 2), y ^ ((m >> 1) & 1), c ^ (m & 1)) for m in range(1, N_DEV)]
        _run_copies([(in_ref, slots.at[me])], [(in_ref, slots.at[me], dev) for dev in peers], lsem, ssem, rsem)
        total = slots[0]
        for d in range(1, N_DEV):
            total = total + slots[d]
        out_ref[...] = total

    vmem = pl.BlockSpec(memory_space=pltpu.VMEM)
    return pl.pallas_call(
        body, name="small_all_reduce", in_specs=[vmem], out_specs=vmem, out_shape=_sds((r, w), F32),
        scratch_shapes=[pltpu.VMEM((N_DEV, r, w), F32), pltpu.SemaphoreType.DMA((1,)),
                        pltpu.SemaphoreType.DMA((N_DEV - 1,)), pltpu.SemaphoreType.DMA((N_DEV - 1,))])(packed)


SMALL_ORDER = ("w_conv_dw", "b_conv_dw", "conv_ln_g", "conv_ln_b", "hgrn_lb_logits", "hgrn_norm_g",
               "ln1_g", "ln1_b", "w_ffn_dw", "b_ffn_dw", "ln2_g", "ln2_b")
REPLICATED_SMALL = tuple(n for n in SMALL_ORDER if n not in ("w_conv_dw", "w_ffn_dw"))
WEIGHT_ORDER = ("w_in", "w_conv_dw", "b_conv_dw", "conv_ln_g", "conv_ln_b", "w_conv_out", "hgrn_lb_logits",
                "hgrn_norm_g", "w_hgrn_out", "w_out", "ln1_g", "ln1_b", "w_ffn_in", "w_ffn_dw", "b_ffn_dw",
                "w_ffn_out", "ln2_g", "ln2_b")


def _pack(arrs):
    flat = jnp.concatenate([a.reshape(-1) for a in arrs])
    assert flat.shape[0] % 128 == 0
    return flat.reshape(-1, 128)


def _unpack(packed, shapes):
    flat = packed.reshape(-1)
    out, pos = [], 0
    for shp in shapes:
        size = 1
        for d in shp:
            size *= d
        out.append(flat[pos:pos + size].reshape(shp))
        pos += size
    return out


def kernel(x, w_in, w_conv_dw, b_conv_dw, conv_ln_g, conv_ln_b, w_conv_out, hgrn_lb_logits, hgrn_norm_g, w_hgrn_out, w_out, ln1_g, ln1_b, w_ffn_in, w_ffn_dw, b_ffn_dw, w_ffn_out, ln2_g, ln2_b, loss_target, m_w_in, m_w_conv_dw, m_b_conv_dw, m_conv_ln_g, m_conv_ln_b, m_w_conv_out, m_hgrn_lb_logits, m_hgrn_norm_g, m_w_hgrn_out, m_w_out, m_ln1_g, m_ln1_b, m_w_ffn_in, m_w_ffn_dw, m_b_ffn_dw, m_w_ffn_out, m_ln2_g, m_ln2_b, v_w_in, v_w_conv_dw, v_b_conv_dw, v_conv_ln_g, v_conv_ln_b, v_w_conv_out, v_hgrn_lb_logits, v_hgrn_norm_g, v_w_hgrn_out, v_w_out, v_ln1_g, v_ln1_b, v_w_ffn_in, v_w_ffn_dw, v_b_ffn_dw, v_w_ffn_out, v_ln2_g, v_ln2_b):
    w = dict(w_in=w_in, w_conv_dw=w_conv_dw, b_conv_dw=b_conv_dw, conv_ln_g=conv_ln_g, conv_ln_b=conv_ln_b,
             w_conv_out=w_conv_out, hgrn_lb_logits=hgrn_lb_logits, hgrn_norm_g=hgrn_norm_g, w_hgrn_out=w_hgrn_out,
             w_out=w_out, ln1_g=ln1_g, ln1_b=ln1_b, w_ffn_in=w_ffn_in, w_ffn_dw=w_ffn_dw, b_ffn_dw=b_ffn_dw,
             w_ffn_out=w_ffn_out, ln2_g=ln2_g, ln2_b=ln2_b)
    m = dict(w_in=m_w_in, w_conv_dw=m_w_conv_dw, b_conv_dw=m_b_conv_dw, conv_ln_g=m_conv_ln_g, conv_ln_b=m_conv_ln_b,
             w_conv_out=m_w_conv_out, hgrn_lb_logits=m_hgrn_lb_logits, hgrn_norm_g=m_hgrn_norm_g,
             w_hgrn_out=m_w_hgrn_out, w_out=m_w_out, ln1_g=m_ln1_g, ln1_b=m_ln1_b, w_ffn_in=m_w_ffn_in,
             w_ffn_dw=m_w_ffn_dw, b_ffn_dw=m_b_ffn_dw, w_ffn_out=m_w_ffn_out, ln2_g=m_ln2_g, ln2_b=m_ln2_b)
    v = dict(w_in=v_w_in, w_conv_dw=v_w_conv_dw, b_conv_dw=v_b_conv_dw, conv_ln_g=v_conv_ln_g, conv_ln_b=v_conv_ln_b,
             w_conv_out=v_w_conv_out, hgrn_lb_logits=v_hgrn_lb_logits, hgrn_norm_g=v_hgrn_norm_g,
             w_hgrn_out=v_w_hgrn_out, w_out=v_w_out, ln1_g=v_ln1_g, ln1_b=v_ln1_b, w_ffn_in=v_w_ffn_in,
             w_ffn_dw=v_w_ffn_dw, b_ffn_dw=v_b_ffn_dw, w_ffn_out=v_w_ffn_out, ln2_g=v_ln2_g, ln2_b=v_ln2_b)
    big_names = [n for n, _, _, _ in BIG]
    w2 = {n: a[0] if a.ndim == 3 else a for n, a in w.items()}
    m2 = {n: a[0] if a.ndim == 3 else a for n, a in m.items()}
    v2 = {n: a[0] if a.ndim == 3 else a for n, a in v.items()}

    sel = jnp.stack([2 * lax.axis_index("x") + lax.axis_index("y"), lax.axis_index("c")]).astype(jnp.int32)

    placed = [_windowed(lambda a: (a,), sel, [(w2[n], None)], [((r, c), BF16, _sharded(ax, 0, N_CHIPS))],
                        window=w2[n].shape, name="cast_" + n)[0] for n, r, c, ax in BIG]
    wi, wcd4, wfd4 = _gather_weights(placed[:N_EARLY], [w2["w_conv_dw"], w2["w_ffn_dw"]])
    wcd = jnp.transpose(wcd4, (1, 0, 2)).reshape(CONV_KERNEL, CONV_DIM)
    wfd = jnp.transpose(wfd4, (1, 0, 2)).reshape(FFN_KERNEL, D_FF)
    gather_sems, in_flight, wi = _split_start("gather_late_start", placed[N_EARLY:], wi, _late_copies,
                                               3 * len(LATE), _chip_of, N_CHIPS)
    late_weights = lambda after: _split_wait("gather_late_wait", gather_sems, in_flight, after, _late_copies)

    late_reduce = _LateReduce(sel)
    loss_part, grad_x, early_grads, late_shard_grads, small_grads = _local_step(
        x[0], loss_target[0], wi, late_weights, late_reduce,
        wcd, w2["b_conv_dw"], w2["conv_ln_g"], w2["conv_ln_b"],
        w2["hgrn_lb_logits"], w2["hgrn_norm_g"], w2["ln1_g"], w2["ln1_b"], wfd, w2["b_ffn_dw"],
        w2["ln2_g"], w2["ln2_b"])
    loss = lax.psum(loss_part[0, 0], ("x", "y", "c"))

    got = _sibling_exchange(early_grads)
    chip_sums = [_chip_sum(spec, sel, g_, h_) for spec, g_, h_ in zip(BIG, early_grads, got)]
    recv = _chip_exchange(chip_sums)
    half_filled = [_shard_sum(spec, sel, cs_, recv_) for spec, cs_, recv_ in zip(BIG, chip_sums, recv)]
    shard_grads = dict(zip(big_names, list(_sibling_assemble(half_filled)) + list(late_shard_grads)))

    small_shapes = [small_grads[n].shape for n in SMALL_ORDER]
    reduced = dict(zip(SMALL_ORDER, _unpack(_all_reduce_small(_pack([small_grads[n] for n in SMALL_ORDER])),
                                            small_shapes)))
    shard = 2 * lax.axis_index("x") + lax.axis_index("y")
    grads = dict(shard_grads)
    for n in REPLICATED_SMALL:
        grads[n] = reduced[n]
    grads["w_conv_dw"] = lax.dynamic_slice_in_dim(reduced["w_conv_dw"], shard * (CONV_DIM // N_CHIPS),
                                                  CONV_DIM // N_CHIPS, axis=1)
    grads["w_ffn_dw"] = lax.dynamic_slice_in_dim(reduced["w_ffn_dw"], shard * (D_FF // N_CHIPS),
                                                 D_FF // N_CHIPS, axis=1)

    delta, new_m, new_v = {}, {}, {}
    for n in big_names + ["w_conv_dw", "w_ffn_dw"]:
        delta[n], new_m[n], new_v[n] = _adamw(w2[n], grads[n], m2[n], v2[n], name="adamw_" + n)
    rep_shapes = [w2[n].shape for n in REPLICATED_SMALL]
    packed = _adamw(*[_pack([src[n] for n in REPLICATED_SMALL]) for src in (w2, grads, m2, v2)], name="adamw_small")
    for dst, pk in zip((delta, new_m, new_v), packed):
        for n, a in zip(REPLICATED_SMALL, _unpack(pk, rep_shapes)):
            dst[n] = a

    def shaped(d):
        return [d[n].reshape(w[n].shape) for n in WEIGHT_ORDER]

    return (loss, grad_x[None], *shaped(grads), *shaped(delta), *shaped(new_m), *shaped(new_v))
```

```python
import functools

import jax
import jax.numpy as jnp
from jax import lax
from jax.experimental import pallas as pl
from jax.experimental.pallas import tpu as pltpu

F32 = jnp.float32
BF16 = jnp.bfloat16

D_MODEL = 1024
CONV_DIM = 512
CONV_KERNEL = 31
HGRN_DIM = 1024
HGRN_HEADS = 8
HEAD_DIM = 128
CHUNK = 64
SUB = 16
N_SUB = CHUNK // SUB
D_FF = 2816
FFN_KERNEL = 3
IN_COLS = 7168
LN_EPS = 1e-5
RMS_EPS = 1e-6
ALPHA = 2.0 ** 0.25
GELU_C = 0.7978845608028654
GELU_A = 0.044715

ADAM_LR = 0.001
ADAM_B1 = 0.9
ADAM_B2 = 0.999
ADAM_EPS = 1e-08
ADAM_WD = 0.01
ADAM_STEP = 10
ADAM_BC1 = 1.0 - ADAM_B1 ** ADAM_STEP
ADAM_BC2 = 1.0 - ADAM_B2 ** ADAM_STEP

N_CHIPS = 4
N_DEV = 8
ROW_BLOCK = 32
CONV_HALO = 32
FFN_HALO = 8
MESH = pl.DeviceIdType.MESH
ANY = pl.BlockSpec(memory_space=pl.ANY)


def _dot(a, b):
    return jnp.dot(a, b, preferred_element_type=F32)


def _dot_nt(a, b):
    return lax.dot_general(a, b, (((1,), (1,)), ((), ())), preferred_element_type=F32)


def _dot_tn(a, b):
    return lax.dot_general(a, b, (((0,), (0,)), ((), ())), preferred_element_type=F32)


def _sigmoid(z):
    return jax.nn.sigmoid(z)


def _silu_grad(z, s):
    return s * (1.0 + z * (1.0 - s))


def _gelu_and_grad(u):
    u2 = u * u
    th = jnp.tanh(u * (GELU_C + (GELU_C * GELU_A) * u2))
    half = 0.5 + 0.5 * th
    g = u * half
    dg = half + (0.5 * u) * (1.0 - th * th) * (GELU_C + (3.0 * GELU_C * GELU_A) * u2)
    return g, dg


def _ln_stats(r):
    mu = jnp.mean(r, axis=-1, keepdims=True)
    xc = r - mu
    var = jnp.mean(xc * xc, axis=-1, keepdims=True)
    rstd = lax.rsqrt(var + LN_EPS)
    return xc * rstd, rstd


def _ln_bwd(dy, xhat, rstd, g):
    dxh = dy * g
    m1 = jnp.mean(dxh, axis=-1, keepdims=True)
    m2 = jnp.mean(dxh * xhat, axis=-1, keepdims=True)
    return rstd * (dxh - m1 - xhat * m2)


def _fold8(x):
    acc = x[0:8, :]
    for r in range(8, x.shape[0], 8):
        acc = acc + x[r:r + 8, :]
    return acc


def _in_hbm(a):
    return pltpu.with_memory_space_constraint(a, pltpu.HBM)


def _hbm(shape, dtype):
    return pltpu.HBM(shape, dtype)


def _out_hbm(out_shape):
    if isinstance(out_shape, (list, tuple)):
        return [_hbm(s.shape, s.dtype) for s in out_shape]
    return _hbm(out_shape.shape, out_shape.dtype)


def _call(body, *, name, grid, in_specs, out_specs, out_shape, scratch=(), vmem_mb=32, aliases=None):
    call = pl.pallas_call(
        body, name=name, grid=grid, in_specs=in_specs, out_specs=out_specs, out_shape=_out_hbm(out_shape),
        scratch_shapes=list(scratch), input_output_aliases=aliases or {},
        compiler_params=pltpu.CompilerParams(
            dimension_semantics=("arbitrary",) * len(grid), vmem_limit_bytes=vmem_mb * 2 ** 20))
    return lambda *args: call(*[_in_hbm(a) for a in args])


def _sds(shape, dtype):
    return jax.ShapeDtypeStruct(shape, dtype)


def _proj(x, w):
    t = x.shape[0]
    tm, tn = min(t, 1024), 1024

    def body(x_ref, w_ref, p_ref, xb_ref):
        @pl.when(pl.program_id(1) == 0)
        def _():
            xb_ref[...] = x_ref[...].astype(BF16)
        p_ref[...] = _dot(xb_ref[...], w_ref[...])

    return _call(
        body, name="proj", grid=(t // tm, IN_COLS // tn),
        in_specs=[pl.BlockSpec((tm, D_MODEL), lambda i, j: (i, 0)),
                  pl.BlockSpec((D_MODEL, tn), lambda i, j: (0, j))],
        out_specs=[pl.BlockSpec((tm, tn), lambda i, j: (i, j)),
                   pl.BlockSpec((tm, D_MODEL), lambda i, j: (i, 0))],
        out_shape=[_sds((t, IN_COLS), F32), _sds((t, D_MODEL), BF16)], vmem_mb=48)(x, w)


def _mm_nn(a, w, *, tm, tn, name, vmem_mb=32):
    t, k = a.shape
    n = w.shape[1]
    tm = min(tm, t)

    def body(a_ref, w_ref, o_ref):
        o_ref[...] = _dot(a_ref[...], w_ref[...])

    return _call(
        body, name=name, grid=(t // tm, n // tn),
        in_specs=[pl.BlockSpec((tm, k), lambda i, j: (i, 0)), pl.BlockSpec((k, tn), lambda i, j: (0, j))],
        out_specs=pl.BlockSpec((tm, tn), lambda i, j: (i, j)),
        out_shape=_sds((t, n), F32), vmem_mb=vmem_mb)(a, w)


def _views(*arrs):
    out = []
    for a in arrs:
        if a.ndim == 2:
            out.append((a, None))
        else:
            out.extend((a, p) for p in range(a.shape[0]))
    return out


def _piece_layout(views, tile):
    starts, counts, total = [], [], 0
    for arr, _ in views:
        width = arr.shape[-1]
        assert width % tile == 0
        starts.append(total)
        counts.append(width // tile)
        total += width // tile
    return starts, counts, total


def _mm_tn(a, views, *, tn, name, tt=512, vmem_mb=32):
    t, m = a.shape
    tt = min(tt, t)
    starts, counts, nj = _piece_layout(views, tn)
    n_views = len(views)

    def body(a_ref, *refs):
        b_refs, o_ref = refs[:n_views], refs[n_views]
        j = pl.program_id(0)

        @pl.when(pl.program_id(1) == 0)
        def _():
            o_ref[...] = jnp.zeros_like(o_ref)

        for b_ref, st, nb, (_, p) in zip(b_refs, starts, counts, views):
            @pl.when((j >= st) & (j < st + nb))
            def _(b_ref=b_ref, p=p):
                blk = b_ref[...] if p is None else b_ref[0]
                o_ref[...] += _dot_tn(a_ref[...], blk)

    def b_spec(st, nb, p):
        def rows(j, k):
            return jnp.where((j >= st) & (j < st + nb), k, 0)

        def cols(j):
            return jnp.clip(j - st, 0, nb - 1)

        if p is None:
            return pl.BlockSpec((tt, tn), lambda j, k: (rows(j, k), cols(j)))
        return pl.BlockSpec((1, tt, tn), lambda j, k: (p, rows(j, k), cols(j)))

    return _call(
        body, name=name, grid=(nj, t // tt),
        in_specs=[pl.BlockSpec((tt, m), lambda j, k: (k, 0))]
        + [b_spec(st, nb, p) for st, nb, (_, p) in zip(starts, counts, views)],
        out_specs=pl.BlockSpec((m, tn), lambda j, k: (0, j)),
        out_shape=_sds((m, nj * tn), F32), vmem_mb=vmem_mb)(a, *[arr for arr, _ in views])


DPROJ_SLABS = 7
DPROJ_GATE_SLAB = 4
DPROJ_CONV_SLAB = 6


def _slab_cols(s):
    return (s + 1) % DPROJ_SLABS


def _grad_w_in(xb, dstack):
    t = xb.shape[0]
    tt = min(t, 1024)

    def body(a_ref, b_ref, o_ref):
        @pl.when(pl.program_id(1) == 0)
        def _():
            o_ref[...] = jnp.zeros_like(o_ref)

        o_ref[...] += _dot_tn(a_ref[...], b_ref[0])

    return _call(
        body, name="grad_w_in", grid=(DPROJ_SLABS, t // tt),
        in_specs=[pl.BlockSpec((tt, D_MODEL), lambda j, k: (k, 0)),
                  pl.BlockSpec((1, tt, D_MODEL), lambda j, k: (j, k, 0))],
        out_specs=pl.BlockSpec((D_MODEL, D_MODEL), lambda j, k: (0, _slab_cols(j))),
        out_shape=_sds((D_MODEL, IN_COLS), F32), vmem_mb=48)(xb, dstack)


def _grad_x(dstack, wi, dr1, part, other=None):
    t = dr1.shape[0]
    tm = min(t // 2, 1024)
    steps = t // 2 // tm
    first = part * steps

    def body(add_ref, b_ref, w_ref, *refs):
        o_ref = refs[-1]

        @pl.when(pl.program_id(1) == 0)
        def _():
            o_ref[...] = ALPHA * add_ref[...]

        o_ref[...] += _dot_nt(b_ref[0], w_ref[...])

    extra = [] if other is None else [other]
    return _call(
        body, name="grad_x_%d" % part, grid=(steps, DPROJ_SLABS),
        in_specs=[pl.BlockSpec((tm, D_MODEL), lambda i, k: (first + i, 0)),
                  pl.BlockSpec((1, tm, D_MODEL), lambda i, k: (k, first + i, 0)),
                  pl.BlockSpec((D_MODEL, D_MODEL), lambda i, k: (0, _slab_cols(k)))] + [ANY] * len(extra),
        out_specs=pl.BlockSpec((tm, D_MODEL), lambda i, k: (first + i, 0)),
        out_shape=_sds((t, D_MODEL), F32), aliases={3: 0} if extra else None, vmem_mb=48)(dr1, dstack, wi, *extra)


LANES = 128
CONV_FWD_OFFSETS = {k: 2 + k for k in range(CONV_KERNEL)}
CONV_BWD_OFFSETS = {k: CONV_KERNEL - 1 - k for k in range(CONV_KERNEL)}


def _conv_taps(win, offsets):
    n = win.shape[0]
    for b in range(8):
        taps = [k for k, o in offsets.items() if o % 8 == b]
        if not taps:
            continue
        shifted = win if b == 0 else pltpu.roll(win, n - b, 0)
        for k in taps:
            first = offsets[k] - b
            yield k, shifted[first:first + ROW_BLOCK, :]


def _conv_fwd(proj, wcd, bcd, lng, lnb):
    t = proj.shape[0]
    tm = 512

    def body(cv_ref, cg_ref, w_ref, b_ref, g_ref, be_ref, cc_ref, cs_ref, ext_ref):
        i = pl.program_id(0)

        @pl.when(i == 0)
        def _():
            ext_ref[0:CONV_HALO, :] = jnp.zeros((CONV_HALO, CONV_DIM), F32)

        @pl.when(i > 0)
        def _():
            ext_ref[0:CONV_HALO, :] = ext_ref[tm:tm + CONV_HALO, :]

        ext_ref[CONV_HALO:CONV_HALO + tm, :] = cv_ref[...] * _sigmoid(cg_ref[...])

        def block(r, carry):
            r0 = pl.multiple_of(r * ROW_BLOCK, ROW_BLOCK)
            groups = []
            for g in range(CONV_DIM // LANES):
                lanes = slice(g * LANES, (g + 1) * LANES)
                win = ext_ref[pl.ds(r0, ROW_BLOCK + CONV_HALO), lanes]
                acc = jnp.broadcast_to(b_ref[:, lanes], (ROW_BLOCK, LANES))
                for k, rows_k in _conv_taps(win, CONV_FWD_OFFSETS):
                    acc = acc + w_ref[k:k + 1, lanes] * rows_k
                groups.append(acc)
            acc = jnp.concatenate(groups, axis=1)
            cc_ref[pl.ds(r0, ROW_BLOCK), :] = acc
            xhat, _ = _ln_stats(acc)
            a = xhat * g_ref[...] + be_ref[...]
            cs_ref[pl.ds(r0, ROW_BLOCK), :] = (a * _sigmoid(a)).astype(BF16)
            return carry

        lax.fori_loop(0, tm // ROW_BLOCK, block, 0)

    vec = pl.BlockSpec((1, CONV_DIM), lambda i: (0, 0))
    return _call(
        body, name="conv_fwd", grid=(t // tm,),
        in_specs=[pl.BlockSpec((tm, CONV_DIM), lambda i: (i, 0)), pl.BlockSpec((tm, CONV_DIM), lambda i: (i, 1)),
                  pl.BlockSpec((CONV_KERNEL, CONV_DIM), lambda i: (0, 0)), vec, vec, vec],
        out_specs=[pl.BlockSpec((tm, CONV_DIM), lambda i: (i, 0)), pl.BlockSpec((tm, CONV_DIM), lambda i: (i, 0))],
        out_shape=[_sds((t, CONV_DIM), F32), _sds((t, CONV_DIM), BF16)],
        scratch=[pltpu.VMEM((tm + CONV_HALO, CONV_DIM), F32)])(proj, proj, wcd, bcd, lng, lnb)


def _conv_bwd_a(dcs, cc, proj, lng, lnb):
    t = proj.shape[0]
    tm = 512
    nt = t // tm

    def body(dcs_ref, cc_ref, cv_ref, cg_ref, g_ref, be_ref,
             dcc_ref, dw_ref, db_ref, dg_ref, dbe_ref, ext_ref, accw_ref, acc3_ref):
        i = pl.program_id(0)

        @pl.when(i == 0)
        def _():
            ext_ref[0:CONV_HALO, :] = jnp.zeros((CONV_HALO, CONV_DIM), F32)
            accw_ref[...] = jnp.zeros_like(accw_ref)
            acc3_ref[...] = jnp.zeros_like(acc3_ref)

        @pl.when(i > 0)
        def _():
            ext_ref[0:CONV_HALO, :] = ext_ref[tm:tm + CONV_HALO, :]

        ext_ref[CONV_HALO:CONV_HALO + tm, :] = cv_ref[...] * _sigmoid(cg_ref[...])

        def block(r, carry):
            r0 = pl.multiple_of(r * ROW_BLOCK, ROW_BLOCK)
            rows = pl.ds(r0, ROW_BLOCK)
            xhat, rstd = _ln_stats(cc_ref[rows, :])
            a = xhat * g_ref[...] + be_ref[...]
            sg = _sigmoid(a)
            da = dcs_ref[rows, :] * _silu_grad(a, sg)
            acc3_ref[8:16, :] += _fold8(da * xhat)
            acc3_ref[16:24, :] += _fold8(da)
            dcc = _ln_bwd(da, xhat, rstd, g_ref[...])
            dcc_ref[rows, :] = dcc
            acc3_ref[0:8, :] += _fold8(dcc)
            for g in range(CONV_DIM // LANES):
                lanes = slice(g * LANES, (g + 1) * LANES)
                win = ext_ref[pl.ds(r0, ROW_BLOCK + CONV_HALO), lanes]
                dcc_g = dcc[:, lanes]
                for k, rows_k in _conv_taps(win, CONV_FWD_OFFSETS):
                    accw_ref[8 * k:8 * k + 8, lanes] += _fold8(dcc_g * rows_k)
            return carry

        lax.fori_loop(0, tm // ROW_BLOCK, block, 0)

        @pl.when(i == nt - 1)
        def _():
            for k in range(CONV_KERNEL):
                dw_ref[k:k + 1, :] = jnp.sum(accw_ref[8 * k:8 * k + 8, :], axis=0, keepdims=True)
            db_ref[...] = jnp.sum(acc3_ref[0:8, :], axis=0, keepdims=True)
            dg_ref[...] = jnp.sum(acc3_ref[8:16, :], axis=0, keepdims=True)
            dbe_ref[...] = jnp.sum(acc3_ref[16:24, :], axis=0, keepdims=True)

    vec = pl.BlockSpec((1, CONV_DIM), lambda i: (0, 0))
    tile = pl.BlockSpec((tm, CONV_DIM), lambda i: (i, 0))
    return _call(
        body, name="conv_bwd_a", grid=(nt,),
        in_specs=[tile, tile, tile, pl.BlockSpec((tm, CONV_DIM), lambda i: (i, 1)), vec, vec],
        out_specs=[tile, pl.BlockSpec((CONV_KERNEL, CONV_DIM), lambda i: (0, 0)), vec, vec, vec],
        out_shape=[_sds((t, CONV_DIM), F32), _sds((CONV_KERNEL, CONV_DIM), F32),
                   _sds((1, CONV_DIM), F32), _sds((1, CONV_DIM), F32), _sds((1, CONV_DIM), F32)],
        scratch=[pltpu.VMEM((tm + CONV_HALO, CONV_DIM), F32),
                 pltpu.VMEM((8 * CONV_KERNEL, CONV_DIM), F32),
                 pltpu.VMEM((24, CONV_DIM), F32)])(dcs, cc, proj, proj, lng, lnb)


def _conv_bwd_b(dcc, proj, wcd, dstack):
    t = proj.shape[0]
    tm = 512
    nt = t // tm

    def body(dcc_ref, cv_ref, cg_ref, w_ref, stack_ref, out_ref, ext_ref):
        del stack_ref
        i = pl.program_id(0)

        @pl.when(i == 0)
        def _():
            ext_ref[tm:tm + CONV_HALO, :] = jnp.zeros((CONV_HALO, CONV_DIM), F32)

        @pl.when(i > 0)
        def _():
            ext_ref[tm:tm + CONV_HALO, :] = ext_ref[0:CONV_HALO, :]

        ext_ref[0:tm, :] = dcc_ref[...]

        def block(r, carry):
            r0 = pl.multiple_of(r * ROW_BLOCK, ROW_BLOCK)
            rows = pl.ds(r0, ROW_BLOCK)
            for g in range(CONV_DIM // LANES):
                lanes = slice(g * LANES, (g + 1) * LANES)
                gate_lanes = slice(CONV_DIM + g * LANES, CONV_DIM + (g + 1) * LANES)
                win = ext_ref[pl.ds(r0, ROW_BLOCK + CONV_HALO), lanes]
                acc = jnp.zeros((ROW_BLOCK, LANES), F32)
                for k, rows_k in _conv_taps(win, CONV_BWD_OFFSETS):
                    acc = acc + w_ref[k:k + 1, lanes] * rows_k
                sg = _sigmoid(cg_ref[rows, lanes])
                out_ref[0, rows, lanes] = (acc * sg).astype(BF16)
                out_ref[0, rows, gate_lanes] = (acc * cv_ref[rows, lanes] * sg * (1.0 - sg)).astype(BF16)
            return carry

        lax.fori_loop(0, tm // ROW_BLOCK, block, 0)

    rev = lambda i: (nt - 1 - i, 0)
    return _call(
        body, name="conv_bwd_b", grid=(nt,),
        in_specs=[pl.BlockSpec((tm, CONV_DIM), rev), pl.BlockSpec((tm, CONV_DIM), rev),
                  pl.BlockSpec((tm, CONV_DIM), lambda i: (nt - 1 - i, 1)),
                  pl.BlockSpec((CONV_KERNEL, CONV_DIM), lambda i: (0, 0)), ANY],
        out_specs=pl.BlockSpec((1, tm, 2 * CONV_DIM), lambda i: (DPROJ_CONV_SLAB, nt - 1 - i, 0)),
        out_shape=_sds(dstack.shape, BF16), aliases={4: 0},
        scratch=[pltpu.VMEM((tm + CONV_HALO, CONV_DIM), F32)])(dcc, proj, proj, wcd, dstack)


def _lower_bound(lg_ref):
    a0, a1 = lg_ref[0:1, :], lg_ref[1:2, :]
    m = jnp.maximum(a0, a1)
    e0, e1 = jnp.exp(a0 - m), jnp.exp(a1 - m)
    return e0 / (e0 + e1)


def _block_tri(n, upper):
    r = lax.broadcasted_iota(jnp.int32, (n, n), 0)
    c = lax.broadcasted_iota(jnp.int32, (n, n), 1)
    same = (r >> 6) == (c >> 6)
    tri = (c >= r) if upper else (c <= r)
    return jnp.where(same & tri, 1.0, 0.0).astype(BF16)


def _block_cumsum(x, tri):
    w = x.shape[1]
    hi = x.astype(BF16)
    r1 = x - hi.astype(F32)
    mid = r1.astype(BF16)
    lo = (r1 - mid.astype(F32)).astype(BF16)
    y = _dot(tri, jnp.concatenate([hi, mid, lo], axis=1))
    return y[:, 0:w] + y[:, w:2 * w] + y[:, 2 * w:3 * w]


def _first_step():
    return (pl.program_id(0) == 0) & (pl.program_id(1) == 0)


def _block_rows(i):
    return slice(SUB * i, SUB * (i + 1))


def _chunk_terms(qc, kc, bc, b_ref, first_row, lanes=slice(None)):
    betas = [jnp.zeros((1, HEAD_DIM), F32)]
    betas += [b_ref[first_row + SUB * i - 1:first_row + SUB * i, lanes] for i in range(1, N_SUB)]
    b_last = b_ref[first_row + CHUNK - 1:first_row + CHUNK, lanes]
    zeros = lambda rows: jnp.zeros((rows, HEAD_DIM), BF16)
    qscale = [jnp.exp(bc[_block_rows(i), :] - betas[i]) for i in range(N_SUB)]
    qs = [(qc[_block_rows(i), :] * qscale[i]).astype(BF16) for i in range(N_SUB)]
    kscale = [jnp.exp(betas[n] - bc[0:SUB * (n + 1), :]) for n in range(N_SUB)]
    ks = [(kc[0:SUB * (n + 1), :] * kscale[n]).astype(BF16) for n in range(N_SUB)]

    def tall(parts):
        parts = [p for p in parts if p.shape[0]]
        return parts[0] if len(parts) == 1 else jnp.concatenate(parts, axis=0)

    qcat = jnp.concatenate([tall([zeros(SUB * n), qs[n], zeros(CHUNK - SUB * (n + 1))]) for n in range(N_SUB)],
                           axis=1)
    kcat = jnp.concatenate([tall([ks[n], zeros(CHUNK - SUB * (n + 1))]) for n in range(N_SUB)], axis=1)
    return dict(qscale=qscale, qcat=qcat, kscale=kscale, kcat=kcat,
                eb=jnp.exp(bc), e_last=jnp.exp(b_last), ktscale=jnp.exp(b_last - bc))


def _causal(shape_rows_first):
    r = lax.broadcasted_iota(jnp.int32, (CHUNK, CHUNK), 0)
    c = lax.broadcasted_iota(jnp.int32, (CHUNK, CHUNK), 1)
    return (c <= r) if shape_rows_first else (r <= c)


HEADS_PER_STEP = 2
GROUP = HEADS_PER_STEP * HEAD_DIM
HEAD_GROUPS = HGRN_HEADS // HEADS_PER_STEP


def _head_lanes(h):
    return slice(h * HEAD_DIM, (h + 1) * HEAD_DIM)


def _hgrn_specs(tm, tile_of):
    per_piece = HGRN_DIM // GROUP
    col = lambda piece: (lambda g, i: (tile_of(i), per_piece * piece + g))
    return [pl.BlockSpec((tm, GROUP), col(1)), pl.BlockSpec((tm, GROUP), col(2)),
            pl.BlockSpec((tm, GROUP), col(3)), pl.BlockSpec((tm, GROUP), col(4)),
            pl.BlockSpec((2, GROUP), lambda g, i: (0, g)), pl.BlockSpec((1, GROUP), lambda g, i: (0, g))]


def _per_head_mean(x):
    return jnp.concatenate(
        [jnp.broadcast_to(jnp.mean(x[:, _head_lanes(h)], axis=-1, keepdims=True), (x.shape[0], HEAD_DIM))
         for h in range(HEADS_PER_STEP)], axis=1)


def _hgrn_fwd(proj, logits, ng):
    t = proj.shape[0]
    tm = 512
    nc = tm // CHUNK
    nt = t // tm

    def body(zq_ref, zf_ref, v_ref, zg_ref, lg_ref, ng_ref, o_ref, og_ref, st_ref,
             s_scr, q_scr, k_scr, b_scr, tri_scr):
        @pl.when(_first_step())
        def _():
            tri_scr[...] = _block_tri(tm, upper=False)

        @pl.when(pl.program_id(1) == 0)
        def _():
            s_scr[...] = jnp.zeros_like(s_scr)

        lb = _lower_bound(lg_ref)
        zf = zf_ref[...]
        f = lb + (1.0 - lb) * _sigmoid(zf)
        k_scr[...] = (1.0 - lb) * _sigmoid(-zf)
        zq = zq_ref[...]
        q_scr[...] = zq * _sigmoid(zq)
        b_scr[...] = _block_cumsum(jnp.log(f), tri_scr[...])

        st = [s_scr[h] for h in range(HEADS_PER_STEP)]
        for c in range(nc):
            rows = pl.ds(c * CHUNK, CHUNK)
            for h in range(HEADS_PER_STEP):
                lanes = _head_lanes(h)
                qc, kc, bc, vc = q_scr[rows, lanes], k_scr[rows, lanes], b_scr[rows, lanes], v_ref[rows, lanes]
                st_ref[h, c] = st[h]
                tr = _chunk_terms(qc, kc, bc, b_scr, c * CHUNK, lanes)
                a = jnp.where(_causal(True), _dot_nt(tr["qcat"], tr["kcat"]), 0.0)
                vb = vc.astype(BF16)
                o_ref[rows, lanes] = (_dot(a.astype(BF16), vb)
                                      + _dot_nt((qc * tr["eb"]).astype(BF16), st[h].astype(BF16)))
                st[h] = st[h] * tr["e_last"] + _dot_tn(vb, (kc * tr["ktscale"]).astype(BF16))
        for h in range(HEADS_PER_STEP):
            s_scr[h] = st[h]

        o = o_ref[...]
        rinv = lax.rsqrt(_per_head_mean(o * o) + RMS_EPS)
        zg = zg_ref[...]
        og_ref[...] = (o * rinv * ng_ref[...] * (zg * _sigmoid(zg))).astype(BF16)

    tile = pl.BlockSpec((tm, GROUP), lambda g, i: (i, g))
    return _call(
        body, name="hgrn_fwd", grid=(HEAD_GROUPS, nt),
        in_specs=_hgrn_specs(tm, lambda i: i),
        out_specs=[tile, tile,
                   pl.BlockSpec((HEADS_PER_STEP, nc, HEAD_DIM, HEAD_DIM), lambda g, i: (g, i, 0, 0))],
        out_shape=[_sds((t, HGRN_DIM), F32), _sds((t, HGRN_DIM), BF16),
                   _sds((HGRN_HEADS, t // CHUNK, HEAD_DIM, HEAD_DIM), F32)],
        scratch=[pltpu.VMEM((HEADS_PER_STEP, HEAD_DIM, HEAD_DIM), F32)] + [pltpu.VMEM((tm, GROUP), F32)] * 3
        + [pltpu.VMEM((tm, tm), BF16)],
    )(proj, proj, proj, proj, logits, ng)


def _hgrn_bwd(dog, o, states, proj, logits, ng, dstack):
    t = proj.shape[0]
    tm = 512
    nc = tm // CHUNK
    nt = t // tm

    def body(dog_ref, o_ref, st_ref, zq_ref, zf_ref, v_ref, zg_ref, lg_ref, ng_ref, stack_ref,
             dp_ref, dlg_ref, dng_ref,
             ds_scr, q_scr, k_scr, b_scr, do_scr, dq_scr, dk_scr, dv_scr, db_scr, dlb_scr, tri_scr):
        i = pl.program_id(1)

        @pl.when(_first_step())
        def _():
            tri_scr[0] = _block_tri(tm, upper=False)
            tri_scr[1] = _block_tri(tm, upper=True)

        @pl.when(i == 0)
        def _():
            ds_scr[...] = jnp.zeros_like(ds_scr)
            dlb_scr[...] = jnp.zeros_like(dlb_scr)
            dng_ref[...] = jnp.zeros_like(dng_ref)

        lb = _lower_bound(lg_ref)
        ng_row = ng_ref[...]
        o = o_ref[...]
        rinv = lax.rsqrt(_per_head_mean(o * o) + RMS_EPS)
        ohat = o * rinv
        zg = zg_ref[...]
        sg = _sigmoid(zg)
        dog_v = dog_ref[...]
        don = dog_v * (zg * sg)
        dp_ref[3] = (dog_v * (ohat * ng_row) * _silu_grad(zg, sg)).astype(BF16)
        dng_ref[...] += jnp.sum(don * ohat, axis=0, keepdims=True)
        dohat = don * ng_row
        do_scr[...] = rinv * (dohat - ohat * _per_head_mean(dohat * ohat))

        zf = zf_ref[...]
        s = _sigmoid(zf)
        s_neg = _sigmoid(-zf)
        f = lb + (1.0 - lb) * s
        k_scr[...] = (1.0 - lb) * s_neg
        zq = zq_ref[...]
        sq = _sigmoid(zq)
        q_scr[...] = zq * sq
        b_scr[...] = _block_cumsum(jnp.log(f), tri_scr[0])

        dsts = [ds_scr[h] for h in range(HEADS_PER_STEP)]
        for c, h in [(c, h) for c in reversed(range(nc)) for h in range(HEADS_PER_STEP)]:
            rows = pl.ds(c * CHUNK, CHUNK)
            lanes = _head_lanes(h)
            qc, kc, bc, vc, doc = (q_scr[rows, lanes], k_scr[rows, lanes], b_scr[rows, lanes], v_ref[rows, lanes],
                                   do_scr[rows, lanes])
            st = st_ref[h, c]
            dst = dsts[h]
            tr = _chunk_terms(qc, kc, bc, b_scr, c * CHUNK, lanes)
            qcb, kcb = tr["qcat"], tr["kcat"]
            dob, vb, dstb = doc.astype(BF16), vc.astype(BF16), dst.astype(BF16)
            a_t = jnp.where(_causal(False), _dot_nt(kcb, qcb), 0.0)
            da = jnp.where(_causal(True), _dot_nt(dob, vb), 0.0)
            da_t = jnp.where(_causal(False), _dot_nt(vb, dob), 0.0)
            dqcat = _dot(da.astype(BF16), kcb)
            dkcat = _dot(da_t.astype(BF16), qcb)
            kt = kc * tr["ktscale"]
            dv_scr[rows, lanes] = _dot(a_t.astype(BF16), dob) + _dot_nt(kt.astype(BF16), dstb)
            dq_blocks, dk_blocks, db_blocks = [], [], []
            for j in range(N_SUB):
                rows_j = _block_rows(j)
                lanes_j = slice(j * HEAD_DIM, (j + 1) * HEAD_DIM)
                dq_j = dqcat[rows_j, lanes_j]
                dq_blocks.append(dq_j * tr["qscale"][j])
                db_j = qcb[rows_j, lanes_j].astype(F32) * dq_j
                dk_j = jnp.zeros((SUB, HEAD_DIM), F32)
                for n in range(j, N_SUB):
                    lanes_n = slice(n * HEAD_DIM, (n + 1) * HEAD_DIM)
                    dk_jn = dkcat[rows_j, lanes_n]
                    dk_j = dk_j + dk_jn * tr["kscale"][n][rows_j, :]
                    db_j = db_j - kcb[rows_j, lanes_n].astype(F32) * dk_jn
                dk_blocks.append(dk_j)
                db_blocks.append(db_j)
            dq_inter = _dot(dob, st.astype(BF16)) * tr["eb"]
            dkt = _dot(vb, dstb)
            dk_inter = dkt * tr["ktscale"]
            extra = (jnp.sum(dkt * kt, axis=0, keepdims=True)
                     + tr["e_last"] * jnp.sum(dst * st, axis=0, keepdims=True))
            dq_scr[rows, lanes] = jnp.concatenate(dq_blocks, axis=0) + dq_inter
            dk_scr[rows, lanes] = jnp.concatenate(dk_blocks, axis=0) + dk_inter
            db_scr[rows, lanes] = jnp.concatenate(db_blocks, axis=0) + qc * dq_inter - kc * dk_inter
            last = c * CHUNK + CHUNK - 1
            db_scr[last:last + 1, lanes] += extra
            dsts[h] = dst * tr["e_last"] + _dot_tn(dob, (qc * tr["eb"]).astype(BF16))
        for h in range(HEADS_PER_STEP):
            ds_scr[h] = dsts[h]

        dlogf = _block_cumsum(db_scr[...], tri_scr[1])
        df = dlogf / f - dk_scr[...]
        dp_ref[0] = (dq_scr[...] * _silu_grad(zq, sq)).astype(BF16)
        dp_ref[1] = (df * (1.0 - lb) * s * (1.0 - s)).astype(BF16)
        dp_ref[2] = dv_scr[...].astype(BF16)
        dlb_scr[...] += jnp.sum(df * s_neg, axis=0, keepdims=True)

        @pl.when(i == nt - 1)
        def _():
            dlogit = dlb_scr[...] * lb * (1.0 - lb)
            dlg_ref[0:1, :] = dlogit
            dlg_ref[1:2, :] = -dlogit

    rev = lambda i: nt - 1 - i
    tile = pl.BlockSpec((tm, GROUP), lambda g, i: (rev(i), g))
    return _call(
        body, name="hgrn_bwd", grid=(HEAD_GROUPS, nt),
        in_specs=[tile, tile,
                  pl.BlockSpec((HEADS_PER_STEP, nc, HEAD_DIM, HEAD_DIM), lambda g, i: (g, rev(i), 0, 0))]
        + _hgrn_specs(tm, rev) + [ANY],
        out_specs=[pl.BlockSpec((4, tm, GROUP), lambda g, i: (0, rev(i), g)),
                   pl.BlockSpec((2, GROUP), lambda g, i: (0, g)),
                   pl.BlockSpec((1, GROUP), lambda g, i: (0, g))],
        out_shape=[_sds(dstack.shape, BF16), _sds((2, HGRN_DIM), F32), _sds((1, HGRN_DIM), F32)],
        aliases={9: 0},
        scratch=[pltpu.VMEM((HEADS_PER_STEP, HEAD_DIM, HEAD_DIM), F32)] + [pltpu.VMEM((tm, GROUP), F32)] * 8
        + [pltpu.VMEM((1, GROUP), F32), pltpu.VMEM((2, tm, tm), BF16)],
    )(dog, o, states, proj, proj, proj, proj, logits, ng, dstack)


def _merge_fwd(cs, og, proj, x, wco, wh, wo, g1, b1):
    t = x.shape[0]
    tm = 256

    def body(cs_ref, og_ref, m0_ref, m1_ref, x_ref, wco_ref, wh_ref, wo_ref, g_ref, b_ref,
             y_ref, mixed_ref, r1_ref, x1_ref, x1b_ref):
        yc = _dot(cs_ref[...], wco_ref[...])
        yh = _dot(og_ref[...], wh_ref[...])
        y_ref[0] = yc
        y_ref[1] = yh
        mixed = (_sigmoid(m0_ref[...]) * yc + _sigmoid(m1_ref[...]) * yh).astype(BF16)
        mixed_ref[...] = mixed
        r1 = ALPHA * x_ref[...] + _dot(mixed, wo_ref[...])
        r1_ref[...] = r1
        xhat, _ = _ln_stats(r1)
        x1 = xhat * g_ref[...] + b_ref[...]
        x1_ref[...] = x1
        x1b_ref[...] = x1.astype(BF16)

    row = lambda w: pl.BlockSpec((tm, w), lambda i: (i, 0))
    full = lambda a: pl.BlockSpec(a.shape, lambda i: (0, 0))
    return _call(
        body, name="merge_fwd", grid=(t // tm,),
        in_specs=[row(CONV_DIM), row(HGRN_DIM),
                  pl.BlockSpec((tm, D_MODEL), lambda i: (i, 5)), pl.BlockSpec((tm, D_MODEL), lambda i: (i, 6)),
                  row(D_MODEL), full(wco), full(wh), full(wo), full(g1), full(b1)],
        out_specs=[pl.BlockSpec((2, tm, D_MODEL), lambda i: (0, i, 0)), row(D_MODEL), row(D_MODEL),
                   row(D_MODEL), row(D_MODEL)],
        out_shape=[_sds((2, t, D_MODEL), F32), _sds((t, D_MODEL), BF16), _sds((t, D_MODEL), F32),
                   _sds((t, D_MODEL), F32), _sds((t, D_MODEL), BF16)],
        vmem_mb=48)(cs, og, proj, proj, x, wco, wh, wo, g1, b1)


def _merge_bwd(dr1b, ycat, proj, wo, wco, wh):
    t = dr1b.shape[0]
    tm = 256

    def body(dr_ref, y_ref, m0_ref, m1_ref, wo_ref, wco_ref, wh_ref, dpm_ref, dy_ref, dcs_ref, dog_ref):
        dmixed = _dot_nt(dr_ref[...], wo_ref[...])
        g0 = _sigmoid(m0_ref[...])
        g1 = _sigmoid(m1_ref[...])
        dpm_ref[0] = (dmixed * y_ref[0] * g0 * (1.0 - g0)).astype(BF16)
        dpm_ref[1] = (dmixed * y_ref[1] * g1 * (1.0 - g1)).astype(BF16)
        dyc = (dmixed * g0).astype(BF16)
        dyh = (dmixed * g1).astype(BF16)
        dy_ref[0] = dyc
        dy_ref[1] = dyh
        dcs_ref[...] = _dot_nt(dyc, wco_ref[...])
        dog_ref[...] = _dot_nt(dyh, wh_ref[...])

    row = lambda w: pl.BlockSpec((tm, w), lambda i: (i, 0))
    pair = pl.BlockSpec((2, tm, D_MODEL), lambda i: (0, i, 0))
    full = lambda a: pl.BlockSpec(a.shape, lambda i: (0, 0))
    return _call(
        body, name="merge_bwd", grid=(t // tm,),
        in_specs=[row(D_MODEL), pair,
                  pl.BlockSpec((tm, D_MODEL), lambda i: (i, 5)), pl.BlockSpec((tm, D_MODEL), lambda i: (i, 6)),
                  full(wo), full(wco), full(wh)],
        out_specs=[pl.BlockSpec((2, tm, D_MODEL), lambda i: (DPROJ_GATE_SLAB // 2, i, 0)), pair,
                   row(CONV_DIM), row(HGRN_DIM)],
        out_shape=[_sds((DPROJ_SLABS, t, D_MODEL), BF16), _sds((2, t, D_MODEL), BF16),
                   _sds((t, CONV_DIM), F32), _sds((t, HGRN_DIM), F32)],
        vmem_mb=48)(dr1b, ycat, proj, proj, wo, wco, wh)


def _ffn_taps(win):
    return (pltpu.roll(win, 2, 0)[FFN_HALO:, :], pltpu.roll(win, 1, 0)[FFN_HALO:, :], win[FFN_HALO:, :])


def _ffn_conv3(taps, w_ref):
    return w_ref[0:1, :] * taps[0] + w_ref[1:2, :] * taps[1] + w_ref[2:3, :] * taps[2]


def _ffn_mid(z, wfd, bfd):
    t = z.shape[0]
    tm = 256

    def body(u_ref, gv_ref, w_ref, b_ref, h_ref, ext_ref):
        i = pl.program_id(0)

        @pl.when(i == 0)
        def _():
            ext_ref[0:FFN_HALO, :] = jnp.zeros((FFN_HALO, D_FF), F32)

        @pl.when(i > 0)
        def _():
            ext_ref[0:FFN_HALO, :] = ext_ref[tm:tm + FFN_HALO, :]

        ext_ref[FFN_HALO:FFN_HALO + tm, :] = u_ref[...]

        def block(r, carry):
            r0 = pl.multiple_of(r * ROW_BLOCK, ROW_BLOCK)
            rows = pl.ds(r0, ROW_BLOCK)
            win = ext_ref[pl.ds(r0, ROW_BLOCK + FFN_HALO), :]
            uc = _ffn_conv3(_ffn_taps(win), w_ref) + b_ref[...]
            g, _ = _gelu_and_grad(uc)
            h_ref[rows, :] = (g * gv_ref[rows, :]).astype(BF16)
            return carry

        lax.fori_loop(0, tm // ROW_BLOCK, block, 0)

    return _call(
        body, name="ffn_mid", grid=(t // tm,),
        in_specs=[pl.BlockSpec((tm, D_FF), lambda i: (i, 0)), pl.BlockSpec((tm, D_FF), lambda i: (i, 1)),
                  pl.BlockSpec((FFN_KERNEL, D_FF), lambda i: (0, 0)), pl.BlockSpec((1, D_FF), lambda i: (0, 0))],
        out_specs=pl.BlockSpec((tm, D_FF), lambda i: (i, 0)),
        out_shape=_sds((t, D_FF), BF16),
        scratch=[pltpu.VMEM((tm + FFN_HALO, D_FF), F32)], vmem_mb=40)(z, z, wfd, bfd)


def _ffn_out_loss(hmid, x1, target, wfo, g2, b2):
    t = x1.shape[0]
    tm = 256
    inv_n = 1.0 / D_MODEL

    def body(h_ref, x1_ref, tg_ref, w_ref, g_ref, b_ref, dr_ref, drb_ref, loss_ref, dg_ref, db_ref):
        @pl.when(pl.program_id(0) == 0)
        def _():
            loss_ref[...] = jnp.zeros_like(loss_ref)
            dg_ref[...] = jnp.zeros_like(dg_ref)
            db_ref[...] = jnp.zeros_like(db_ref)

        r2 = ALPHA * x1_ref[...] + _dot(h_ref[...], w_ref[...])
        xhat, rstd = _ln_stats(r2)
        err = xhat * g_ref[...] + b_ref[...] - tg_ref[...]
        loss_ref[...] += 0.5 * inv_n * jnp.sum(err * err)
        dy = err * inv_n
        dg_ref[...] += jnp.sum(dy * xhat, axis=0, keepdims=True)
        db_ref[...] += jnp.sum(dy, axis=0, keepdims=True)
        dr = _ln_bwd(dy, xhat, rstd, g_ref[...])
        dr_ref[...] = dr
        drb_ref[...] = dr.astype(BF16)

    row = lambda w: pl.BlockSpec((tm, w), lambda i: (i, 0))
    vec = pl.BlockSpec((1, D_MODEL), lambda i: (0, 0))
    return _call(
        body, name="ffn_out_loss", grid=(t // tm,),
        in_specs=[row(D_FF), row(D_MODEL), row(D_MODEL), pl.BlockSpec((D_FF, D_MODEL), lambda i: (0, 0)), vec, vec],
        out_specs=[row(D_MODEL), row(D_MODEL), pl.BlockSpec((1, 128), lambda i: (0, 0)), vec, vec],
        out_shape=[_sds((t, D_MODEL), F32), _sds((t, D_MODEL), BF16), _sds((1, 128), F32),
                   _sds((1, D_MODEL), F32), _sds((1, D_MODEL), F32)],
        vmem_mb=40)(hmid, x1, target, wfo, g2, b2)


def _ffn_bwd_a(dr2b, z, wfo, wfd, bfd):
    t = z.shape[0]
    tm = 256
    nt = t // tm

    def body(dr_ref, u_ref, gv_ref, wfo_ref, w_ref, b_ref, dgv_ref, duc_ref, dw_ref, db_ref,
             ext_ref, dh_ref, acc_ref):
        i = pl.program_id(0)

        @pl.when(i == 0)
        def _():
            ext_ref[0:FFN_HALO, :] = jnp.zeros((FFN_HALO, D_FF), F32)
            acc_ref[...] = jnp.zeros_like(acc_ref)

        @pl.when(i > 0)
        def _():
            ext_ref[0:FFN_HALO, :] = ext_ref[tm:tm + FFN_HALO, :]

        ext_ref[FFN_HALO:FFN_HALO + tm, :] = u_ref[...]
        dh_ref[...] = _dot_nt(dr_ref[...], wfo_ref[...])

        def block(r, carry):
            r0 = pl.multiple_of(r * ROW_BLOCK, ROW_BLOCK)
            rows = pl.ds(r0, ROW_BLOCK)
            taps = _ffn_taps(ext_ref[pl.ds(r0, ROW_BLOCK + FFN_HALO), :])
            uc = _ffn_conv3(taps, w_ref) + b_ref[...]
            g, dg = _gelu_and_grad(uc)
            dh = dh_ref[rows, :]
            dgv_ref[rows, :] = (dh * g).astype(BF16)
            duc = dh * gv_ref[rows, :] * dg
            duc_ref[rows, :] = duc
            acc_ref[0:8, :] += _fold8(duc)
            for k in range(FFN_KERNEL):
                acc_ref[8 + 8 * k:16 + 8 * k, :] += _fold8(duc * taps[k])
            return carry

        lax.fori_loop(0, tm // ROW_BLOCK, block, 0)

        @pl.when(i == nt - 1)
        def _():
            db_ref[...] = jnp.sum(acc_ref[0:8, :], axis=0, keepdims=True)
            for k in range(FFN_KERNEL):
                dw_ref[k:k + 1, :] = jnp.sum(acc_ref[8 + 8 * k:16 + 8 * k, :], axis=0, keepdims=True)

    tile = pl.BlockSpec((tm, D_FF), lambda i: (i, 0))
    return _call(
        body, name="ffn_bwd_a", grid=(nt,),
        in_specs=[pl.BlockSpec((tm, D_MODEL), lambda i: (i, 0)), tile, pl.BlockSpec((tm, D_FF), lambda i: (i, 1)),
                  pl.BlockSpec((D_FF, D_MODEL), lambda i: (0, 0)),
                  pl.BlockSpec((FFN_KERNEL, D_FF), lambda i: (0, 0)), pl.BlockSpec((1, D_FF), lambda i: (0, 0))],
        out_specs=[tile, tile, pl.BlockSpec((FFN_KERNEL, D_FF), lambda i: (0, 0)),
                   pl.BlockSpec((1, D_FF), lambda i: (0, 0))],
        out_shape=[_sds((t, D_FF), BF16), _sds((t, D_FF), F32), _sds((FFN_KERNEL, D_FF), F32), _sds((1, D_FF), F32)],
        scratch=[pltpu.VMEM((tm + FFN_HALO, D_FF), F32), pltpu.VMEM((tm, D_FF), F32),
                 pltpu.VMEM((8 + 8 * FFN_KERNEL, D_FF), F32)],
        vmem_mb=56)(dr2b, z, z, wfo, wfd, bfd)


def _ffn_bwd_b(duc, wfd):
    t = duc.shape[0]
    tm = 256
    nt = t // tm

    def body(duc_ref, w_ref, du_ref, ext_ref):
        i = pl.program_id(0)

        @pl.when(i == 0)
        def _():
            ext_ref[tm:tm + FFN_HALO, :] = jnp.zeros((FFN_HALO, D_FF), F32)

        @pl.when(i > 0)
        def _():
            ext_ref[tm:tm + FFN_HALO, :] = ext_ref[0:FFN_HALO, :]

        ext_ref[0:tm, :] = duc_ref[...]

        def block(r, carry):
            r0 = pl.multiple_of(r * ROW_BLOCK, ROW_BLOCK)
            win = ext_ref[pl.ds(r0, ROW_BLOCK + FFN_HALO), :]
            n = ROW_BLOCK + FFN_HALO
            du = (w_ref[2:3, :] * win[0:ROW_BLOCK, :] + w_ref[1:2, :] * pltpu.roll(win, n - 1, 0)[0:ROW_BLOCK, :]
                  + w_ref[0:1, :] * pltpu.roll(win, n - 2, 0)[0:ROW_BLOCK, :])
            du_ref[pl.ds(r0, ROW_BLOCK), :] = du.astype(BF16)
            return carry

        lax.fori_loop(0, tm // ROW_BLOCK, block, 0)

    rev = lambda i: (nt - 1 - i, 0)
    return _call(
        body, name="ffn_bwd_b", grid=(nt,),
        in_specs=[pl.BlockSpec((tm, D_FF), rev), pl.BlockSpec((FFN_KERNEL, D_FF), lambda i: (0, 0))],
        out_specs=pl.BlockSpec((tm, D_FF), rev),
        out_shape=_sds((t, D_FF), BF16),
        scratch=[pltpu.VMEM((tm + FFN_HALO, D_FF), F32)], vmem_mb=40)(duc, wfd)


def _ffn_in_bwd(dr2, dub, dgvb, wfi, r1, g1):
    t = dr2.shape[0]
    tm = 256

    def body(dr2_ref, du_ref, dgv_ref, wu_ref, wg_ref, r1_ref, g_ref, dr1_ref, dr1b_ref, dg_ref, db_ref):
        @pl.when(pl.program_id(0) == 0)
        def _():
            dg_ref[...] = jnp.zeros_like(dg_ref)
            db_ref[...] = jnp.zeros_like(db_ref)

        dx1 = ALPHA * dr2_ref[...] + _dot_nt(du_ref[...], wu_ref[...]) + _dot_nt(dgv_ref[...], wg_ref[...])
        xhat, rstd = _ln_stats(r1_ref[...])
        dg_ref[...] += jnp.sum(dx1 * xhat, axis=0, keepdims=True)
        db_ref[...] += jnp.sum(dx1, axis=0, keepdims=True)
        dr1 = _ln_bwd(dx1, xhat, rstd, g_ref[...])
        dr1_ref[...] = dr1
        dr1b_ref[...] = dr1.astype(BF16)

    row = lambda w: pl.BlockSpec((tm, w), lambda i: (i, 0))
    vec = pl.BlockSpec((1, D_MODEL), lambda i: (0, 0))
    return _call(
        body, name="ffn_in_bwd", grid=(t // tm,),
        in_specs=[row(D_MODEL), row(D_FF), row(D_FF),
                  pl.BlockSpec((D_MODEL, D_FF), lambda i: (0, 0)), pl.BlockSpec((D_MODEL, D_FF), lambda i: (0, 1)),
                  row(D_MODEL), vec],
        out_specs=[row(D_MODEL), row(D_MODEL), vec, vec],
        out_shape=[_sds((t, D_MODEL), F32), _sds((t, D_MODEL), BF16), _sds((1, D_MODEL), F32), _sds((1, D_MODEL), F32)],
        vmem_mb=56)(dr2, dub, dgvb, wfi, wfi, r1, g1)


def _local_step(x, target, wi, late_weights, early_reduce, late_reduce, wcd, bcd, clg, clb, logits, ng, g1, b1,
                wfd, bfd, g2, b2):
    proj, xb = _proj(x, wi)
    cc, cs = _conv_fwd(proj, wcd, bcd, clg, clb)
    o, og, states = _hgrn_fwd(proj, logits, ng)
    wco, wh, wo, wfi, wfo = late_weights(og)
    ycat, mixed, r1, x1, x1b = _merge_fwd(cs, og, proj, x, wco, wh, wo, g1, b1)
    z = _mm_nn(x1b, wfi, tm=1024, tn=1408, name="ffn_in", vmem_mb=48)
    hmid = _ffn_mid(z, wfd, bfd)
    dr2, dr2b, loss, d_g2, d_b2 = _ffn_out_loss(hmid, x1, target, wfo, g2, b2)

    g_wfo = _mm_tn(hmid, _views(dr2b), tn=512, tt=1024, name="grad_w_ffn_out", vmem_mb=48)
    dgvb, duc, d_wfd, d_bfd = _ffn_bwd_a(dr2b, z, wfo, wfd, bfd)
    dub = _ffn_bwd_b(duc, wfd)
    g_wfi = _mm_tn(x1b, _views(dub, dgvb), tn=1408, tt=1024, name="grad_w_ffn_in", vmem_mb=48)
    dr1, dr1b, d_g1, d_b1 = _ffn_in_bwd(dr2, dub, dgvb, wfi, r1, g1)

    g_wo = _mm_tn(mixed, _views(dr1b), tn=1024, tt=1024, name="grad_w_out")
    dstack, dyb, dcs, dog = _merge_bwd(dr1b, ycat, proj, wo, wco, wh)
    g_wco = _mm_tn(cs, [(dyb, 0)], tn=1024, tt=1024, name="grad_w_conv_out")
    g_wh = _mm_tn(og, [(dyb, 1)], tn=1024, tt=1024, name="grad_w_hgrn_out")

    dcs = late_reduce.begin((g_wco, g_wh, g_wo, g_wfi, g_wfo), dcs)
    dcc, d_wcd, d_bcd, d_clg, d_clb = _conv_bwd_a(dcs, cc, proj, clg, clb)
    dcc = late_reduce.across_chips(dcc)
    dstack = _conv_bwd_b(dcc, proj, wcd, dstack)
    dstack, d_logits, d_ng = _hgrn_bwd(dog, o, states, proj, logits, ng, dstack)
    dstack = late_reduce.assemble(d_ng, dstack)

    g_wi = _grad_w_in(xb, dstack)
    dstack = early_reduce.begin((g_wi,), dstack)
    grad_x = _grad_x(dstack, wi, dr1, 0)
    grad_x = early_reduce.across_chips(grad_x)
    grad_x = _grad_x(dstack, wi, dr1, 1, grad_x)
    grad_x = early_reduce.assemble(grad_x, grad_x)
    early = early_reduce.finish(grad_x)
    late = late_reduce.finish(grad_x)

    small = dict(w_conv_dw=d_wcd, b_conv_dw=d_bcd, conv_ln_g=d_clg, conv_ln_b=d_clb, hgrn_lb_logits=d_logits,
                 hgrn_norm_g=d_ng, ln1_g=d_g1, ln1_b=d_b1, w_ffn_dw=d_wfd, b_ffn_dw=d_bfd, ln2_g=d_g2, ln2_b=d_b2)
    return loss, grad_x, early, late, small


ELEMENTWISE_BLOCK_ELEMS = 256 * 1024


def _row_tile(rows, cols):
    cap = max(16, ELEMENTWISE_BLOCK_ELEMS // cols)
    if rows <= cap:
        return rows
    best = None
    for cand in range(16, cap + 1, 16):
        if rows % cand == 0:
            best = cand
    assert best is not None
    return best


def _elementwise(fn, ins, out_dtypes, *, name):
    r, c = ins[0].shape
    tr = _row_tile(r, c)

    def body(*refs):
        outs = fn(*[ref[...] for ref in refs[:len(ins)]])
        for ref, val in zip(refs[len(ins):], outs):
            ref[...] = val.astype(ref.dtype)

    spec = pl.BlockSpec((tr, c), lambda i: (i, 0))
    return _call(
        body, name=name, grid=(r // tr,), in_specs=[spec] * len(ins), out_specs=[spec] * len(out_dtypes),
        out_shape=[_sds((r, c), dt) for dt in out_dtypes])(*ins)


def _windowed(fn, sel, ins, outs, *, window, name):
    rows, cols = window
    tr = _row_tile(rows, cols)
    steps = rows // tr

    def spec(where):
        if where is None:
            return pl.BlockSpec((tr, cols), lambda i, s: (i, 0))
        kind, p, _ = where
        if kind == "rows":
            return pl.BlockSpec((tr, cols), lambda i, s: (s[p] * steps + i, 0))
        return pl.BlockSpec((tr, cols), lambda i, s: (i, s[p]))

    n_in = len(ins)

    def body(s_ref, *refs):
        vals = fn(*[ref[...] for ref in refs[:n_in]])
        for ref, val in zip(refs[n_in:], vals):
            ref[...] = val.astype(ref.dtype)

    return pl.pallas_call(
        body, name=name, out_shape=[_hbm(shape, dt) for shape, dt, _ in outs],
        grid_spec=pltpu.PrefetchScalarGridSpec(
            num_scalar_prefetch=1, grid=(steps,), in_specs=[spec(where) for _, where in ins],
            out_specs=[spec(where) for _, _, where in outs]),
        compiler_params=pltpu.CompilerParams(dimension_semantics=("arbitrary",), vmem_limit_bytes=32 * 2 ** 20),
    )(sel, *[_in_hbm(a) for a, _ in ins])


def _adamw(w, g, m, v, *, name):
    def fn(w_, g_, m_, v_):
        m_new = ADAM_B1 * m_ + (1.0 - ADAM_B1) * g_
        v_new = ADAM_B2 * v_ + (1.0 - ADAM_B2) * (g_ * g_)
        m_hat = m_new / ADAM_BC1
        v_hat = v_new / ADAM_BC2
        delta = -ADAM_LR * (m_hat / (jnp.sqrt(v_hat) + ADAM_EPS) + ADAM_WD * w_)
        return delta, m_new, v_new

    return _elementwise(fn, [w, g, m, v], [F32, F32, F32], name=name)


def _place():
    return lax.axis_index("x"), lax.axis_index("y"), lax.axis_index("c")


def _other_chips(x, y):
    return [(1 - x, y), (x, 1 - y), (1 - x, 1 - y)]


SHARD_XOR = (2, 1, 3)


DMA_CHUNK_BYTES = 512 * 1024


def _n_chunks(ref):
    rows = ref.shape[0]
    total = ref.dtype.itemsize
    for d in ref.shape:
        total *= d
    best = 1
    for cand in range(2, min(rows, total // DMA_CHUNK_BYTES) + 1):
        if rows % cand == 0 and (rows // cand) % 16 == 0:
            best = cand
    return best


class _Copy:
    def __init__(self, src, dst, sems, dev=None):
        if dev is None:
            make = lambda s_, d_: pltpu.make_async_copy(s_, d_, sems[0])
        else:
            make = lambda s_, d_: pltpu.make_async_remote_copy(
                src_ref=s_, dst_ref=d_, send_sem=sems[0], recv_sem=sems[1], device_id=dev, device_id_type=MESH)
        self.local = dev is None
        self.whole = make(src, dst)
        n = _n_chunks(src)
        step = src.shape[0] // n
        self.parts = ([self.whole] if n == 1 else
                      [make(src.at[pl.ds(i * step, step)], dst.at[pl.ds(i * step, step)]) for i in range(n)])

    def start(self):
        for part in self.parts:
            part.start()

    def wait_recv(self):
        self.whole.wait_recv()

    def wait_send(self):
        self.whole.wait_send()

    def wait(self):
        self.whole.wait()


def _run_copies(local_ops, remote_ops, lsem, ssem, rsem):
    local = [_Copy(src, dst, (lsem.at[n],)) for n, (src, dst) in enumerate(local_ops)]
    remote = [_Copy(src, dst, (ssem.at[n], rsem.at[n]), dev) for n, (src, dst, dev) in enumerate(remote_ops)]
    for cp in local + remote:
        cp.start()
    for cp in remote:
        cp.wait_recv()
    for cp in remote:
        cp.wait_send()
    for cp in local:
        cp.wait()


BIG = (("w_in", D_MODEL, IN_COLS, 1), ("w_conv_out", CONV_DIM, D_MODEL, 1), ("w_hgrn_out", HGRN_DIM, D_MODEL, 0),
       ("w_out", D_MODEL, D_MODEL, 0), ("w_ffn_in", D_MODEL, 2 * D_FF, 1), ("w_ffn_out", D_FF, D_MODEL, 0))


def _shard_slice(ref, rows, cols, axis, k):
    if axis == 1:
        w = cols // N_CHIPS
        return ref.at[:, pl.ds(k * w, w)]
    h = rows // N_CHIPS
    return ref.at[pl.ds(k * h, h), :]


def _half_slice(ref, rows, cols, axis, hc):
    if axis == 1:
        return ref.at[pl.ds(hc * (rows // 2), rows // 2), :]
    return ref.at[:, pl.ds(hc * (cols // 2), cols // 2)]


def _half_shape(rows, cols, axis):
    return (rows // 2, cols) if axis == 1 else (rows, cols // 2)


def _gather_weights(full, small):
    n_big, n_small = len(full), len(small)
    n_arr = n_big + n_small
    out_shape = ([_sds((r, c), BF16) for _, r, c, _ in BIG[:n_big]]
                 + [_sds((N_CHIPS,) + a.shape, F32) for a in small])
    shard_shape = [(r, c // N_CHIPS) if ax == 1 else (r // N_CHIPS, c) for _, r, c, ax in BIG]

    def body(*refs):
        ins, outs = refs[:n_arr], refs[n_arr:2 * n_arr]
        lsem, ssem, rsem, fsem_s, fsem_r = refs[2 * n_arr:]
        x, y, c = _place()
        me = 2 * x + y
        chips = _other_chips(x, y)
        sibling = (x, y, 1 - c)

        def region(idx, k, hc, bufs=outs):
            (_, r, cc, ax), (sr, _) = BIG[idx], shard_shape[idx]
            return _shard_slice(bufs[idx], r, cc, ax, k).at[pl.ds(hc * (sr // 2), sr // 2)]

        for k in range(N_CHIPS):
            for hc in range(2):
                @pl.when((me == k) & (c == hc))
                def _(k=k, hc=hc):
                    local = [_Copy(ins[n_big + i], outs[n_big + i].at[k], (lsem.at[i],)) for i in range(n_small)]
                    sends, fwds = [], []
                    for j, (cx, cy) in enumerate(chips):
                        for i in range(n_big):
                            n = j * n_arr + i
                            sends.append(_Copy(region(i, k, hc, ins), region(i, k, hc), (ssem.at[n], rsem.at[n]),
                                               (cx, cy, c)))
                            reg = region(i, k ^ SHARD_XOR[j], hc)
                            fwds.append(_Copy(reg, reg, (fsem_s.at[j * n_big + i], fsem_r.at[j * n_big + i]), sibling))
                        for i in range(n_small):
                            n = j * n_arr + n_big + i
                            sends.append(_Copy(ins[n_big + i], outs[n_big + i].at[k], (ssem.at[n], rsem.at[n]),
                                               (cx, cy, c)))
                    for cp in local + sends:
                        cp.start()
                    for j in range(3):
                        for i in range(n_big):
                            sends[j * n_arr + i].wait_recv()
                        for i in range(n_big):
                            fwds[j * n_big + i].start()
                    for j in range(3):
                        for i in range(n_small):
                            sends[j * n_arr + n_big + i].wait_recv()
                    for cp in fwds:
                        cp.wait_recv()
                    for cp in sends + fwds:
                        cp.wait_send()
                    for cp in local:
                        cp.wait()

    return pl.pallas_call(
        body, name="gather_weights", in_specs=[ANY] * n_arr, out_specs=[ANY] * n_arr, out_shape=out_shape,
        input_output_aliases={i: i for i in range(n_big)},
        scratch_shapes=[pltpu.SemaphoreType.DMA((n_small,)), pltpu.SemaphoreType.DMA((3 * n_arr,)),
                        pltpu.SemaphoreType.DMA((3 * n_arr,)), pltpu.SemaphoreType.DMA((3 * n_big,)),
                        pltpu.SemaphoreType.DMA((3 * n_big,))])(*full, *small)


SEM = pl.BlockSpec(memory_space=pltpu.SEMAPHORE)
N_EARLY = 1
LATE = BIG[N_EARLY:]


def _late_copies(src_bufs, dst_bufs, ssem, rsem, x, y, c, k):
    copies = []
    for j, (cx, cy) in enumerate(_other_chips(x, y)):
        for i, (src, dst, (_, r, cc, ax)) in enumerate(zip(src_bufs, dst_bufs, LATE)):
            n = j * len(LATE) + i
            copies.append(_Copy(_shard_slice(src, r, cc, ax, k), _shard_slice(dst, r, cc, ax, k),
                                (ssem.at[n], rsem.at[n]), (cx, cy, c)))
    return copies


def _split_sibling_copies(specs, src_bufs, dst_bufs, ssem, rsem, x, y, c, k):
    n = len(specs)
    return [_Copy(_half_slice(src_bufs[i], r, cc, ax, 1 - k), dst_bufs[n + i], (ssem.at[i], rsem.at[i]), (x, y, 1 - c))
            for i, (_, r, cc, ax) in enumerate(specs)]


def _split_chip_copies(specs, src_bufs, dst_bufs, ssem, rsem, x, y, c, k):
    n = len(specs)
    copies = []
    for j, (cx, cy) in enumerate(_other_chips(x, y)):
        for i, (_, r, cc, ax) in enumerate(specs):
            hr, hc = _half_shape(r, cc, ax)
            copies.append(_Copy(_shard_slice(src_bufs[i], hr, hc, ax, k ^ SHARD_XOR[j]), dst_bufs[n + 3 * i + j],
                                (ssem.at[j * n + i], rsem.at[j * n + i]), (cx, cy, c)))
    return copies


def _split_assemble_copies(specs, src_bufs, dst_bufs, ssem, rsem, x, y, c, k):
    copies = []
    for i, (_, r, cc, ax) in enumerate(specs):
        sr, sc = (r, cc // N_CHIPS) if ax == 1 else (r // N_CHIPS, cc)
        copies.append(_Copy(_half_slice(src_bufs[i], sr, sc, ax, k), _half_slice(dst_bufs[i], sr, sc, ax, k),
                            (ssem.at[i], rsem.at[i]), (x, y, 1 - c)))
    return copies


SIDE_EFFECT = pltpu.CompilerParams(has_side_effects=pltpu.SideEffectType.DATAFLOW_SIDE_EFFECTING)


def _split_start(name, bufs, carry, make_copies, n_copies, key_of, n_keys):
    n = len(bufs)

    def body(*refs):
        ins, (ssem, rsem), outs = refs[:n], refs[n + 1:n + 3], refs[n + 3:2 * n + 3]
        x, y, c = _place()
        key = key_of(x, y, c)
        for k in range(n_keys):
            @pl.when(key == k)
            def _(k=k):
                for cp in make_copies(ins, outs, ssem, rsem, x, y, c, k):
                    cp.start()

    arrays = list(bufs) + [carry]
    outs = pl.pallas_call(
        body, name=name, in_specs=[ANY] * (n + 1), out_specs=[SEM, SEM] + [ANY] * (n + 1),
        out_shape=[pltpu.SemaphoreType.DMA((n_copies,)), pltpu.SemaphoreType.DMA((n_copies,))]
        + [_hbm(a.shape, a.dtype) for a in arrays],
        input_output_aliases={i: 2 + i for i in range(n + 1)}, compiler_params=SIDE_EFFECT,
    )(*[_in_hbm(a) for a in arrays])
    return (outs[0], outs[1]), list(outs[2:2 + n]), outs[2 + n]


def _split_wait(name, sems, bufs, after, make_copies):
    n = len(bufs)

    def body(*refs):
        ins, ssem, rsem = refs[:n], refs[n], refs[n + 1]
        x, y, c = _place()
        for cp in make_copies(ins, ins, ssem, rsem, x, y, c, 0):
            cp.wait_send()
            cp.wait_recv()

    return pl.pallas_call(
        body, name=name, in_specs=[ANY] * n + [SEM, SEM, ANY], out_specs=[ANY] * n,
        out_shape=[_hbm(a.shape, a.dtype) for a in bufs], input_output_aliases={i: i for i in range(n)},
        compiler_params=SIDE_EFFECT,
    )(*[_in_hbm(a) for a in bufs], *sems, after)


def _chip_of(x, y, c):
    return 2 * x + y


def _core_of(x, y, c):
    return c


def _piece_shape(rows, cols, axis):
    hr, hc = _half_shape(rows, cols, axis)
    return (hr, hc // N_CHIPS) if axis == 1 else (hr // N_CHIPS, hc)


def _sharded(axis, p, n):
    return ("cols", p, n) if axis == 1 else ("rows", p, n)


def _across(axis, p, n):
    return ("rows", p, n) if axis == 1 else ("cols", p, n)


def _shard_shape(rows, cols, axis):
    return (rows, cols // N_CHIPS) if axis == 1 else (rows // N_CHIPS, cols)


def _chip_sum(spec, sel, grad, got):
    name, r, c, ax = spec
    return _windowed(lambda a, b: (a + b,), sel, [(grad, _across(ax, 1, 2)), (got, None)],
                     [(_half_shape(r, c, ax), BF16, None)], window=_half_shape(r, c, ax), name="chip_sum_" + name)[0]


def _shard_sum(spec, sel, chip_sum, received):
    name, r, c, ax = spec
    add4 = lambda a, b0, b1, b2: (a.astype(F32) + b0.astype(F32) + b1.astype(F32) + b2.astype(F32),)
    return _windowed(add4, sel, [(chip_sum, _sharded(ax, 0, N_CHIPS))] + [(r_, None) for r_ in received],
                     [(_shard_shape(r, c, ax), F32, _across(ax, 1, 2))], window=_piece_shape(r, c, ax),
                     name="shard_sum_" + name)[0]


class _SplitReduce:
    def __init__(self, specs, sel, tag):
        self.specs, self.sel, self.tag = specs, sel, tag
        self.sibling = functools.partial(_split_sibling_copies, specs)
        self.chip = functools.partial(_split_chip_copies, specs)
        self.halves = functools.partial(_split_assemble_copies, specs)

    def begin(self, grads, carry):
        landing = [lax.empty(_half_shape(r, c, ax), F32) for _, r, c, ax in self.specs]
        self.sems, self.bufs, carry = _split_start(self.tag + "_sibling_start", list(grads) + landing, carry,
                                                   self.sibling, len(self.specs), _core_of, 2)
        return carry

    def across_chips(self, carry):
        n = len(self.specs)
        bufs = _split_wait(self.tag + "_sibling_wait", self.sems, self.bufs, carry, self.sibling)
        chip_sums = [_chip_sum(spec, self.sel, g_, h_) for spec, g_, h_ in zip(self.specs, bufs[:n], bufs[n:])]
        landing = [lax.empty(_piece_shape(r, c, ax), BF16) for _, r, c, ax in self.specs for _ in range(3)]
        self.sems, self.bufs, carry = _split_start(self.tag + "_chip_start", chip_sums + landing, carry,
                                                   self.chip, 3 * n, _chip_of, N_CHIPS)
        return carry

    def assemble(self, after, carry):
        n = len(self.specs)
        bufs = _split_wait(self.tag + "_chip_wait", self.sems, self.bufs, after, self.chip)
        half_filled = [_shard_sum(spec, self.sel, bufs[i], bufs[n + 3 * i:n + 3 * i + 3])
                       for i, spec in enumerate(self.specs)]
        self.sems, self.bufs, carry = _split_start(self.tag + "_assemble_start", half_filled, carry,
                                                   self.halves, n, _core_of, 2)
        return carry

    def finish(self, after):
        return _split_wait(self.tag + "_assemble_wait", self.sems, self.bufs, after, self.halves)


def _all_reduce_small(packed):
    r, w = packed.shape

    def body(in_ref, out_ref, slots, lsem, ssem, rsem):
        x, y, c = _place()
        me = 4 * x + 2 * y + c
        peers = [(x ^ (m >> 2), y ^ ((m >> 1) & 1), c ^ (m & 1)) for m in range(1, N_DEV)]
        _run_copies([(in_ref, slots.at[me])], [(in_ref, slots.at[me], dev) for dev in peers], lsem, ssem, rsem)
        total = slots[0]
        for d in range(1, N_DEV):
            total = total + slots[d]
        out_ref[...] = total

    vmem = pl.BlockSpec(memory_space=pltpu.VMEM)
    return pl.pallas_call(
        body, name="small_all_reduce", in_specs=[vmem], out_specs=vmem, out_shape=_sds((r, w), F32),
        scratch_shapes=[pltpu.VMEM((N_DEV, r, w), F32), pltpu.SemaphoreType.DMA((1,)),
                        pltpu.SemaphoreType.DMA((N_DEV - 1,)), pltpu.SemaphoreType.DMA((N_DEV - 1,))])(packed)


SMALL_ORDER = ("w_conv_dw", "b_conv_dw", "conv_ln_g", "conv_ln_b", "hgrn_lb_logits", "hgrn_norm_g",
               "ln1_g", "ln1_b", "w_ffn_dw", "b_ffn_dw", "ln2_g", "ln2_b")
REPLICATED_SMALL = tuple(n for n in SMALL_ORDER if n not in ("w_conv_dw", "w_ffn_dw"))
WEIGHT_ORDER = ("w_in", "w_conv_dw", "b_conv_dw", "conv_ln_g", "conv_ln_b", "w_conv_out", "hgrn_lb_logits",
                "hgrn_norm_g", "w_hgrn_out", "w_out", "ln1_g", "ln1_b", "w_ffn_in", "w_ffn_dw", "b_ffn_dw",
                "w_ffn_out", "ln2_g", "ln2_b")


def _pack(arrs):
    flat = jnp.concatenate([a.reshape(-1) for a in arrs])
    assert flat.shape[0] % 128 == 0
    return flat.reshape(-1, 128)


def _unpack(packed, shapes):
    flat = packed.reshape(-1)
    out, pos = [], 0
    for shp in shapes:
        size = 1
        for d in shp:
            size *= d
        out.append(flat[pos:pos + size].reshape(shp))
        pos += size
    return out


def kernel(x, w_in, w_conv_dw, b_conv_dw, conv_ln_g, conv_ln_b, w_conv_out, hgrn_lb_logits, hgrn_norm_g, w_hgrn_out, w_out, ln1_g, ln1_b, w_ffn_in, w_ffn_dw, b_ffn_dw, w_ffn_out, ln2_g, ln2_b, loss_target, m_w_in, m_w_conv_dw, m_b_conv_dw, m_conv_ln_g, m_conv_ln_b, m_w_conv_out, m_hgrn_lb_logits, m_hgrn_norm_g, m_w_hgrn_out, m_w_out, m_ln1_g, m_ln1_b, m_w_ffn_in, m_w_ffn_dw, m_b_ffn_dw, m_w_ffn_out, m_ln2_g, m_ln2_b, v_w_in, v_w_conv_dw, v_b_conv_dw, v_conv_ln_g, v_conv_ln_b, v_w_conv_out, v_hgrn_lb_logits, v_hgrn_norm_g, v_w_hgrn_out, v_w_out, v_ln1_g, v_ln1_b, v_w_ffn_in, v_w_ffn_dw, v_b_ffn_dw, v_w_ffn_out, v_ln2_g, v_ln2_b):
    w = dict(w_in=w_in, w_conv_dw=w_conv_dw, b_conv_dw=b_conv_dw, conv_ln_g=conv_ln_g, conv_ln_b=conv_ln_b,
             w_conv_out=w_conv_out, hgrn_lb_logits=hgrn_lb_logits, hgrn_norm_g=hgrn_norm_g, w_hgrn_out=w_hgrn_out,
             w_out=w_out, ln1_g=ln1_g, ln1_b=ln1_b, w_ffn_in=w_ffn_in, w_ffn_dw=w_ffn_dw, b_ffn_dw=b_ffn_dw,
             w_ffn_out=w_ffn_out, ln2_g=ln2_g, ln2_b=ln2_b)
    m = dict(w_in=m_w_in, w_conv_dw=m_w_conv_dw, b_conv_dw=m_b_conv_dw, conv_ln_g=m_conv_ln_g, conv_ln_b=m_conv_ln_b,
             w_conv_out=m_w_conv_out, hgrn_lb_logits=m_hgrn_lb_logits, hgrn_norm_g=m_hgrn_norm_g,
             w_hgrn_out=m_w_hgrn_out, w_out=m_w_out, ln1_g=m_ln1_g, ln1_b=m_ln1_b, w_ffn_in=m_w_ffn_in,
             w_ffn_dw=m_w_ffn_dw, b_ffn_dw=m_b_ffn_dw, w_ffn_out=m_w_ffn_out, ln2_g=m_ln2_g, ln2_b=m_ln2_b)
    v = dict(w_in=v_w_in, w_conv_dw=v_w_conv_dw, b_conv_dw=v_b_conv_dw, conv_ln_g=v_conv_ln_g, conv_ln_b=v_conv_ln_b,
             w_conv_out=v_w_conv_out, hgrn_lb_logits=v_hgrn_lb_logits, hgrn_norm_g=v_hgrn_norm_g,
             w_hgrn_out=v_w_hgrn_out, w_out=v_w_out, ln1_g=v_ln1_g, ln1_b=v_ln1_b, w_ffn_in=v_w_ffn_in,
             w_ffn_dw=v_w_ffn_dw, b_ffn_dw=v_b_ffn_dw, w_ffn_out=v_w_ffn_out, ln2_g=v_ln2_g, ln2_b=v_ln2_b)
    big_names = [n for n, _, _, _ in BIG]
    w2 = {n: a[0] if a.ndim == 3 else a for n, a in w.items()}
    m2 = {n: a[0] if a.ndim == 3 else a for n, a in m.items()}
    v2 = {n: a[0] if a.ndim == 3 else a for n, a in v.items()}

    sel = jnp.stack([2 * lax.axis_index("x") + lax.axis_index("y"), lax.axis_index("c")]).astype(jnp.int32)

    placed = [_windowed(lambda a: (a,), sel, [(w2[n], None)], [((r, c), BF16, _sharded(ax, 0, N_CHIPS))],
                        window=w2[n].shape, name="cast_" + n)[0] for n, r, c, ax in BIG]
    wi, wcd4, wfd4 = _gather_weights(placed[:N_EARLY], [w2["w_conv_dw"], w2["w_ffn_dw"]])
    wcd = jnp.transpose(wcd4, (1, 0, 2)).reshape(CONV_KERNEL, CONV_DIM)
    wfd = jnp.transpose(wfd4, (1, 0, 2)).reshape(FFN_KERNEL, D_FF)
    gather_sems, in_flight, wi = _split_start("gather_late_start", placed[N_EARLY:], wi, _late_copies,
                                               3 * len(LATE), _chip_of, N_CHIPS)
    late_weights = lambda after: _split_wait("gather_late_wait", gather_sems, in_flight, after, _late_copies)

    loss_part, grad_x, early_shard_grads, late_shard_grads, small_grads = _local_step(
        x[0], loss_target[0], wi, late_weights,
        _SplitReduce(BIG[:N_EARLY], sel, "early"), _SplitReduce(LATE, sel, "late"),
        wcd, w2["b_conv_dw"], w2["conv_ln_g"], w2["conv_ln_b"],
        w2["hgrn_lb_logits"], w2["hgrn_norm_g"], w2["ln1_g"], w2["ln1_b"], wfd, w2["b_ffn_dw"],
        w2["ln2_g"], w2["ln2_b"])
    loss = lax.psum(loss_part[0, 0], ("x", "y", "c"))
    shard_grads = dict(zip(big_names, list(early_shard_grads) + list(late_shard_grads)))

    small_shapes = [small_grads[n].shape for n in SMALL_ORDER]
    reduced = dict(zip(SMALL_ORDER, _unpack(_all_reduce_small(_pack([small_grads[n] for n in SMALL_ORDER])),
                                            small_shapes)))
    shard = 2 * lax.axis_index("x") + lax.axis_index("y")
    grads = dict(shard_grads)
    for n in REPLICATED_SMALL:
        grads[n] = reduced[n]
    grads["w_conv_dw"] = lax.dynamic_slice_in_dim(reduced["w_conv_dw"], shard * (CONV_DIM // N_CHIPS),
                                                  CONV_DIM // N_CHIPS, axis=1)
    grads["w_ffn_dw"] = lax.dynamic_slice_in_dim(reduced["w_ffn_dw"], shard * (D_FF // N_CHIPS),
                                                 D_FF // N_CHIPS, axis=1)

    delta, new_m, new_v = {}, {}, {}
    for n in big_names + ["w_conv_dw", "w_ffn_dw"]:
        delta[n], new_m[n], new_v[n] = _adamw(w2[n], grads[n], m2[n], v2[n], name="adamw_" + n)
    rep_shapes = [w2[n].shape for n in REPLICATED_SMALL]
    packed = _adamw(*[_pack([src[n] for n in REPLICATED_SMALL]) for src in (w2, grads, m2, v2)], name="adamw_small")
    for dst, pk in zip((delta, new_m, new_v), packed):
        for n, a in zip(REPLICATED_SMALL, _unpack(pk, rep_shapes)):
            dst[n] = a

    def shaped(d):
        return [d[n].reshape(w[n].shape) for n in WEIGHT_ORDER]

    return (loss, grad_x[None], *shaped(grads), *shaped(delta), *shaped(new_m), *shaped(new_v))
```

```python
import functools

import jax
import jax.numpy as jnp
from jax import lax
from jax.experimental import pallas as pl
from jax.experimental.pallas import tpu as pltpu

F32 = jnp.float32
BF16 = jnp.bfloat16

D_MODEL = 1024
CONV_DIM = 512
CONV_KERNEL = 31
HGRN_DIM = 1024
HGRN_HEADS = 8
HEAD_DIM = 128
CHUNK = 64
SUB = 16
N_SUB = CHUNK // SUB
D_FF = 2816
FFN_KERNEL = 3
IN_COLS = 7168
LN_EPS = 1e-5
RMS_EPS = 1e-6
ALPHA = 2.0 ** 0.25
GELU_C = 0.7978845608028654
GELU_A = 0.044715

ADAM_LR = 0.001
ADAM_B1 = 0.9
ADAM_B2 = 0.999
ADAM_EPS = 1e-08
ADAM_WD = 0.01
ADAM_STEP = 10
ADAM_BC1 = 1.0 - ADAM_B1 ** ADAM_STEP
ADAM_BC2 = 1.0 - ADAM_B2 ** ADAM_STEP

N_CHIPS = 4
N_DEV = 8
ROW_BLOCK = 32
CONV_HALO = 32
FFN_HALO = 8
MESH = pl.DeviceIdType.MESH
ANY = pl.BlockSpec(memory_space=pl.ANY)


def _dot(a, b):
    return jnp.dot(a, b, preferred_element_type=F32)


def _dot_nt(a, b):
    return lax.dot_general(a, b, (((1,), (1,)), ((), ())), preferred_element_type=F32)


def _dot_tn(a, b):
    return lax.dot_general(a, b, (((0,), (0,)), ((), ())), preferred_element_type=F32)


def _sigmoid(z):
    return jax.nn.sigmoid(z)


def _silu_grad(z, s):
    return s * (1.0 + z * (1.0 - s))


def _gelu_and_grad(u):
    u2 = u * u
    th = jnp.tanh(u * (GELU_C + (GELU_C * GELU_A) * u2))
    half = 0.5 + 0.5 * th
    g = u * half
    dg = half + (0.5 * u) * (1.0 - th * th) * (GELU_C + (3.0 * GELU_C * GELU_A) * u2)
    return g, dg


def _ln_stats(r):
    mu = jnp.mean(r, axis=-1, keepdims=True)
    xc = r - mu
    var = jnp.mean(xc * xc, axis=-1, keepdims=True)
    rstd = lax.rsqrt(var + LN_EPS)
    return xc * rstd, rstd


def _ln_bwd(dy, xhat, rstd, g):
    dxh = dy * g
    m1 = jnp.mean(dxh, axis=-1, keepdims=True)
    m2 = jnp.mean(dxh * xhat, axis=-1, keepdims=True)
    return rstd * (dxh - m1 - xhat * m2)


def _fold8(x):
    acc = x[0:8, :]
    for r in range(8, x.shape[0], 8):
        acc = acc + x[r:r + 8, :]
    return acc


def _in_hbm(a):
    return pltpu.with_memory_space_constraint(a, pltpu.HBM)


def _hbm(shape, dtype):
    return pltpu.HBM(shape, dtype)


def _out_hbm(out_shape):
    if isinstance(out_shape, (list, tuple)):
        return [_hbm(s.shape, s.dtype) for s in out_shape]
    return _hbm(out_shape.shape, out_shape.dtype)


def _call(body, *, name, grid, in_specs, out_specs, out_shape, scratch=(), vmem_mb=32, aliases=None):
    call = pl.pallas_call(
        body, name=name, grid=grid, in_specs=in_specs, out_specs=out_specs, out_shape=_out_hbm(out_shape),
        scratch_shapes=list(scratch), input_output_aliases=aliases or {},
        compiler_params=pltpu.CompilerParams(
            dimension_semantics=("arbitrary",) * len(grid), vmem_limit_bytes=vmem_mb * 2 ** 20))
    return lambda *args: call(*[_in_hbm(a) for a in args])


def _sds(shape, dtype):
    return jax.ShapeDtypeStruct(shape, dtype)


def _proj(x, w):
    t = x.shape[0]
    tm, tn = min(t, 1024), 1024

    def body(x_ref, w_ref, p_ref, xb_ref):
        @pl.when(pl.program_id(1) == 0)
        def _():
            xb_ref[...] = x_ref[...].astype(BF16)
        p_ref[...] = _dot(xb_ref[...], w_ref[...])

    return _call(
        body, name="proj", grid=(t // tm, IN_COLS // tn),
        in_specs=[pl.BlockSpec((tm, D_MODEL), lambda i, j: (i, 0)),
                  pl.BlockSpec((D_MODEL, tn), lambda i, j: (0, j))],
        out_specs=[pl.BlockSpec((tm, tn), lambda i, j: (i, j)),
                   pl.BlockSpec((tm, D_MODEL), lambda i, j: (i, 0))],
        out_shape=[_sds((t, IN_COLS), F32), _sds((t, D_MODEL), BF16)], vmem_mb=48)(x, w)


def _mm_nn(a, w, *, tm, tn, name, vmem_mb=32):
    t, k = a.shape
    n = w.shape[1]
    tm = min(tm, t)

    def body(a_ref, w_ref, o_ref):
        o_ref[...] = _dot(a_ref[...], w_ref[...])

    return _call(
        body, name=name, grid=(t // tm, n // tn),
        in_specs=[pl.BlockSpec((tm, k), lambda i, j: (i, 0)), pl.BlockSpec((k, tn), lambda i, j: (0, j))],
        out_specs=pl.BlockSpec((tm, tn), lambda i, j: (i, j)),
        out_shape=_sds((t, n), F32), vmem_mb=vmem_mb)(a, w)


def _views(*arrs):
    out = []
    for a in arrs:
        if a.ndim == 2:
            out.append((a, None))
        else:
            out.extend((a, p) for p in range(a.shape[0]))
    return out


def _piece_layout(views, tile):
    starts, counts, total = [], [], 0
    for arr, _ in views:
        width = arr.shape[-1]
        assert width % tile == 0
        starts.append(total)
        counts.append(width // tile)
        total += width // tile
    return starts, counts, total


def _mm_tn(a, views, *, tn, name, tt=512, vmem_mb=32):
    t, m = a.shape
    tt = min(tt, t)
    starts, counts, nj = _piece_layout(views, tn)
    n_views = len(views)

    def body(a_ref, *refs):
        b_refs, o_ref = refs[:n_views], refs[n_views]
        j = pl.program_id(0)

        @pl.when(pl.program_id(1) == 0)
        def _():
            o_ref[...] = jnp.zeros_like(o_ref)

        for b_ref, st, nb, (_, p) in zip(b_refs, starts, counts, views):
            @pl.when((j >= st) & (j < st + nb))
            def _(b_ref=b_ref, p=p):
                blk = b_ref[...] if p is None else b_ref[0]
                o_ref[...] += _dot_tn(a_ref[...], blk)

    def b_spec(st, nb, p):
        def rows(j, k):
            return jnp.where((j >= st) & (j < st + nb), k, 0)

        def cols(j):
            return jnp.clip(j - st, 0, nb - 1)

        if p is None:
            return pl.BlockSpec((tt, tn), lambda j, k: (rows(j, k), cols(j)))
        return pl.BlockSpec((1, tt, tn), lambda j, k: (p, rows(j, k), cols(j)))

    return _call(
        body, name=name, grid=(nj, t // tt),
        in_specs=[pl.BlockSpec((tt, m), lambda j, k: (k, 0))]
        + [b_spec(st, nb, p) for st, nb, (_, p) in zip(starts, counts, views)],
        out_specs=pl.BlockSpec((m, tn), lambda j, k: (0, j)),
        out_shape=_sds((m, nj * tn), F32), vmem_mb=vmem_mb)(a, *[arr for arr, _ in views])


DPROJ_SLABS = 7
DPROJ_GATE_SLAB = 4
DPROJ_CONV_SLAB = 6


def _slab_cols(s):
    return (s + 1) % DPROJ_SLABS


def _grad_w_in(xb, dstack):
    t = xb.shape[0]
    tt = min(t, 1024)

    def body(a_ref, b_ref, o_ref):
        @pl.when(pl.program_id(1) == 0)
        def _():
            o_ref[...] = jnp.zeros_like(o_ref)

        o_ref[...] += _dot_tn(a_ref[...], b_ref[0])

    return _call(
        body, name="grad_w_in", grid=(DPROJ_SLABS, t // tt),
        in_specs=[pl.BlockSpec((tt, D_MODEL), lambda j, k: (k, 0)),
                  pl.BlockSpec((1, tt, D_MODEL), lambda j, k: (j, k, 0))],
        out_specs=pl.BlockSpec((D_MODEL, D_MODEL), lambda j, k: (0, _slab_cols(j))),
        out_shape=_sds((D_MODEL, IN_COLS), F32), vmem_mb=48)(xb, dstack)


def _grad_x(dstack, wi, dr1, part, other=None):
    t = dr1.shape[0]
    tm = min(t // 2, 1024)
    steps = t // 2 // tm
    first = part * steps

    def body(add_ref, b_ref, w_ref, *refs):
        o_ref = refs[-1]

        @pl.when(pl.program_id(1) == 0)
        def _():
            o_ref[...] = ALPHA * add_ref[...]

        o_ref[...] += _dot_nt(b_ref[0], w_ref[...])

    extra = [] if other is None else [other]
    return _call(
        body, name="grad_x_%d" % part, grid=(steps, DPROJ_SLABS),
        in_specs=[pl.BlockSpec((tm, D_MODEL), lambda i, k: (first + i, 0)),
                  pl.BlockSpec((1, tm, D_MODEL), lambda i, k: (k, first + i, 0)),
                  pl.BlockSpec((D_MODEL, D_MODEL), lambda i, k: (0, _slab_cols(k)))] + [ANY] * len(extra),
        out_specs=pl.BlockSpec((tm, D_MODEL), lambda i, k: (first + i, 0)),
        out_shape=_sds((t, D_MODEL), F32), aliases={3: 0} if extra else None, vmem_mb=48)(dr1, dstack, wi, *extra)


LANES = 128
CONV_FWD_OFFSETS = {k: 2 + k for k in range(CONV_KERNEL)}
CONV_BWD_OFFSETS = {k: CONV_KERNEL - 1 - k for k in range(CONV_KERNEL)}


def _conv_taps(win, offsets):
    n = win.shape[0]
    for b in range(8):
        taps = [k for k, o in offsets.items() if o % 8 == b]
        if not taps:
            continue
        shifted = win if b == 0 else pltpu.roll(win, n - b, 0)
        for k in taps:
            first = offsets[k] - b
            yield k, shifted[first:first + ROW_BLOCK, :]


def _conv_fwd(proj, wcd, bcd, lng, lnb):
    t = proj.shape[0]
    tm = 512

    def body(cv_ref, cg_ref, w_ref, b_ref, g_ref, be_ref, cc_ref, cs_ref, ext_ref):
        i = pl.program_id(0)

        @pl.when(i == 0)
        def _():
            ext_ref[0:CONV_HALO, :] = jnp.zeros((CONV_HALO, CONV_DIM), F32)

        @pl.when(i > 0)
        def _():
            ext_ref[0:CONV_HALO, :] = ext_ref[tm:tm + CONV_HALO, :]

        ext_ref[CONV_HALO:CONV_HALO + tm, :] = cv_ref[...] * _sigmoid(cg_ref[...])

        def block(r, carry):
            r0 = pl.multiple_of(r * ROW_BLOCK, ROW_BLOCK)
            groups = []
            for g in range(CONV_DIM // LANES):
                lanes = slice(g * LANES, (g + 1) * LANES)
                win = ext_ref[pl.ds(r0, ROW_BLOCK + CONV_HALO), lanes]
                acc = jnp.broadcast_to(b_ref[:, lanes], (ROW_BLOCK, LANES))
                for k, rows_k in _conv_taps(win, CONV_FWD_OFFSETS):
                    acc = acc + w_ref[k:k + 1, lanes] * rows_k
                groups.append(acc)
            acc = jnp.concatenate(groups, axis=1)
            cc_ref[pl.ds(r0, ROW_BLOCK), :] = acc
            xhat, _ = _ln_stats(acc)
            a = xhat * g_ref[...] + be_ref[...]
            cs_ref[pl.ds(r0, ROW_BLOCK), :] = (a * _sigmoid(a)).astype(BF16)
            return carry

        lax.fori_loop(0, tm // ROW_BLOCK, block, 0)

    vec = pl.BlockSpec((1, CONV_DIM), lambda i: (0, 0))
    return _call(
        body, name="conv_fwd", grid=(t // tm,),
        in_specs=[pl.BlockSpec((tm, CONV_DIM), lambda i: (i, 0)), pl.BlockSpec((tm, CONV_DIM), lambda i: (i, 1)),
                  pl.BlockSpec((CONV_KERNEL, CONV_DIM), lambda i: (0, 0)), vec, vec, vec],
        out_specs=[pl.BlockSpec((tm, CONV_DIM), lambda i: (i, 0)), pl.BlockSpec((tm, CONV_DIM), lambda i: (i, 0))],
        out_shape=[_sds((t, CONV_DIM), F32), _sds((t, CONV_DIM), BF16)],
        scratch=[pltpu.VMEM((tm + CONV_HALO, CONV_DIM), F32)])(proj, proj, wcd, bcd, lng, lnb)


def _conv_bwd_a(dcs, cc, proj, lng, lnb):
    t = proj.shape[0]
    tm = 512
    nt = t // tm

    def body(dcs_ref, cc_ref, cv_ref, cg_ref, g_ref, be_ref,
             dcc_ref, dw_ref, db_ref, dg_ref, dbe_ref, ext_ref, accw_ref, acc3_ref):
        i = pl.program_id(0)

        @pl.when(i == 0)
        def _():
            ext_ref[0:CONV_HALO, :] = jnp.zeros((CONV_HALO, CONV_DIM), F32)
            accw_ref[...] = jnp.zeros_like(accw_ref)
            acc3_ref[...] = jnp.zeros_like(acc3_ref)

        @pl.when(i > 0)
        def _():
            ext_ref[0:CONV_HALO, :] = ext_ref[tm:tm + CONV_HALO, :]

        ext_ref[CONV_HALO:CONV_HALO + tm, :] = cv_ref[...] * _sigmoid(cg_ref[...])

        def block(r, carry):
            r0 = pl.multiple_of(r * ROW_BLOCK, ROW_BLOCK)
            rows = pl.ds(r0, ROW_BLOCK)
            xhat, rstd = _ln_stats(cc_ref[rows, :])
            a = xhat * g_ref[...] + be_ref[...]
            sg = _sigmoid(a)
            da = dcs_ref[rows, :] * _silu_grad(a, sg)
            acc3_ref[8:16, :] += _fold8(da * xhat)
            acc3_ref[16:24, :] += _fold8(da)
            dcc = _ln_bwd(da, xhat, rstd, g_ref[...])
            dcc_ref[rows, :] = dcc
            acc3_ref[0:8, :] += _fold8(dcc)
            for g in range(CONV_DIM // LANES):
                lanes = slice(g * LANES, (g + 1) * LANES)
                win = ext_ref[pl.ds(r0, ROW_BLOCK + CONV_HALO), lanes]
                dcc_g = dcc[:, lanes]
                for k, rows_k in _conv_taps(win, CONV_FWD_OFFSETS):
                    accw_ref[8 * k:8 * k + 8, lanes] += _fold8(dcc_g * rows_k)
            return carry

        lax.fori_loop(0, tm // ROW_BLOCK, block, 0)

        @pl.when(i == nt - 1)
        def _():
            for k in range(CONV_KERNEL):
                dw_ref[k:k + 1, :] = jnp.sum(accw_ref[8 * k:8 * k + 8, :], axis=0, keepdims=True)
            db_ref[...] = jnp.sum(acc3_ref[0:8, :], axis=0, keepdims=True)
            dg_ref[...] = jnp.sum(acc3_ref[8:16, :], axis=0, keepdims=True)
            dbe_ref[...] = jnp.sum(acc3_ref[16:24, :], axis=0, keepdims=True)

    vec = pl.BlockSpec((1, CONV_DIM), lambda i: (0, 0))
    tile = pl.BlockSpec((tm, CONV_DIM), lambda i: (i, 0))
    return _call(
        body, name="conv_bwd_a", grid=(nt,),
        in_specs=[tile, tile, tile, pl.BlockSpec((tm, CONV_DIM), lambda i: (i, 1)), vec, vec],
        out_specs=[tile, pl.BlockSpec((CONV_KERNEL, CONV_DIM), lambda i: (0, 0)), vec, vec, vec],
        out_shape=[_sds((t, CONV_DIM), F32), _sds((CONV_KERNEL, CONV_DIM), F32),
                   _sds((1, CONV_DIM), F32), _sds((1, CONV_DIM), F32), _sds((1, CONV_DIM), F32)],
        scratch=[pltpu.VMEM((tm + CONV_HALO, CONV_DIM), F32),
                 pltpu.VMEM((8 * CONV_KERNEL, CONV_DIM), F32),
                 pltpu.VMEM((24, CONV_DIM), F32)])(dcs, cc, proj, proj, lng, lnb)


def _conv_bwd_b(dcc, proj, wcd, dstack):
    t = proj.shape[0]
    tm = 512
    nt = t // tm

    def body(dcc_ref, cv_ref, cg_ref, w_ref, stack_ref, out_ref, ext_ref):
        del stack_ref
        i = pl.program_id(0)

        @pl.when(i == 0)
        def _():
            ext_ref[tm:tm + CONV_HALO, :] = jnp.zeros((CONV_HALO, CONV_DIM), F32)

        @pl.when(i > 0)
        def _():
            ext_ref[tm:tm + CONV_HALO, :] = ext_ref[0:CONV_HALO, :]

        ext_ref[0:tm, :] = dcc_ref[...]

        def block(r, carry):
            r0 = pl.multiple_of(r * ROW_BLOCK, ROW_BLOCK)
            rows = pl.ds(r0, ROW_BLOCK)
            for g in range(CONV_DIM // LANES):
                lanes = slice(g * LANES, (g + 1) * LANES)
                gate_lanes = slice(CONV_DIM + g * LANES, CONV_DIM + (g + 1) * LANES)
                win = ext_ref[pl.ds(r0, ROW_BLOCK + CONV_HALO), lanes]
                acc = jnp.zeros((ROW_BLOCK, LANES), F32)
                for k, rows_k in _conv_taps(win, CONV_BWD_OFFSETS):
                    acc = acc + w_ref[k:k + 1, lanes] * rows_k
                sg = _sigmoid(cg_ref[rows, lanes])
                out_ref[0, rows, lanes] = (acc * sg).astype(BF16)
                out_ref[0, rows, gate_lanes] = (acc * cv_ref[rows, lanes] * sg * (1.0 - sg)).astype(BF16)
            return carry

        lax.fori_loop(0, tm // ROW_BLOCK, block, 0)

    rev = lambda i: (nt - 1 - i, 0)
    return _call(
        body, name="conv_bwd_b", grid=(nt,),
        in_specs=[pl.BlockSpec((tm, CONV_DIM), rev), pl.BlockSpec((tm, CONV_DIM), rev),
                  pl.BlockSpec((tm, CONV_DIM), lambda i: (nt - 1 - i, 1)),
                  pl.BlockSpec((CONV_KERNEL, CONV_DIM), lambda i: (0, 0)), ANY],
        out_specs=pl.BlockSpec((1, tm, 2 * CONV_DIM), lambda i: (DPROJ_CONV_SLAB, nt - 1 - i, 0)),
        out_shape=_sds(dstack.shape, BF16), aliases={4: 0},
        scratch=[pltpu.VMEM((tm + CONV_HALO, CONV_DIM), F32)])(dcc, proj, proj, wcd, dstack)


def _lower_bound(lg_ref):
    a0, a1 = lg_ref[0:1, :], lg_ref[1:2, :]
    m = jnp.maximum(a0, a1)
    e0, e1 = jnp.exp(a0 - m), jnp.exp(a1 - m)
    return e0 / (e0 + e1)


def _block_tri(n, upper):
    r = lax.broadcasted_iota(jnp.int32, (n, n), 0)
    c = lax.broadcasted_iota(jnp.int32, (n, n), 1)
    same = (r >> 6) == (c >> 6)
    tri = (c >= r) if upper else (c <= r)
    return jnp.where(same & tri, 1.0, 0.0).astype(BF16)


def _block_cumsum(x, tri):
    w = x.shape[1]
    hi = x.astype(BF16)
    r1 = x - hi.astype(F32)
    mid = r1.astype(BF16)
    lo = (r1 - mid.astype(F32)).astype(BF16)
    y = _dot(tri, jnp.concatenate([hi, mid, lo], axis=1))
    return y[:, 0:w] + y[:, w:2 * w] + y[:, 2 * w:3 * w]


def _first_step():
    return (pl.program_id(0) == 0) & (pl.program_id(1) == 0)


def _block_rows(i):
    return slice(SUB * i, SUB * (i + 1))


def _chunk_terms(qc, kc, bc, b_ref, first_row, lanes=slice(None)):
    betas = [jnp.zeros((1, HEAD_DIM), F32)]
    betas += [b_ref[first_row + SUB * i - 1:first_row + SUB * i, lanes] for i in range(1, N_SUB)]
    b_last = b_ref[first_row + CHUNK - 1:first_row + CHUNK, lanes]
    zeros = lambda rows: jnp.zeros((rows, HEAD_DIM), BF16)
    qscale = [jnp.exp(bc[_block_rows(i), :] - betas[i]) for i in range(N_SUB)]
    qs = [(qc[_block_rows(i), :] * qscale[i]).astype(BF16) for i in range(N_SUB)]
    kscale = [jnp.exp(betas[n] - bc[0:SUB * (n + 1), :]) for n in range(N_SUB)]
    ks = [(kc[0:SUB * (n + 1), :] * kscale[n]).astype(BF16) for n in range(N_SUB)]

    def tall(parts):
        parts = [p for p in parts if p.shape[0]]
        return parts[0] if len(parts) == 1 else jnp.concatenate(parts, axis=0)

    qcat = jnp.concatenate([tall([zeros(SUB * n), qs[n], zeros(CHUNK - SUB * (n + 1))]) for n in range(N_SUB)],
                           axis=1)
    kcat = jnp.concatenate([tall([ks[n], zeros(CHUNK - SUB * (n + 1))]) for n in range(N_SUB)], axis=1)
    return dict(qscale=qscale, qcat=qcat, kscale=kscale, kcat=kcat,
                eb=jnp.exp(bc), e_last=jnp.exp(b_last), ktscale=jnp.exp(b_last - bc))


def _causal(shape_rows_first):
    r = lax.broadcasted_iota(jnp.int32, (CHUNK, CHUNK), 0)
    c = lax.broadcasted_iota(jnp.int32, (CHUNK, CHUNK), 1)
    return (c <= r) if shape_rows_first else (r <= c)


HEADS_PER_STEP = 8
HGRN_TILE = 256
GROUP = HEADS_PER_STEP * HEAD_DIM
HEAD_GROUPS = HGRN_HEADS // HEADS_PER_STEP


def _head_lanes(h):
    return slice(h * HEAD_DIM, (h + 1) * HEAD_DIM)


def _hgrn_specs(tm, tile_of):
    per_piece = HGRN_DIM // GROUP
    col = lambda piece: (lambda g, i: (tile_of(i), per_piece * piece + g))
    return [pl.BlockSpec((tm, GROUP), col(1)), pl.BlockSpec((tm, GROUP), col(2)),
            pl.BlockSpec((tm, GROUP), col(3)), pl.BlockSpec((tm, GROUP), col(4)),
            pl.BlockSpec((2, GROUP), lambda g, i: (0, g)), pl.BlockSpec((1, GROUP), lambda g, i: (0, g))]


def _per_head_mean(x):
    return jnp.concatenate(
        [jnp.broadcast_to(jnp.mean(x[:, _head_lanes(h)], axis=-1, keepdims=True), (x.shape[0], HEAD_DIM))
         for h in range(HEADS_PER_STEP)], axis=1)


def _hgrn_fwd(proj, logits, ng):
    t = proj.shape[0]
    tm = HGRN_TILE
    nc = tm // CHUNK
    nt = t // tm

    def body(zq_ref, zf_ref, v_ref, zg_ref, lg_ref, ng_ref, o_ref, og_ref, st_ref,
             s_scr, q_scr, k_scr, b_scr, tri_scr):
        @pl.when(_first_step())
        def _():
            tri_scr[...] = _block_tri(tm, upper=False)

        @pl.when(pl.program_id(1) == 0)
        def _():
            s_scr[...] = jnp.zeros_like(s_scr)

        lb = _lower_bound(lg_ref)
        zf = zf_ref[...]
        f = lb + (1.0 - lb) * _sigmoid(zf)
        k_scr[...] = (1.0 - lb) * _sigmoid(-zf)
        zq = zq_ref[...]
        q_scr[...] = zq * _sigmoid(zq)
        b_scr[...] = _block_cumsum(jnp.log(f), tri_scr[...])

        st = [s_scr[h] for h in range(HEADS_PER_STEP)]
        for c in range(nc):
            rows = pl.ds(c * CHUNK, CHUNK)
            for h in range(HEADS_PER_STEP):
                lanes = _head_lanes(h)
                qc, kc, bc, vc = q_scr[rows, lanes], k_scr[rows, lanes], b_scr[rows, lanes], v_ref[rows, lanes]
                st_ref[h, c] = st[h]
                tr = _chunk_terms(qc, kc, bc, b_scr, c * CHUNK, lanes)
                a = jnp.where(_causal(True), _dot_nt(tr["qcat"], tr["kcat"]), 0.0)
                vb = vc.astype(BF16)
                o_ref[rows, lanes] = (_dot(a.astype(BF16), vb)
                                      + _dot_nt((qc * tr["eb"]).astype(BF16), st[h].astype(BF16)))
                st[h] = st[h] * tr["e_last"] + _dot_tn(vb, (kc * tr["ktscale"]).astype(BF16))
        for h in range(HEADS_PER_STEP):
            s_scr[h] = st[h]

        o = o_ref[...]
        rinv = lax.rsqrt(_per_head_mean(o * o) + RMS_EPS)
        zg = zg_ref[...]
        og_ref[...] = (o * rinv * ng_ref[...] * (zg * _sigmoid(zg))).astype(BF16)

    tile = pl.BlockSpec((tm, GROUP), lambda g, i: (i, g))
    return _call(
        body, name="hgrn_fwd", grid=(HEAD_GROUPS, nt),
        in_specs=_hgrn_specs(tm, lambda i: i),
        out_specs=[tile, tile,
                   pl.BlockSpec((HEADS_PER_STEP, nc, HEAD_DIM, HEAD_DIM), lambda g, i: (g, i, 0, 0))],
        out_shape=[_sds((t, HGRN_DIM), F32), _sds((t, HGRN_DIM), BF16),
                   _sds((HGRN_HEADS, t // CHUNK, HEAD_DIM, HEAD_DIM), F32)],
        scratch=[pltpu.VMEM((HEADS_PER_STEP, HEAD_DIM, HEAD_DIM), F32)] + [pltpu.VMEM((tm, GROUP), F32)] * 3
        + [pltpu.VMEM((tm, tm), BF16)], vmem_mb=56,
    )(proj, proj, proj, proj, logits, ng)


def _hgrn_bwd(dog, o, states, proj, logits, ng, dstack):
    t = proj.shape[0]
    tm = HGRN_TILE
    nc = tm // CHUNK
    nt = t // tm

    def body(dog_ref, o_ref, st_ref, zq_ref, zf_ref, v_ref, zg_ref, lg_ref, ng_ref, stack_ref,
             dp_ref, dlg_ref, dng_ref,
             ds_scr, q_scr, k_scr, b_scr, do_scr, dq_scr, dk_scr, dv_scr, db_scr, dlb_scr, tri_scr):
        i = pl.program_id(1)

        @pl.when(_first_step())
        def _():
            tri_scr[0] = _block_tri(tm, upper=False)
            tri_scr[1] = _block_tri(tm, upper=True)

        @pl.when(i == 0)
        def _():
            ds_scr[...] = jnp.zeros_like(ds_scr)
            dlb_scr[...] = jnp.zeros_like(dlb_scr)
            dng_ref[...] = jnp.zeros_like(dng_ref)

        lb = _lower_bound(lg_ref)
        ng_row = ng_ref[...]
        o = o_ref[...]
        rinv = lax.rsqrt(_per_head_mean(o * o) + RMS_EPS)
        ohat = o * rinv
        zg = zg_ref[...]
        sg = _sigmoid(zg)
        dog_v = dog_ref[...]
        don = dog_v * (zg * sg)
        dp_ref[3] = (dog_v * (ohat * ng_row) * _silu_grad(zg, sg)).astype(BF16)
        dng_ref[...] += jnp.sum(don * ohat, axis=0, keepdims=True)
        dohat = don * ng_row
        do_scr[...] = rinv * (dohat - ohat * _per_head_mean(dohat * ohat))

        zf = zf_ref[...]
        s = _sigmoid(zf)
        s_neg = _sigmoid(-zf)
        f = lb + (1.0 - lb) * s
        k_scr[...] = (1.0 - lb) * s_neg
        zq = zq_ref[...]
        sq = _sigmoid(zq)
        q_scr[...] = zq * sq
        b_scr[...] = _block_cumsum(jnp.log(f), tri_scr[0])

        dsts = [ds_scr[h] for h in range(HEADS_PER_STEP)]
        for c, h in [(c, h) for c in reversed(range(nc)) for h in range(HEADS_PER_STEP)]:
            rows = pl.ds(c * CHUNK, CHUNK)
            lanes = _head_lanes(h)
            qc, kc, bc, vc, doc = (q_scr[rows, lanes], k_scr[rows, lanes], b_scr[rows, lanes], v_ref[rows, lanes],
                                   do_scr[rows, lanes])
            st = st_ref[h, c]
            dst = dsts[h]
            tr = _chunk_terms(qc, kc, bc, b_scr, c * CHUNK, lanes)
            qcb, kcb = tr["qcat"], tr["kcat"]
            dob, vb, dstb = doc.astype(BF16), vc.astype(BF16), dst.astype(BF16)
            a_t = jnp.where(_causal(False), _dot_nt(kcb, qcb), 0.0)
            da = jnp.where(_causal(True), _dot_nt(dob, vb), 0.0)
            da_t = jnp.where(_causal(False), _dot_nt(vb, dob), 0.0)
            dqcat = _dot(da.astype(BF16), kcb)
            dkcat = _dot(da_t.astype(BF16), qcb)
            kt = kc * tr["ktscale"]
            dv_scr[rows, lanes] = _dot(a_t.astype(BF16), dob) + _dot_nt(kt.astype(BF16), dstb)
            dq_blocks, dk_blocks, db_blocks = [], [], []
            for j in range(N_SUB):
                rows_j = _block_rows(j)
                lanes_j = slice(j * HEAD_DIM, (j + 1) * HEAD_DIM)
                dq_j = dqcat[rows_j, lanes_j]
                dq_blocks.append(dq_j * tr["qscale"][j])
                db_j = qcb[rows_j, lanes_j].astype(F32) * dq_j
                dk_j = jnp.zeros((SUB, HEAD_DIM), F32)
                for n in range(j, N_SUB):
                    lanes_n = slice(n * HEAD_DIM, (n + 1) * HEAD_DIM)
                    dk_jn = dkcat[rows_j, lanes_n]
                    dk_j = dk_j + dk_jn * tr["kscale"][n][rows_j, :]
                    db_j = db_j - kcb[rows_j, lanes_n].astype(F32) * dk_jn
                dk_blocks.append(dk_j)
                db_blocks.append(db_j)
            dq_inter = _dot(dob, st.astype(BF16)) * tr["eb"]
            dkt = _dot(vb, dstb)
            dk_inter = dkt * tr["ktscale"]
            extra = (jnp.sum(dkt * kt, axis=0, keepdims=True)
                     + tr["e_last"] * jnp.sum(dst * st, axis=0, keepdims=True))
            dq_scr[rows, lanes] = jnp.concatenate(dq_blocks, axis=0) + dq_inter
            dk_scr[rows, lanes] = jnp.concatenate(dk_blocks, axis=0) + dk_inter
            db_scr[rows, lanes] = jnp.concatenate(db_blocks, axis=0) + qc * dq_inter - kc * dk_inter
            last = c * CHUNK + CHUNK - 1
            db_scr[last:last + 1, lanes] += extra
            dsts[h] = dst * tr["e_last"] + _dot_tn(dob, (qc * tr["eb"]).astype(BF16))
        for h in range(HEADS_PER_STEP):
            ds_scr[h] = dsts[h]

        dlogf = _block_cumsum(db_scr[...], tri_scr[1])
        df = dlogf / f - dk_scr[...]
        dp_ref[0] = (dq_scr[...] * _silu_grad(zq, sq)).astype(BF16)
        dp_ref[1] = (df * (1.0 - lb) * s * (1.0 - s)).astype(BF16)
        dp_ref[2] = dv_scr[...].astype(BF16)
        dlb_scr[...] += jnp.sum(df * s_neg, axis=0, keepdims=True)

        @pl.when(i == nt - 1)
        def _():
            dlogit = dlb_scr[...] * lb * (1.0 - lb)
            dlg_ref[0:1, :] = dlogit
            dlg_ref[1:2, :] = -dlogit

    rev = lambda i: nt - 1 - i
    tile = pl.BlockSpec((tm, GROUP), lambda g, i: (rev(i), g))
    return _call(
        body, name="hgrn_bwd", grid=(HEAD_GROUPS, nt),
        in_specs=[tile, tile,
                  pl.BlockSpec((HEADS_PER_STEP, nc, HEAD_DIM, HEAD_DIM), lambda g, i: (g, rev(i), 0, 0))]
        + _hgrn_specs(tm, rev) + [ANY],
        out_specs=[pl.BlockSpec((4, tm, GROUP), lambda g, i: (0, rev(i), g)),
                   pl.BlockSpec((2, GROUP), lambda g, i: (0, g)),
                   pl.BlockSpec((1, GROUP), lambda g, i: (0, g))],
        out_shape=[_sds(dstack.shape, BF16), _sds((2, HGRN_DIM), F32), _sds((1, HGRN_DIM), F32)],
        aliases={9: 0},
        scratch=[pltpu.VMEM((HEADS_PER_STEP, HEAD_DIM, HEAD_DIM), F32)] + [pltpu.VMEM((tm, GROUP), F32)] * 8
        + [pltpu.VMEM((1, GROUP), F32), pltpu.VMEM((2, tm, tm), BF16)], vmem_mb=56,
    )(dog, o, states, proj, proj, proj, proj, logits, ng, dstack)


def _merge_fwd(cs, og, proj, x, wco, wh, wo, g1, b1):
    t = x.shape[0]
    tm = 256

    def body(cs_ref, og_ref, m0_ref, m1_ref, x_ref, wco_ref, wh_ref, wo_ref, g_ref, b_ref,
             y_ref, mixed_ref, r1_ref, x1_ref, x1b_ref):
        yc = _dot(cs_ref[...], wco_ref[...])
        yh = _dot(og_ref[...], wh_ref[...])
        y_ref[0] = yc
        y_ref[1] = yh
        mixed = (_sigmoid(m0_ref[...]) * yc + _sigmoid(m1_ref[...]) * yh).astype(BF16)
        mixed_ref[...] = mixed
        r1 = ALPHA * x_ref[...] + _dot(mixed, wo_ref[...])
        r1_ref[...] = r1
        xhat, _ = _ln_stats(r1)
        x1 = xhat * g_ref[...] + b_ref[...]
        x1_ref[...] = x1
        x1b_ref[...] = x1.astype(BF16)

    row = lambda w: pl.BlockSpec((tm, w), lambda i: (i, 0))
    full = lambda a: pl.BlockSpec(a.shape, lambda i: (0, 0))
    return _call(
        body, name="merge_fwd", grid=(t // tm,),
        in_specs=[row(CONV_DIM), row(HGRN_DIM),
                  pl.BlockSpec((tm, D_MODEL), lambda i: (i, 5)), pl.BlockSpec((tm, D_MODEL), lambda i: (i, 6)),
                  row(D_MODEL), full(wco), full(wh), full(wo), full(g1), full(b1)],
        out_specs=[pl.BlockSpec((2, tm, D_MODEL), lambda i: (0, i, 0)), row(D_MODEL), row(D_MODEL),
                   row(D_MODEL), row(D_MODEL)],
        out_shape=[_sds((2, t, D_MODEL), F32), _sds((t, D_MODEL), BF16), _sds((t, D_MODEL), F32),
                   _sds((t, D_MODEL), F32), _sds((t, D_MODEL), BF16)],
        vmem_mb=48)(cs, og, proj, proj, x, wco, wh, wo, g1, b1)


def _merge_bwd(dr1b, ycat, proj, wo, wco, wh):
    t = dr1b.shape[0]
    tm = 256

    def body(dr_ref, y_ref, m0_ref, m1_ref, wo_ref, wco_ref, wh_ref, dpm_ref, dy_ref, dcs_ref, dog_ref):
        dmixed = _dot_nt(dr_ref[...], wo_ref[...])
        g0 = _sigmoid(m0_ref[...])
        g1 = _sigmoid(m1_ref[...])
        dpm_ref[0] = (dmixed * y_ref[0] * g0 * (1.0 - g0)).astype(BF16)
        dpm_ref[1] = (dmixed * y_ref[1] * g1 * (1.0 - g1)).astype(BF16)
        dyc = (dmixed * g0).astype(BF16)
        dyh = (dmixed * g1).astype(BF16)
        dy_ref[0] = dyc
        dy_ref[1] = dyh
        dcs_ref[...] = _dot_nt(dyc, wco_ref[...])
        dog_ref[...] = _dot_nt(dyh, wh_ref[...])

    row = lambda w: pl.BlockSpec((tm, w), lambda i: (i, 0))
    pair = pl.BlockSpec((2, tm, D_MODEL), lambda i: (0, i, 0))
    full = lambda a: pl.BlockSpec(a.shape, lambda i: (0, 0))
    return _call(
        body, name="merge_bwd", grid=(t // tm,),
        in_specs=[row(D_MODEL), pair,
                  pl.BlockSpec((tm, D_MODEL), lambda i: (i, 5)), pl.BlockSpec((tm, D_MODEL), lambda i: (i, 6)),
                  full(wo), full(wco), full(wh)],
        out_specs=[pl.BlockSpec((2, tm, D_MODEL), lambda i: (DPROJ_GATE_SLAB // 2, i, 0)), pair,
                   row(CONV_DIM), row(HGRN_DIM)],
        out_shape=[_sds((DPROJ_SLABS, t, D_MODEL), BF16), _sds((2, t, D_MODEL), BF16),
                   _sds((t, CONV_DIM), F32), _sds((t, HGRN_DIM), F32)],
        vmem_mb=48)(dr1b, ycat, proj, proj, wo, wco, wh)


def _ffn_taps(win):
    return (pltpu.roll(win, 2, 0)[FFN_HALO:, :], pltpu.roll(win, 1, 0)[FFN_HALO:, :], win[FFN_HALO:, :])


def _ffn_conv3(taps, w_ref):
    return w_ref[0:1, :] * taps[0] + w_ref[1:2, :] * taps[1] + w_ref[2:3, :] * taps[2]


def _ffn_mid(z, wfd, bfd):
    t = z.shape[0]
    tm = 256

    def body(u_ref, gv_ref, w_ref, b_ref, h_ref, ext_ref):
        i = pl.program_id(0)

        @pl.when(i == 0)
        def _():
            ext_ref[0:FFN_HALO, :] = jnp.zeros((FFN_HALO, D_FF), F32)

        @pl.when(i > 0)
        def _():
            ext_ref[0:FFN_HALO, :] = ext_ref[tm:tm + FFN_HALO, :]

        ext_ref[FFN_HALO:FFN_HALO + tm, :] = u_ref[...]

        def block(r, carry):
            r0 = pl.multiple_of(r * ROW_BLOCK, ROW_BLOCK)
            rows = pl.ds(r0, ROW_BLOCK)
            win = ext_ref[pl.ds(r0, ROW_BLOCK + FFN_HALO), :]
            uc = _ffn_conv3(_ffn_taps(win), w_ref) + b_ref[...]
            g, _ = _gelu_and_grad(uc)
            h_ref[rows, :] = (g * gv_ref[rows, :]).astype(BF16)
            return carry

        lax.fori_loop(0, tm // ROW_BLOCK, block, 0)

    return _call(
        body, name="ffn_mid", grid=(t // tm,),
        in_specs=[pl.BlockSpec((tm, D_FF), lambda i: (i, 0)), pl.BlockSpec((tm, D_FF), lambda i: (i, 1)),
                  pl.BlockSpec((FFN_KERNEL, D_FF), lambda i: (0, 0)), pl.BlockSpec((1, D_FF), lambda i: (0, 0))],
        out_specs=pl.BlockSpec((tm, D_FF), lambda i: (i, 0)),
        out_shape=_sds((t, D_FF), BF16),
        scratch=[pltpu.VMEM((tm + FFN_HALO, D_FF), F32)], vmem_mb=40)(z, z, wfd, bfd)


def _ffn_out_loss(hmid, x1, target, wfo, g2, b2):
    t = x1.shape[0]
    tm = 256
    inv_n = 1.0 / D_MODEL

    def body(h_ref, x1_ref, tg_ref, w_ref, g_ref, b_ref, dr_ref, drb_ref, loss_ref, dg_ref, db_ref):
        @pl.when(pl.program_id(0) == 0)
        def _():
            loss_ref[...] = jnp.zeros_like(loss_ref)
            dg_ref[...] = jnp.zeros_like(dg_ref)
            db_ref[...] = jnp.zeros_like(db_ref)

        r2 = ALPHA * x1_ref[...] + _dot(h_ref[...], w_ref[...])
        xhat, rstd = _ln_stats(r2)
        err = xhat * g_ref[...] + b_ref[...] - tg_ref[...]
        loss_ref[...] += 0.5 * inv_n * jnp.sum(err * err)
        dy = err * inv_n
        dg_ref[...] += jnp.sum(dy * xhat, axis=0, keepdims=True)
        db_ref[...] += jnp.sum(dy, axis=0, keepdims=True)
        dr = _ln_bwd(dy, xhat, rstd, g_ref[...])
        dr_ref[...] = dr
        drb_ref[...] = dr.astype(BF16)

    row = lambda w: pl.BlockSpec((tm, w), lambda i: (i, 0))
    vec = pl.BlockSpec((1, D_MODEL), lambda i: (0, 0))
    return _call(
        body, name="ffn_out_loss", grid=(t // tm,),
        in_specs=[row(D_FF), row(D_MODEL), row(D_MODEL), pl.BlockSpec((D_FF, D_MODEL), lambda i: (0, 0)), vec, vec],
        out_specs=[row(D_MODEL), row(D_MODEL), pl.BlockSpec((1, 128), lambda i: (0, 0)), vec, vec],
        out_shape=[_sds((t, D_MODEL), F32), _sds((t, D_MODEL), BF16), _sds((1, 128), F32),
                   _sds((1, D_MODEL), F32), _sds((1, D_MODEL), F32)],
        vmem_mb=40)(hmid, x1, target, wfo, g2, b2)


def _ffn_bwd_a(dr2b, z, wfo, wfd, bfd):
    t = z.shape[0]
    tm = 256
    nt = t // tm

    def body(dr_ref, u_ref, gv_ref, wfo_ref, w_ref, b_ref, dgv_ref, duc_ref, dw_ref, db_ref,
             ext_ref, dh_ref, acc_ref):
        i = pl.program_id(0)

        @pl.when(i == 0)
        def _():
            ext_ref[0:FFN_HALO, :] = jnp.zeros((FFN_HALO, D_FF), F32)
            acc_ref[...] = jnp.zeros_like(acc_ref)

        @pl.when(i > 0)
        def _():
            ext_ref[0:FFN_HALO, :] = ext_ref[tm:tm + FFN_HALO, :]

        ext_ref[FFN_HALO:FFN_HALO + tm, :] = u_ref[...]
        dh_ref[...] = _dot_nt(dr_ref[...], wfo_ref[...])

        def block(r, carry):
            r0 = pl.multiple_of(r * ROW_BLOCK, ROW_BLOCK)
            rows = pl.ds(r0, ROW_BLOCK)
            taps = _ffn_taps(ext_ref[pl.ds(r0, ROW_BLOCK + FFN_HALO), :])
            uc = _ffn_conv3(taps, w_ref) + b_ref[...]
            g, dg = _gelu_and_grad(uc)
            dh = dh_ref[rows, :]
            dgv_ref[rows, :] = (dh * g).astype(BF16)
            duc = dh * gv_ref[rows, :] * dg
            duc_ref[rows, :] = duc
            acc_ref[0:8, :] += _fold8(duc)
            for k in range(FFN_KERNEL):
                acc_ref[8 + 8 * k:16 + 8 * k, :] += _fold8(duc * taps[k])
            return carry

        lax.fori_loop(0, tm // ROW_BLOCK, block, 0)

        @pl.when(i == nt - 1)
        def _():
            db_ref[...] = jnp.sum(acc_ref[0:8, :], axis=0, keepdims=True)
            for k in range(FFN_KERNEL):
                dw_ref[k:k + 1, :] = jnp.sum(acc_ref[8 + 8 * k:16 + 8 * k, :], axis=0, keepdims=True)

    tile = pl.BlockSpec((tm, D_FF), lambda i: (i, 0))
    return _call(
        body, name="ffn_bwd_a", grid=(nt,),
        in_specs=[pl.BlockSpec((tm, D_MODEL), lambda i: (i, 0)), tile, pl.BlockSpec((tm, D_FF), lambda i: (i, 1)),
                  pl.BlockSpec((D_FF, D_MODEL), lambda i: (0, 0)),
                  pl.BlockSpec((FFN_KERNEL, D_FF), lambda i: (0, 0)), pl.BlockSpec((1, D_FF), lambda i: (0, 0))],
        out_specs=[tile, tile, pl.BlockSpec((FFN_KERNEL, D_FF), lambda i: (0, 0)),
                   pl.BlockSpec((1, D_FF), lambda i: (0, 0))],
        out_shape=[_sds((t, D_FF), BF16), _sds((t, D_FF), F32), _sds((FFN_KERNEL, D_FF), F32), _sds((1, D_FF), F32)],
        scratch=[pltpu.VMEM((tm + FFN_HALO, D_FF), F32), pltpu.VMEM((tm, D_FF), F32),
                 pltpu.VMEM((8 + 8 * FFN_KERNEL, D_FF), F32)],
        vmem_mb=56)(dr2b, z, z, wfo, wfd, bfd)


def _ffn_bwd_b(duc, wfd):
    t = duc.shape[0]
    tm = 256
    nt = t // tm

    def body(duc_ref, w_ref, du_ref, ext_ref):
        i = pl.program_id(0)

        @pl.when(i == 0)
        def _():
            ext_ref[tm:tm + FFN_HALO, :] = jnp.zeros((FFN_HALO, D_FF), F32)

        @pl.when(i > 0)
        def _():
            ext_ref[tm:tm + FFN_HALO, :] = ext_ref[0:FFN_HALO, :]

        ext_ref[0:tm, :] = duc_ref[...]

        def block(r, carry):
            r0 = pl.multiple_of(r * ROW_BLOCK, ROW_BLOCK)
            win = ext_ref[pl.ds(r0, ROW_BLOCK + FFN_HALO), :]
            n = ROW_BLOCK + FFN_HALO
            du = (w_ref[2:3, :] * win[0:ROW_BLOCK, :] + w_ref[1:2, :] * pltpu.roll(win, n - 1, 0)[0:ROW_BLOCK, :]
                  + w_ref[0:1, :] * pltpu.roll(win, n - 2, 0)[0:ROW_BLOCK, :])
            du_ref[pl.ds(r0, ROW_BLOCK), :] = du.astype(BF16)
            return carry

        lax.fori_loop(0, tm // ROW_BLOCK, block, 0)

    rev = lambda i: (nt - 1 - i, 0)
    return _call(
        body, name="ffn_bwd_b", grid=(nt,),
        in_specs=[pl.BlockSpec((tm, D_FF), rev), pl.BlockSpec((FFN_KERNEL, D_FF), lambda i: (0, 0))],
        out_specs=pl.BlockSpec((tm, D_FF), rev),
        out_shape=_sds((t, D_FF), BF16),
        scratch=[pltpu.VMEM((tm + FFN_HALO, D_FF), F32)], vmem_mb=40)(duc, wfd)


def _ffn_in_bwd(dr2, dub, dgvb, wfi, r1, g1):
    t = dr2.shape[0]
    tm = 256

    def body(dr2_ref, du_ref, dgv_ref, wu_ref, wg_ref, r1_ref, g_ref, dr1_ref, dr1b_ref, dg_ref, db_ref):
        @pl.when(pl.program_id(0) == 0)
        def _():
            dg_ref[...] = jnp.zeros_like(dg_ref)
            db_ref[...] = jnp.zeros_like(db_ref)

        dx1 = ALPHA * dr2_ref[...] + _dot_nt(du_ref[...], wu_ref[...]) + _dot_nt(dgv_ref[...], wg_ref[...])
        xhat, rstd = _ln_stats(r1_ref[...])
        dg_ref[...] += jnp.sum(dx1 * xhat, axis=0, keepdims=True)
        db_ref[...] += jnp.sum(dx1, axis=0, keepdims=True)
        dr1 = _ln_bwd(dx1, xhat, rstd, g_ref[...])
        dr1_ref[...] = dr1
        dr1b_ref[...] = dr1.astype(BF16)

    row = lambda w: pl.BlockSpec((tm, w), lambda i: (i, 0))
    vec = pl.BlockSpec((1, D_MODEL), lambda i: (0, 0))
    return _call(
        body, name="ffn_in_bwd", grid=(t // tm,),
        in_specs=[row(D_MODEL), row(D_FF), row(D_FF),
                  pl.BlockSpec((D_MODEL, D_FF), lambda i: (0, 0)), pl.BlockSpec((D_MODEL, D_FF), lambda i: (0, 1)),
                  row(D_MODEL), vec],
        out_specs=[row(D_MODEL), row(D_MODEL), vec, vec],
        out_shape=[_sds((t, D_MODEL), F32), _sds((t, D_MODEL), BF16), _sds((1, D_MODEL), F32), _sds((1, D_MODEL), F32)],
        vmem_mb=56)(dr2, dub, dgvb, wfi, wfi, r1, g1)


def _local_step(x, target, wi, late_weights, early_reduce, late_reduce, wcd, bcd, clg, clb, logits, ng, g1, b1,
                wfd, bfd, g2, b2):
    proj, xb = _proj(x, wi)
    cc, cs = _conv_fwd(proj, wcd, bcd, clg, clb)
    o, og, states = _hgrn_fwd(proj, logits, ng)
    wco, wh, wo, wfi, wfo = late_weights(og)
    ycat, mixed, r1, x1, x1b = _merge_fwd(cs, og, proj, x, wco, wh, wo, g1, b1)
    z = _mm_nn(x1b, wfi, tm=1024, tn=1408, name="ffn_in", vmem_mb=48)
    hmid = _ffn_mid(z, wfd, bfd)
    dr2, dr2b, loss, d_g2, d_b2 = _ffn_out_loss(hmid, x1, target, wfo, g2, b2)

    g_wfo = _mm_tn(hmid, _views(dr2b), tn=512, tt=1024, name="grad_w_ffn_out", vmem_mb=48)
    dgvb, duc, d_wfd, d_bfd = _ffn_bwd_a(dr2b, z, wfo, wfd, bfd)
    dub = _ffn_bwd_b(duc, wfd)
    g_wfi = _mm_tn(x1b, _views(dub, dgvb), tn=1408, tt=1024, name="grad_w_ffn_in", vmem_mb=48)
    dr1, dr1b, d_g1, d_b1 = _ffn_in_bwd(dr2, dub, dgvb, wfi, r1, g1)

    g_wo = _mm_tn(mixed, _views(dr1b), tn=1024, tt=1024, name="grad_w_out")
    dstack, dyb, dcs, dog = _merge_bwd(dr1b, ycat, proj, wo, wco, wh)
    g_wco = _mm_tn(cs, [(dyb, 0)], tn=1024, tt=1024, name="grad_w_conv_out")
    g_wh = _mm_tn(og, [(dyb, 1)], tn=1024, tt=1024, name="grad_w_hgrn_out")

    dcs = late_reduce.begin((g_wco, g_wh, g_wo, g_wfi, g_wfo), dcs)
    dcc, d_wcd, d_bcd, d_clg, d_clb = _conv_bwd_a(dcs, cc, proj, clg, clb)
    dcc = late_reduce.across_chips(dcc)
    dstack = _conv_bwd_b(dcc, proj, wcd, dstack)
    dstack, d_logits, d_ng = _hgrn_bwd(dog, o, states, proj, logits, ng, dstack)
    dstack = late_reduce.assemble(d_ng, dstack)

    g_wi = _grad_w_in(xb, dstack)
    dstack = early_reduce.begin((g_wi,), dstack)
    grad_x = _grad_x(dstack, wi, dr1, 0)
    grad_x = early_reduce.across_chips(grad_x)
    grad_x = _grad_x(dstack, wi, dr1, 1, grad_x)
    early_reduce.assemble(grad_x, dstack)
    early = early_reduce.finish(grad_x)
    late = late_reduce.finish(grad_x)

    small = dict(w_conv_dw=d_wcd, b_conv_dw=d_bcd, conv_ln_g=d_clg, conv_ln_b=d_clb, hgrn_lb_logits=d_logits,
                 hgrn_norm_g=d_ng, ln1_g=d_g1, ln1_b=d_b1, w_ffn_dw=d_wfd, b_ffn_dw=d_bfd, ln2_g=d_g2, ln2_b=d_b2)
    return loss, grad_x, early, late, small


ELEMENTWISE_BLOCK_ELEMS = 256 * 1024


def _row_tile(rows, cols):
    cap = max(16, ELEMENTWISE_BLOCK_ELEMS // cols)
    if rows <= cap:
        return rows
    best = None
    for cand in range(16, cap + 1, 16):
        if rows % cand == 0:
            best = cand
    assert best is not None
    return best


def _elementwise(fn, ins, out_dtypes, *, name):
    r, c = ins[0].shape
    tr = _row_tile(r, c)

    def body(*refs):
        outs = fn(*[ref[...] for ref in refs[:len(ins)]])
        for ref, val in zip(refs[len(ins):], outs):
            ref[...] = val.astype(ref.dtype)

    spec = pl.BlockSpec((tr, c), lambda i: (i, 0))
    return _call(
        body, name=name, grid=(r // tr,), in_specs=[spec] * len(ins), out_specs=[spec] * len(out_dtypes),
        out_shape=[_sds((r, c), dt) for dt in out_dtypes])(*ins)


def _windowed(fn, sel, ins, outs, *, window, name):
    rows, cols = window
    tr = _row_tile(rows, cols)
    steps = rows // tr

    def spec(where):
        if where is None:
            return pl.BlockSpec((tr, cols), lambda i, s: (i, 0))
        kind, p, _ = where
        if kind == "rows":
            return pl.BlockSpec((tr, cols), lambda i, s: (s[p] * steps + i, 0))
        return pl.BlockSpec((tr, cols), lambda i, s: (i, s[p]))

    n_in = len(ins)

    def body(s_ref, *refs):
        vals = fn(*[ref[...] for ref in refs[:n_in]])
        for ref, val in zip(refs[n_in:], vals):
            ref[...] = val.astype(ref.dtype)

    return pl.pallas_call(
        body, name=name, out_shape=[_hbm(shape, dt) for shape, dt, _ in outs],
        grid_spec=pltpu.PrefetchScalarGridSpec(
            num_scalar_prefetch=1, grid=(steps,), in_specs=[spec(where) for _, where in ins],
            out_specs=[spec(where) for _, _, where in outs]),
        compiler_params=pltpu.CompilerParams(dimension_semantics=("arbitrary",), vmem_limit_bytes=32 * 2 ** 20),
    )(sel, *[_in_hbm(a) for a, _ in ins])


def _adamw(w, g, m, v, *, name):
    def fn(w_, g_, m_, v_):
        m_new = ADAM_B1 * m_ + (1.0 - ADAM_B1) * g_
        v_new = ADAM_B2 * v_ + (1.0 - ADAM_B2) * (g_ * g_)
        m_hat = m_new / ADAM_BC1
        v_hat = v_new / ADAM_BC2
        delta = -ADAM_LR * (m_hat / (jnp.sqrt(v_hat) + ADAM_EPS) + ADAM_WD * w_)
        return delta, m_new, v_new

    return _elementwise(fn, [w, g, m, v], [F32, F32, F32], name=name)


def _place():
    return lax.axis_index("x"), lax.axis_index("y"), lax.axis_index("c")


def _other_chips(x, y):
    return [(1 - x, y), (x, 1 - y), (1 - x, 1 - y)]


SHARD_XOR = (2, 1, 3)


DMA_CHUNK_BYTES = 512 * 1024


def _n_chunks(ref):
    rows = ref.shape[0]
    total = ref.dtype.itemsize
    for d in ref.shape:
        total *= d
    best = 1
    for cand in range(2, min(rows, total // DMA_CHUNK_BYTES) + 1):
        if rows % cand == 0 and (rows // cand) % 16 == 0:
            best = cand
    return best


class _Copy:
    def __init__(self, src, dst, sems, dev=None):
        if dev is None:
            make = lambda s_, d_: pltpu.make_async_copy(s_, d_, sems[0])
        else:
            make = lambda s_, d_: pltpu.make_async_remote_copy(
                src_ref=s_, dst_ref=d_, send_sem=sems[0], recv_sem=sems[1], device_id=dev, device_id_type=MESH)
        self.local = dev is None
        self.whole = make(src, dst)
        n = _n_chunks(src)
        step = src.shape[0] // n
        self.parts = ([self.whole] if n == 1 else
                      [make(src.at[pl.ds(i * step, step)], dst.at[pl.ds(i * step, step)]) for i in range(n)])

    def start(self):
        for part in self.parts:
            part.start()

    def wait_recv(self):
        self.whole.wait_recv()

    def wait_send(self):
        self.whole.wait_send()

    def wait(self):
        self.whole.wait()


def _run_copies(local_ops, remote_ops, lsem, ssem, rsem):
    local = [_Copy(src, dst, (lsem.at[n],)) for n, (src, dst) in enumerate(local_ops)]
    remote = [_Copy(src, dst, (ssem.at[n], rsem.at[n]), dev) for n, (src, dst, dev) in enumerate(remote_ops)]
    for cp in local + remote:
        cp.start()
    for cp in remote:
        cp.wait_recv()
    for cp in remote:
        cp.wait_send()
    for cp in local:
        cp.wait()


BIG = (("w_in", D_MODEL, IN_COLS, 1), ("w_conv_out", CONV_DIM, D_MODEL, 1), ("w_hgrn_out", HGRN_DIM, D_MODEL, 0),
       ("w_out", D_MODEL, D_MODEL, 0), ("w_ffn_in", D_MODEL, 2 * D_FF, 1), ("w_ffn_out", D_FF, D_MODEL, 0))


def _shard_slice(ref, rows, cols, axis, k):
    if axis == 1:
        w = cols // N_CHIPS
        return ref.at[:, pl.ds(k * w, w)]
    h = rows // N_CHIPS
    return ref.at[pl.ds(k * h, h), :]


def _half_slice(ref, rows, cols, axis, hc):
    if axis == 1:
        return ref.at[pl.ds(hc * (rows // 2), rows // 2), :]
    return ref.at[:, pl.ds(hc * (cols // 2), cols // 2)]


def _half_shape(rows, cols, axis):
    return (rows // 2, cols) if axis == 1 else (rows, cols // 2)


def _gather_weights(full, small):
    n_big, n_small = len(full), len(small)
    n_arr = n_big + n_small
    out_shape = ([_sds((r, c), BF16) for _, r, c, _ in BIG[:n_big]]
                 + [_sds((N_CHIPS,) + a.shape, F32) for a in small])
    shard_shape = [(r, c // N_CHIPS) if ax == 1 else (r // N_CHIPS, c) for _, r, c, ax in BIG]

    def body(*refs):
        ins, outs = refs[:n_arr], refs[n_arr:2 * n_arr]
        lsem, ssem, rsem, fsem_s, fsem_r = refs[2 * n_arr:]
        x, y, c = _place()
        me = 2 * x + y
        chips = _other_chips(x, y)
        sibling = (x, y, 1 - c)

        def region(idx, k, hc, bufs=outs):
            (_, r, cc, ax), (sr, _) = BIG[idx], shard_shape[idx]
            return _shard_slice(bufs[idx], r, cc, ax, k).at[pl.ds(hc * (sr // 2), sr // 2)]

        for k in range(N_CHIPS):
            for hc in range(2):
                @pl.when((me == k) & (c == hc))
                def _(k=k, hc=hc):
                    local = [_Copy(ins[n_big + i], outs[n_big + i].at[k], (lsem.at[i],)) for i in range(n_small)]
                    sends, fwds = [], []
                    for j, (cx, cy) in enumerate(chips):
                        for i in range(n_big):
                            n = j * n_arr + i
                            sends.append(_Copy(region(i, k, hc, ins), region(i, k, hc), (ssem.at[n], rsem.at[n]),
                                               (cx, cy, c)))
                            reg = region(i, k ^ SHARD_XOR[j], hc)
                            fwds.append(_Copy(reg, reg, (fsem_s.at[j * n_big + i], fsem_r.at[j * n_big + i]), sibling))
                        for i in range(n_small):
                            n = j * n_arr + n_big + i
                            sends.append(_Copy(ins[n_big + i], outs[n_big + i].at[k], (ssem.at[n], rsem.at[n]),
                                               (cx, cy, c)))
                    for cp in local + sends:
                        cp.start()
                    for j in range(3):
                        for i in range(n_big):
                            sends[j * n_arr + i].wait_recv()
                        for i in range(n_big):
                            fwds[j * n_big + i].start()
                    for j in range(3):
                        for i in range(n_small):
                            sends[j * n_arr + n_big + i].wait_recv()
                    for cp in fwds:
                        cp.wait_recv()
                    for cp in sends + fwds:
                        cp.wait_send()
                    for cp in local:
                        cp.wait()

    return pl.pallas_call(
        body, name="gather_weights", in_specs=[ANY] * n_arr, out_specs=[ANY] * n_arr, out_shape=out_shape,
        input_output_aliases={i: i for i in range(n_big)},
        scratch_shapes=[pltpu.SemaphoreType.DMA((n_small,)), pltpu.SemaphoreType.DMA((3 * n_arr,)),
                        pltpu.SemaphoreType.DMA((3 * n_arr,)), pltpu.SemaphoreType.DMA((3 * n_big,)),
                        pltpu.SemaphoreType.DMA((3 * n_big,))])(*full, *small)


SEM = pl.BlockSpec(memory_space=pltpu.SEMAPHORE)
N_EARLY = 1
LATE = BIG[N_EARLY:]


def _late_copies(src_bufs, dst_bufs, ssem, rsem, x, y, c, k):
    copies = []
    for j, (cx, cy) in enumerate(_other_chips(x, y)):
        for i, (src, dst, (_, r, cc, ax)) in enumerate(zip(src_bufs, dst_bufs, LATE)):
            n = j * len(LATE) + i
            copies.append(_Copy(_shard_slice(src, r, cc, ax, k), _shard_slice(dst, r, cc, ax, k),
                                (ssem.at[n], rsem.at[n]), (cx, cy, c)))
    return copies


def _split_sibling_copies(specs, src_bufs, dst_bufs, ssem, rsem, x, y, c, k):
    n = len(specs)
    return [_Copy(_half_slice(src_bufs[i], r, cc, ax, 1 - k), dst_bufs[n + i], (ssem.at[i], rsem.at[i]), (x, y, 1 - c))
            for i, (_, r, cc, ax) in enumerate(specs)]


def _split_chip_copies(specs, src_bufs, dst_bufs, ssem, rsem, x, y, c, k):
    n = len(specs)
    copies = []
    for j, (cx, cy) in enumerate(_other_chips(x, y)):
        for i, (_, r, cc, ax) in enumerate(specs):
            hr, hc = _half_shape(r, cc, ax)
            copies.append(_Copy(_shard_slice(src_bufs[i], hr, hc, ax, k ^ SHARD_XOR[j]), dst_bufs[n + 3 * i + j],
                                (ssem.at[j * n + i], rsem.at[j * n + i]), (cx, cy, c)))
    return copies


def _split_assemble_copies(specs, src_bufs, dst_bufs, ssem, rsem, x, y, c, k):
    copies = []
    for i, (_, r, cc, ax) in enumerate(specs):
        sr, sc = (r, cc // N_CHIPS) if ax == 1 else (r // N_CHIPS, cc)
        copies.append(_Copy(_half_slice(src_bufs[i], sr, sc, ax, k), _half_slice(dst_bufs[i], sr, sc, ax, k),
                            (ssem.at[i], rsem.at[i]), (x, y, 1 - c)))
    return copies


SIDE_EFFECT = pltpu.CompilerParams(has_side_effects=pltpu.SideEffectType.DATAFLOW_SIDE_EFFECTING)


def _split_start(name, bufs, carry, make_copies, n_copies, key_of, n_keys):
    n = len(bufs)

    def body(*refs):
        ins, (ssem, rsem), outs = refs[:n], refs[n + 1:n + 3], refs[n + 3:2 * n + 3]
        x, y, c = _place()
        key = key_of(x, y, c)
        for k in range(n_keys):
            @pl.when(key == k)
            def _(k=k):
                for cp in make_copies(ins, outs, ssem, rsem, x, y, c, k):
                    cp.start()

    arrays = list(bufs) + [carry]
    outs = pl.pallas_call(
        body, name=name, in_specs=[ANY] * (n + 1), out_specs=[SEM, SEM] + [ANY] * (n + 1),
        out_shape=[pltpu.SemaphoreType.DMA((n_copies,)), pltpu.SemaphoreType.DMA((n_copies,))]
        + [_hbm(a.shape, a.dtype) for a in arrays],
        input_output_aliases={i: 2 + i for i in range(n + 1)}, compiler_params=SIDE_EFFECT,
    )(*[_in_hbm(a) for a in arrays])
    return (outs[0], outs[1]), list(outs[2:2 + n]), outs[2 + n]


def _split_wait(name, sems, bufs, after, make_copies):
    n = len(bufs)

    def body(*refs):
        ins, ssem, rsem = refs[:n], refs[n], refs[n + 1]
        x, y, c = _place()
        for cp in make_copies(ins, ins, ssem, rsem, x, y, c, 0):
            cp.wait_send()
            cp.wait_recv()

    return pl.pallas_call(
        body, name=name, in_specs=[ANY] * n + [SEM, SEM, ANY], out_specs=[ANY] * n,
        out_shape=[_hbm(a.shape, a.dtype) for a in bufs], input_output_aliases={i: i for i in range(n)},
        compiler_params=SIDE_EFFECT,
    )(*[_in_hbm(a) for a in bufs], *sems, after)


def _chip_of(x, y, c):
    return 2 * x + y


def _core_of(x, y, c):
    return c


def _piece_shape(rows, cols, axis):
    hr, hc = _half_shape(rows, cols, axis)
    return (hr, hc // N_CHIPS) if axis == 1 else (hr // N_CHIPS, hc)


def _sharded(axis, p, n):
    return ("cols", p, n) if axis == 1 else ("rows", p, n)


def _across(axis, p, n):
    return ("rows", p, n) if axis == 1 else ("cols", p, n)


def _shard_shape(rows, cols, axis):
    return (rows, cols // N_CHIPS) if axis == 1 else (rows // N_CHIPS, cols)


def _chip_sum(spec, sel, grad, got):
    name, r, c, ax = spec
    return _windowed(lambda a, b: (a + b,), sel, [(grad, _across(ax, 1, 2)), (got, None)],
                     [(_half_shape(r, c, ax), BF16, None)], window=_half_shape(r, c, ax), name="chip_sum_" + name)[0]


def _shard_sum(spec, sel, chip_sum, received):
    name, r, c, ax = spec
    add4 = lambda a, b0, b1, b2: (a.astype(F32) + b0.astype(F32) + b1.astype(F32) + b2.astype(F32),)
    return _windowed(add4, sel, [(chip_sum, _sharded(ax, 0, N_CHIPS))] + [(r_, None) for r_ in received],
                     [(_shard_shape(r, c, ax), F32, _across(ax, 1, 2))], window=_piece_shape(r, c, ax),
                     name="shard_sum_" + name)[0]


class _SplitReduce:
    def __init__(self, specs, sel, tag):
        self.specs, self.sel, self.tag = specs, sel, tag
        self.sibling = functools.partial(_split_sibling_copies, specs)
        self.chip = functools.partial(_split_chip_copies, specs)
        self.halves = functools.partial(_split_assemble_copies, specs)

    def begin(self, grads, carry):
        landing = [lax.empty(_half_shape(r, c, ax), F32) for _, r, c, ax in self.specs]
        self.sems, self.bufs, carry = _split_start(self.tag + "_sibling_start", list(grads) + landing, carry,
                                                   self.sibling, len(self.specs), _core_of, 2)
        return carry

    def across_chips(self, carry):
        n = len(self.specs)
        bufs = _split_wait(self.tag + "_sibling_wait", self.sems, self.bufs, carry, self.sibling)
        chip_sums = [_chip_sum(spec, self.sel, g_, h_) for spec, g_, h_ in zip(self.specs, bufs[:n], bufs[n:])]
        landing = [lax.empty(_piece_shape(r, c, ax), BF16) for _, r, c, ax in self.specs for _ in range(3)]
        self.sems, self.bufs, carry = _split_start(self.tag + "_chip_start", chip_sums + landing, carry,
                                                   self.chip, 3 * n, _chip_of, N_CHIPS)
        return carry

    def assemble(self, after, carry):
        n = len(self.specs)
        bufs = _split_wait(self.tag + "_chip_wait", self.sems, self.bufs, after, self.chip)
        half_filled = [_shard_sum(spec, self.sel, bufs[i], bufs[n + 3 * i:n + 3 * i + 3])
                       for i, spec in enumerate(self.specs)]
        self.sems, self.bufs, carry = _split_start(self.tag + "_assemble_start", half_filled, carry,
                                                   self.halves, n, _core_of, 2)
        return carry

    def finish(self, after):
        return _split_wait(self.tag + "_assemble_wait", self.sems, self.bufs, after, self.halves)


def _all_reduce_small(packed):
    r, w = packed.shape

    def body(in_ref, out_ref, slots, lsem, ssem, rsem):
        x, y, c = _place()
        me = 4 * x + 2 * y + c
        peers = [(x ^ (m >> 2), y ^ ((m >> 1) & 1), c ^ (m & 1)) for m in range(1, N_DEV)]
        _run_copies([(in_ref, slots.at[me])], [(in_ref, slots.at[me], dev) for dev in peers], lsem, ssem, rsem)
        total = slots[0]
        for d in range(1, N_DEV):
            total = total + slots[d]
        out_ref[...] = total

    vmem = pl.BlockSpec(memory_space=pltpu.VMEM)
    return pl.pallas_call(
        body, name="small_all_reduce", in_specs=[vmem], out_specs=vmem, out_shape=_sds((r, w), F32),
        scratch_shapes=[pltpu.VMEM((N_DEV, r, w), F32), pltpu.SemaphoreType.DMA((1,)),
                        pltpu.SemaphoreType.DMA((N_DEV - 1,)), pltpu.SemaphoreType.DMA((N_DEV - 1,))])(packed)


SMALL_ORDER = ("w_conv_dw", "b_conv_dw", "conv_ln_g", "conv_ln_b", "hgrn_lb_logits", "hgrn_norm_g",
               "ln1_g", "ln1_b", "w_ffn_dw", "b_ffn_dw", "ln2_g", "ln2_b")
REPLICATED_SMALL = tuple(n for n in SMALL_ORDER if n not in ("w_conv_dw", "w_ffn_dw"))
WEIGHT_ORDER = ("w_in", "w_conv_dw", "b_conv_dw", "conv_ln_g", "conv_ln_b", "w_conv_out", "hgrn_lb_logits",
                "hgrn_norm_g", "w_hgrn_out", "w_out", "ln1_g", "ln1_b", "w_ffn_in", "w_ffn_dw", "b_ffn_dw",
                "w_ffn_out", "ln2_g", "ln2_b")


def _pack(arrs):
    flat = jnp.concatenate([a.reshape(-1) for a in arrs])
    assert flat.shape[0] % 128 == 0
    return flat.reshape(-1, 128)


def _unpack(packed, shapes):
    flat = packed.reshape(-1)
    out, pos = [], 0
    for shp in shapes:
        size = 1
        for d in shp:
            size *= d
        out.append(flat[pos:pos + size].reshape(shp))
        pos += size
    return out


def kernel(x, w_in, w_conv_dw, b_conv_dw, conv_ln_g, conv_ln_b, w_conv_out, hgrn_lb_logits, hgrn_norm_g, w_hgrn_out, w_out, ln1_g, ln1_b, w_ffn_in, w_ffn_dw, b_ffn_dw, w_ffn_out, ln2_g, ln2_b, loss_target, m_w_in, m_w_conv_dw, m_b_conv_dw, m_conv_ln_g, m_conv_ln_b, m_w_conv_out, m_hgrn_lb_logits, m_hgrn_norm_g, m_w_hgrn_out, m_w_out, m_ln1_g, m_ln1_b, m_w_ffn_in, m_w_ffn_dw, m_b_ffn_dw, m_w_ffn_out, m_ln2_g, m_ln2_b, v_w_in, v_w_conv_dw, v_b_conv_dw, v_conv_ln_g, v_conv_ln_b, v_w_conv_out, v_hgrn_lb_logits, v_hgrn_norm_g, v_w_hgrn_out, v_w_out, v_ln1_g, v_ln1_b, v_w_ffn_in, v_w_ffn_dw, v_b_ffn_dw, v_w_ffn_out, v_ln2_g, v_ln2_b):
    w = dict(w_in=w_in, w_conv_dw=w_conv_dw, b_conv_dw=b_conv_dw, conv_ln_g=conv_ln_g, conv_ln_b=conv_ln_b,
             w_conv_out=w_conv_out, hgrn_lb_logits=hgrn_lb_logits, hgrn_norm_g=hgrn_norm_g, w_hgrn_out=w_hgrn_out,
             w_out=w_out, ln1_g=ln1_g, ln1_b=ln1_b, w_ffn_in=w_ffn_in, w_ffn_dw=w_ffn_dw, b_ffn_dw=b_ffn_dw,
             w_ffn_out=w_ffn_out, ln2_g=ln2_g, ln2_b=ln2_b)
    m = dict(w_in=m_w_in, w_conv_dw=m_w_conv_dw, b_conv_dw=m_b_conv_dw, conv_ln_g=m_conv_ln_g, conv_ln_b=m_conv_ln_b,
             w_conv_out=m_w_conv_out, hgrn_lb_logits=m_hgrn_lb_logits, hgrn_norm_g=m_hgrn_norm_g,
             w_hgrn_out=m_w_hgrn_out, w_out=m_w_out, ln1_g=m_ln1_g, ln1_b=m_ln1_b, w_ffn_in=m_w_ffn_in,
             w_ffn_dw=m_w_ffn_dw, b_ffn_dw=m_b_ffn_dw, w_ffn_out=m_w_ffn_out, ln2_g=m_ln2_g, ln2_b=m_ln2_b)
    v = dict(w_in=v_w_in, w_conv_dw=v_w_conv_dw, b_conv_dw=v_b_conv_dw, conv_ln_g=v_conv_ln_g, conv_ln_b=v_conv_ln_b,
             w_conv_out=v_w_conv_out, hgrn_lb_logits=v_hgrn_lb_logits, hgrn_norm_g=v_hgrn_norm_g,
             w_hgrn_out=v_w_hgrn_out, w_out=v_w_out, ln1_g=v_ln1_g, ln1_b=v_ln1_b, w_ffn_in=v_w_ffn_in,
             w_ffn_dw=v_w_ffn_dw, b_ffn_dw=v_b_ffn_dw, w_ffn_out=v_w_ffn_out, ln2_g=v_ln2_g, ln2_b=v_ln2_b)
    big_names = [n for n, _, _, _ in BIG]
    w2 = {n: a[0] if a.ndim == 3 else a for n, a in w.items()}
    m2 = {n: a[0] if a.ndim == 3 else a for n, a in m.items()}
    v2 = {n: a[0] if a.ndim == 3 else a for n, a in v.items()}

    sel = jnp.stack([2 * lax.axis_index("x") + lax.axis_index("y"), lax.axis_index("c")]).astype(jnp.int32)

    placed = [_windowed(lambda a: (a,), sel, [(w2[n], None)], [((r, c), BF16, _sharded(ax, 0, N_CHIPS))],
                        window=w2[n].shape, name="cast_" + n)[0] for n, r, c, ax in BIG]
    wi, wcd4, wfd4 = _gather_weights(placed[:N_EARLY], [w2["w_conv_dw"], w2["w_ffn_dw"]])
    wcd = jnp.transpose(wcd4, (1, 0, 2)).reshape(CONV_KERNEL, CONV_DIM)
    wfd = jnp.transpose(wfd4, (1, 0, 2)).reshape(FFN_KERNEL, D_FF)
    gather_sems, in_flight, wi = _split_start("gather_late_start", placed[N_EARLY:], wi, _late_copies,
                                               3 * len(LATE), _chip_of, N_CHIPS)
    late_weights = lambda after: _split_wait("gather_late_wait", gather_sems, in_flight, after, _late_copies)

    loss_part, grad_x, early_shard_grads, late_shard_grads, small_grads = _local_step(
        x[0], loss_target[0], wi, late_weights,
        _SplitReduce(BIG[:N_EARLY], sel, "early"), _SplitReduce(LATE, sel, "late"),
        wcd, w2["b_conv_dw"], w2["conv_ln_g"], w2["conv_ln_b"],
        w2["hgrn_lb_logits"], w2["hgrn_norm_g"], w2["ln1_g"], w2["ln1_b"], wfd, w2["b_ffn_dw"],
        w2["ln2_g"], w2["ln2_b"])
    loss = lax.psum(loss_part[0, 0], ("x", "y", "c"))
    shard_grads = dict(zip(big_names, list(early_shard_grads) + list(late_shard_grads)))

    small_shapes = [small_grads[n].shape for n in SMALL_ORDER]
    reduced = dict(zip(SMALL_ORDER, _unpack(_all_reduce_small(_pack([small_grads[n] for n in SMALL_ORDER])),
                                            small_shapes)))
    shard = 2 * lax.axis_index("x") + lax.axis_index("y")
    grads = dict(shard_grads)
    for n in REPLICATED_SMALL:
        grads[n] = reduced[n]
    grads["w_conv_dw"] = lax.dynamic_slice_in_dim(reduced["w_conv_dw"], shard * (CONV_DIM // N_CHIPS),
                                                  CONV_DIM // N_CHIPS, axis=1)
    grads["w_ffn_dw"] = lax.dynamic_slice_in_dim(reduced["w_ffn_dw"], shard * (D_FF // N_CHIPS),
                                                 D_FF // N_CHIPS, axis=1)

    delta, new_m, new_v = {}, {}, {}
    for n in big_names + ["w_conv_dw", "w_ffn_dw"]:
        delta[n], new_m[n], new_v[n] = _adamw(w2[n], grads[n], m2[n], v2[n], name="adamw_" + n)
    rep_shapes = [w2[n].shape for n in REPLICATED_SMALL]
    packed = _adamw(*[_pack([src[n] for n in REPLICATED_SMALL]) for src in (w2, grads, m2, v2)], name="adamw_small")
    for dst, pk in zip((delta, new_m, new_v), packed):
        for n, a in zip(REPLICATED_SMALL, _unpack(pk, rep_shapes)):
            dst[n] = a

    def shaped(d):
        return [d[n].reshape(w[n].shape) for n in WEIGHT_ORDER]

    return (loss, grad_x[None], *shaped(grads), *shaped(delta), *shaped(new_m), *shaped(new_v))
```
